```python
import math
import jax, jax.numpy as jnp
from jax import lax
import numpy as np

D_MODEL = 1024
BATCH = 8
SEQ = 4096
DEPTH = 1

N_META = 16
EPS = 1e-6
ROPE_THETA = 10000.0

DA_HEADS = 4
DA_HEAD_DIM = D_MODEL // 16
DA_V_DIM = 2 * DA_HEAD_DIM
DA_QK_WIDTH = DA_HEADS * 2 * DA_HEAD_DIM
DA_V_WIDTH = DA_HEADS * DA_V_DIM
Q_BLOCK = 128

GLA_HEADS = 4
GLA_DK = D_MODEL // 16
GLA_DV = D_MODEL // 8
GLA_QK_WIDTH = GLA_HEADS * GLA_DK
GLA_V_WIDTH = GLA_HEADS * GLA_DV
GLA_GATE_RANK = 16
GLA_GATE_TAU = 16.0
GLA_CHUNK = 64

N_BRANCH = 2
BRANCH_WIDTH = DA_V_WIDTH

IN_SPLITS = (DA_QK_WIDTH, DA_QK_WIDTH, DA_V_WIDTH,
             GLA_QK_WIDTH, GLA_QK_WIDTH, GLA_V_WIDTH, GLA_V_WIDTH, GLA_GATE_RANK,
             N_BRANCH * D_MODEL)
D_IN = 2 * DA_QK_WIDTH + DA_V_WIDTH + 2 * GLA_QK_WIDTH + 2 * GLA_V_WIDTH + GLA_GATE_RANK + N_BRANCH * D_MODEL

N_GROUPS = 4
EXPERTS_PER_GROUP = 8
N_EXPERTS = N_GROUPS * EXPERTS_PER_GROUP
TOP_K = 2
D_EXPERT = D_MODEL // 2
MOE_BLOCK = 128

kernel_name = 'hybrid_diffattn_gla_hier_moe'


def rms_norm(x, gain):
    xf = x.astype(jnp.float32)
    y = xf * lax.rsqrt(jnp.mean(xf * xf, axis=-1, keepdims=True) + EPS)
    return (y * gain.astype(jnp.float32)).astype(x.dtype)


def apply_rope(x, pos):
    d = x.shape[-1]
    half = d // 2
    inv_freq = jnp.power(ROPE_THETA, -jnp.arange(half, dtype=jnp.float32) * 2.0 / d)
    ang = pos[:, None] * inv_freq[None, :]
    bshape = (pos.shape[0],) + (1,) * (x.ndim - 3) + (half,)
    cos = jnp.cos(ang).reshape(bshape).astype(x.dtype)
    sin = jnp.sin(ang).reshape(bshape).astype(x.dtype)
    x1, x2 = x[..., :half], x[..., half:]
    return jnp.concatenate([x1 * cos - x2 * sin, x2 * cos + x1 * sin], axis=-1)


def pad_time(x, left, right):
    pw = [(0, 0)] * x.ndim
    pw[1] = (left, right)
    return jnp.pad(x, pw)


def split_columns(z):
    outs = []
    off = 0
    for s in IN_SPLITS:
        outs.append(z[..., off:off + s])
        off += s
    return outs


def diff_attention(q, k, v, g_q, g_k, g_subln, lam_q1, lam_k1, lam_q2, lam_k2, layer_idx):
    bsz, length, _ = q.shape
    q = rms_norm(q.reshape(bsz, length, DA_HEADS, 2, DA_HEAD_DIM), g_q)
    k = rms_norm(k.reshape(bsz, length, DA_HEADS, 2, DA_HEAD_DIM), g_k)
    pos = jnp.arange(length, dtype=jnp.float32)
    q = apply_rope(q, pos)
    k = apply_rope(k, pos)
    n_blocks = -(-length // Q_BLOCK)
    lp = n_blocks * Q_BLOCK
    q = pad_time(q, 0, lp - length).transpose(0, 2, 3, 1, 4)
    k = pad_time(k, 0, lp - length).transpose(0, 2, 3, 1, 4)
    v = pad_time(v.reshape(bsz, length, DA_HEADS, DA_V_DIM), 0, lp - length).transpose(0, 2, 1, 3)
    lam_init = 0.8 - 0.6 * math.exp(-0.3 * layer_idx)
    lam = (jnp.exp(jnp.sum(lam_q1.astype(jnp.float32) * lam_k1.astype(jnp.float32)))
           - jnp.exp(jnp.sum(lam_q2.astype(jnp.float32) * lam_k2.astype(jnp.float32))) + lam_init)
    scale = DA_HEAD_DIM ** -0.5
    k_pos = jnp.arange(lp)

    def one_block(i):
        q_blk = lax.dynamic_slice_in_dim(q, i * Q_BLOCK, Q_BLOCK, axis=3)
        s = jnp.einsum('bhmqd,bhmkd->bhmqk', q_blk, k, preferred_element_type=jnp.float32) * scale
        q_pos = i * Q_BLOCK + jnp.arange(Q_BLOCK)
        s = jnp.where(k_pos[None, :] <= q_pos[:, None], s, -jnp.inf)
        p = jax.nn.softmax(s, axis=-1)
        a = p[:, :, 0] - lam * p[:, :, 1]
        return jnp.einsum('bhqk,bhke->bhqe', a.astype(v.dtype), v)

    o = lax.map(one_block, jnp.arange(n_blocks))
    o = o.transpose(1, 0, 3, 2, 4).reshape(bsz, lp, DA_HEADS, DA_V_DIM)[:, :length]
    o = rms_norm(o, g_subln) * (1.0 - lam_init)
    return o.reshape(bsz, length, DA_V_WIDTH)


def gated_linear_attention(q, k, v, r, g_lr, w_gate_up, b_gate, g_norm):
    bsz, length, _ = q.shape
    f32 = jnp.float32
    q = q.astype(f32).reshape(bsz, length, GLA_HEADS, GLA_DK) * (GLA_DK ** -0.5)
    k = k.astype(f32).reshape(bsz, length, GLA_HEADS, GLA_DK)
    v = v.astype(f32).reshape(bsz, length, GLA_HEADS, GLA_DV)
    log_a = jax.nn.log_sigmoid(jnp.einsum('blr,rk->blk', g_lr, w_gate_up, preferred_element_type=f32)
                               + b_gate.astype(f32)) / GLA_GATE_TAU
    log_a = log_a.reshape(bsz, length, GLA_HEADS, GLA_DK)
    left = (-N_META) % GLA_CHUNK
    right = (-(length + left)) % GLA_CHUNK
    n_chunks = (length + left + right) // GLA_CHUNK

    def to_chunks(t):
        t = pad_time(t, left, right)
        return t.reshape(bsz, n_chunks, GLA_CHUNK, GLA_HEADS, t.shape[-1]).transpose(1, 0, 3, 2, 4)

    causal = jnp.tril(jnp.ones((GLA_CHUNK, GLA_CHUNK), dtype=bool))

    def chunk_step(state, inp):
        q_c, k_c, v_c, a_c = inp
        b = jnp.cumsum(a_c, axis=2)
        b_last = b[:, :, -1:, :]
        inter = jnp.einsum('bhtd,bhde->bhte', q_c * jnp.exp(b), state)
        rel = jnp.where(causal[:, :, None], b[:, :, :, None, :] - b[:, :, None, :, :], -jnp.inf)
        scores = jnp.einsum('bhtd,bhsd,bhtsd->bhts', q_c, k_c, jnp.exp(rel))
        o = inter + jnp.einsum('bhts,bhse->bhte', scores, v_c)
        state = (jnp.exp(b_last[:, :, 0, :])[..., None] * state
                 + jnp.einsum('bhsd,bhse->bhde', k_c * jnp.exp(b_last - b), v_c))
        return state, o

    state0 = jnp.zeros((bsz, GLA_HEADS, GLA_DK, GLA_DV), f32)
    _, o = lax.scan(chunk_step, state0, (to_chunks(q), to_chunks(k), to_chunks(v), to_chunks(log_a)))
    o = o.transpose(1, 0, 3, 2, 4).reshape(bsz, n_chunks * GLA_CHUNK, GLA_HEADS, GLA_DV)[:, left:left + length]
    o = rms_norm(o, g_norm).reshape(bsz, length, GLA_V_WIDTH) * jax.nn.silu(r.astype(f32))
    return o.astype(r.dtype)


def hierarchical_moe(h, w_rg, b_rg, w_re, b_re, w_g, w_u, w_d):
    bsz, length, d = h.shape
    n = bsz * length
    t = h.reshape(n, d)
    g_logits = jnp.einsum('nd,dg->ng', t, w_rg, preferred_element_type=jnp.float32) + b_rg.astype(jnp.float32)
    g_prob = jax.nn.softmax(g_logits, axis=-1)
    g_idx = jnp.argmax(g_logits, axis=-1)
    g_w = jnp.take_along_axis(g_prob, g_idx[:, None], axis=1)[:, 0]
    e_logits_all = jnp.einsum('nd,dge->nge', t, w_re, preferred_element_type=jnp.float32) + b_re.astype(jnp.float32)
    e_logits = jnp.take_along_axis(e_logits_all, g_idx[:, None, None], axis=1)[:, 0]
    top_v, top_i = lax.top_k(e_logits, TOP_K)
    e_w = jax.nn.softmax(top_v, axis=-1)
    ids = (g_idx[:, None] * EXPERTS_PER_GROUP + top_i).reshape(-1).astype(jnp.int32)
    wts = (g_w[:, None] * e_w).reshape(-1)
    tok = jnp.repeat(jnp.arange(n, dtype=jnp.int32), TOP_K)
    order = jnp.argsort(ids)
    ids_s, tok_s, w_s = ids[order], tok[order], wts[order]
    counts = jnp.bincount(ids, length=N_EXPERTS)
    starts = jnp.cumsum(counts) - counts
    padded = (counts + MOE_BLOCK - 1) // MOE_BLOCK * MOE_BLOCK
    pends = jnp.cumsum(padded)
    pstarts = pends - padded
    dest = pstarts[ids_s] + jnp.arange(n * TOP_K) - starts[ids_s]
    n_slots = -(-(n * TOP_K) // MOE_BLOCK) * MOE_BLOCK + N_EXPERTS * MOE_BLOCK
    n_blocks = n_slots // MOE_BLOCK
    slot_tok = jnp.full((n_slots,), n, jnp.int32).at[dest].set(tok_s)
    slot_w = jnp.zeros((n_slots,), jnp.float32).at[dest].set(w_s)
    block_e = jnp.minimum(jnp.searchsorted(pends, jnp.arange(n_blocks) * MOE_BLOCK, side='right'), N_EXPERTS - 1)
    t_pad = jnp.concatenate([t, jnp.zeros((1, d), t.dtype)], axis=0)

    def run_block(args):
        tok_b, w_b, e = args
        xb = t_pad[tok_b]
        hid = jax.nn.silu(xb @ w_g[e]) * (xb @ w_u[e])
        return (hid @ w_d[e]) * w_b[:, None].astype(xb.dtype)

    y = lax.map(run_block, (slot_tok.reshape(n_blocks, MOE_BLOCK), slot_w.reshape(n_blocks, MOE_BLOCK), block_e))
    out = jnp.zeros((n + 1, d), h.dtype).at[slot_tok].add(y.reshape(n_slots, d).astype(h.dtype))
    return out[:n].reshape(bsz, length, d)


def setup_inputs(seed: int = 0) -> dict:
    key = jax.random.key(seed)
    ks = jax.random.split(key, 26)
    f32 = jnp.float32

    def nrm(k, shape, scale):
        return jax.random.normal(k, shape, f32) * scale

    return {
        'x': nrm(ks[0], (BATCH, SEQ, D_MODEL), 1.0),
        'meta_tokens': nrm(ks[1], (N_META, D_MODEL), 1.0),
        'g_mix_norm': 1.0 + nrm(ks[2], (DEPTH, D_MODEL), 0.02),
        'w_in': nrm(ks[3], (DEPTH, D_MODEL, D_IN), D_MODEL ** -0.5),
        'g_q_norm': 1.0 + nrm(ks[4], (DEPTH, DA_HEAD_DIM), 0.02),
        'g_k_norm': 1.0 + nrm(ks[5], (DEPTH, DA_HEAD_DIM), 0.02),
        'lambda_q1': nrm(ks[6], (DEPTH, DA_HEAD_DIM), 0.1),
        'lambda_k1': nrm(ks[7], (DEPTH, DA_HEAD_DIM), 0.1),
        'lambda_q2': nrm(ks[8], (DEPTH, DA_HEAD_DIM), 0.1),
        'lambda_k2': nrm(ks[9], (DEPTH, DA_HEAD_DIM), 0.1),
        'g_diff_subln': 1.0 + nrm(ks[10], (DEPTH, DA_V_DIM), 0.02),
        'w_gla_gate_up': nrm(ks[11], (DEPTH, GLA_GATE_RANK, GLA_QK_WIDTH), GLA_GATE_RANK ** -0.5),
        'b_gla_gate': nrm(ks[12], (DEPTH, GLA_QK_WIDTH), 0.1),
        'g_gla_norm': 1.0 + nrm(ks[13], (DEPTH, GLA_DV), 0.02),
        'w_branch': nrm(ks[14], (DEPTH, N_BRANCH, BRANCH_WIDTH, D_MODEL), BRANCH_WIDTH ** -0.5),
        'b_merge_gate': nrm(ks[15], (DEPTH, N_BRANCH, D_MODEL), 0.1),
        'w_out': nrm(ks[16], (DEPTH, D_MODEL, D_MODEL), D_MODEL ** -0.5),
        'g_ffn_norm': 1.0 + nrm(ks[17], (DEPTH, D_MODEL), 0.02),
        'w_router_group': nrm(ks[18], (DEPTH, D_MODEL, N_GROUPS), D_MODEL ** -0.5),
        'b_router_group': nrm(ks[19], (DEPTH, N_GROUPS), 0.01),
        'w_router_expert': nrm(ks[20], (DEPTH, D_MODEL, N_GROUPS, EXPERTS_PER_GROUP), D_MODEL ** -0.5),
        'b_router_expert': nrm(ks[21], (DEPTH, N_GROUPS, EXPERTS_PER_GROUP), 0.01),
        'w_exp_gate': nrm(ks[22], (DEPTH, N_EXPERTS, D_MODEL, D_EXPERT), D_MODEL ** -0.5),
        'w_exp_up': nrm(ks[23], (DEPTH, N_EXPERTS, D_MODEL, D_EXPERT), D_MODEL ** -0.5),
        'w_exp_down': nrm(ks[24], (DEPTH, N_EXPERTS, D_EXPERT, D_MODEL), D_EXPERT ** -0.5),
    }


def reference(x, meta_tokens, g_mix_norm, w_in, g_q_norm, g_k_norm, lambda_q1, lambda_k1, lambda_q2, lambda_k2,
              g_diff_subln, w_gla_gate_up, b_gla_gate, g_gla_norm, w_branch, b_merge_gate, w_out, g_ffn_norm,
              w_router_group, b_router_group, w_router_expert, b_router_expert, w_exp_gate, w_exp_up, w_exp_down):
    bsz = x.shape[0]
    meta = jnp.broadcast_to(meta_tokens.astype(x.dtype)[None], (bsz, N_META, D_MODEL))
    u = jnp.concatenate([meta, x], axis=1)
    length = u.shape[1]
    for l in range(DEPTH):
        h = rms_norm(u, g_mix_norm[l])
        z = jnp.einsum('bld,de->ble', h, w_in[l])
        da_q, da_k, da_v, gla_q, gla_k, gla_v, gla_r, gla_g, gate_logits = split_columns(z)
        o_a = diff_attention(da_q, da_k, da_v, g_q_norm[l], g_k_norm[l], g_diff_subln[l],
                             lambda_q1[l], lambda_k1[l], lambda_q2[l], lambda_k2[l], l)
        o_b = gated_linear_attention(gla_q, gla_k, gla_v, gla_r, gla_g, w_gla_gate_up[l], b_gla_gate[l], g_gla_norm[l])
        branches = jnp.stack([o_a, o_b], axis=2)
        y = jnp.einsum('blnw,nwd->blnd', branches, w_branch[l])
        gates = jax.nn.sigmoid(gate_logits.reshape(bsz, length, N_BRANCH, D_MODEL) + b_merge_gate[l])
        merged = jnp.sum(gates * y, axis=2)
        u = u + jnp.einsum('bld,de->ble', merged, w_out[l])
        h2 = rms_norm(u, g_ffn_norm[l])
        u = u + hierarchical_moe(h2, w_router_group[l], b_router_group[l], w_router_expert[l], b_router_expert[l],
                                 w_exp_gate[l], w_exp_up[l], w_exp_down[l])
    return u[:, N_META:]
```

```python
import functools
import math

import jax
import jax.numpy as jnp
from jax import lax
from jax.experimental import pallas as pl
from jax.experimental.pallas import tpu as pltpu

F32 = jnp.float32
BF16 = jnp.bfloat16

D_MODEL = 1024
N_META = 16
EPS = 1e-6
ROPE_THETA = 10000.0

DA_HEADS = 4
DA_HEAD_DIM = 64
DA_V_DIM = 128
GLA_HEADS = 4
GLA_DK = 64
GLA_DV = 128
GLA_RANK = 16
GLA_TAU = 16.0
GLA_CHUNK = 64
N_GROUPS = 4
EXPERTS_PER_GROUP = 8
N_EXPERTS = 32
D_EXPERT = 512

LANES = 128
FRONT = 256
ATT_TILE = 256
ROW_TILE = 512
GLA_GROUP = 512
MOE_TILE = 256
DMA_TILE = 256
NEG_BIG = -1e30
VMEM_LIMIT = 56 * 1024 * 1024


def _cparams(sem):
    return pltpu.CompilerParams(dimension_semantics=sem, vmem_limit_bytes=VMEM_LIMIT)


def _const_spec(shape):
    nd = len(shape)
    return pl.BlockSpec(shape, lambda *_: (0,) * nd)


RT = D_MODEL // LANES


def _row_tile_spec(rows, index_map):
    return pl.BlockSpec((rows * RT, LANES), index_map)


def _token_rows(ref, tok):
    return ref.at[pl.ds(pl.multiple_of(tok * RT, RT), RT)]


def _load_row_tiles(ref, rows, first=0):
    return jnp.concatenate([ref[pl.ds(first * RT + c, rows, stride=RT), :] for c in range(RT)], axis=1)


def _store_row_tiles(ref, val):
    for c in range(RT):
        ref[pl.ds(c, val.shape[0], stride=RT), :] = val[:, c * LANES:(c + 1) * LANES]


def _sigmoid(x):
    return 1.0 / (1.0 + jnp.exp(-x))


def _log_sigmoid(x):
    return jnp.minimum(x, 0.0) - jnp.log1p(jnp.exp(-jnp.abs(x)))


def _inproj_body(u_ref, gmix_ref, cos_ref, sin_ref, gqn_ref, gkn_ref,
                 wq_ref, wk_ref, wv_ref, wgq_ref, wgk_ref, wgv_ref, wgr_ref, wgg_ref,
                 wup_ref, bup_ref, wgate_ref, bgate_ref,
                 q_out, k_out, v_out, gq_out, gk_out, gv_out, sr_out, la_out, gate_out):
    x = u_ref[...]
    ms = jnp.mean(x * x, axis=-1, keepdims=True)
    h = (x * lax.rsqrt(ms + EPS) * gmix_ref[...]).astype(BF16)

    cos = cos_ref[...]
    sin = sin_ref[...]
    lane = lax.broadcasted_iota(jnp.int32, (1, LANES), 1)
    first_half = (lane % DA_HEAD_DIM) < (DA_HEAD_DIM // 2)
    gi = lax.broadcasted_iota(jnp.int32, (LANES, LANES), 0) // DA_HEAD_DIM
    gj = lax.broadcasted_iota(jnp.int32, (LANES, LANES), 1) // DA_HEAD_DIM
    group_sum = (gi == gj).astype(BF16)

    def norm_rope(w_ref, gain_ref, out_ref, scale):
        z = jnp.dot(h, w_ref[...], preferred_element_type=F32)
        for hh in range(DA_HEADS):
            zh = z[:, hh * LANES:(hh + 1) * LANES]
            ssq = jnp.dot((zh * zh).astype(BF16), group_sum, preferred_element_type=F32)
            zn = zh * lax.rsqrt(ssq * (1.0 / DA_HEAD_DIM) + EPS) * gain_ref[...]
            rot = jnp.where(first_half,
                            pltpu.roll(zn, LANES - DA_HEAD_DIM // 2, 1),
                            pltpu.roll(zn, DA_HEAD_DIM // 2, 1))
            zr = zn * cos + rot * sin
            out_ref[:, hh * LANES:(hh + 1) * LANES] = (zr * scale).astype(out_ref.dtype)

    norm_rope(wq_ref, gqn_ref, q_out, DA_HEAD_DIM ** -0.5)
    norm_rope(wk_ref, gkn_ref, k_out, 1.0)
    v_out[...] = jnp.dot(h, wv_ref[...], preferred_element_type=F32).astype(v_out.dtype)

    gq_out[...] = (jnp.dot(h, wgq_ref[...], preferred_element_type=F32) * (GLA_DK ** -0.5)).astype(gq_out.dtype)
    gk_out[...] = jnp.dot(h, wgk_ref[...], preferred_element_type=F32).astype(gk_out.dtype)
    gv_out[...] = jnp.dot(h, wgv_ref[...], preferred_element_type=F32).astype(gv_out.dtype)
    r = jnp.dot(h, wgr_ref[...], preferred_element_type=F32)
    sr_out[...] = (r * _sigmoid(r)).astype(sr_out.dtype)

    g_lr = jnp.dot(h, wgg_ref[...], preferred_element_type=F32)
    pre = jnp.dot(g_lr.astype(BF16), wup_ref[...], preferred_element_type=F32) + bup_ref[...]
    la_out[...] = _log_sigmoid(pre) * (1.0 / GLA_TAU)

    gl = jnp.dot(h, wgate_ref[...], preferred_element_type=F32) + bgate_ref[...]
    gate_out[...] = _sigmoid(gl).astype(gate_out.dtype)


def _inproj(u, tm, cos, sin, p):
    rows = u.shape[0]
    n_tab = cos.shape[0] // tm
    row = lambda w: pl.BlockSpec((tm, w), lambda i: (i, 0))
    tab = pl.BlockSpec((tm, LANES), lambda i: (i % n_tab, 0))
    weights = [p['wq'], p['wk'], p['wv'], p['wgq'], p['wgk'], p['wgv'], p['wgr'], p['wgg'],
               p['wup'], p['bup'], p['wgate'], p['bgate']]
    out_widths = [(512, BF16), (512, BF16), (512, BF16), (256, BF16), (256, BF16), (512, BF16),
                  (512, BF16), (256, F32), (2048, BF16)]
    return pl.pallas_call(
        _inproj_body,
        grid=(rows // tm,),
        in_specs=[row(D_MODEL), _const_spec((1, D_MODEL)), tab, tab,
                  _const_spec((1, LANES)), _const_spec((1, LANES))]
                 + [_const_spec(w.shape) for w in weights],
        out_specs=[row(w) for w, _ in out_widths],
        out_shape=[jax.ShapeDtypeStruct((rows, w), dt) for w, dt in out_widths],
        compiler_params=_cparams(("parallel",)),
        name="inproj",
    )(u, p['gmix'], cos, sin, p['gqn'], p['gkn'], *weights)


def _diff_attn_body(lq1_ref, lk1_ref, lq2_ref, lk2_ref, gsub_ref,
                    q_ref, kf_ref, vf_ref, kx_ref, vx_ref, o_ref, m_sc, l_sc, acc_sc, *, lam_init):
    qi = pl.program_id(2)
    tq = ATT_TILE
    q = q_ref[...]
    lane = lax.broadcasted_iota(jnp.int32, (1, LANES), 1)
    zero = jnp.zeros_like(q)
    qs = jnp.concatenate([jnp.where(lane < DA_HEAD_DIM, q, zero),
                          jnp.where(lane >= DA_HEAD_DIM, q, zero)], axis=0)

    m_sc[...] = jnp.full(m_sc.shape, NEG_BIG, F32)
    l_sc[...] = jnp.zeros(l_sc.shape, F32)
    acc_sc[...] = jnp.zeros(acc_sc.shape, F32)

    def step(kb, vb, mask):
        s = lax.dot_general(qs, kb, (((1,), (1,)), ((), ())), preferred_element_type=F32)
        if mask is not None:
            s = jnp.where(mask, s, NEG_BIG)
        m_old = m_sc[...]
        m_new = jnp.maximum(m_old, jnp.max(s, axis=-1, keepdims=True))
        alpha = jnp.exp(m_old - m_new)
        pr = jnp.exp(s - m_new)
        l_sc[...] = alpha * l_sc[...] + jnp.sum(pr, axis=-1, keepdims=True)
        acc_sc[...] = alpha * acc_sc[...] + jnp.dot(pr.astype(BF16), vb, preferred_element_type=F32)
        m_sc[...] = m_new

    col = lax.broadcasted_iota(jnp.int32, (2 * tq, tq), 1)
    row = lax.broadcasted_iota(jnp.int32, (2 * tq, tq), 0) % tq
    step(kf_ref[...], vf_ref[...], col >= FRONT - N_META)

    def interior(j, carry):
        off = pl.multiple_of(j * tq, tq)
        step(kx_ref[pl.ds(off, tq), :], vx_ref[pl.ds(off, tq), :], None)
        return carry

    lax.fori_loop(0, qi, interior, 0)
    off = pl.multiple_of(qi * tq, tq)
    step(kx_ref[pl.ds(off, tq), :], vx_ref[pl.ds(off, tq), :], col <= row)

    lam = (jnp.exp(jnp.sum(lq1_ref[...] * lk1_ref[...], axis=-1, keepdims=True))
           - jnp.exp(jnp.sum(lq2_ref[...] * lk2_ref[...], axis=-1, keepdims=True)) + lam_init)
    acc = acc_sc[...]
    inv_l = 1.0 / l_sc[...]
    o = acc[:tq] * inv_l[:tq] - lam * (acc[tq:] * inv_l[tq:])
    ms = jnp.mean(o * o, axis=-1, keepdims=True)
    o = o * lax.rsqrt(ms + EPS) * gsub_ref[...] * (1.0 - lam_init)
    o_ref[...] = o.astype(o_ref.dtype)


def _diff_attn(q, kf, vf, kx, vx, lam_vecs, gsub, lam_init):
    bsz, seq, _ = q.shape
    tq = ATT_TILE
    vec = _const_spec((1, DA_HEAD_DIM))
    return pl.pallas_call(
        functools.partial(_diff_attn_body, lam_init=lam_init),
        grid=(bsz, DA_HEADS, seq // tq),
        in_specs=[vec, vec, vec, vec, _const_spec((1, DA_V_DIM)),
                  pl.BlockSpec((None, tq, LANES), lambda b, h, i: (b, i, h)),
                  pl.BlockSpec((FRONT, LANES), lambda b, h, i: (0, h)),
                  pl.BlockSpec((FRONT, LANES), lambda b, h, i: (0, h)),
                  pl.BlockSpec((None, seq, LANES), lambda b, h, i: (b, 0, h)),
                  pl.BlockSpec((None, seq, LANES), lambda b, h, i: (b, 0, h))],
        out_specs=pl.BlockSpec((None, tq, LANES), lambda b, h, i: (b, i, h)),
        out_shape=jax.ShapeDtypeStruct((bsz, seq, DA_HEADS * DA_V_DIM), BF16),
        scratch_shapes=[pltpu.VMEM((2 * tq, 1), F32), pltpu.VMEM((2 * tq, 1), F32),
                        pltpu.VMEM((2 * tq, DA_V_DIM), F32)],
        compiler_params=_cparams(("parallel", "parallel", "arbitrary")),
        name="diff_attn",
    )(*lam_vecs, gsub, q, kf, vf, kx, vx)


def _split3(a):
    a1 = a.astype(BF16)
    r1 = a - a1.astype(F32)
    a2 = r1.astype(BF16)
    a3 = (r1 - a2.astype(F32)).astype(BF16)
    return a1, a2, a3


def _gla_body(gn_ref, kf_ref, vf_ref, laf_ref, q_ref, k_ref, v_ref, la_ref, sr_ref, o_ref, st_sc):
    g = pl.program_id(1)
    c = GLA_CHUNK
    kw = GLA_HEADS * GLA_DK
    vw = GLA_HEADS * GLA_DV

    ti = lax.broadcasted_iota(jnp.int32, (c, c), 0)
    si = lax.broadcasted_iota(jnp.int32, (c, c), 1)
    tri = (si <= ti).astype(BF16)
    hv = lax.broadcasted_iota(jnp.int32, (vw, kw), 0) // GLA_DV
    hk = lax.broadcasted_iota(jnp.int32, (vw, kw), 1) // GLA_DK
    head_mask = hv == hk
    lane_head = lax.broadcasted_iota(jnp.int32, (1, kw), 1) // GLA_DK
    causal = lax.broadcasted_iota(jnp.int32, (GLA_HEADS * c, c), 0) % c >= \
        lax.broadcasted_iota(jnp.int32, (GLA_HEADS * c, c), 1)

    def cumsum(a):
        a1, a2, a3 = _split3(a)
        return (jnp.dot(tri, a1, preferred_element_type=F32)
                + jnp.dot(tri, a2, preferred_element_type=F32)
                + jnp.dot(tri, a3, preferred_element_type=F32))

    def update_state(k, v, b):
        b_last = b[c - 1:c, :]
        kd = (k * jnp.exp(b_last - b)).astype(BF16)
        upd = lax.dot_general(v, kd, (((0,), (0,)), ((), ())), preferred_element_type=F32)
        st_sc[...] = jnp.where(head_mask, jnp.exp(b_last) * st_sc[...] + upd, 0.0)

    @pl.when(g == 0)
    def _():
        st_sc[...] = jnp.zeros(st_sc.shape, F32)
        update_state(kf_ref[...].astype(F32), vf_ref[...], cumsum(laf_ref[...]))

    gn = gn_ref[...]
    for ci in range(GLA_GROUP // c):
        rs = slice(ci * c, (ci + 1) * c)
        b = cumsum(la_ref[rs, :])
        q = q_ref[rs, :].astype(F32)
        k = k_ref[rs, :].astype(F32)
        v = v_ref[rs, :]
        qe = q * jnp.exp(b)
        ke = (k * jnp.exp(-b)).astype(BF16)
        inter = lax.dot_general(qe.astype(BF16), st_sc[...].astype(BF16), (((1,), (1,)), ((), ())),
                                preferred_element_type=F32)
        qstack = jnp.concatenate([jnp.where(lane_head == hh, qe, 0.0) for hh in range(GLA_HEADS)],
                                 axis=0).astype(BF16)
        sc = lax.dot_general(qstack, ke, (((1,), (1,)), ((), ())), preferred_element_type=F32)
        sc = jnp.where(causal, sc, 0.0).astype(BF16)
        pv = jnp.dot(sc, v, preferred_element_type=F32)
        for hh in range(GLA_HEADS):
            cs = slice(hh * GLA_DV, (hh + 1) * GLA_DV)
            o = inter[:, cs] + pv[hh * c:(hh + 1) * c, cs]
            ms = jnp.mean(o * o, axis=-1, keepdims=True)
            o = o * lax.rsqrt(ms + EPS) * gn * sr_ref[rs, cs].astype(F32)
            o_ref[rs, cs] = o.astype(o_ref.dtype)
        update_state(k, v, b)


def _gla(kf, vf, laf, q, k, v, la, sr, gn):
    bsz, seq, _ = q.shape
    t = GLA_GROUP
    kw = GLA_HEADS * GLA_DK
    vw = GLA_HEADS * GLA_DV
    last = FRONT // GLA_CHUNK - 1
    fr = lambda w: pl.BlockSpec((GLA_CHUNK, w), lambda b, g: (last, 0))
    xs = lambda w: pl.BlockSpec((None, t, w), lambda b, g: (b, g, 0))
    return pl.pallas_call(
        _gla_body,
        grid=(bsz, seq // t),
        in_specs=[_const_spec((1, GLA_DV)), fr(kw), fr(vw), fr(kw), xs(kw), xs(kw), xs(vw), xs(kw), xs(vw)],
        out_specs=xs(vw),
        out_shape=jax.ShapeDtypeStruct((bsz, seq, vw), BF16),
        scratch_shapes=[pltpu.VMEM((vw, kw), F32)],
        compiler_params=_cparams(("parallel", "arbitrary")),
        name="gla",
    )(gn, kf, vf, laf, q, k, v, la, sr)


def _merge_body(oa_ref, ob_ref, gate_ref, u_ref, wb0_ref, wb1_ref, wout_ref, gffn_ref, wr_ref, br_ref,
                u1_out, h2_out, info_out, cnt_out, cnt_sc):
    i = pl.program_id(0)
    tm = ROW_TILE

    @pl.when(i == 0)
    def _():
        cnt_sc[...] = jnp.zeros(cnt_sc.shape, F32)

    ya = jnp.dot(oa_ref[...], wb0_ref[...], preferred_element_type=F32)
    yb = jnp.dot(ob_ref[...], wb1_ref[...], preferred_element_type=F32)
    gate = gate_ref[...].astype(F32)
    merged = gate[:, :D_MODEL] * ya + gate[:, D_MODEL:] * yb
    u1 = u_ref[...] + jnp.dot(merged.astype(BF16), wout_ref[...], preferred_element_type=F32)
    u1_out[...] = u1
    ms = jnp.mean(u1 * u1, axis=-1, keepdims=True)
    h2f = u1 * lax.rsqrt(ms + EPS) * gffn_ref[...]
    _store_row_tiles(h2_out, h2f)
    h2 = h2f.astype(BF16)

    logits = jnp.dot(h2, wr_ref[...], preferred_element_type=F32) + br_ref[...]
    lane = lax.broadcasted_iota(jnp.int32, (tm, LANES), 1)
    is_group = lane < N_GROUPS
    gl = jnp.where(is_group, logits, NEG_BIG)
    gmax = jnp.max(gl, axis=-1, keepdims=True)
    g_idx = jnp.min(jnp.where(gl == gmax, lane, LANES), axis=-1, keepdims=True)
    g_w = 1.0 / jnp.sum(jnp.where(is_group, jnp.exp(gl - gmax), 0.0), axis=-1, keepdims=True)
    lo = N_GROUPS + EXPERTS_PER_GROUP * g_idx
    el = jnp.where((lane >= lo) & (lane < lo + EXPERTS_PER_GROUP), logits, NEG_BIG)
    v1 = jnp.max(el, axis=-1, keepdims=True)
    i1 = jnp.min(jnp.where(el == v1, lane, LANES), axis=-1, keepdims=True)
    el2 = jnp.where(lane == i1, NEG_BIG, el)
    v2 = jnp.max(el2, axis=-1, keepdims=True)
    i2 = jnp.min(jnp.where(el2 == v2, lane, LANES), axis=-1, keepdims=True)
    e21 = jnp.exp(v2 - v1)
    w1 = g_w / (1.0 + e21)
    w2 = w1 * e21
    id1 = i1 - N_GROUPS
    id2 = i2 - N_GROUPS

    onehot = ((lane == id1) | (lane == id2))
    ti = lax.broadcasted_iota(jnp.int32, (tm, tm), 0)
    si = lax.broadcasted_iota(jnp.int32, (tm, tm), 1)
    strict = (si < ti).astype(BF16)
    before = jnp.dot(strict, onehot.astype(BF16), preferred_element_type=F32) + cnt_sc[...]
    r1 = jnp.sum(jnp.where(lane == id1, before, 0.0), axis=-1, keepdims=True)
    r2 = jnp.sum(jnp.where(lane == id2, before, 0.0), axis=-1, keepdims=True)
    cnt = cnt_sc[...] + jnp.sum(onehot.astype(F32), axis=0, keepdims=True)
    cnt_sc[...] = cnt
    cnt_out[...] = jnp.broadcast_to(cnt, cnt_out.shape)

    info = (jnp.where(lane == 0, id1.astype(F32), 0.0) + jnp.where(lane == 1, id2.astype(F32), 0.0)
            + jnp.where(lane == 2, w1, 0.0) + jnp.where(lane == 3, w2, 0.0)
            + jnp.where(lane == 4, r1, 0.0) + jnp.where(lane == 5, r2, 0.0))
    info_out[...] = jnp.transpose(info)[:8, :]


def _merge(oa, ob, gates, u, p):
    n = u.shape[0]
    tm = ROW_TILE
    row = lambda w: pl.BlockSpec((tm, w), lambda i: (i, 0))
    return pl.pallas_call(
        _merge_body,
        grid=(n // tm,),
        in_specs=[row(512), row(512), row(2 * D_MODEL), row(D_MODEL),
                  _const_spec((512, D_MODEL)), _const_spec((512, D_MODEL)), _const_spec((D_MODEL, D_MODEL)),
                  _const_spec((1, D_MODEL)), _const_spec((D_MODEL, LANES)), _const_spec((1, LANES))],
        out_specs=[row(D_MODEL), _row_tile_spec(tm, lambda i: (i, 0)), pl.BlockSpec((8, tm), lambda i: (0, i)),
                   _const_spec((8, LANES))],
        out_shape=[jax.ShapeDtypeStruct((n, D_MODEL), F32), jax.ShapeDtypeStruct((n * RT, LANES), F32),
                   jax.ShapeDtypeStruct((8, n), F32), jax.ShapeDtypeStruct((8, LANES), F32)],
        scratch_shapes=[pltpu.VMEM((1, LANES), F32)],
        compiler_params=_cparams(("arbitrary",)),
        name="merge_router",
    )(oa, ob, gates, u, p['wb0'], p['wb1'], p['wout'], p['gffn'], p['wr'], p['br'])


def _dispatch_body(dest_ref, h2_hbm, xs_init_hbm, xs_hbm, sem):
    del xs_init_hbm
    i = pl.program_id(0)
    base = i * DMA_TILE

    def copies(r):
        src = _token_rows(h2_hbm, base + r)
        return (pltpu.make_async_copy(src, _token_rows(xs_hbm, dest_ref[0, 0, r]), sem),
                pltpu.make_async_copy(src, _token_rows(xs_hbm, dest_ref[0, 1, r]), sem))

    def start(r, carry):
        for cp in copies(r):
            cp.start()
        return carry

    def wait(r, carry):
        for cp in copies(r):
            cp.wait()
        return carry

    lax.fori_loop(0, DMA_TILE, start, 0)
    lax.fori_loop(0, DMA_TILE, wait, 0)


def _dispatch(dest, h2, xs_init):
    n = h2.shape[0] // RT
    return pl.pallas_call(
        _dispatch_body,
        grid=(n // DMA_TILE,),
        in_specs=[pl.BlockSpec((1, 2, DMA_TILE), lambda i: (i, 0, 0), memory_space=pltpu.SMEM),
                  pl.BlockSpec(memory_space=pl.ANY), pl.BlockSpec(memory_space=pl.ANY)],
        out_specs=pl.BlockSpec(memory_space=pl.ANY),
        out_shape=jax.ShapeDtypeStruct(xs_init.shape, xs_init.dtype),
        scratch_shapes=[pltpu.SemaphoreType.DMA(())],
        input_output_aliases={2: 0},
        compiler_params=_cparams(("arbitrary",)),
        name="dispatch",
    )(dest, h2, xs_init)


def _experts_body(be_ref, nb_ref, x_ref, wg_ref, wu_ref, wd_ref, y_ref, wgu_sc, wd_sc):
    i = pl.program_id(0)
    prev = be_ref[jnp.maximum(i - 1, 0)]
    fresh = (i == 0) | (be_ref[i] != prev)

    @pl.when(fresh)
    def _():
        wgu_sc[:, :D_EXPERT] = wg_ref[...].astype(BF16)
        wgu_sc[:, D_EXPERT:] = wu_ref[...].astype(BF16)
        wd_sc[...] = wd_ref[...].astype(BF16)

    @pl.when(i < nb_ref[0])
    def _():
        x = _load_row_tiles(x_ref, MOE_TILE).astype(BF16)
        gu = jnp.dot(x, wgu_sc[...], preferred_element_type=F32)
        gp = gu[:, :D_EXPERT]
        hid = gp * _sigmoid(gp) * gu[:, D_EXPERT:]
        _store_row_tiles(y_ref, jnp.dot(hid.astype(BF16), wd_sc[...], preferred_element_type=F32))

    @pl.when(i >= nb_ref[0])
    def _():
        y_ref[...] = jnp.zeros(y_ref.shape, F32)


def _experts(block_e, n_used, xs, wg, wu, wd):
    n_slots = xs.shape[0] // RT
    n_blocks = n_slots // MOE_TILE
    xmap = lambda i, be, nb: (jnp.minimum(i, nb[0] - 1), 0)
    wmap = lambda i, be, nb: (be[i], 0, 0)
    grid_spec = pltpu.PrefetchScalarGridSpec(
        num_scalar_prefetch=2,
        grid=(n_blocks,),
        in_specs=[_row_tile_spec(MOE_TILE, xmap),
                  pl.BlockSpec((None, D_MODEL, D_EXPERT), wmap),
                  pl.BlockSpec((None, D_MODEL, D_EXPERT), wmap),
                  pl.BlockSpec((None, D_EXPERT, D_MODEL), wmap)],
        out_specs=_row_tile_spec(MOE_TILE, lambda i, be, nb: (i, 0)),
        scratch_shapes=[pltpu.VMEM((D_MODEL, 2 * D_EXPERT), BF16), pltpu.VMEM((D_EXPERT, D_MODEL), BF16)],
    )
    return pl.pallas_call(
        _experts_body,
        grid_spec=grid_spec,
        out_shape=jax.ShapeDtypeStruct((n_slots * RT, LANES), F32),
        compiler_params=_cparams(("arbitrary",)),
        name="experts",
    )(block_e, n_used, xs, wg, wu, wd)


def _combine_body(dest_ref, w_ref, u1_ref, ys_hbm, o_ref, ybuf, sem):
    def copies(r):
        return (pltpu.make_async_copy(_token_rows(ys_hbm, dest_ref[0, 0, r]), _token_rows(ybuf, r), sem),
                pltpu.make_async_copy(_token_rows(ys_hbm, dest_ref[0, 1, r]), _token_rows(ybuf, DMA_TILE + r), sem))

    def start(r, carry):
        for cp in copies(r):
            cp.start()
        return carry

    def wait(r, carry):
        for cp in copies(r):
            cp.wait()
        return carry

    lax.fori_loop(0, DMA_TILE, start, 0)
    lax.fori_loop(0, DMA_TILE, wait, 0)
    w = w_ref[...]
    t = DMA_TILE
    o_ref[...] = (u1_ref[...] + w[:, 0:1] * _load_row_tiles(ybuf, t) + w[:, 1:2] * _load_row_tiles(ybuf, t, t))


def _combine(dest, w, u1, ys):
    n = u1.shape[0]
    t = DMA_TILE
    return pl.pallas_call(
        _combine_body,
        grid=(n // t,),
        in_specs=[pl.BlockSpec((1, 2, t), lambda i: (i, 0, 0), memory_space=pltpu.SMEM),
                  pl.BlockSpec((t, 2), lambda i: (i, 0)),
                  pl.BlockSpec((t, D_MODEL), lambda i: (i, 0)),
                  pl.BlockSpec(memory_space=pl.ANY)],
        out_specs=pl.BlockSpec((t, D_MODEL), lambda i: (i, 0)),
        out_shape=jax.ShapeDtypeStruct((n, D_MODEL), F32),
        scratch_shapes=[pltpu.VMEM((2 * t * RT, LANES), F32), pltpu.SemaphoreType.DMA(())],
        compiler_params=_cparams(("arbitrary",)),
        name="combine",
    )(dest, w, u1, ys)


def _rope_tables(pos):
    half = DA_HEAD_DIM // 2
    inv_freq = jnp.power(ROPE_THETA, -jnp.arange(half, dtype=F32) * 2.0 / DA_HEAD_DIM)
    ang = pos[:, None] * inv_freq[None, :]
    cos, sin = jnp.cos(ang), jnp.sin(ang)
    cos_t = jnp.tile(cos, (1, LANES // half))
    sin_t = jnp.tile(jnp.concatenate([-sin, sin], axis=1), (1, LANES // DA_HEAD_DIM))
    return cos_t, sin_t


def _layer(x, meta_tokens, l, g_mix_norm, w_in, g_q_norm, g_k_norm, lambda_q1, lambda_k1, lambda_q2, lambda_k2,
           g_diff_subln, w_gla_gate_up, b_gla_gate, g_gla_norm, w_branch, b_merge_gate, w_out, g_ffn_norm,
           w_router_group, b_router_group, w_router_expert, b_router_expert, w_exp_gate, w_exp_up, w_exp_down):
    bsz, seq, _ = x.shape
    n = bsz * seq

    wi = w_in[l]
    offs = [0]
    for s in (512, 512, 512, 256, 256, 512, 512, GLA_RANK, 2 * D_MODEL):
        offs.append(offs[-1] + s)
    sec = lambda j: wi[:, offs[j]:offs[j + 1]].astype(BF16)
    p = {
        'gmix': g_mix_norm[l][None, :],
        'gqn': jnp.tile(g_q_norm[l], LANES // DA_HEAD_DIM)[None, :],
        'gkn': jnp.tile(g_k_norm[l], LANES // DA_HEAD_DIM)[None, :],
        'wq': sec(0), 'wk': sec(1), 'wv': sec(2), 'wgq': sec(3), 'wgk': sec(4), 'wgv': sec(5), 'wgr': sec(6),
        'wgg': jnp.pad(sec(7), ((0, 0), (0, LANES - GLA_RANK))),
        'wup': jnp.pad(w_gla_gate_up[l].astype(BF16), ((0, LANES - GLA_RANK), (0, 0))),
        'bup': b_gla_gate[l][None, :],
        'wgate': sec(8),
        'bgate': b_merge_gate[l].reshape(1, 2 * D_MODEL),
        'wb0': w_branch[l, 0].astype(BF16), 'wb1': w_branch[l, 1].astype(BF16),
        'wout': w_out[l].astype(BF16),
        'gffn': g_ffn_norm[l][None, :],
        'wr': jnp.pad(jnp.concatenate([w_router_group[l], w_router_expert[l].reshape(D_MODEL, N_EXPERTS)],
                                      axis=1).astype(BF16), ((0, 0), (0, LANES - N_GROUPS - N_EXPERTS))),
        'br': jnp.pad(jnp.concatenate([b_router_group[l], b_router_expert[l].reshape(N_EXPERTS)]),
                      (0, LANES - N_GROUPS - N_EXPERTS))[None, :],
    }

    u_front = jnp.concatenate([jnp.zeros((FRONT - N_META, D_MODEL), F32), meta_tokens.astype(F32)], axis=0)
    cos_f, sin_f = _rope_tables(jnp.arange(FRONT, dtype=F32) - (FRONT - N_META))
    cos_x, sin_x = _rope_tables(jnp.arange(seq, dtype=F32) + N_META)
    front = _inproj(u_front, FRONT, cos_f, sin_f, p)
    xin = _inproj(x.reshape(n, D_MODEL), ROW_TILE, cos_x, sin_x, p)
    q, k, v, gq, gk, gv, sr, la, gates = [a.reshape(bsz, seq, a.shape[-1]) for a in xin]
    _, kf, vf, _, gkf, gvf, _, laf, _ = front

    lam_init = 0.8 - 0.6 * math.exp(-0.3 * l)
    lam_vecs = [a[l][None, :] for a in (lambda_q1, lambda_k1, lambda_q2, lambda_k2)]
    o_a = _diff_attn(q, kf, vf, k, v, lam_vecs, g_diff_subln[l][None, :], lam_init)
    o_b = _gla(gkf, gvf, laf, gq, gk, gv, la, sr, g_gla_norm[l][None, :])

    u1, h2, info, cnt = _merge(o_a.reshape(n, -1), o_b.reshape(n, -1), gates.reshape(n, -1),
                               x.reshape(n, D_MODEL), p)

    ids = info[0:2].astype(jnp.int32)
    wts = info[2:4]
    rank = info[4:6].astype(jnp.int32)
    counts = cnt[0, :N_EXPERTS].astype(jnp.int32)
    padded = (counts + MOE_TILE - 1) // MOE_TILE * MOE_TILE
    pends = jnp.cumsum(padded)
    pstarts = pends - padded
    dest = pstarts[ids] + rank
    n_slots = (2 * n // MOE_TILE + N_EXPERTS) * MOE_TILE
    n_blocks = n_slots // MOE_TILE
    n_used = (pends[-1] // MOE_TILE).astype(jnp.int32)
    blk = jnp.minimum(jnp.arange(n_blocks, dtype=jnp.int32), n_used - 1) * MOE_TILE
    block_e = jnp.minimum(jnp.searchsorted(pends, blk, side='right'), N_EXPERTS - 1).astype(jnp.int32)
    dest_t = dest.reshape(2, n // DMA_TILE, DMA_TILE).transpose(1, 0, 2)

    xs = _dispatch(dest_t, h2, jnp.zeros((n_slots * RT, LANES), F32))
    ys = _experts(block_e, n_used[None], xs, w_exp_gate[l], w_exp_up[l], w_exp_down[l])
    out = _combine(dest_t, wts.T, u1, ys)
    return out.reshape(bsz, seq, D_MODEL)


def kernel(x, meta_tokens, g_mix_norm, w_in, g_q_norm, g_k_norm, lambda_q1, lambda_k1, lambda_q2, lambda_k2,
           g_diff_subln, w_gla_gate_up, b_gla_gate, g_gla_norm, w_branch, b_merge_gate, w_out, g_ffn_norm,
           w_router_group, b_router_group, w_router_expert, b_router_expert, w_exp_gate, w_exp_up, w_exp_down):
    depth = w_in.shape[0]
    assert depth == 1, "meta tokens are only carried through a single layer in this implementation"
    assert x.shape[1] % ROW_TILE == 0 and x.shape[2] == D_MODEL
    return _layer(x, meta_tokens, 0, g_mix_norm, w_in, g_q_norm, g_k_norm, lambda_q1, lambda_k1, lambda_q2,
                  lambda_k2, g_diff_subln, w_gla_gate_up, b_gla_gate, g_gla_norm, w_branch, b_merge_gate, w_out,
                  g_ffn_norm, w_router_group, b_router_group, w_router_expert, b_router_expert,
                  w_exp_gate, w_exp_up, w_exp_down)
```

```python
import functools
import math

import jax
import jax.numpy as jnp
from jax import lax
from jax.experimental import pallas as pl
from jax.experimental.pallas import tpu as pltpu

F32 = jnp.float32
BF16 = jnp.bfloat16

D_MODEL = 1024
N_META = 16
EPS = 1e-6
ROPE_THETA = 10000.0

DA_HEADS = 4
DA_HEAD_DIM = 64
DA_V_DIM = 128
GLA_HEADS = 4
GLA_DK = 64
GLA_DV = 128
GLA_RANK = 16
GLA_TAU = 16.0
GLA_CHUNK = 64
N_GROUPS = 4
EXPERTS_PER_GROUP = 8
N_EXPERTS = 32
D_EXPERT = 512

LANES = 128
FRONT = 256
ATT_TILE = 256
ROW_TILE = 512
GLA_GROUP = 512
MOE_TILE = 256
DMA_TILE = 256
NEG_BIG = -1e30
VMEM_LIMIT = 56 * 1024 * 1024


def _cparams(sem):
    return pltpu.CompilerParams(dimension_semantics=sem, vmem_limit_bytes=VMEM_LIMIT)


def _const_spec(shape):
    nd = len(shape)
    return pl.BlockSpec(shape, lambda *_: (0,) * nd)


RT = D_MODEL // LANES


def _row_tile_spec(rows, index_map):
    return pl.BlockSpec((rows * RT, LANES), index_map)


def _token_rows(ref, tok):
    return ref.at[pl.ds(pl.multiple_of(tok * RT, RT), RT)]


def _load_row_tiles(ref, rows, first=0):
    return jnp.concatenate([ref[pl.ds(first * RT + c, rows, stride=RT), :] for c in range(RT)], axis=1)


def _store_row_tiles(ref, val):
    for c in range(RT):
        ref[pl.ds(c, val.shape[0], stride=RT), :] = val[:, c * LANES:(c + 1) * LANES]


def _sigmoid(x):
    return 1.0 / (1.0 + jnp.exp(-x))


def _log_sigmoid(x):
    return jnp.minimum(x, 0.0) - jnp.log1p(jnp.exp(-jnp.abs(x)))


def _inproj_body(u_ref, gmix_ref, cos_ref, sin_ref, gqn_ref, gkn_ref,
                 wq_ref, wk_ref, wv_ref, wgq_ref, wgk_ref, wgv_ref, wgr_ref, wgg_ref,
                 wup_ref, bup_ref, wgate_ref, bgate_ref,
                 q_out, k_out, v_out, gq_out, gk_out, gv_out, sr_out, la_out, gate_out):
    x = u_ref[...]
    ms = jnp.mean(x * x, axis=-1, keepdims=True)
    h = (x * lax.rsqrt(ms + EPS) * gmix_ref[...]).astype(BF16)

    cos = cos_ref[...]
    sin = sin_ref[...]
    lane = lax.broadcasted_iota(jnp.int32, (1, LANES), 1)
    first_half = (lane % DA_HEAD_DIM) < (DA_HEAD_DIM // 2)
    gi = lax.broadcasted_iota(jnp.int32, (LANES, LANES), 0) // DA_HEAD_DIM
    gj = lax.broadcasted_iota(jnp.int32, (LANES, LANES), 1) // DA_HEAD_DIM
    group_sum = (gi == gj).astype(BF16)

    def norm_rope(w_ref, gain_ref, out_ref, scale):
        z = jnp.dot(h, w_ref[...], preferred_element_type=F32)
        for hh in range(DA_HEADS):
            zh = z[:, hh * LANES:(hh + 1) * LANES]
            ssq = jnp.dot((zh * zh).astype(BF16), group_sum, preferred_element_type=F32)
            zn = zh * lax.rsqrt(ssq * (1.0 / DA_HEAD_DIM) + EPS) * gain_ref[...]
            rot = jnp.where(first_half,
                            pltpu.roll(zn, LANES - DA_HEAD_DIM // 2, 1),
                            pltpu.roll(zn, DA_HEAD_DIM // 2, 1))
            zr = zn * cos + rot * sin
            out_ref[:, hh * LANES:(hh + 1) * LANES] = (zr * scale).astype(out_ref.dtype)

    norm_rope(wq_ref, gqn_ref, q_out, DA_HEAD_DIM ** -0.5 * math.log2(math.e))
    norm_rope(wk_ref, gkn_ref, k_out, 1.0)
    v_out[...] = jnp.dot(h, wv_ref[...], preferred_element_type=F32).astype(v_out.dtype)

    gq_out[...] = (jnp.dot(h, wgq_ref[...], preferred_element_type=F32) * (GLA_DK ** -0.5)).astype(gq_out.dtype)
    gk_out[...] = jnp.dot(h, wgk_ref[...], preferred_element_type=F32).astype(gk_out.dtype)
    gv_out[...] = jnp.dot(h, wgv_ref[...], preferred_element_type=F32).astype(gv_out.dtype)
    r = jnp.dot(h, wgr_ref[...], preferred_element_type=F32)
    sr_out[...] = (r * _sigmoid(r)).astype(sr_out.dtype)

    g_lr = jnp.dot(h, wgg_ref[...], preferred_element_type=F32)
    pre = jnp.dot(g_lr.astype(BF16), wup_ref[...], preferred_element_type=F32) + bup_ref[...]
    la_out[...] = _log_sigmoid(pre) * (1.0 / GLA_TAU)

    gl = jnp.dot(h, wgate_ref[...], preferred_element_type=F32) + bgate_ref[...]
    gate_out[...] = _sigmoid(gl).astype(gate_out.dtype)


def _inproj(u, tm, cos, sin, p):
    rows = u.shape[0]
    n_tab = cos.shape[0] // tm
    row = lambda w: pl.BlockSpec((tm, w), lambda i: (i, 0))
    tab = pl.BlockSpec((tm, LANES), lambda i: (i % n_tab, 0))
    weights = [p['wq'], p['wk'], p['wv'], p['wgq'], p['wgk'], p['wgv'], p['wgr'], p['wgg'],
               p['wup'], p['bup'], p['wgate'], p['bgate']]
    out_widths = [(512, BF16), (512, BF16), (512, BF16), (256, BF16), (256, BF16), (512, BF16),
                  (512, BF16), (256, F32), (2048, BF16)]
    return pl.pallas_call(
        _inproj_body,
        grid=(rows // tm,),
        in_specs=[row(D_MODEL), _const_spec((1, D_MODEL)), tab, tab,
                  _const_spec((1, LANES)), _const_spec((1, LANES))]
                 + [_const_spec(w.shape) for w in weights],
        out_specs=[row(w) for w, _ in out_widths],
        out_shape=[jax.ShapeDtypeStruct((rows, w), dt) for w, dt in out_widths],
        compiler_params=_cparams(("parallel",)),
        name="inproj",
    )(u, p['gmix'], cos, sin, p['gqn'], p['gkn'], *weights)


def _diff_attn_body(lq1_ref, lk1_ref, lq2_ref, lk2_ref, gsub_ref,
                    q_ref, kf_ref, vf_ref, kx_ref, vx_ref, o_ref, vt_sc, qs_sc, m_sc, l_sc, acc_sc,
                    *, lam_init):
    qi = pl.program_id(1)
    tq = ATT_TILE
    n_kv = kx_ref.shape[0] // tq
    heads = range(DA_HEADS)
    hs = lambda h: slice(h * LANES, (h + 1) * LANES)

    @pl.when(qi == 0)
    def _():
        for h in heads:
            vt_sc[h, 0] = jnp.transpose(vf_ref[:, hs(h)].astype(F32)).astype(BF16)

        def tr(j, carry):
            off = pl.multiple_of(j * tq, tq)
            for h in heads:
                vt_sc[h, j + 1] = jnp.transpose(vx_ref[pl.ds(off, tq), hs(h)].astype(F32)).astype(BF16)
            return carry

        lax.fori_loop(0, n_kv, tr, 0)

    d = lax.broadcasted_iota(jnp.int32, (LANES, 1), 0)
    for h in heads:
        qt = jnp.transpose(q_ref[:, hs(h)].astype(F32))
        qs_sc[h] = jnp.concatenate([jnp.where(d < DA_HEAD_DIM, qt, 0.0),
                                    jnp.where(d >= DA_HEAD_DIM, qt, 0.0)], axis=1).astype(BF16)

    m_sc[...] = jnp.full(m_sc.shape, NEG_BIG, F32)
    l_sc[...] = jnp.zeros(l_sc.shape, F32)
    acc_sc[...] = jnp.zeros(acc_sc.shape, F32)

    def step(k_of, vt_of, mask):
        for h in heads:
            s = jnp.dot(k_of(h), qs_sc[h], preferred_element_type=F32)
            if mask is not None:
                s = jnp.where(mask, s, NEG_BIG)
            m_old = m_sc[h]
            m_new = jnp.maximum(m_old, jnp.max(s, axis=0, keepdims=True))
            alpha = jnp.exp2(m_old - m_new)
            pr = jnp.exp2(s - m_new)
            l_sc[h] = alpha * l_sc[h] + jnp.sum(pr, axis=0, keepdims=True)
            acc_sc[h] = alpha * acc_sc[h] + jnp.dot(vt_of(h), pr.astype(BF16), preferred_element_type=F32)
            m_sc[h] = m_new

    key = lax.broadcasted_iota(jnp.int32, (tq, 2 * tq), 0)
    qry = lax.broadcasted_iota(jnp.int32, (tq, 2 * tq), 1) % tq
    step(lambda h: kf_ref[:, hs(h)], lambda h: vt_sc[h, 0], key >= FRONT - N_META)

    def interior(j, carry):
        off = pl.multiple_of(j * tq, tq)
        step(lambda h: kx_ref[pl.ds(off, tq), hs(h)], lambda h: vt_sc[h, j + 1], None)
        return carry

    lax.fori_loop(0, qi, interior, 0)
    off = pl.multiple_of(qi * tq, tq)
    step(lambda h: kx_ref[pl.ds(off, tq), hs(h)], lambda h: vt_sc[h, qi + 1], key <= qry)

    lam = (jnp.exp(jnp.sum(lq1_ref[...] * lk1_ref[...], axis=-1, keepdims=True))
           - jnp.exp(jnp.sum(lq2_ref[...] * lk2_ref[...], axis=-1, keepdims=True)) + lam_init)
    for h in heads:
        acc = acc_sc[h]
        inv_l = 1.0 / l_sc[h]
        ot = acc[:, :tq] * inv_l[:, :tq] - lam * (acc[:, tq:] * inv_l[:, tq:])
        ms = jnp.mean(ot * ot, axis=0, keepdims=True)
        o = jnp.transpose(ot * lax.rsqrt(ms + EPS)) * gsub_ref[...] * (1.0 - lam_init)
        o_ref[:, hs(h)] = o.astype(o_ref.dtype)


def _diff_attn(q, kf, vf, kx, vx, lam_vecs, gsub, lam_init):
    bsz, seq, _ = q.shape
    tq = ATT_TILE
    assert FRONT == tq, "the front block is handled as one key tile"
    vec = _const_spec((1, DA_HEAD_DIM))
    width = DA_HEADS * LANES
    return pl.pallas_call(
        functools.partial(_diff_attn_body, lam_init=lam_init),
        grid=(bsz, seq // tq),
        in_specs=[vec, vec, vec, vec, _const_spec((1, DA_V_DIM)),
                  pl.BlockSpec((None, tq, width), lambda b, i: (b, i, 0)),
                  _const_spec((FRONT, width)), _const_spec((FRONT, width)),
                  pl.BlockSpec((None, seq, width), lambda b, i: (b, 0, 0)),
                  pl.BlockSpec((None, seq, width), lambda b, i: (b, 0, 0))],
        out_specs=pl.BlockSpec((None, tq, width), lambda b, i: (b, i, 0)),
        out_shape=jax.ShapeDtypeStruct((bsz, seq, width), BF16),
        scratch_shapes=[pltpu.VMEM((DA_HEADS, seq // tq + 1, DA_V_DIM, tq), BF16),
                        pltpu.VMEM((DA_HEADS, LANES, 2 * tq), BF16),
                        pltpu.VMEM((DA_HEADS, 1, 2 * tq), F32), pltpu.VMEM((DA_HEADS, 1, 2 * tq), F32),
                        pltpu.VMEM((DA_HEADS, DA_V_DIM, 2 * tq), F32)],
        compiler_params=_cparams(("parallel", "arbitrary")),
        name="diff_attn",
    )(*lam_vecs, gsub, q, kf, vf, kx, vx)


def _split3(a):
    a1 = a.astype(BF16)
    r1 = a - a1.astype(F32)
    a2 = r1.astype(BF16)
    a3 = (r1 - a2.astype(F32)).astype(BF16)
    return a1, a2, a3


def _gla_body(gn_ref, kf_ref, vf_ref, laf_ref, q_ref, k_ref, v_ref, la_ref, sr_ref, o_ref, st_sc):
    g = pl.program_id(1)
    c = GLA_CHUNK
    kw = GLA_HEADS * GLA_DK
    vw = GLA_HEADS * GLA_DV

    ti = lax.broadcasted_iota(jnp.int32, (c, c), 0)
    si = lax.broadcasted_iota(jnp.int32, (c, c), 1)
    tri = (si <= ti).astype(BF16)
    hv = lax.broadcasted_iota(jnp.int32, (vw, kw), 0) // GLA_DV
    hk = lax.broadcasted_iota(jnp.int32, (vw, kw), 1) // GLA_DK
    head_mask = hv == hk
    lane_head = lax.broadcasted_iota(jnp.int32, (1, kw), 1) // GLA_DK
    causal = lax.broadcasted_iota(jnp.int32, (GLA_HEADS * c, c), 0) % c >= \
        lax.broadcasted_iota(jnp.int32, (GLA_HEADS * c, c), 1)

    def cumsum(a):
        a1, a2, a3 = _split3(a)
        return (jnp.dot(tri, a1, preferred_element_type=F32)
                + jnp.dot(tri, a2, preferred_element_type=F32)
                + jnp.dot(tri, a3, preferred_element_type=F32))

    def update_state(k, v, b):
        b_last = b[c - 1:c, :]
        kd = (k * jnp.exp(b_last - b)).astype(BF16)
        upd = lax.dot_general(v, kd, (((0,), (0,)), ((), ())), preferred_element_type=F32)
        st_sc[...] = jnp.where(head_mask, jnp.exp(b_last) * st_sc[...] + upd, 0.0)

    @pl.when(g == 0)
    def _():
        st_sc[...] = jnp.zeros(st_sc.shape, F32)
        update_state(kf_ref[...].astype(F32), vf_ref[...], cumsum(laf_ref[...]))

    gn = gn_ref[...]
    for ci in range(GLA_GROUP // c):
        rs = slice(ci * c, (ci + 1) * c)
        b = cumsum(la_ref[rs, :])
        q = q_ref[rs, :].astype(F32)
        k = k_ref[rs, :].astype(F32)
        v = v_ref[rs, :]
        qe = q * jnp.exp(b)
        ke = (k * jnp.exp(-b)).astype(BF16)
        inter = lax.dot_general(qe.astype(BF16), st_sc[...].astype(BF16), (((1,), (1,)), ((), ())),
                                preferred_element_type=F32)
        qstack = jnp.concatenate([jnp.where(lane_head == hh, qe, 0.0) for hh in range(GLA_HEADS)],
                                 axis=0).astype(BF16)
        sc = lax.dot_general(qstack, ke, (((1,), (1,)), ((), ())), preferred_element_type=F32)
        sc = jnp.where(causal, sc, 0.0).astype(BF16)
        pv = jnp.dot(sc, v, preferred_element_type=F32)
        for hh in range(GLA_HEADS):
            cs = slice(hh * GLA_DV, (hh + 1) * GLA_DV)
            o = inter[:, cs] + pv[hh * c:(hh + 1) * c, cs]
            ms = jnp.mean(o * o, axis=-1, keepdims=True)
            o = o * lax.rsqrt(ms + EPS) * gn * sr_ref[rs, cs].astype(F32)
            o_ref[rs, cs] = o.astype(o_ref.dtype)
        update_state(k, v, b)


def _gla(kf, vf, laf, q, k, v, la, sr, gn):
    bsz, seq, _ = q.shape
    t = GLA_GROUP
    kw = GLA_HEADS * GLA_DK
    vw = GLA_HEADS * GLA_DV
    last = FRONT // GLA_CHUNK - 1
    fr = lambda w: pl.BlockSpec((GLA_CHUNK, w), lambda b, g: (last, 0))
    xs = lambda w: pl.BlockSpec((None, t, w), lambda b, g: (b, g, 0))
    return pl.pallas_call(
        _gla_body,
        grid=(bsz, seq // t),
        in_specs=[_const_spec((1, GLA_DV)), fr(kw), fr(vw), fr(kw), xs(kw), xs(kw), xs(vw), xs(kw), xs(vw)],
        out_specs=xs(vw),
        out_shape=jax.ShapeDtypeStruct((bsz, seq, vw), BF16),
        scratch_shapes=[pltpu.VMEM((vw, kw), F32)],
        compiler_params=_cparams(("parallel", "arbitrary")),
        name="gla",
    )(gn, kf, vf, laf, q, k, v, la, sr)


def _merge_body(oa_ref, ob_ref, gate_ref, u_ref, wb0_ref, wb1_ref, wout_ref, gffn_ref, wr_ref, br_ref,
                u1_out, h2_out, info_out, cnt_out, cnt_sc):
    i = pl.program_id(0)
    tm = ROW_TILE

    @pl.when(i == 0)
    def _():
        cnt_sc[...] = jnp.zeros(cnt_sc.shape, F32)

    ya = jnp.dot(oa_ref[...], wb0_ref[...], preferred_element_type=F32)
    yb = jnp.dot(ob_ref[...], wb1_ref[...], preferred_element_type=F32)
    gate = gate_ref[...].astype(F32)
    merged = gate[:, :D_MODEL] * ya + gate[:, D_MODEL:] * yb
    u1 = u_ref[...] + jnp.dot(merged.astype(BF16), wout_ref[...], preferred_element_type=F32)
    u1_out[...] = u1
    ms = jnp.mean(u1 * u1, axis=-1, keepdims=True)
    h2f = u1 * lax.rsqrt(ms + EPS) * gffn_ref[...]
    _store_row_tiles(h2_out, h2f)
    h2 = h2f.astype(BF16)

    logits = jnp.dot(h2, wr_ref[...], preferred_element_type=F32) + br_ref[...]
    lane = lax.broadcasted_iota(jnp.int32, (tm, LANES), 1)
    is_group = lane < N_GROUPS
    gl = jnp.where(is_group, logits, NEG_BIG)
    gmax = jnp.max(gl, axis=-1, keepdims=True)
    g_idx = jnp.min(jnp.where(gl == gmax, lane, LANES), axis=-1, keepdims=True)
    g_w = 1.0 / jnp.sum(jnp.where(is_group, jnp.exp(gl - gmax), 0.0), axis=-1, keepdims=True)
    lo = N_GROUPS + EXPERTS_PER_GROUP * g_idx
    el = jnp.where((lane >= lo) & (lane < lo + EXPERTS_PER_GROUP), logits, NEG_BIG)
    v1 = jnp.max(el, axis=-1, keepdims=True)
    i1 = jnp.min(jnp.where(el == v1, lane, LANES), axis=-1, keepdims=True)
    el2 = jnp.where(lane == i1, NEG_BIG, el)
    v2 = jnp.max(el2, axis=-1, keepdims=True)
    i2 = jnp.min(jnp.where(el2 == v2, lane, LANES), axis=-1, keepdims=True)
    e21 = jnp.exp(v2 - v1)
    w1 = g_w / (1.0 + e21)
    w2 = w1 * e21
    id1 = i1 - N_GROUPS
    id2 = i2 - N_GROUPS

    onehot = ((lane == id1) | (lane == id2))
    ti = lax.broadcasted_iota(jnp.int32, (tm, tm), 0)
    si = lax.broadcasted_iota(jnp.int32, (tm, tm), 1)
    strict = (si < ti).astype(BF16)
    before = jnp.dot(strict, onehot.astype(BF16), preferred_element_type=F32) + cnt_sc[...]
    r1 = jnp.sum(jnp.where(lane == id1, before, 0.0), axis=-1, keepdims=True)
    r2 = jnp.sum(jnp.where(lane == id2, before, 0.0), axis=-1, keepdims=True)
    cnt = cnt_sc[...] + jnp.sum(onehot.astype(F32), axis=0, keepdims=True)
    cnt_sc[...] = cnt
    cnt_out[...] = jnp.broadcast_to(cnt, cnt_out.shape)

    info = (jnp.where(lane == 0, id1.astype(F32), 0.0) + jnp.where(lane == 1, id2.astype(F32), 0.0)
            + jnp.where(lane == 2, w1, 0.0) + jnp.where(lane == 3, w2, 0.0)
            + jnp.where(lane == 4, r1, 0.0) + jnp.where(lane == 5, r2, 0.0))
    info_out[...] = jnp.transpose(info)[:8, :]


def _merge(oa, ob, gates, u, p):
    n = u.shape[0]
    tm = ROW_TILE
    row = lambda w: pl.BlockSpec((tm, w), lambda i: (i, 0))
    return pl.pallas_call(
        _merge_body,
        grid=(n // tm,),
        in_specs=[row(512), row(512), row(2 * D_MODEL), row(D_MODEL),
                  _const_spec((512, D_MODEL)), _const_spec((512, D_MODEL)), _const_spec((D_MODEL, D_MODEL)),
                  _const_spec((1, D_MODEL)), _const_spec((D_MODEL, LANES)), _const_spec((1, LANES))],
        out_specs=[row(D_MODEL), _row_tile_spec(tm, lambda i: (i, 0)), pl.BlockSpec((8, tm), lambda i: (0, i)),
                   _const_spec((8, LANES))],
        out_shape=[jax.ShapeDtypeStruct((n, D_MODEL), F32), jax.ShapeDtypeStruct((n * RT, LANES), F32),
                   jax.ShapeDtypeStruct((8, n), F32), jax.ShapeDtypeStruct((8, LANES), F32)],
        scratch_shapes=[pltpu.VMEM((1, LANES), F32)],
        compiler_params=_cparams(("arbitrary",)),
        name="merge_router",
    )(oa, ob, gates, u, p['wb0'], p['wb1'], p['wout'], p['gffn'], p['wr'], p['br'])


def _dispatch_body(tail_ref, nb_ref, dest_ref, h2_ref, xs_hbm, zero_sc, sem, zsem):
    i = pl.program_id(0)
    blk_rows = MOE_TILE * RT
    n_blocks = xs_hbm.shape[0] // blk_rows

    def zero_copy(blk):
        dst = xs_hbm.at[pl.ds(pl.multiple_of(blk * blk_rows, blk_rows), blk_rows)]
        return pltpu.make_async_copy(zero_sc, dst, zsem)

    @pl.when(i == 0)
    def _():
        zero_sc[...] = jnp.zeros(zero_sc.shape, F32)

        def tails(fn):
            def body(e, carry):
                @pl.when(tail_ref[e] >= 0)
                def _():
                    fn(zero_copy(tail_ref[e]))
                return carry
            lax.fori_loop(0, N_EXPERTS, body, 0)

        def unused(fn):
            def body(b, carry):
                fn(zero_copy(b))
                return carry
            lax.fori_loop(nb_ref[0], n_blocks, body, 0)

        tails(lambda cp: cp.start())
        unused(lambda cp: cp.start())
        tails(lambda cp: cp.wait())
        unused(lambda cp: cp.wait())

    def start(r, carry):
        src = _token_rows(h2_ref, r)
        pltpu.make_async_copy(src, _token_rows(xs_hbm, dest_ref[0, 0, r]), sem).start()
        pltpu.make_async_copy(src, _token_rows(xs_hbm, dest_ref[0, 1, r]), sem).start()
        return carry

    lax.fori_loop(0, DMA_TILE, start, 0)
    for _ in range(2):
        pltpu.make_async_copy(h2_ref, xs_hbm.at[pl.ds(0, DMA_TILE * RT)], sem).wait()


def _dispatch(tail_blocks, n_used, dest, h2, n_slots):
    n = h2.shape[0] // RT
    grid_spec = pltpu.PrefetchScalarGridSpec(
        num_scalar_prefetch=2,
        grid=(n // DMA_TILE,),
        in_specs=[pl.BlockSpec((1, 2, DMA_TILE), lambda i, tb, nb: (i, 0, 0), memory_space=pltpu.SMEM),
                  _row_tile_spec(DMA_TILE, lambda i, tb, nb: (i, 0))],
        out_specs=pl.BlockSpec(memory_space=pl.ANY),
        scratch_shapes=[pltpu.VMEM((MOE_TILE * RT, LANES), F32), pltpu.SemaphoreType.DMA(()),
                        pltpu.SemaphoreType.DMA(())],
    )
    return pl.pallas_call(
        _dispatch_body,
        grid_spec=grid_spec,
        out_shape=jax.ShapeDtypeStruct((n_slots * RT, LANES), F32),
        compiler_params=_cparams(("arbitrary",)),
        name="dispatch",
    )(tail_blocks, n_used, dest, h2)


def _experts_body(be_ref, nb_ref, x_ref, wg_ref, wu_ref, wd_ref, y_ref, wgu_sc, wd_sc):
    i = pl.program_id(0)
    prev = be_ref[jnp.maximum(i - 1, 0)]
    fresh = (i == 0) | (be_ref[i] != prev)

    @pl.when(fresh)
    def _():
        wgu_sc[:, :D_EXPERT] = wg_ref[...].astype(BF16)
        wgu_sc[:, D_EXPERT:] = wu_ref[...].astype(BF16)
        wd_sc[...] = wd_ref[...].astype(BF16)

    @pl.when(i < nb_ref[0])
    def _():
        x = _load_row_tiles(x_ref, MOE_TILE).astype(BF16)
        gu = jnp.dot(x, wgu_sc[...], preferred_element_type=F32)
        gp = gu[:, :D_EXPERT]
        hid = gp * _sigmoid(gp) * gu[:, D_EXPERT:]
        _store_row_tiles(y_ref, jnp.dot(hid.astype(BF16), wd_sc[...], preferred_element_type=F32))

    @pl.when(i >= nb_ref[0])
    def _():
        y_ref[...] = jnp.zeros(y_ref.shape, F32)


def _experts(block_e, n_used, xs, wg, wu, wd):
    n_slots = xs.shape[0] // RT
    n_blocks = n_slots // MOE_TILE
    xmap = lambda i, be, nb: (jnp.minimum(i, nb[0] - 1), 0)
    wmap = lambda i, be, nb: (be[i], 0, 0)
    grid_spec = pltpu.PrefetchScalarGridSpec(
        num_scalar_prefetch=2,
        grid=(n_blocks,),
        in_specs=[_row_tile_spec(MOE_TILE, xmap),
                  pl.BlockSpec((None, D_MODEL, D_EXPERT), wmap),
                  pl.BlockSpec((None, D_MODEL, D_EXPERT), wmap),
                  pl.BlockSpec((None, D_EXPERT, D_MODEL), wmap)],
        out_specs=_row_tile_spec(MOE_TILE, lambda i, be, nb: (i, 0)),
        scratch_shapes=[pltpu.VMEM((D_MODEL, 2 * D_EXPERT), BF16), pltpu.VMEM((D_EXPERT, D_MODEL), BF16)],
    )
    return pl.pallas_call(
        _experts_body,
        grid_spec=grid_spec,
        out_shape=jax.ShapeDtypeStruct((n_slots * RT, LANES), F32),
        compiler_params=_cparams(("arbitrary",)),
        name="experts",
    )(block_e, n_used, xs, wg, wu, wd)


def _combine_body(dest_ref, w_ref, u1_ref, ys_hbm, o_ref, ybuf, sem):
    def start(r, carry):
        pltpu.make_async_copy(_token_rows(ys_hbm, dest_ref[0, 0, r]), _token_rows(ybuf, r), sem).start()
        pltpu.make_async_copy(_token_rows(ys_hbm, dest_ref[0, 1, r]), _token_rows(ybuf, DMA_TILE + r), sem).start()
        return carry

    lax.fori_loop(0, DMA_TILE, start, 0)
    pltpu.make_async_copy(ys_hbm.at[pl.ds(0, ybuf.shape[0])], ybuf, sem).wait()
    w = w_ref[...]
    t = DMA_TILE
    o_ref[...] = (u1_ref[...] + w[:, 0:1] * _load_row_tiles(ybuf, t) + w[:, 1:2] * _load_row_tiles(ybuf, t, t))


def _combine(dest, w, u1, ys):
    n = u1.shape[0]
    t = DMA_TILE
    return pl.pallas_call(
        _combine_body,
        grid=(n // t,),
        in_specs=[pl.BlockSpec((1, 2, t), lambda i: (i, 0, 0), memory_space=pltpu.SMEM),
                  pl.BlockSpec((t, 2), lambda i: (i, 0)),
                  pl.BlockSpec((t, D_MODEL), lambda i: (i, 0)),
                  pl.BlockSpec(memory_space=pl.ANY)],
        out_specs=pl.BlockSpec((t, D_MODEL), lambda i: (i, 0)),
        out_shape=jax.ShapeDtypeStruct((n, D_MODEL), F32),
        scratch_shapes=[pltpu.VMEM((2 * t * RT, LANES), F32), pltpu.SemaphoreType.DMA(())],
        compiler_params=_cparams(("arbitrary",)),
        name="combine",
    )(dest, w, u1, ys)


def _rope_tables(pos):
    half = DA_HEAD_DIM // 2
    inv_freq = jnp.power(ROPE_THETA, -jnp.arange(half, dtype=F32) * 2.0 / DA_HEAD_DIM)
    ang = pos[:, None] * inv_freq[None, :]
    cos, sin = jnp.cos(ang), jnp.sin(ang)
    cos_t = jnp.tile(cos, (1, LANES // half))
    sin_t = jnp.tile(jnp.concatenate([-sin, sin], axis=1), (1, LANES // DA_HEAD_DIM))
    return cos_t, sin_t


def _layer(x, meta_tokens, l, g_mix_norm, w_in, g_q_norm, g_k_norm, lambda_q1, lambda_k1, lambda_q2, lambda_k2,
           g_diff_subln, w_gla_gate_up, b_gla_gate, g_gla_norm, w_branch, b_merge_gate, w_out, g_ffn_norm,
           w_router_group, b_router_group, w_router_expert, b_router_expert, w_exp_gate, w_exp_up, w_exp_down):
    bsz, seq, _ = x.shape
    n = bsz * seq

    wi = w_in[l]
    offs = [0]
    for s in (512, 512, 512, 256, 256, 512, 512, GLA_RANK, 2 * D_MODEL):
        offs.append(offs[-1] + s)
    sec = lambda j: wi[:, offs[j]:offs[j + 1]].astype(BF16)
    p = {
        'gmix': g_mix_norm[l][None, :],
        'gqn': jnp.tile(g_q_norm[l], LANES // DA_HEAD_DIM)[None, :],
        'gkn': jnp.tile(g_k_norm[l], LANES // DA_HEAD_DIM)[None, :],
        'wq': sec(0), 'wk': sec(1), 'wv': sec(2), 'wgq': sec(3), 'wgk': sec(4), 'wgv': sec(5), 'wgr': sec(6),
        'wgg': jnp.pad(sec(7), ((0, 0), (0, LANES - GLA_RANK))),
        'wup': jnp.pad(w_gla_gate_up[l].astype(BF16), ((0, LANES - GLA_RANK), (0, 0))),
        'bup': b_gla_gate[l][None, :],
        'wgate': sec(8),
        'bgate': b_merge_gate[l].reshape(1, 2 * D_MODEL),
        'wb0': w_branch[l, 0].astype(BF16), 'wb1': w_branch[l, 1].astype(BF16),
        'wout': w_out[l].astype(BF16),
        'gffn': g_ffn_norm[l][None, :],
        'wr': jnp.pad(jnp.concatenate([w_router_group[l], w_router_expert[l].reshape(D_MODEL, N_EXPERTS)],
                                      axis=1).astype(BF16), ((0, 0), (0, LANES - N_GROUPS - N_EXPERTS))),
        'br': jnp.pad(jnp.concatenate([b_router_group[l], b_router_expert[l].reshape(N_EXPERTS)]),
                      (0, LANES - N_GROUPS - N_EXPERTS))[None, :],
    }

    u_front = jnp.concatenate([jnp.zeros((FRONT - N_META, D_MODEL), F32), meta_tokens.astype(F32)], axis=0)
    cos_f, sin_f = _rope_tables(jnp.arange(FRONT, dtype=F32) - (FRONT - N_META))
    cos_x, sin_x = _rope_tables(jnp.arange(seq, dtype=F32) + N_META)
    front = _inproj(u_front, FRONT, cos_f, sin_f, p)
    xin = _inproj(x.reshape(n, D_MODEL), ROW_TILE, cos_x, sin_x, p)
    q, k, v, gq, gk, gv, sr, la, gates = [a.reshape(bsz, seq, a.shape[-1]) for a in xin]
    _, kf, vf, _, gkf, gvf, _, laf, _ = front

    lam_init = 0.8 - 0.6 * math.exp(-0.3 * l)
    lam_vecs = [a[l][None, :] for a in (lambda_q1, lambda_k1, lambda_q2, lambda_k2)]
    o_a = _diff_attn(q, kf, vf, k, v, lam_vecs, g_diff_subln[l][None, :], lam_init)
    o_b = _gla(gkf, gvf, laf, gq, gk, gv, la, sr, g_gla_norm[l][None, :])

    u1, h2, info, cnt = _merge(o_a.reshape(n, -1), o_b.reshape(n, -1), gates.reshape(n, -1),
                               x.reshape(n, D_MODEL), p)

    ids = info[0:2].astype(jnp.int32)
    wts = info[2:4]
    rank = info[4:6].astype(jnp.int32)
    counts = cnt[0, :N_EXPERTS].astype(jnp.int32)
    padded = (counts + MOE_TILE - 1) // MOE_TILE * MOE_TILE
    pends = jnp.cumsum(padded)
    pstarts = pends - padded
    expert = jnp.arange(N_EXPERTS, dtype=jnp.int32)
    dest = jnp.sum(jnp.where(ids[..., None] == expert, pstarts, 0), axis=-1) + rank
    n_slots = (2 * n // MOE_TILE + N_EXPERTS) * MOE_TILE
    n_blocks = n_slots // MOE_TILE
    n_used = (pends[-1] // MOE_TILE).astype(jnp.int32)
    blk = jnp.minimum(jnp.arange(n_blocks, dtype=jnp.int32), n_used - 1) * MOE_TILE
    block_e = jnp.minimum(jnp.sum(pends[None, :] <= blk[:, None], axis=1), N_EXPERTS - 1).astype(jnp.int32)
    tail_blocks = jnp.where(counts > 0, pends // MOE_TILE - 1, -1).astype(jnp.int32)
    dest_t = dest.reshape(2, n // DMA_TILE, DMA_TILE).transpose(1, 0, 2)

    xs = _dispatch(tail_blocks, n_used[None], dest_t, h2, n_slots)
    ys = _experts(block_e, n_used[None], xs, w_exp_gate[l], w_exp_up[l], w_exp_down[l])
    out = _combine(dest_t, wts.T, u1, ys)
    return out.reshape(bsz, seq, D_MODEL)


def kernel(x, meta_tokens, g_mix_norm, w_in, g_q_norm, g_k_norm, lambda_q1, lambda_k1, lambda_q2, lambda_k2,
           g_diff_subln, w_gla_gate_up, b_gla_gate, g_gla_norm, w_branch, b_merge_gate, w_out, g_ffn_norm,
           w_router_group, b_router_group, w_router_expert, b_router_expert, w_exp_gate, w_exp_up, w_exp_down):
    depth = w_in.shape[0]
    assert depth == 1, "meta tokens are only carried through a single layer in this implementation"
    assert x.shape[1] % ROW_TILE == 0 and x.shape[2] == D_MODEL
    return _layer(x, meta_tokens, 0, g_mix_norm, w_in, g_q_norm, g_k_norm, lambda_q1, lambda_k1, lambda_q2,
                  lambda_k2, g_diff_subln, w_gla_gate_up, b_gla_gate, g_gla_norm, w_branch, b_merge_gate, w_out,
                  g_ffn_norm, w_router_group, b_router_group, w_router_expert, b_router_expert,
                  w_exp_gate, w_exp_up, w_exp_down)
```

```python
import functools
import math

import jax
import jax.numpy as jnp
from jax import lax
from jax.experimental import pallas as pl
from jax.experimental.pallas import tpu as pltpu

F32 = jnp.float32
BF16 = jnp.bfloat16

D_MODEL = 1024
N_META = 16
EPS = 1e-6
ROPE_THETA = 10000.0

DA_HEADS = 4
DA_HEAD_DIM = 64
DA_V_DIM = 128
GLA_HEADS = 4
GLA_DK = 64
GLA_DV = 128
GLA_RANK = 16
GLA_TAU = 16.0
GLA_CHUNK = 64
N_GROUPS = 4
EXPERTS_PER_GROUP = 8
N_EXPERTS = 32
D_EXPERT = 512

LANES = 128
FRONT = 256
ATT_TILE = 256
ROW_TILE = 512
GLA_GROUP = 512
MOE_TILE = 256
DMA_TILE = 256
NEG_BIG = -1e30
VMEM_LIMIT = 56 * 1024 * 1024


def _cparams(sem):
    return pltpu.CompilerParams(dimension_semantics=sem, vmem_limit_bytes=VMEM_LIMIT)


def _const_spec(shape):
    nd = len(shape)
    return pl.BlockSpec(shape, lambda *_: (0,) * nd)


RT = D_MODEL // LANES


def _row_tile_spec(rows, index_map):
    return pl.BlockSpec((rows * RT, LANES), index_map)


def _token_rows(ref, tok):
    return ref.at[pl.ds(pl.multiple_of(tok * RT, RT), RT)]


def _load_row_tiles(ref, rows, first=0):
    return jnp.concatenate([ref[pl.ds(first * RT + c, rows, stride=RT), :] for c in range(RT)], axis=1)


def _store_row_tiles(ref, val):
    for c in range(RT):
        ref[pl.ds(c, val.shape[0], stride=RT), :] = val[:, c * LANES:(c + 1) * LANES]


def _sigmoid(x):
    return 1.0 / (1.0 + jnp.exp(-x))


def _log_sigmoid(x):
    return jnp.minimum(x, 0.0) - jnp.log1p(jnp.exp(-jnp.abs(x)))


def _inproj_body(u_ref, gmix_ref, cos_ref, sin_ref, gqn_ref, gkn_ref,
                 wq_ref, wk_ref, wv_ref, wgq_ref, wgk_ref, wgv_ref, wgr_ref, wgg_ref,
                 wup_ref, bup_ref, wgate_ref, bgate_ref,
                 q_out, k_out, v_out, gq_out, gk_out, gv_out, sr_out, la_out, gate_out):
    x = u_ref[...]
    ms = jnp.mean(x * x, axis=-1, keepdims=True)
    h = (x * lax.rsqrt(ms + EPS) * gmix_ref[...]).astype(BF16)

    cos = cos_ref[...]
    sin = sin_ref[...]
    lane = lax.broadcasted_iota(jnp.int32, (1, LANES), 1)
    first_half = (lane % DA_HEAD_DIM) < (DA_HEAD_DIM // 2)
    gi = lax.broadcasted_iota(jnp.int32, (LANES, LANES), 0) // DA_HEAD_DIM
    gj = lax.broadcasted_iota(jnp.int32, (LANES, LANES), 1) // DA_HEAD_DIM
    group_sum = (gi == gj).astype(BF16)

    def norm_rope(w_ref, gain_ref, out_ref, scale):
        z = jnp.dot(h, w_ref[...], preferred_element_type=F32)
        for hh in range(DA_HEADS):
            zh = z[:, hh * LANES:(hh + 1) * LANES]
            ssq = jnp.dot((zh * zh).astype(BF16), group_sum, preferred_element_type=F32)
            zn = zh * lax.rsqrt(ssq * (1.0 / DA_HEAD_DIM) + EPS) * gain_ref[...]
            rot = jnp.where(first_half,
                            pltpu.roll(zn, LANES - DA_HEAD_DIM // 2, 1),
                            pltpu.roll(zn, DA_HEAD_DIM // 2, 1))
            zr = zn * cos + rot * sin
            out_ref[:, hh * LANES:(hh + 1) * LANES] = (zr * scale).astype(out_ref.dtype)

    norm_rope(wq_ref, gqn_ref, q_out, DA_HEAD_DIM ** -0.5 * math.log2(math.e))
    norm_rope(wk_ref, gkn_ref, k_out, 1.0)
    v_out[...] = jnp.dot(h, wv_ref[...], preferred_element_type=F32).astype(v_out.dtype)

    gq_out[...] = (jnp.dot(h, wgq_ref[...], preferred_element_type=F32) * (GLA_DK ** -0.5)).astype(gq_out.dtype)
    gk_out[...] = jnp.dot(h, wgk_ref[...], preferred_element_type=F32).astype(gk_out.dtype)
    gv_out[...] = jnp.dot(h, wgv_ref[...], preferred_element_type=F32).astype(gv_out.dtype)
    r = jnp.dot(h, wgr_ref[...], preferred_element_type=F32)
    sr_out[...] = (r * _sigmoid(r)).astype(sr_out.dtype)

    g_lr = jnp.dot(h, wgg_ref[...], preferred_element_type=F32)
    pre = jnp.dot(g_lr.astype(BF16), wup_ref[...], preferred_element_type=F32) + bup_ref[...]
    la_out[...] = _log_sigmoid(pre) * (1.0 / GLA_TAU)

    gl = jnp.dot(h, wgate_ref[...], preferred_element_type=F32) + bgate_ref[...]
    gate_out[...] = _sigmoid(gl).astype(gate_out.dtype)


def _inproj(u, tm, cos, sin, p):
    rows = u.shape[0]
    n_tab = cos.shape[0] // tm
    row = lambda w: pl.BlockSpec((tm, w), lambda i: (i, 0))
    tab = pl.BlockSpec((tm, LANES), lambda i: (i % n_tab, 0))
    weights = [p['wq'], p['wk'], p['wv'], p['wgq'], p['wgk'], p['wgv'], p['wgr'], p['wgg'],
               p['wup'], p['bup'], p['wgate'], p['bgate']]
    out_widths = [(512, BF16), (512, BF16), (512, BF16), (256, BF16), (256, BF16), (512, BF16),
                  (512, BF16), (256, F32), (2048, BF16)]
    return pl.pallas_call(
        _inproj_body,
        grid=(rows // tm,),
        in_specs=[row(D_MODEL), _const_spec((1, D_MODEL)), tab, tab,
                  _const_spec((1, LANES)), _const_spec((1, LANES))]
                 + [_const_spec(w.shape) for w in weights],
        out_specs=[row(w) for w, _ in out_widths],
        out_shape=[jax.ShapeDtypeStruct((rows, w), dt) for w, dt in out_widths],
        compiler_params=_cparams(("parallel",)),
        name="inproj",
    )(u, p['gmix'], cos, sin, p['gqn'], p['gkn'], *weights)


def _diff_attn_body(lq1_ref, lk1_ref, lq2_ref, lk2_ref, gsub_ref,
                    q_ref, kf_ref, vf_ref, kx_ref, vx_ref, o_ref, vt_sc, qs_sc, s_sc, cmax_sc, m_sc, l_sc, acc_sc,
                    *, lam_init):
    qi = pl.program_id(1)
    tq = ATT_TILE
    n_kv = kx_ref.shape[0] // tq
    heads = range(DA_HEADS)
    hs = lambda h: slice(h * LANES, (h + 1) * LANES)

    @pl.when(qi == 0)
    def _():
        for h in heads:
            vt_sc[h, 0] = jnp.transpose(vf_ref[:, hs(h)].astype(F32)).astype(BF16)

        def tr(j, carry):
            off = pl.multiple_of(j * tq, tq)
            for h in heads:
                vt_sc[h, j + 1] = jnp.transpose(vx_ref[pl.ds(off, tq), hs(h)].astype(F32)).astype(BF16)
            return carry

        lax.fori_loop(0, n_kv, tr, 0)

    d = lax.broadcasted_iota(jnp.int32, (LANES, 1), 0)
    for h in heads:
        qt = jnp.transpose(q_ref[:, hs(h)].astype(F32))
        qs_sc[h] = jnp.concatenate([jnp.where(d < DA_HEAD_DIM, qt, 0.0),
                                    jnp.where(d >= DA_HEAD_DIM, qt, 0.0)], axis=1).astype(BF16)

    m_sc[...] = jnp.full(m_sc.shape, NEG_BIG, F32)
    l_sc[...] = jnp.zeros(l_sc.shape, F32)
    acc_sc[...] = jnp.zeros(acc_sc.shape, F32)

    def scores(k_of, slot, mask):
        for h in heads:
            s = jnp.dot(k_of(h), qs_sc[h], preferred_element_type=F32)
            if mask is not None:
                s = jnp.where(mask, s, NEG_BIG)
            s_sc[h, slot] = s
            cmax_sc[h, slot] = jnp.max(s, axis=0, keepdims=True)

    def accumulate(t, slot):
        for h in heads:
            m_old = m_sc[h]
            m_new = jnp.maximum(m_old, cmax_sc[h, slot])
            alpha = jnp.exp2(m_old - m_new)
            pr = jnp.exp2(s_sc[h, slot] - m_new)
            l_sc[h] = alpha * l_sc[h] + jnp.sum(pr, axis=0, keepdims=True)
            acc_sc[h] = alpha * acc_sc[h] + jnp.dot(vt_sc[h, t], pr.astype(BF16), preferred_element_type=F32)
            m_sc[h] = m_new

    def x_block(j):
        off = pl.multiple_of(j * tq, tq)
        return lambda h: kx_ref[pl.ds(off, tq), hs(h)]

    key = lax.broadcasted_iota(jnp.int32, (tq, 2 * tq), 0)
    qry = lax.broadcasted_iota(jnp.int32, (tq, 2 * tq), 1) % tq
    scores(lambda h: kf_ref[:, hs(h)], 0, key >= FRONT - N_META)

    def pair(u, carry):
        t = 2 * u
        scores(x_block(t), 1, None)
        accumulate(t, 0)
        scores(x_block(t + 1), 0, None)
        accumulate(t + 1, 1)
        return carry

    lax.fori_loop(0, qi // 2, pair, 0)
    diag = key <= qry

    @pl.when(qi % 2 == 0)
    def _():
        scores(x_block(qi), 1, diag)
        accumulate(qi, 0)
        accumulate(qi + 1, 1)

    @pl.when(qi % 2 == 1)
    def _():
        scores(x_block(qi - 1), 1, None)
        accumulate(qi - 1, 0)
        scores(x_block(qi), 0, diag)
        accumulate(qi, 1)
        accumulate(qi + 1, 0)

    lam = (jnp.exp(jnp.sum(lq1_ref[...] * lk1_ref[...], axis=-1, keepdims=True))
           - jnp.exp(jnp.sum(lq2_ref[...] * lk2_ref[...], axis=-1, keepdims=True)) + lam_init)
    for h in heads:
        acc = acc_sc[h]
        inv_l = 1.0 / l_sc[h]
        ot = acc[:, :tq] * inv_l[:, :tq] - lam * (acc[:, tq:] * inv_l[:, tq:])
        ms = jnp.mean(ot * ot, axis=0, keepdims=True)
        o = jnp.transpose(ot * lax.rsqrt(ms + EPS)) * gsub_ref[...] * (1.0 - lam_init)
        o_ref[:, hs(h)] = o.astype(o_ref.dtype)


def _diff_attn(q, kf, vf, kx, vx, lam_vecs, gsub, lam_init):
    bsz, seq, _ = q.shape
    tq = ATT_TILE
    assert FRONT == tq, "the front block is handled as one key tile"
    vec = _const_spec((1, DA_HEAD_DIM))
    width = DA_HEADS * LANES
    return pl.pallas_call(
        functools.partial(_diff_attn_body, lam_init=lam_init),
        grid=(bsz, seq // tq),
        in_specs=[vec, vec, vec, vec, _const_spec((1, DA_V_DIM)),
                  pl.BlockSpec((None, tq, width), lambda b, i: (b, i, 0)),
                  _const_spec((FRONT, width)), _const_spec((FRONT, width)),
                  pl.BlockSpec((None, seq, width), lambda b, i: (b, 0, 0)),
                  pl.BlockSpec((None, seq, width), lambda b, i: (b, 0, 0))],
        out_specs=pl.BlockSpec((None, tq, width), lambda b, i: (b, i, 0)),
        out_shape=jax.ShapeDtypeStruct((bsz, seq, width), BF16),
        scratch_shapes=[pltpu.VMEM((DA_HEADS, seq // tq + 1, DA_V_DIM, tq), BF16),
                        pltpu.VMEM((DA_HEADS, LANES, 2 * tq), BF16),
                        pltpu.VMEM((DA_HEADS, 2, tq, 2 * tq), F32), pltpu.VMEM((DA_HEADS, 2, 1, 2 * tq), F32),
                        pltpu.VMEM((DA_HEADS, 1, 2 * tq), F32), pltpu.VMEM((DA_HEADS, 1, 2 * tq), F32),
                        pltpu.VMEM((DA_HEADS, DA_V_DIM, 2 * tq), F32)],
        compiler_params=_cparams(("parallel", "arbitrary")),
        name="diff_attn",
    )(*lam_vecs, gsub, q, kf, vf, kx, vx)


def _split3(a):
    a1 = a.astype(BF16)
    r1 = a - a1.astype(F32)
    a2 = r1.astype(BF16)
    a3 = (r1 - a2.astype(F32)).astype(BF16)
    return a1, a2, a3


def _gla_body(gn_ref, kf_ref, vf_ref, laf_ref, q_ref, k_ref, v_ref, la_ref, sr_ref, o_ref, st_sc):
    g = pl.program_id(1)
    c = GLA_CHUNK
    kw = GLA_HEADS * GLA_DK
    vw = GLA_HEADS * GLA_DV

    ti = lax.broadcasted_iota(jnp.int32, (c, c), 0)
    si = lax.broadcasted_iota(jnp.int32, (c, c), 1)
    tri = (si <= ti).astype(BF16)
    hv = lax.broadcasted_iota(jnp.int32, (vw, kw), 0) // GLA_DV
    hk = lax.broadcasted_iota(jnp.int32, (vw, kw), 1) // GLA_DK
    head_mask = hv == hk
    lane_head = lax.broadcasted_iota(jnp.int32, (1, kw), 1) // GLA_DK
    causal = lax.broadcasted_iota(jnp.int32, (GLA_HEADS * c, c), 0) % c >= \
        lax.broadcasted_iota(jnp.int32, (GLA_HEADS * c, c), 1)

    def cumsum(a):
        a1, a2, a3 = _split3(a)
        return (jnp.dot(tri, a1, preferred_element_type=F32)
                + jnp.dot(tri, a2, preferred_element_type=F32)
                + jnp.dot(tri, a3, preferred_element_type=F32))

    def update_state(k, v, b):
        b_last = b[c - 1:c, :]
        kd = (k * jnp.exp(b_last - b)).astype(BF16)
        upd = lax.dot_general(v, kd, (((0,), (0,)), ((), ())), preferred_element_type=F32)
        st_sc[...] = jnp.where(head_mask, jnp.exp(b_last) * st_sc[...] + upd, 0.0)

    @pl.when(g == 0)
    def _():
        st_sc[...] = jnp.zeros(st_sc.shape, F32)
        update_state(kf_ref[...].astype(F32), vf_ref[...], cumsum(laf_ref[...]))

    gn = gn_ref[...]
    for ci in range(GLA_GROUP // c):
        rs = slice(ci * c, (ci + 1) * c)
        b = cumsum(la_ref[rs, :])
        q = q_ref[rs, :].astype(F32)
        k = k_ref[rs, :].astype(F32)
        v = v_ref[rs, :]
        qe = q * jnp.exp(b)
        ke = (k * jnp.exp(-b)).astype(BF16)
        inter = lax.dot_general(qe.astype(BF16), st_sc[...].astype(BF16), (((1,), (1,)), ((), ())),
                                preferred_element_type=F32)
        qstack = jnp.concatenate([jnp.where(lane_head == hh, qe, 0.0) for hh in range(GLA_HEADS)],
                                 axis=0).astype(BF16)
        sc = lax.dot_general(qstack, ke, (((1,), (1,)), ((), ())), preferred_element_type=F32)
        sc = jnp.where(causal, sc, 0.0).astype(BF16)
        pv = jnp.dot(sc, v, preferred_element_type=F32)
        for hh in range(GLA_HEADS):
            cs = slice(hh * GLA_DV, (hh + 1) * GLA_DV)
            o = inter[:, cs] + pv[hh * c:(hh + 1) * c, cs]
            ms = jnp.mean(o * o, axis=-1, keepdims=True)
            o = o * lax.rsqrt(ms + EPS) * gn * sr_ref[rs, cs].astype(F32)
            o_ref[rs, cs] = o.astype(o_ref.dtype)
        update_state(k, v, b)


def _gla(kf, vf, laf, q, k, v, la, sr, gn):
    bsz, seq, _ = q.shape
    t = GLA_GROUP
    kw = GLA_HEADS * GLA_DK
    vw = GLA_HEADS * GLA_DV
    last = FRONT // GLA_CHUNK - 1
    fr = lambda w: pl.BlockSpec((GLA_CHUNK, w), lambda b, g: (last, 0))
    xs = lambda w: pl.BlockSpec((None, t, w), lambda b, g: (b, g, 0))
    return pl.pallas_call(
        _gla_body,
        grid=(bsz, seq // t),
        in_specs=[_const_spec((1, GLA_DV)), fr(kw), fr(vw), fr(kw), xs(kw), xs(kw), xs(vw), xs(kw), xs(vw)],
        out_specs=xs(vw),
        out_shape=jax.ShapeDtypeStruct((bsz, seq, vw), BF16),
        scratch_shapes=[pltpu.VMEM((vw, kw), F32)],
        compiler_params=_cparams(("parallel", "arbitrary")),
        name="gla",
    )(gn, kf, vf, laf, q, k, v, la, sr)


def _merge_body(oa_ref, ob_ref, gate_ref, u_ref, wb0_ref, wb1_ref, wout_ref, gffn_ref, wr_ref, br_ref,
                u1_out, h2_out, info_out, cnt_out, cnt_sc):
    i = pl.program_id(0)
    tm = ROW_TILE

    @pl.when(i == 0)
    def _():
        cnt_sc[...] = jnp.zeros(cnt_sc.shape, F32)

    ya = jnp.dot(oa_ref[...], wb0_ref[...], preferred_element_type=F32)
    yb = jnp.dot(ob_ref[...], wb1_ref[...], preferred_element_type=F32)
    gate = gate_ref[...].astype(F32)
    merged = gate[:, :D_MODEL] * ya + gate[:, D_MODEL:] * yb
    u1 = u_ref[...] + jnp.dot(merged.astype(BF16), wout_ref[...], preferred_element_type=F32)
    u1_out[...] = u1
    ms = jnp.mean(u1 * u1, axis=-1, keepdims=True)
    h2f = u1 * lax.rsqrt(ms + EPS) * gffn_ref[...]
    _store_row_tiles(h2_out, h2f)
    h2 = h2f.astype(BF16)

    logits = jnp.dot(h2, wr_ref[...], preferred_element_type=F32) + br_ref[...]
    lane = lax.broadcasted_iota(jnp.int32, (tm, LANES), 1)
    is_group = lane < N_GROUPS
    gl = jnp.where(is_group, logits, NEG_BIG)
    gmax = jnp.max(gl, axis=-1, keepdims=True)
    g_idx = jnp.min(jnp.where(gl == gmax, lane, LANES), axis=-1, keepdims=True)
    g_w = 1.0 / jnp.sum(jnp.where(is_group, jnp.exp(gl - gmax), 0.0), axis=-1, keepdims=True)
    lo = N_GROUPS + EXPERTS_PER_GROUP * g_idx
    el = jnp.where((lane >= lo) & (lane < lo + EXPERTS_PER_GROUP), logits, NEG_BIG)
    v1 = jnp.max(el, axis=-1, keepdims=True)
    i1 = jnp.min(jnp.where(el == v1, lane, LANES), axis=-1, keepdims=True)
    el2 = jnp.where(lane == i1, NEG_BIG, el)
    v2 = jnp.max(el2, axis=-1, keepdims=True)
    i2 = jnp.min(jnp.where(el2 == v2, lane, LANES), axis=-1, keepdims=True)
    e21 = jnp.exp(v2 - v1)
    w1 = g_w / (1.0 + e21)
    w2 = w1 * e21
    id1 = i1 - N_GROUPS
    id2 = i2 - N_GROUPS

    onehot = ((lane == id1) | (lane == id2))
    ti = lax.broadcasted_iota(jnp.int32, (tm, tm), 0)
    si = lax.broadcasted_iota(jnp.int32, (tm, tm), 1)
    strict = (si < ti).astype(BF16)
    before = jnp.dot(strict, onehot.astype(BF16), preferred_element_type=F32) + cnt_sc[...]
    r1 = jnp.sum(jnp.where(lane == id1, before, 0.0), axis=-1, keepdims=True)
    r2 = jnp.sum(jnp.where(lane == id2, before, 0.0), axis=-1, keepdims=True)
    cnt = cnt_sc[...] + jnp.sum(onehot.astype(F32), axis=0, keepdims=True)
    cnt_sc[...] = cnt
    cnt_out[...] = jnp.broadcast_to(cnt, cnt_out.shape)

    info = (jnp.where(lane == 0, id1.astype(F32), 0.0) + jnp.where(lane == 1, id2.astype(F32), 0.0)
            + jnp.where(lane == 2, w1, 0.0) + jnp.where(lane == 3, w2, 0.0)
            + jnp.where(lane == 4, r1, 0.0) + jnp.where(lane == 5, r2, 0.0))
    info_out[...] = jnp.transpose(info)[:8, :]


def _merge(oa, ob, gates, u, p):
    n = u.shape[0]
    tm = ROW_TILE
    row = lambda w: pl.BlockSpec((tm, w), lambda i: (i, 0))
    return pl.pallas_call(
        _merge_body,
        grid=(n // tm,),
        in_specs=[row(512), row(512), row(2 * D_MODEL), row(D_MODEL),
                  _const_spec((512, D_MODEL)), _const_spec((512, D_MODEL)), _const_spec((D_MODEL, D_MODEL)),
                  _const_spec((1, D_MODEL)), _const_spec((D_MODEL, LANES)), _const_spec((1, LANES))],
        out_specs=[row(D_MODEL), _row_tile_spec(tm, lambda i: (i, 0)), pl.BlockSpec((8, tm), lambda i: (0, i)),
                   _const_spec((8, LANES))],
        out_shape=[jax.ShapeDtypeStruct((n, D_MODEL), F32), jax.ShapeDtypeStruct((n * RT, LANES), F32),
                   jax.ShapeDtypeStruct((8, n), F32), jax.ShapeDtypeStruct((8, LANES), F32)],
        scratch_shapes=[pltpu.VMEM((1, LANES), F32)],
        compiler_params=_cparams(("arbitrary",)),
        name="merge_router",
    )(oa, ob, gates, u, p['wb0'], p['wb1'], p['wout'], p['gffn'], p['wr'], p['br'])


def _dispatch_body(tail_ref, nb_ref, dest_ref, h2_ref, xs_hbm, zero_sc, sem, zsem):
    i = pl.program_id(0)
    blk_rows = MOE_TILE * RT
    n_blocks = xs_hbm.shape[0] // blk_rows

    def zero_copy(blk):
        dst = xs_hbm.at[pl.ds(pl.multiple_of(blk * blk_rows, blk_rows), blk_rows)]
        return pltpu.make_async_copy(zero_sc, dst, zsem)

    @pl.when(i == 0)
    def _():
        zero_sc[...] = jnp.zeros(zero_sc.shape, F32)

        def tails(fn):
            def body(e, carry):
                @pl.when(tail_ref[e] >= 0)
                def _():
                    fn(zero_copy(tail_ref[e]))
                return carry
            lax.fori_loop(0, N_EXPERTS, body, 0)

        def unused(fn):
            def body(b, carry):
                fn(zero_copy(b))
                return carry
            lax.fori_loop(nb_ref[0], n_blocks, body, 0)

        tails(lambda cp: cp.start())
        unused(lambda cp: cp.start())
        tails(lambda cp: cp.wait())
        unused(lambda cp: cp.wait())

    def start(r, carry):
        src = _token_rows(h2_ref, r)
        pltpu.make_async_copy(src, _token_rows(xs_hbm, dest_ref[0, 0, r]), sem).start()
        pltpu.make_async_copy(src, _token_rows(xs_hbm, dest_ref[0, 1, r]), sem).start()
        return carry

    lax.fori_loop(0, DMA_TILE, start, 0)
    for _ in range(2):
        pltpu.make_async_copy(h2_ref, xs_hbm.at[pl.ds(0, DMA_TILE * RT)], sem).wait()


def _dispatch(tail_blocks, n_used, dest, h2, n_slots):
    n = h2.shape[0] // RT
    grid_spec = pltpu.PrefetchScalarGridSpec(
        num_scalar_prefetch=2,
        grid=(n // DMA_TILE,),
        in_specs=[pl.BlockSpec((1, 2, DMA_TILE), lambda i, tb, nb: (i, 0, 0), memory_space=pltpu.SMEM),
                  _row_tile_spec(DMA_TILE, lambda i, tb, nb: (i, 0))],
        out_specs=pl.BlockSpec(memory_space=pl.ANY),
        scratch_shapes=[pltpu.VMEM((MOE_TILE * RT, LANES), F32), pltpu.SemaphoreType.DMA(()),
                        pltpu.SemaphoreType.DMA(())],
    )
    return pl.pallas_call(
        _dispatch_body,
        grid_spec=grid_spec,
        out_shape=jax.ShapeDtypeStruct((n_slots * RT, LANES), F32),
        compiler_params=_cparams(("arbitrary",)),
        name="dispatch",
    )(tail_blocks, n_used, dest, h2)


def _experts_body(be_ref, nb_ref, x_ref, wg_ref, wu_ref, wd_ref, y_ref, wgu_sc, wd_sc):
    i = pl.program_id(0)
    prev = be_ref[jnp.maximum(i - 1, 0)]
    fresh = (i == 0) | (be_ref[i] != prev)

    @pl.when(fresh)
    def _():
        wgu_sc[:, :D_EXPERT] = wg_ref[...].astype(BF16)
        wgu_sc[:, D_EXPERT:] = wu_ref[...].astype(BF16)
        wd_sc[...] = wd_ref[...].astype(BF16)

    @pl.when(i < nb_ref[0])
    def _():
        x = _load_row_tiles(x_ref, MOE_TILE).astype(BF16)
        gu = jnp.dot(x, wgu_sc[...], preferred_element_type=F32)
        gp = gu[:, :D_EXPERT]
        hid = gp * _sigmoid(gp) * gu[:, D_EXPERT:]
        _store_row_tiles(y_ref, jnp.dot(hid.astype(BF16), wd_sc[...], preferred_element_type=F32))

    @pl.when(i >= nb_ref[0])
    def _():
        y_ref[...] = jnp.zeros(y_ref.shape, F32)


def _experts(block_e, n_used, xs, wg, wu, wd):
    n_slots = xs.shape[0] // RT
    n_blocks = n_slots // MOE_TILE
    xmap = lambda i, be, nb: (jnp.minimum(i, nb[0] - 1), 0)
    wmap = lambda i, be, nb: (be[i], 0, 0)
    grid_spec = pltpu.PrefetchScalarGridSpec(
        num_scalar_prefetch=2,
        grid=(n_blocks,),
        in_specs=[_row_tile_spec(MOE_TILE, xmap),
                  pl.BlockSpec((None, D_MODEL, D_EXPERT), wmap),
                  pl.BlockSpec((None, D_MODEL, D_EXPERT), wmap),
                  pl.BlockSpec((None, D_EXPERT, D_MODEL), wmap)],
        out_specs=_row_tile_spec(MOE_TILE, lambda i, be, nb: (i, 0)),
        scratch_shapes=[pltpu.VMEM((D_MODEL, 2 * D_EXPERT), BF16), pltpu.VMEM((D_EXPERT, D_MODEL), BF16)],
    )
    return pl.pallas_call(
        _experts_body,
        grid_spec=grid_spec,
        out_shape=jax.ShapeDtypeStruct((n_slots * RT, LANES), F32),
        compiler_params=_cparams(("arbitrary",)),
        name="experts",
    )(block_e, n_used, xs, wg, wu, wd)


def _combine_body(dest_ref, w_ref, u1_ref, ys_hbm, o_ref, ybuf, sem):
    def start(r, carry):
        pltpu.make_async_copy(_token_rows(ys_hbm, dest_ref[0, 0, r]), _token_rows(ybuf, r), sem).start()
        pltpu.make_async_copy(_token_rows(ys_hbm, dest_ref[0, 1, r]), _token_rows(ybuf, DMA_TILE + r), sem).start()
        return carry

    lax.fori_loop(0, DMA_TILE, start, 0)
    pltpu.make_async_copy(ys_hbm.at[pl.ds(0, ybuf.shape[0])], ybuf, sem).wait()
    w = w_ref[...]
    t = DMA_TILE
    o_ref[...] = (u1_ref[...] + w[:, 0:1] * _load_row_tiles(ybuf, t) + w[:, 1:2] * _load_row_tiles(ybuf, t, t))


def _combine(dest, w, u1, ys):
    n = u1.shape[0]
    t = DMA_TILE
    return pl.pallas_call(
        _combine_body,
        grid=(n // t,),
        in_specs=[pl.BlockSpec((1, 2, t), lambda i: (i, 0, 0), memory_space=pltpu.SMEM),
                  pl.BlockSpec((t, 2), lambda i: (i, 0)),
                  pl.BlockSpec((t, D_MODEL), lambda i: (i, 0)),
                  pl.BlockSpec(memory_space=pl.ANY)],
        out_specs=pl.BlockSpec((t, D_MODEL), lambda i: (i, 0)),
        out_shape=jax.ShapeDtypeStruct((n, D_MODEL), F32),
        scratch_shapes=[pltpu.VMEM((2 * t * RT, LANES), F32), pltpu.SemaphoreType.DMA(())],
        compiler_params=_cparams(("arbitrary",)),
        name="combine",
    )(dest, w, u1, ys)


def _rope_tables(pos):
    half = DA_HEAD_DIM // 2
    inv_freq = jnp.power(ROPE_THETA, -jnp.arange(half, dtype=F32) * 2.0 / DA_HEAD_DIM)
    ang = pos[:, None] * inv_freq[None, :]
    cos, sin = jnp.cos(ang), jnp.sin(ang)
    cos_t = jnp.tile(cos, (1, LANES // half))
    sin_t = jnp.tile(jnp.concatenate([-sin, sin], axis=1), (1, LANES // DA_HEAD_DIM))
    return cos_t, sin_t


def _layer(x, meta_tokens, l, g_mix_norm, w_in, g_q_norm, g_k_norm, lambda_q1, lambda_k1, lambda_q2, lambda_k2,
           g_diff_subln, w_gla_gate_up, b_gla_gate, g_gla_norm, w_branch, b_merge_gate, w_out, g_ffn_norm,
           w_router_group, b_router_group, w_router_expert, b_router_expert, w_exp_gate, w_exp_up, w_exp_down):
    bsz, seq, _ = x.shape
    n = bsz * seq

    wi = w_in[l]
    offs = [0]
    for s in (512, 512, 512, 256, 256, 512, 512, GLA_RANK, 2 * D_MODEL):
        offs.append(offs[-1] + s)
    sec = lambda j: wi[:, offs[j]:offs[j + 1]].astype(BF16)
    p = {
        'gmix': g_mix_norm[l][None, :],
        'gqn': jnp.tile(g_q_norm[l], LANES // DA_HEAD_DIM)[None, :],
        'gkn': jnp.tile(g_k_norm[l], LANES // DA_HEAD_DIM)[None, :],
        'wq': sec(0), 'wk': sec(1), 'wv': sec(2), 'wgq': sec(3), 'wgk': sec(4), 'wgv': sec(5), 'wgr': sec(6),
        'wgg': jnp.pad(sec(7), ((0, 0), (0, LANES - GLA_RANK))),
        'wup': jnp.pad(w_gla_gate_up[l].astype(BF16), ((0, LANES - GLA_RANK), (0, 0))),
        'bup': b_gla_gate[l][None, :],
        'wgate': sec(8),
        'bgate': b_merge_gate[l].reshape(1, 2 * D_MODEL),
        'wb0': w_branch[l, 0].astype(BF16), 'wb1': w_branch[l, 1].astype(BF16),
        'wout': w_out[l].astype(BF16),
        'gffn': g_ffn_norm[l][None, :],
        'wr': jnp.pad(jnp.concatenate([w_router_group[l], w_router_expert[l].reshape(D_MODEL, N_EXPERTS)],
                                      axis=1).astype(BF16), ((0, 0), (0, LANES - N_GROUPS - N_EXPERTS))),
        'br': jnp.pad(jnp.concatenate([b_router_group[l], b_router_expert[l].reshape(N_EXPERTS)]),
                      (0, LANES - N_GROUPS - N_EXPERTS))[None, :],
    }

    u_front = jnp.concatenate([jnp.zeros((FRONT - N_META, D_MODEL), F32), meta_tokens.astype(F32)], axis=0)
    cos_f, sin_f = _rope_tables(jnp.arange(FRONT, dtype=F32) - (FRONT - N_META))
    cos_x, sin_x = _rope_tables(jnp.arange(seq, dtype=F32) + N_META)
    front = _inproj(u_front, FRONT, cos_f, sin_f, p)
    xin = _inproj(x.reshape(n, D_MODEL), ROW_TILE, cos_x, sin_x, p)
    q, k, v, gq, gk, gv, sr, la, gates = [a.reshape(bsz, seq, a.shape[-1]) for a in xin]
    _, kf, vf, _, gkf, gvf, _, laf, _ = front

    lam_init = 0.8 - 0.6 * math.exp(-0.3 * l)
    lam_vecs = [a[l][None, :] for a in (lambda_q1, lambda_k1, lambda_q2, lambda_k2)]
    o_a = _diff_attn(q, kf, vf, k, v, lam_vecs, g_diff_subln[l][None, :], lam_init)
    o_b = _gla(gkf, gvf, laf, gq, gk, gv, la, sr, g_gla_norm[l][None, :])

    u1, h2, info, cnt = _merge(o_a.reshape(n, -1), o_b.reshape(n, -1), gates.reshape(n, -1),
                               x.reshape(n, D_MODEL), p)

    ids = info[0:2].astype(jnp.int32)
    wts = info[2:4]
    rank = info[4:6].astype(jnp.int32)
    counts = cnt[0, :N_EXPERTS].astype(jnp.int32)
    padded = (counts + MOE_TILE - 1) // MOE_TILE * MOE_TILE
    pends = jnp.cumsum(padded)
    pstarts = pends - padded
    expert = jnp.arange(N_EXPERTS, dtype=jnp.int32)
    dest = jnp.sum(jnp.where(ids[..., None] == expert, pstarts, 0), axis=-1) + rank
    n_slots = (2 * n // MOE_TILE + N_EXPERTS) * MOE_TILE
    n_blocks = n_slots // MOE_TILE
    n_used = (pends[-1] // MOE_TILE).astype(jnp.int32)
    blk = jnp.minimum(jnp.arange(n_blocks, dtype=jnp.int32), n_used - 1) * MOE_TILE
    block_e = jnp.minimum(jnp.sum(pends[None, :] <= blk[:, None], axis=1), N_EXPERTS - 1).astype(jnp.int32)
    tail_blocks = jnp.where(counts > 0, pends // MOE_TILE - 1, -1).astype(jnp.int32)
    dest_t = dest.reshape(2, n // DMA_TILE, DMA_TILE).transpose(1, 0, 2)

    xs = _dispatch(tail_blocks, n_used[None], dest_t, h2, n_slots)
    ys = _experts(block_e, n_used[None], xs, w_exp_gate[l], w_exp_up[l], w_exp_down[l])
    out = _combine(dest_t, wts.T, u1, ys)
    return out.reshape(bsz, seq, D_MODEL)


def kernel(x, meta_tokens, g_mix_norm, w_in, g_q_norm, g_k_norm, lambda_q1, lambda_k1, lambda_q2, lambda_k2,
           g_diff_subln, w_gla_gate_up, b_gla_gate, g_gla_norm, w_branch, b_merge_gate, w_out, g_ffn_norm,
           w_router_group, b_router_group, w_router_expert, b_router_expert, w_exp_gate, w_exp_up, w_exp_down):
    depth = w_in.shape[0]
    assert depth == 1, "meta tokens are only carried through a single layer in this implementation"
    assert x.shape[1] % ROW_TILE == 0 and x.shape[2] == D_MODEL
    return _layer(x, meta_tokens, 0, g_mix_norm, w_in, g_q_norm, g_k_norm, lambda_q1, lambda_k1, lambda_q2,
                  lambda_k2, g_diff_subln, w_gla_gate_up, b_gla_gate, g_gla_norm, w_branch, b_merge_gate, w_out,
                  g_ffn_norm, w_router_group, b_router_group, w_router_expert, b_router_expert,
                  w_exp_gate, w_exp_up, w_exp_down)
```

```python
import functools
import math

import jax
import jax.numpy as jnp
from jax import lax
from jax.experimental import pallas as pl
from jax.experimental.pallas import tpu as pltpu

F32 = jnp.float32
BF16 = jnp.bfloat16

D_MODEL = 1024
N_META = 16
EPS = 1e-6
ROPE_THETA = 10000.0

DA_HEADS = 4
DA_HEAD_DIM = 64
DA_V_DIM = 128
GLA_HEADS = 4
GLA_DK = 64
GLA_DV = 128
GLA_RANK = 16
GLA_TAU = 16.0
GLA_CHUNK = 64
GLA_SAFE_DECAY = 60.0
N_GROUPS = 4
EXPERTS_PER_GROUP = 8
N_EXPERTS = 32
D_EXPERT = 512

LANES = 128
FRONT = 256
ATT_TILE = 256
ROW_TILE = 512
GLA_GROUP = 512
MOE_TILE = 256
EXPERT_CHUNK = 256
DMA_TILE = 256
NEG_BIG = -1e30
VMEM_LIMIT = 56 * 1024 * 1024


def _cparams(sem):
    return pltpu.CompilerParams(dimension_semantics=sem, vmem_limit_bytes=VMEM_LIMIT)


def _const_spec(shape):
    nd = len(shape)
    return pl.BlockSpec(shape, lambda *_: (0,) * nd)


RT = D_MODEL // LANES


def _row_tile_spec(rows, index_map):
    return pl.BlockSpec((rows * RT, LANES), index_map)


def _token_rows(ref, tok):
    return ref.at[pl.ds(pl.multiple_of(tok * RT, RT), RT)]


def _load_row_tiles(ref, rows, first=0):
    return jnp.concatenate([ref[pl.ds(first * RT + c, rows, stride=RT), :] for c in range(RT)], axis=1)


def _store_row_tiles(ref, val):
    for c in range(RT):
        ref[pl.ds(c, val.shape[0], stride=RT), :] = val[:, c * LANES:(c + 1) * LANES]


def _sigmoid(x):
    return 1.0 / (1.0 + jnp.exp(-x))


def _log_sigmoid(x):
    return jnp.minimum(x, 0.0) - jnp.log1p(jnp.exp(-jnp.abs(x)))


def _inproj_body(u_ref, gmix_ref, cos_ref, sin_ref, gqn_ref, gkn_ref,
                 wq_ref, wk_ref, wv_ref, wgq_ref, wgk_ref, wgv_ref, wgr_ref, wgg_ref,
                 wup_ref, bup_ref, wgate_ref, bgate_ref,
                 q_out, k_out, v_out, gq_out, gk_out, gv_out, sr_out, la_out, gate_out):
    x = u_ref[...]
    ms = jnp.mean(x * x, axis=-1, keepdims=True)
    h = (x * lax.rsqrt(ms + EPS) * gmix_ref[...]).astype(BF16)

    cos = cos_ref[...]
    sin = sin_ref[...]
    lane = lax.broadcasted_iota(jnp.int32, (1, LANES), 1)
    first_half = (lane % DA_HEAD_DIM) < (DA_HEAD_DIM // 2)
    gi = lax.broadcasted_iota(jnp.int32, (LANES, LANES), 0) // DA_HEAD_DIM
    gj = lax.broadcasted_iota(jnp.int32, (LANES, LANES), 1) // DA_HEAD_DIM
    group_sum = (gi == gj).astype(BF16)

    def norm_rope(w_ref, gain_ref, out_ref, scale):
        z = jnp.dot(h, w_ref[...], preferred_element_type=F32)
        for hh in range(DA_HEADS):
            zh = z[:, hh * LANES:(hh + 1) * LANES]
            ssq = jnp.dot((zh * zh).astype(BF16), group_sum, preferred_element_type=F32)
            zn = zh * lax.rsqrt(ssq * (1.0 / DA_HEAD_DIM) + EPS) * gain_ref[...]
            rot = jnp.where(first_half,
                            pltpu.roll(zn, LANES - DA_HEAD_DIM // 2, 1),
                            pltpu.roll(zn, DA_HEAD_DIM // 2, 1))
            zr = zn * cos + rot * sin
            out_ref[:, hh * LANES:(hh + 1) * LANES] = (zr * scale).astype(out_ref.dtype)

    norm_rope(wq_ref, gqn_ref, q_out, DA_HEAD_DIM ** -0.5 * math.log2(math.e))
    norm_rope(wk_ref, gkn_ref, k_out, 1.0)
    v_out[...] = jnp.dot(h, wv_ref[...], preferred_element_type=F32).astype(v_out.dtype)

    gq_out[...] = (jnp.dot(h, wgq_ref[...], preferred_element_type=F32) * (GLA_DK ** -0.5)).astype(gq_out.dtype)
    gk_out[...] = jnp.dot(h, wgk_ref[...], preferred_element_type=F32).astype(gk_out.dtype)
    gv_out[...] = jnp.dot(h, wgv_ref[...], preferred_element_type=F32).astype(gv_out.dtype)
    r = jnp.dot(h, wgr_ref[...], preferred_element_type=F32)
    sr_out[...] = (r * _sigmoid(r)).astype(sr_out.dtype)

    g_lr = jnp.dot(h, wgg_ref[...], preferred_element_type=F32)
    pre = jnp.dot(g_lr.astype(BF16), wup_ref[...], preferred_element_type=F32) + bup_ref[...]
    la_out[...] = _log_sigmoid(pre) * (1.0 / GLA_TAU)

    gl = jnp.dot(h, wgate_ref[...], preferred_element_type=F32) + bgate_ref[...]
    gate_out[...] = _sigmoid(gl).astype(gate_out.dtype)


def _inproj(u, tm, cos, sin, p):
    rows = u.shape[0]
    n_tab = cos.shape[0] // tm
    row = lambda w: pl.BlockSpec((tm, w), lambda i: (i, 0))
    tab = pl.BlockSpec((tm, LANES), lambda i: (i % n_tab, 0))
    weights = [p['wq'], p['wk'], p['wv'], p['wgq'], p['wgk'], p['wgv'], p['wgr'], p['wgg'],
               p['wup'], p['bup'], p['wgate'], p['bgate']]
    out_widths = [(512, BF16), (512, BF16), (512, BF16), (256, BF16), (256, BF16), (512, BF16),
                  (512, BF16), (256, F32), (2048, BF16)]
    return pl.pallas_call(
        _inproj_body,
        grid=(rows // tm,),
        in_specs=[row(D_MODEL), _const_spec((1, D_MODEL)), tab, tab,
                  _const_spec((1, LANES)), _const_spec((1, LANES))]
                 + [_const_spec(w.shape) for w in weights],
        out_specs=[row(w) for w, _ in out_widths],
        out_shape=[jax.ShapeDtypeStruct((rows, w), dt) for w, dt in out_widths],
        compiler_params=_cparams(("parallel",)),
        name="inproj",
    )(u, p['gmix'], cos, sin, p['gqn'], p['gkn'], *weights)


def _diff_attn_body(lq1_ref, lk1_ref, lq2_ref, lk2_ref, gsub_ref,
                    q_ref, kf_ref, vf_ref, kx_ref, vx_ref, o_ref, vt_sc, qs_sc, s_sc, cmax_sc, m_sc, l_sc, acc_sc,
                    *, lam_init):
    qi = pl.program_id(1)
    tq = ATT_TILE
    n_kv = kx_ref.shape[0] // tq
    heads = range(DA_HEADS)
    hs = lambda h: slice(h * LANES, (h + 1) * LANES)

    @pl.when(qi == 0)
    def _():
        for h in heads:
            vt_sc[h, 0] = jnp.transpose(vf_ref[:, hs(h)].astype(F32)).astype(BF16)

        def tr(j, carry):
            off = pl.multiple_of(j * tq, tq)
            for h in heads:
                vt_sc[h, j + 1] = jnp.transpose(vx_ref[pl.ds(off, tq), hs(h)].astype(F32)).astype(BF16)
            return carry

        lax.fori_loop(0, n_kv, tr, 0)

    d = lax.broadcasted_iota(jnp.int32, (LANES, 1), 0)
    for h in heads:
        qt = jnp.transpose(q_ref[:, hs(h)].astype(F32))
        qs_sc[h] = jnp.concatenate([jnp.where(d < DA_HEAD_DIM, qt, 0.0),
                                    jnp.where(d >= DA_HEAD_DIM, qt, 0.0)], axis=1).astype(BF16)

    m_sc[...] = jnp.full(m_sc.shape, NEG_BIG, F32)
    l_sc[...] = jnp.zeros(l_sc.shape, F32)
    acc_sc[...] = jnp.zeros(acc_sc.shape, F32)

    def scores(k_of, slot, mask):
        for h in heads:
            s = jnp.dot(k_of(h), qs_sc[h], preferred_element_type=F32)
            if mask is not None:
                s = jnp.where(mask, s, NEG_BIG)
            s_sc[h, slot] = s
            cmax_sc[h, slot] = jnp.max(s, axis=0, keepdims=True)

    def accumulate(t, slot):
        for h in heads:
            m_old = m_sc[h]
            m_new = jnp.maximum(m_old, cmax_sc[h, slot])
            alpha = jnp.exp2(m_old - m_new)
            pr = jnp.exp2(s_sc[h, slot] - m_new)
            l_sc[h] = alpha * l_sc[h] + jnp.sum(pr, axis=0, keepdims=True)
            acc_sc[h] = alpha * acc_sc[h] + jnp.dot(vt_sc[h, t], pr.astype(BF16), preferred_element_type=F32)
            m_sc[h] = m_new

    def x_block(j):
        off = pl.multiple_of(j * tq, tq)
        return lambda h: kx_ref[pl.ds(off, tq), hs(h)]

    key = lax.broadcasted_iota(jnp.int32, (tq, 2 * tq), 0)
    qry = lax.broadcasted_iota(jnp.int32, (tq, 2 * tq), 1) % tq
    scores(lambda h: kf_ref[:, hs(h)], 0, key >= FRONT - N_META)

    def pair(u, carry):
        t = 2 * u
        scores(x_block(t), 1, None)
        accumulate(t, 0)
        scores(x_block(t + 1), 0, None)
        accumulate(t + 1, 1)
        return carry

    lax.fori_loop(0, qi // 2, pair, 0)
    diag = key <= qry

    @pl.when(qi % 2 == 0)
    def _():
        scores(x_block(qi), 1, diag)
        accumulate(qi, 0)
        accumulate(qi + 1, 1)

    @pl.when(qi % 2 == 1)
    def _():
        scores(x_block(qi - 1), 1, None)
        accumulate(qi - 1, 0)
        scores(x_block(qi), 0, diag)
        accumulate(qi, 1)
        accumulate(qi + 1, 0)

    lam = (jnp.exp(jnp.sum(lq1_ref[...] * lk1_ref[...], axis=-1, keepdims=True))
           - jnp.exp(jnp.sum(lq2_ref[...] * lk2_ref[...], axis=-1, keepdims=True)) + lam_init)
    for h in heads:
        acc = acc_sc[h]
        inv_l = 1.0 / l_sc[h]
        ot = acc[:, :tq] * inv_l[:, :tq] - lam * (acc[:, tq:] * inv_l[:, tq:])
        ms = jnp.mean(ot * ot, axis=0, keepdims=True)
        o = jnp.transpose(ot * lax.rsqrt(ms + EPS)) * gsub_ref[...] * (1.0 - lam_init)
        o_ref[:, hs(h)] = o.astype(o_ref.dtype)


def _diff_attn(q, kf, vf, kx, vx, lam_vecs, gsub, lam_init):
    bsz, seq, _ = q.shape
    tq = ATT_TILE
    assert FRONT == tq, "the front block is handled as one key tile"
    vec = _const_spec((1, DA_HEAD_DIM))
    width = DA_HEADS * LANES
    return pl.pallas_call(
        functools.partial(_diff_attn_body, lam_init=lam_init),
        grid=(bsz, seq // tq),
        in_specs=[vec, vec, vec, vec, _const_spec((1, DA_V_DIM)),
                  pl.BlockSpec((None, tq, width), lambda b, i: (b, i, 0)),
                  _const_spec((FRONT, width)), _const_spec((FRONT, width)),
                  pl.BlockSpec((None, seq, width), lambda b, i: (b, 0, 0)),
                  pl.BlockSpec((None, seq, width), lambda b, i: (b, 0, 0))],
        out_specs=pl.BlockSpec((None, tq, width), lambda b, i: (b, i, 0)),
        out_shape=jax.ShapeDtypeStruct((bsz, seq, width), BF16),
        scratch_shapes=[pltpu.VMEM((DA_HEADS, seq // tq + 1, DA_V_DIM, tq), BF16),
                        pltpu.VMEM((DA_HEADS, LANES, 2 * tq), BF16),
                        pltpu.VMEM((DA_HEADS, 2, tq, 2 * tq), F32), pltpu.VMEM((DA_HEADS, 2, 1, 2 * tq), F32),
                        pltpu.VMEM((DA_HEADS, 1, 2 * tq), F32), pltpu.VMEM((DA_HEADS, 1, 2 * tq), F32),
                        pltpu.VMEM((DA_HEADS, DA_V_DIM, 2 * tq), F32)],
        compiler_params=_cparams(("parallel", "arbitrary")),
        name="diff_attn",
    )(*lam_vecs, gsub, q, kf, vf, kx, vx)


def _split3(a):
    a1 = a.astype(BF16)
    r1 = a - a1.astype(F32)
    a2 = r1.astype(BF16)
    a3 = (r1 - a2.astype(F32)).astype(BF16)
    return a1, a2, a3


def _gla_body(gn_ref, kf_ref, vf_ref, laf_ref, q_ref, k_ref, v_ref, la_ref, sr_ref, o_ref, st_sc, b_sc):
    g = pl.program_id(1)
    c = GLA_CHUNK
    kw = GLA_HEADS * GLA_DK
    vw = GLA_HEADS * GLA_DV

    ti = lax.broadcasted_iota(jnp.int32, (c, c), 0)
    si = lax.broadcasted_iota(jnp.int32, (c, c), 1)
    tri = (si <= ti).astype(BF16)
    hv = lax.broadcasted_iota(jnp.int32, (vw, kw), 0) // GLA_DV
    hk = lax.broadcasted_iota(jnp.int32, (vw, kw), 1) // GLA_DK
    head_mask = hv == hk
    lane_head = lax.broadcasted_iota(jnp.int32, (1, kw), 1) // GLA_DK
    causal = lax.broadcasted_iota(jnp.int32, (GLA_HEADS * c, c), 0) % c >= \
        lax.broadcasted_iota(jnp.int32, (GLA_HEADS * c, c), 1)

    def cumsum(a):
        a1, a2, a3 = _split3(a)
        return (jnp.dot(tri, a1, preferred_element_type=F32)
                + jnp.dot(tri, a2, preferred_element_type=F32)
                + jnp.dot(tri, a3, preferred_element_type=F32))

    def update_state(k, v, b):
        b_last = b[c - 1:c, :]
        kd = (k * jnp.exp(b_last - b)).astype(BF16)
        upd = lax.dot_general(v, kd, (((0,), (0,)), ((), ())), preferred_element_type=F32)
        st_sc[...] = jnp.where(head_mask, jnp.exp(b_last) * st_sc[...] + upd, 0.0)

    @pl.when(g == 0)
    def _():
        st_sc[...] = jnp.zeros(st_sc.shape, F32)
        update_state(kf_ref[...].astype(F32), vf_ref[...], cumsum(laf_ref[...]))

    gn = gn_ref[...]

    def finish(o, sr):
        outs = []
        for hh in range(GLA_HEADS):
            cs = slice(hh * GLA_DV, (hh + 1) * GLA_DV)
            oh = o[:, cs]
            ms = jnp.mean(oh * oh, axis=-1, keepdims=True)
            outs.append(oh * lax.rsqrt(ms + EPS) * gn * sr[:, cs])
        return jnp.concatenate(outs, axis=1)

    n_chunks = GLA_GROUP // c
    for ci in range(n_chunks):
        rs = slice(ci * c, (ci + 1) * c)
        b_sc[rs, :] = cumsum(la_ref[rs, :])
    steep = jnp.min(b_sc[...]) < -GLA_SAFE_DECAY

    @pl.when(jnp.logical_not(steep))
    def _():
        for ci in range(n_chunks):
            rs = slice(ci * c, (ci + 1) * c)
            b = b_sc[rs, :]
            q = q_ref[rs, :].astype(F32)
            k = k_ref[rs, :].astype(F32)
            v = v_ref[rs, :]
            qe = q * jnp.exp(b)
            ke = (k * jnp.exp(-b)).astype(BF16)
            inter = lax.dot_general(qe.astype(BF16), st_sc[...].astype(BF16), (((1,), (1,)), ((), ())),
                                    preferred_element_type=F32)
            qstack = jnp.concatenate([jnp.where(lane_head == hh, qe, 0.0) for hh in range(GLA_HEADS)],
                                     axis=0).astype(BF16)
            sc = lax.dot_general(qstack, ke, (((1,), (1,)), ((), ())), preferred_element_type=F32)
            sc = jnp.where(causal, sc, 0.0).astype(BF16)
            pv = jnp.dot(sc, v, preferred_element_type=F32)
            o = jnp.concatenate([inter[:, hh * GLA_DV:(hh + 1) * GLA_DV]
                                 + pv[hh * c:(hh + 1) * c, hh * GLA_DV:(hh + 1) * GLA_DV]
                                 for hh in range(GLA_HEADS)], axis=1)
            o_ref[rs, :] = finish(o, sr_ref[rs, :].astype(F32)).astype(o_ref.dtype)
            update_state(k, v, b)

    @pl.when(steep)
    def _():
        rows = 16

        def tile(i, carry):
            off = pl.multiple_of(i * rows, rows)
            a = la_ref[pl.ds(off, rows), :]
            q = q_ref[pl.ds(off, rows), :].astype(F32)
            k = k_ref[pl.ds(off, rows), :].astype(F32)
            v = v_ref[pl.ds(off, rows), :].astype(F32)
            outs = []
            for r in range(rows):
                row = lambda x: x[r:r + 1].astype(BF16)
                upd = lax.dot_general(row(v), row(k), (((0,), (0,)), ((), ())),
                                      preferred_element_type=F32)
                st = jnp.where(head_mask, jnp.exp(a[r:r + 1]) * st_sc[...] + upd, 0.0)
                st_sc[...] = st
                outs.append(lax.dot_general(row(q), st.astype(BF16), (((1,), (1,)), ((), ())),
                                            preferred_element_type=F32))
            o = jnp.concatenate(outs, axis=0)
            o_ref[pl.ds(off, rows), :] = finish(o, sr_ref[pl.ds(off, rows), :].astype(F32)).astype(o_ref.dtype)
            return carry

        lax.fori_loop(0, GLA_GROUP // rows, tile, 0)


def _gla(kf, vf, laf, q, k, v, la, sr, gn):
    bsz, seq, _ = q.shape
    t = GLA_GROUP
    kw = GLA_HEADS * GLA_DK
    vw = GLA_HEADS * GLA_DV
    last = FRONT // GLA_CHUNK - 1
    fr = lambda w: pl.BlockSpec((GLA_CHUNK, w), lambda b, g: (last, 0))
    xs = lambda w: pl.BlockSpec((None, t, w), lambda b, g: (b, g, 0))
    return pl.pallas_call(
        _gla_body,
        grid=(bsz, seq // t),
        in_specs=[_const_spec((1, GLA_DV)), fr(kw), fr(vw), fr(kw), xs(kw), xs(kw), xs(vw), xs(kw), xs(vw)],
        out_specs=xs(vw),
        out_shape=jax.ShapeDtypeStruct((bsz, seq, vw), BF16),
        scratch_shapes=[pltpu.VMEM((vw, kw), F32), pltpu.VMEM((t, kw), F32)],
        compiler_params=_cparams(("parallel", "arbitrary")),
        name="gla",
    )(gn, kf, vf, laf, q, k, v, la, sr)


def _merge_body(oa_ref, ob_ref, gate_ref, u_ref, wb0_ref, wb1_ref, wout_ref, gffn_ref, wr_ref, br_ref,
                u1_out, h2_out, info_out, cnt_out, cnt_sc):
    i = pl.program_id(0)
    tm = ROW_TILE

    @pl.when(i == 0)
    def _():
        cnt_sc[...] = jnp.zeros(cnt_sc.shape, F32)

    ya = jnp.dot(oa_ref[...], wb0_ref[...], preferred_element_type=F32)
    yb = jnp.dot(ob_ref[...], wb1_ref[...], preferred_element_type=F32)
    gate = gate_ref[...].astype(F32)
    merged = gate[:, :D_MODEL] * ya + gate[:, D_MODEL:] * yb
    u1 = u_ref[...] + jnp.dot(merged.astype(BF16), wout_ref[...], preferred_element_type=F32)
    u1_out[...] = u1
    ms = jnp.mean(u1 * u1, axis=-1, keepdims=True)
    h2f = u1 * lax.rsqrt(ms + EPS) * gffn_ref[...]
    _store_row_tiles(h2_out, h2f)
    h2 = h2f.astype(BF16)

    logits = jnp.dot(h2, wr_ref[...], preferred_element_type=F32) + br_ref[...]
    lane = lax.broadcasted_iota(jnp.int32, (tm, LANES), 1)
    is_group = lane < N_GROUPS
    gl = jnp.where(is_group, logits, NEG_BIG)
    gmax = jnp.max(gl, axis=-1, keepdims=True)
    g_idx = jnp.min(jnp.where(gl == gmax, lane, LANES), axis=-1, keepdims=True)
    g_w = 1.0 / jnp.sum(jnp.where(is_group, jnp.exp(gl - gmax), 0.0), axis=-1, keepdims=True)
    lo = N_GROUPS + EXPERTS_PER_GROUP * g_idx
    el = jnp.where((lane >= lo) & (lane < lo + EXPERTS_PER_GROUP), logits, NEG_BIG)
    v1 = jnp.max(el, axis=-1, keepdims=True)
    i1 = jnp.min(jnp.where(el == v1, lane, LANES), axis=-1, keepdims=True)
    el2 = jnp.where(lane == i1, NEG_BIG, el)
    v2 = jnp.max(el2, axis=-1, keepdims=True)
    i2 = jnp.min(jnp.where(el2 == v2, lane, LANES), axis=-1, keepdims=True)
    e21 = jnp.exp(v2 - v1)
    w1 = g_w / (1.0 + e21)
    w2 = w1 * e21
    id1 = i1 - N_GROUPS
    id2 = i2 - N_GROUPS

    onehot = ((lane == id1) | (lane == id2))
    ti = lax.broadcasted_iota(jnp.int32, (tm, tm), 0)
    si = lax.broadcasted_iota(jnp.int32, (tm, tm), 1)
    strict = (si < ti).astype(BF16)
    before = jnp.dot(strict, onehot.astype(BF16), preferred_element_type=F32) + cnt_sc[...]
    r1 = jnp.sum(jnp.where(lane == id1, before, 0.0), axis=-1, keepdims=True)
    r2 = jnp.sum(jnp.where(lane == id2, before, 0.0), axis=-1, keepdims=True)
    cnt = cnt_sc[...] + jnp.sum(onehot.astype(F32), axis=0, keepdims=True)
    cnt_sc[...] = cnt
    cnt_out[...] = jnp.broadcast_to(cnt, cnt_out.shape)

    info = (jnp.where(lane == 0, id1.astype(F32), 0.0) + jnp.where(lane == 1, id2.astype(F32), 0.0)
            + jnp.where(lane == 2, w1, 0.0) + jnp.where(lane == 3, w2, 0.0)
            + jnp.where(lane == 4, r1, 0.0) + jnp.where(lane == 5, r2, 0.0))
    info_out[...] = jnp.transpose(info)[:8, :]


def _merge(oa, ob, gates, u, p):
    n = u.shape[0]
    tm = ROW_TILE
    row = lambda w: pl.BlockSpec((tm, w), lambda i: (i, 0))
    return pl.pallas_call(
        _merge_body,
        grid=(n // tm,),
        in_specs=[row(512), row(512), row(2 * D_MODEL), row(D_MODEL),
                  _const_spec((512, D_MODEL)), _const_spec((512, D_MODEL)), _const_spec((D_MODEL, D_MODEL)),
                  _const_spec((1, D_MODEL)), _const_spec((D_MODEL, LANES)), _const_spec((1, LANES))],
        out_specs=[row(D_MODEL), _row_tile_spec(tm, lambda i: (i, 0)), pl.BlockSpec((8, tm), lambda i: (0, i)),
                   _const_spec((8, LANES))],
        out_shape=[jax.ShapeDtypeStruct((n, D_MODEL), F32), jax.ShapeDtypeStruct((n * RT, LANES), F32),
                   jax.ShapeDtypeStruct((8, n), F32), jax.ShapeDtypeStruct((8, LANES), F32)],
        scratch_shapes=[pltpu.VMEM((1, LANES), F32)],
        compiler_params=_cparams(("arbitrary",)),
        name="merge_router",
    )(oa, ob, gates, u, p['wb0'], p['wb1'], p['wout'], p['gffn'], p['wr'], p['br'])


def _dispatch_body(tail_ref, nb_ref, dest_ref, h2_ref, xs_hbm, zero_sc, sem, zsem):
    i = pl.program_id(0)
    blk_rows = MOE_TILE * RT
    n_blocks = xs_hbm.shape[0] // blk_rows

    def zero_copy(blk):
        dst = xs_hbm.at[pl.ds(pl.multiple_of(blk * blk_rows, blk_rows), blk_rows)]
        return pltpu.make_async_copy(zero_sc, dst, zsem)

    @pl.when(i == 0)
    def _():
        zero_sc[...] = jnp.zeros(zero_sc.shape, F32)

        def tails(fn):
            def body(e, carry):
                @pl.when(tail_ref[e] >= 0)
                def _():
                    fn(zero_copy(tail_ref[e]))
                return carry
            lax.fori_loop(0, N_EXPERTS, body, 0)

        def unused(fn):
            def body(b, carry):
                fn(zero_copy(b))
                return carry
            lax.fori_loop(nb_ref[0], n_blocks, body, 0)

        tails(lambda cp: cp.start())
        unused(lambda cp: cp.start())
        tails(lambda cp: cp.wait())
        unused(lambda cp: cp.wait())

    def start(r, carry):
        src = _token_rows(h2_ref, r)
        pltpu.make_async_copy(src, _token_rows(xs_hbm, dest_ref[0, 0, r]), sem).start(priority=0)
        pltpu.make_async_copy(src, _token_rows(xs_hbm, dest_ref[0, 1, r]), sem).start(priority=1)
        return carry

    lax.fori_loop(0, DMA_TILE, start, 0)
    for _ in range(2):
        pltpu.make_async_copy(h2_ref, xs_hbm.at[pl.ds(0, DMA_TILE * RT)], sem).wait()


def _dispatch(tail_blocks, n_used, dest, h2, n_slots):
    n = h2.shape[0] // RT
    grid_spec = pltpu.PrefetchScalarGridSpec(
        num_scalar_prefetch=2,
        grid=(n // DMA_TILE,),
        in_specs=[pl.BlockSpec((1, 2, DMA_TILE), lambda i, tb, nb: (i, 0, 0), memory_space=pltpu.SMEM),
                  _row_tile_spec(DMA_TILE, lambda i, tb, nb: (i, 0))],
        out_specs=pl.BlockSpec(memory_space=pl.ANY),
        scratch_shapes=[pltpu.VMEM((MOE_TILE * RT, LANES), F32), pltpu.SemaphoreType.DMA(()),
                        pltpu.SemaphoreType.DMA(())],
    )
    return pl.pallas_call(
        _dispatch_body,
        grid_spec=grid_spec,
        out_shape=jax.ShapeDtypeStruct((n_slots * RT, LANES), F32),
        compiler_params=_cparams(("arbitrary",)),
        name="dispatch",
    )(tail_blocks, n_used, dest, h2)


def _experts_body(be_ref, nb_ref, x_ref, wg_ref, wu_ref, wd_ref, y_ref, wg_sc, wu_sc, wd_sc):
    i = pl.program_id(0)
    prev = be_ref[jnp.maximum(i - 1, 0)]
    fresh = (i == 0) | (be_ref[i] != prev)

    @pl.when(fresh)
    def _():
        wg_sc[...] = wg_ref[...].astype(BF16)
        wu_sc[...] = wu_ref[...].astype(BF16)
        wd_sc[...] = wd_ref[...].astype(BF16)

    @pl.when(i < nb_ref[0])
    def _():
        x = _load_row_tiles(x_ref, MOE_TILE).astype(BF16)
        y = jnp.zeros((MOE_TILE, D_MODEL), F32)
        for j in range(D_EXPERT // EXPERT_CHUNK):
            cs = slice(j * EXPERT_CHUNK, (j + 1) * EXPERT_CHUNK)
            gp = jnp.dot(x, wg_sc[:, cs], preferred_element_type=F32)
            up = jnp.dot(x, wu_sc[:, cs], preferred_element_type=F32)
            hid = (gp * _sigmoid(gp) * up).astype(BF16)
            y = y + jnp.dot(hid, wd_sc[cs, :], preferred_element_type=F32)
        _store_row_tiles(y_ref, y)

    @pl.when(i >= nb_ref[0])
    def _():
        y_ref[...] = jnp.zeros(y_ref.shape, F32)


def _experts(block_e, n_used, xs, wg, wu, wd):
    n_slots = xs.shape[0] // RT
    n_blocks = n_slots // MOE_TILE
    xmap = lambda i, be, nb: (jnp.minimum(i, nb[0] - 1), 0)
    wmap = lambda i, be, nb: (be[i], 0, 0)
    grid_spec = pltpu.PrefetchScalarGridSpec(
        num_scalar_prefetch=2,
        grid=(n_blocks,),
        in_specs=[_row_tile_spec(MOE_TILE, xmap),
                  pl.BlockSpec((None, D_MODEL, D_EXPERT), wmap),
                  pl.BlockSpec((None, D_MODEL, D_EXPERT), wmap),
                  pl.BlockSpec((None, D_EXPERT, D_MODEL), wmap)],
        out_specs=_row_tile_spec(MOE_TILE, lambda i, be, nb: (i, 0)),
        scratch_shapes=[pltpu.VMEM((D_MODEL, D_EXPERT), BF16), pltpu.VMEM((D_MODEL, D_EXPERT), BF16),
                        pltpu.VMEM((D_EXPERT, D_MODEL), BF16)],
    )
    return pl.pallas_call(
        _experts_body,
        grid_spec=grid_spec,
        out_shape=jax.ShapeDtypeStruct((n_slots * RT, LANES), F32),
        compiler_params=_cparams(("arbitrary",)),
        name="experts",
    )(block_e, n_used, xs, wg, wu, wd)


def _combine_body(dest_ref, dest_next_ref, w_ref, u1_ref, ys_hbm, o_ref, ybuf, sems):
    i = pl.program_id(0)
    n_steps = pl.num_programs(0)
    t = DMA_TILE

    def gather(d_ref, slot):
        buf = ybuf.at[slot]

        def start(r, carry):
            pltpu.make_async_copy(_token_rows(ys_hbm, d_ref[0, 0, r]), _token_rows(buf, r),
                                  sems.at[slot]).start(priority=0)
            pltpu.make_async_copy(_token_rows(ys_hbm, d_ref[0, 1, r]), _token_rows(buf, t + r),
                                  sems.at[slot]).start(priority=1)
            return carry

        lax.fori_loop(0, t, start, 0)

    slot = i % 2

    @pl.when(i == 0)
    def _():
        gather(dest_ref, 0)

    @pl.when(i + 1 < n_steps)
    def _():
        gather(dest_next_ref, 1 - slot)

    buf = ybuf.at[slot]
    pltpu.make_async_copy(ys_hbm.at[pl.ds(0, buf.shape[0])], buf, sems.at[slot]).wait()
    w = w_ref[...]
    o_ref[...] = (u1_ref[...] + w[:, 0:1] * _load_row_tiles(buf, t) + w[:, 1:2] * _load_row_tiles(buf, t, t))


def _combine(dest, w, u1, ys):
    n = u1.shape[0]
    t = DMA_TILE
    n_steps = n // t
    return pl.pallas_call(
        _combine_body,
        grid=(n_steps,),
        in_specs=[pl.BlockSpec((1, 2, t), lambda i: (i, 0, 0), memory_space=pltpu.SMEM),
                  pl.BlockSpec((1, 2, t), lambda i: (jnp.minimum(i + 1, n_steps - 1), 0, 0),
                               memory_space=pltpu.SMEM),
                  pl.BlockSpec((t, 2), lambda i: (i, 0)),
                  pl.BlockSpec((t, D_MODEL), lambda i: (i, 0)),
                  pl.BlockSpec(memory_space=pl.ANY)],
        out_specs=pl.BlockSpec((t, D_MODEL), lambda i: (i, 0)),
        out_shape=jax.ShapeDtypeStruct((n, D_MODEL), F32),
        scratch_shapes=[pltpu.VMEM((2, 2 * t * RT, LANES), F32), pltpu.SemaphoreType.DMA((2,))],
        compiler_params=_cparams(("arbitrary",)),
        name="combine",
    )(dest, dest, w, u1, ys)


def _rope_tables(pos):
    half = DA_HEAD_DIM // 2
    inv_freq = jnp.power(ROPE_THETA, -jnp.arange(half, dtype=F32) * 2.0 / DA_HEAD_DIM)
    ang = pos[:, None] * inv_freq[None, :]
    cos, sin = jnp.cos(ang), jnp.sin(ang)
    cos_t = jnp.tile(cos, (1, LANES // half))
    sin_t = jnp.tile(jnp.concatenate([-sin, sin], axis=1), (1, LANES // DA_HEAD_DIM))
    return cos_t, sin_t


def _layer(x, meta_tokens, l, g_mix_norm, w_in, g_q_norm, g_k_norm, lambda_q1, lambda_k1, lambda_q2, lambda_k2,
           g_diff_subln, w_gla_gate_up, b_gla_gate, g_gla_norm, w_branch, b_merge_gate, w_out, g_ffn_norm,
           w_router_group, b_router_group, w_router_expert, b_router_expert, w_exp_gate, w_exp_up, w_exp_down):
    bsz, seq, _ = x.shape
    n = bsz * seq

    wi = w_in[l]
    offs = [0]
    for s in (512, 512, 512, 256, 256, 512, 512, GLA_RANK, 2 * D_MODEL):
        offs.append(offs[-1] + s)
    sec = lambda j: wi[:, offs[j]:offs[j + 1]].astype(BF16)
    p = {
        'gmix': g_mix_norm[l][None, :],
        'gqn': jnp.tile(g_q_norm[l], LANES // DA_HEAD_DIM)[None, :],
        'gkn': jnp.tile(g_k_norm[l], LANES // DA_HEAD_DIM)[None, :],
        'wq': sec(0), 'wk': sec(1), 'wv': sec(2), 'wgq': sec(3), 'wgk': sec(4), 'wgv': sec(5), 'wgr': sec(6),
        'wgg': jnp.pad(sec(7), ((0, 0), (0, LANES - GLA_RANK))),
        'wup': jnp.pad(w_gla_gate_up[l].astype(BF16), ((0, LANES - GLA_RANK), (0, 0))),
        'bup': b_gla_gate[l][None, :],
        'wgate': sec(8),
        'bgate': b_merge_gate[l].reshape(1, 2 * D_MODEL),
        'wb0': w_branch[l, 0].astype(BF16), 'wb1': w_branch[l, 1].astype(BF16),
        'wout': w_out[l].astype(BF16),
        'gffn': g_ffn_norm[l][None, :],
        'wr': jnp.pad(jnp.concatenate([w_router_group[l], w_router_expert[l].reshape(D_MODEL, N_EXPERTS)],
                                      axis=1).astype(BF16), ((0, 0), (0, LANES - N_GROUPS - N_EXPERTS))),
        'br': jnp.pad(jnp.concatenate([b_router_group[l], b_router_expert[l].reshape(N_EXPERTS)]),
                      (0, LANES - N_GROUPS - N_EXPERTS))[None, :],
    }

    u_front = jnp.concatenate([jnp.zeros((FRONT - N_META, D_MODEL), F32), meta_tokens.astype(F32)], axis=0)
    cos_f, sin_f = _rope_tables(jnp.arange(FRONT, dtype=F32) - (FRONT - N_META))
    cos_x, sin_x = _rope_tables(jnp.arange(seq, dtype=F32) + N_META)
    front = _inproj(u_front, FRONT, cos_f, sin_f, p)
    xin = _inproj(x.reshape(n, D_MODEL), ROW_TILE, cos_x, sin_x, p)
    q, k, v, gq, gk, gv, sr, la, gates = [a.reshape(bsz, seq, a.shape[-1]) for a in xin]
    _, kf, vf, _, gkf, gvf, _, laf, _ = front

    lam_init = 0.8 - 0.6 * math.exp(-0.3 * l)
    lam_vecs = [a[l][None, :] for a in (lambda_q1, lambda_k1, lambda_q2, lambda_k2)]
    o_a = _diff_attn(q, kf, vf, k, v, lam_vecs, g_diff_subln[l][None, :], lam_init)
    o_b = _gla(gkf, gvf, laf, gq, gk, gv, la, sr, g_gla_norm[l][None, :])

    u1, h2, info, cnt = _merge(o_a.reshape(n, -1), o_b.reshape(n, -1), gates.reshape(n, -1),
                               x.reshape(n, D_MODEL), p)

    ids = info[0:2].astype(jnp.int32)
    wts = info[2:4]
    rank = info[4:6].astype(jnp.int32)
    counts = cnt[0, :N_EXPERTS].astype(jnp.int32)
    padded = (counts + MOE_TILE - 1) // MOE_TILE * MOE_TILE
    pends = jnp.cumsum(padded)
    pstarts = pends - padded
    expert = jnp.arange(N_EXPERTS, dtype=jnp.int32)
    dest = jnp.sum(jnp.where(ids[..., None] == expert, pstarts, 0), axis=-1) + rank
    n_slots = (2 * n // MOE_TILE + N_EXPERTS) * MOE_TILE
    n_blocks = n_slots // MOE_TILE
    n_used = (pends[-1] // MOE_TILE).astype(jnp.int32)
    blk = jnp.minimum(jnp.arange(n_blocks, dtype=jnp.int32), n_used - 1) * MOE_TILE
    block_e = jnp.minimum(jnp.sum(pends[None, :] <= blk[:, None], axis=1), N_EXPERTS - 1).astype(jnp.int32)
    tail_blocks = jnp.where(counts > 0, pends // MOE_TILE - 1, -1).astype(jnp.int32)
    dest_t = dest.reshape(2, n // DMA_TILE, DMA_TILE).transpose(1, 0, 2)

    xs = _dispatch(tail_blocks, n_used[None], dest_t, h2, n_slots)
    ys = _experts(block_e, n_used[None], xs, w_exp_gate[l], w_exp_up[l], w_exp_down[l])
    out = _combine(dest_t, wts.T, u1, ys)
    return out.reshape(bsz, seq, D_MODEL)


def kernel(x, meta_tokens, g_mix_norm, w_in, g_q_norm, g_k_norm, lambda_q1, lambda_k1, lambda_q2, lambda_k2,
           g_diff_subln, w_gla_gate_up, b_gla_gate, g_gla_norm, w_branch, b_merge_gate, w_out, g_ffn_norm,
           w_router_group, b_router_group, w_router_expert, b_router_expert, w_exp_gate, w_exp_up, w_exp_down):
    depth = w_in.shape[0]
    assert depth == 1, "meta tokens are only carried through a single layer in this implementation"
    assert x.shape[1] % ROW_TILE == 0 and x.shape[2] == D_MODEL
    return _layer(x, meta_tokens, 0, g_mix_norm, w_in, g_q_norm, g_k_norm, lambda_q1, lambda_k1, lambda_q2,
                  lambda_k2, g_diff_subln, w_gla_gate_up, b_gla_gate, g_gla_norm, w_branch, b_merge_gate, w_out,
                  g_ffn_norm, w_router_group, b_router_group, w_router_expert, b_router_expert,
                  w_exp_gate, w_exp_up, w_exp_down)
```

```python
import functools
import math

import jax
import jax.numpy as jnp
from jax import lax
from jax.experimental import pallas as pl
from jax.experimental.pallas import tpu as pltpu

F32 = jnp.float32
BF16 = jnp.bfloat16

D_MODEL = 1024
N_META = 16
EPS = 1e-6
ROPE_THETA = 10000.0

DA_HEADS = 4
DA_HEAD_DIM = 64
DA_V_DIM = 128
GLA_HEADS = 4
GLA_DK = 64
GLA_DV = 128
GLA_RANK = 16
GLA_TAU = 16.0
GLA_CHUNK = 64
GLA_SAFE_DECAY = 60.0
N_GROUPS = 4
EXPERTS_PER_GROUP = 8
N_EXPERTS = 32
D_EXPERT = 512

LANES = 128
FRONT = 256
ATT_TILE = 256
ROW_TILE = 512
GLA_GROUP = 512
MOE_TILE = 512
EXPERT_CHUNK = 256
DMA_TILE = 256
NEG_BIG = -1e30
VMEM_LIMIT = 56 * 1024 * 1024


def _cparams(sem):
    return pltpu.CompilerParams(dimension_semantics=sem, vmem_limit_bytes=VMEM_LIMIT)


def _const_spec(shape):
    nd = len(shape)
    return pl.BlockSpec(shape, lambda *_: (0,) * nd)


RT = D_MODEL // LANES


def _row_tile_spec(rows, index_map):
    return pl.BlockSpec((rows * RT, LANES), index_map)


def _token_rows(ref, tok):
    return ref.at[pl.ds(pl.multiple_of(tok * RT, RT), RT)]


def _load_row_tiles(ref, rows, first=0):
    return jnp.concatenate([ref[pl.ds(first * RT + c, rows, stride=RT), :] for c in range(RT)], axis=1)


def _store_row_tiles(ref, val):
    for c in range(RT):
        ref[pl.ds(c, val.shape[0], stride=RT), :] = val[:, c * LANES:(c + 1) * LANES]


def _sigmoid(x):
    return 0.5 * jnp.tanh(0.5 * x) + 0.5


def _log_sigmoid(x):
    return jnp.minimum(x, 0.0) - jnp.log1p(jnp.exp(-jnp.abs(x)))


def _inproj_body(u_ref, gmix_ref, cos_ref, sin_ref, gqn_ref, gkn_ref,
                 wq_ref, wk_ref, wv_ref, wgq_ref, wgk_ref, wgv_ref, wgr_ref, wgg_ref,
                 wup_ref, bup_ref, wgate_ref, bgate_ref,
                 q_out, k_out, v_out, gq_out, gk_out, gv_out, sr_out, la_out, gate_out):
    x = u_ref[...]
    ms = jnp.mean(x * x, axis=-1, keepdims=True)
    h = (x * lax.rsqrt(ms + EPS) * gmix_ref[...]).astype(BF16)

    cos = cos_ref[...]
    sin = sin_ref[...]
    lane = lax.broadcasted_iota(jnp.int32, (1, LANES), 1)
    first_half = (lane % DA_HEAD_DIM) < (DA_HEAD_DIM // 2)
    gi = lax.broadcasted_iota(jnp.int32, (LANES, LANES), 0) // DA_HEAD_DIM
    gj = lax.broadcasted_iota(jnp.int32, (LANES, LANES), 1) // DA_HEAD_DIM
    group_sum = (gi == gj).astype(BF16)

    def norm_rope(w_ref, gain_ref, out_ref, scale):
        z = jnp.dot(h, w_ref[...], preferred_element_type=F32)
        for hh in range(DA_HEADS):
            zh = z[:, hh * LANES:(hh + 1) * LANES]
            ssq = jnp.dot((zh * zh).astype(BF16), group_sum, preferred_element_type=F32)
            zn = zh * lax.rsqrt(ssq * (1.0 / DA_HEAD_DIM) + EPS) * gain_ref[...]
            rot = jnp.where(first_half,
                            pltpu.roll(zn, LANES - DA_HEAD_DIM // 2, 1),
                            pltpu.roll(zn, DA_HEAD_DIM // 2, 1))
            zr = zn * cos + rot * sin
            out_ref[:, hh * LANES:(hh + 1) * LANES] = (zr * scale).astype(out_ref.dtype)

    norm_rope(wq_ref, gqn_ref, q_out, DA_HEAD_DIM ** -0.5 * math.log2(math.e))
    norm_rope(wk_ref, gkn_ref, k_out, 1.0)
    v_out[...] = jnp.dot(h, wv_ref[...], preferred_element_type=F32).astype(v_out.dtype)

    gq_out[...] = (jnp.dot(h, wgq_ref[...], preferred_element_type=F32) * (GLA_DK ** -0.5)).astype(gq_out.dtype)
    gk_out[...] = jnp.dot(h, wgk_ref[...], preferred_element_type=F32).astype(gk_out.dtype)
    gv_out[...] = jnp.dot(h, wgv_ref[...], preferred_element_type=F32).astype(gv_out.dtype)
    r = jnp.dot(h, wgr_ref[...], preferred_element_type=F32)
    sr_out[...] = (r * _sigmoid(r)).astype(sr_out.dtype)

    g_lr = jnp.dot(h, wgg_ref[...], preferred_element_type=F32)
    pre = jnp.dot(g_lr.astype(BF16), wup_ref[...], preferred_element_type=F32) + bup_ref[...]
    la_out[...] = _log_sigmoid(pre) * (1.0 / GLA_TAU)

    gl = jnp.dot(h, wgate_ref[...], preferred_element_type=F32) + bgate_ref[...]
    gate_out[...] = _sigmoid(gl).astype(gate_out.dtype)


def _inproj(u, tm, cos, sin, p):
    rows = u.shape[0]
    n_tab = cos.shape[0] // tm
    row = lambda w: pl.BlockSpec((tm, w), lambda i: (i, 0))
    tab = pl.BlockSpec((tm, LANES), lambda i: (i % n_tab, 0))
    weights = [p['wq'], p['wk'], p['wv'], p['wgq'], p['wgk'], p['wgv'], p['wgr'], p['wgg'],
               p['wup'], p['bup'], p['wgate'], p['bgate']]
    out_widths = [(512, BF16), (512, BF16), (512, BF16), (256, BF16), (256, BF16), (512, BF16),
                  (512, BF16), (256, F32), (2048, BF16)]
    return pl.pallas_call(
        _inproj_body,
        grid=(rows // tm,),
        in_specs=[row(D_MODEL), _const_spec((1, D_MODEL)), tab, tab,
                  _const_spec((1, LANES)), _const_spec((1, LANES))]
                 + [_const_spec(w.shape) for w in weights],
        out_specs=[row(w) for w, _ in out_widths],
        out_shape=[jax.ShapeDtypeStruct((rows, w), dt) for w, dt in out_widths],
        compiler_params=_cparams(("parallel",)),
        name="inproj",
    )(u, p['gmix'], cos, sin, p['gqn'], p['gkn'], *weights)


def _diff_attn_body(lq1_ref, lk1_ref, lq2_ref, lk2_ref, gsub_ref,
                    q_ref, kf_ref, vf_ref, kx_ref, vx_ref, o_ref, vt_sc, qs_sc, s_sc, cmax_sc, m_sc, l_sc, acc_sc,
                    *, lam_init):
    qi = pl.program_id(1)
    tq = ATT_TILE
    n_kv = kx_ref.shape[0] // tq
    heads = range(DA_HEADS)
    hs = lambda h: slice(h * LANES, (h + 1) * LANES)

    @pl.when(qi == 0)
    def _():
        def tr(j, carry):
            off = pl.multiple_of(j * tq, tq)
            for h in heads:
                vt_sc[h, j] = jnp.transpose(vx_ref[pl.ds(off, tq), hs(h)].astype(F32)).astype(BF16)
            return carry

        lax.fori_loop(0, n_kv, tr, 0)

    d = lax.broadcasted_iota(jnp.int32, (LANES, 1), 0)
    for h in heads:
        qt = jnp.transpose(q_ref[:, hs(h)].astype(F32))
        qs_sc[h] = jnp.concatenate([jnp.where(d < DA_HEAD_DIM, qt, 0.0),
                                    jnp.where(d >= DA_HEAD_DIM, qt, 0.0)], axis=1).astype(BF16)

    meta = slice(FRONT - N_META, FRONT)
    for h in heads:
        s = jnp.dot(kf_ref[meta, hs(h)], qs_sc[h], preferred_element_type=F32)
        m0 = jnp.max(s, axis=0, keepdims=True)
        pr = jnp.exp2(s - m0)
        vt = jnp.transpose(vf_ref[meta, hs(h)].astype(F32)).astype(BF16)
        m_sc[h] = m0
        l_sc[h] = jnp.sum(pr, axis=0, keepdims=True)
        acc_sc[h] = jnp.dot(vt, pr.astype(BF16), preferred_element_type=F32)

    key = lax.broadcasted_iota(jnp.int32, (tq, 2 * tq), 0)
    qry = lax.broadcasted_iota(jnp.int32, (tq, 2 * tq), 1) % tq
    causal = key <= qry

    def scores(j, slot, diagonal=False):
        off = pl.multiple_of(j * tq, tq)
        for h in heads:
            s = jnp.dot(kx_ref[pl.ds(off, tq), hs(h)], qs_sc[h], preferred_element_type=F32)
            if diagonal:
                s = jnp.where(causal, s, NEG_BIG)
            s_sc[h, slot] = s
            cmax_sc[h, slot] = jnp.max(s, axis=0, keepdims=True)

    def accumulate(j, slot):
        for h in heads:
            m_old = m_sc[h]
            m_new = jnp.maximum(m_old, cmax_sc[h, slot])
            alpha = jnp.exp2(m_old - m_new)
            pr = jnp.exp2(s_sc[h, slot] - m_new)
            l_sc[h] = alpha * l_sc[h] + jnp.sum(pr, axis=0, keepdims=True)
            acc_sc[h] = alpha * acc_sc[h] + jnp.dot(vt_sc[h, j], pr.astype(BF16), preferred_element_type=F32)
            m_sc[h] = m_new

    @pl.when(qi == 0)
    def _():
        scores(0, 0, diagonal=True)
        accumulate(0, 0)

    @pl.when(qi > 0)
    def _():
        scores(0, 0)

        def pair(u, carry):
            j = 2 * u
            scores(j + 1, 1)
            accumulate(j, 0)
            scores(j + 2, 0)
            accumulate(j + 1, 1)
            return carry

        n_pairs = (qi - 1) // 2
        lax.fori_loop(0, n_pairs, pair, 0)

        @pl.when(qi % 2 == 1)
        def _():
            scores(qi, 1, diagonal=True)
            accumulate(qi - 1, 0)
            accumulate(qi, 1)

        @pl.when(qi % 2 == 0)
        def _():
            scores(qi - 1, 1)
            accumulate(qi - 2, 0)
            scores(qi, 0, diagonal=True)
            accumulate(qi - 1, 1)
            accumulate(qi, 0)

    lam = (jnp.exp(jnp.sum(lq1_ref[...] * lk1_ref[...], axis=-1, keepdims=True))
           - jnp.exp(jnp.sum(lq2_ref[...] * lk2_ref[...], axis=-1, keepdims=True)) + lam_init)
    for h in heads:
        acc = acc_sc[h]
        inv_l = 1.0 / l_sc[h]
        ot = acc[:, :tq] * inv_l[:, :tq] - lam * (acc[:, tq:] * inv_l[:, tq:])
        ms = jnp.mean(ot * ot, axis=0, keepdims=True)
        o = jnp.transpose(ot * lax.rsqrt(ms + EPS)) * gsub_ref[...] * (1.0 - lam_init)
        o_ref[:, hs(h)] = o.astype(o_ref.dtype)


def _diff_attn(q, kf, vf, kx, vx, lam_vecs, gsub, lam_init):
    bsz, seq, _ = q.shape
    tq = ATT_TILE
    vec = _const_spec((1, DA_HEAD_DIM))
    width = DA_HEADS * LANES
    return pl.pallas_call(
        functools.partial(_diff_attn_body, lam_init=lam_init),
        grid=(bsz, seq // tq),
        in_specs=[vec, vec, vec, vec, _const_spec((1, DA_V_DIM)),
                  pl.BlockSpec((None, tq, width), lambda b, i: (b, i, 0)),
                  _const_spec((FRONT, width)), _const_spec((FRONT, width)),
                  pl.BlockSpec((None, seq, width), lambda b, i: (b, 0, 0)),
                  pl.BlockSpec((None, seq, width), lambda b, i: (b, 0, 0))],
        out_specs=pl.BlockSpec((None, tq, width), lambda b, i: (b, i, 0)),
        out_shape=jax.ShapeDtypeStruct((bsz, seq, width), BF16),
        scratch_shapes=[pltpu.VMEM((DA_HEADS, seq // tq, DA_V_DIM, tq), BF16),
                        pltpu.VMEM((DA_HEADS, LANES, 2 * tq), BF16),
                        pltpu.VMEM((DA_HEADS, 2, tq, 2 * tq), F32), pltpu.VMEM((DA_HEADS, 2, 1, 2 * tq), F32),
                        pltpu.VMEM((DA_HEADS, 1, 2 * tq), F32), pltpu.VMEM((DA_HEADS, 1, 2 * tq), F32),
                        pltpu.VMEM((DA_HEADS, DA_V_DIM, 2 * tq), F32)],
        compiler_params=_cparams(("parallel", "arbitrary")),
        name="diff_attn",
    )(*lam_vecs, gsub, q, kf, vf, kx, vx)


def _split3(a):
    a1 = a.astype(BF16)
    r1 = a - a1.astype(F32)
    a2 = r1.astype(BF16)
    a3 = (r1 - a2.astype(F32)).astype(BF16)
    return a1, a2, a3


def _gla_body(gn_ref, kf_ref, vf_ref, laf_ref, q_ref, k_ref, v_ref, la_ref, sr_ref, o_ref, st_sc, b_sc):
    g = pl.program_id(1)
    c = GLA_CHUNK
    kw = GLA_HEADS * GLA_DK
    vw = GLA_HEADS * GLA_DV

    ti = lax.broadcasted_iota(jnp.int32, (c, c), 0)
    si = lax.broadcasted_iota(jnp.int32, (c, c), 1)
    tri = (si <= ti).astype(BF16)
    hv = lax.broadcasted_iota(jnp.int32, (vw, kw), 0) // GLA_DV
    hk = lax.broadcasted_iota(jnp.int32, (vw, kw), 1) // GLA_DK
    head_mask = hv == hk
    lane_head = lax.broadcasted_iota(jnp.int32, (1, kw), 1) // GLA_DK
    causal = lax.broadcasted_iota(jnp.int32, (GLA_HEADS * c, c), 0) % c >= \
        lax.broadcasted_iota(jnp.int32, (GLA_HEADS * c, c), 1)

    def cumsum(a):
        a1, a2, a3 = _split3(a)
        return (jnp.dot(tri, a1, preferred_element_type=F32)
                + jnp.dot(tri, a2, preferred_element_type=F32)
                + jnp.dot(tri, a3, preferred_element_type=F32))

    def update_state(k, v, b):
        b_last = b[c - 1:c, :]
        kd = (k * jnp.exp(b_last - b)).astype(BF16)
        upd = lax.dot_general(v, kd, (((0,), (0,)), ((), ())), preferred_element_type=F32)
        st_sc[...] = jnp.where(head_mask, jnp.exp(b_last) * st_sc[...] + upd, 0.0)

    @pl.when(g == 0)
    def _():
        st_sc[...] = jnp.zeros(st_sc.shape, F32)
        update_state(kf_ref[...].astype(F32), vf_ref[...], cumsum(laf_ref[...]))

    gn = gn_ref[...]

    def finish(o, sr):
        outs = []
        for hh in range(GLA_HEADS):
            cs = slice(hh * GLA_DV, (hh + 1) * GLA_DV)
            oh = o[:, cs]
            ms = jnp.mean(oh * oh, axis=-1, keepdims=True)
            outs.append(oh * lax.rsqrt(ms + EPS) * gn * sr[:, cs])
        return jnp.concatenate(outs, axis=1)

    n_chunks = GLA_GROUP // c
    for ci in range(n_chunks):
        rs = slice(ci * c, (ci + 1) * c)
        b_sc[rs, :] = cumsum(la_ref[rs, :])
    steep = jnp.min(b_sc[...]) < -GLA_SAFE_DECAY

    @pl.when(jnp.logical_not(steep))
    def _():
        for ci in range(n_chunks):
            rs = slice(ci * c, (ci + 1) * c)
            b = b_sc[rs, :]
            q = q_ref[rs, :].astype(F32)
            k = k_ref[rs, :].astype(F32)
            v = v_ref[rs, :]
            qe = q * jnp.exp(b)
            ke = (k * jnp.exp(-b)).astype(BF16)
            inter = lax.dot_general(qe.astype(BF16), st_sc[...].astype(BF16), (((1,), (1,)), ((), ())),
                                    preferred_element_type=F32)
            qstack = jnp.concatenate([jnp.where(lane_head == hh, qe, 0.0) for hh in range(GLA_HEADS)],
                                     axis=0).astype(BF16)
            sc = lax.dot_general(qstack, ke, (((1,), (1,)), ((), ())), preferred_element_type=F32)
            sc = jnp.where(causal, sc, 0.0).astype(BF16)
            pv = jnp.dot(sc, v, preferred_element_type=F32)
            o = jnp.concatenate([inter[:, hh * GLA_DV:(hh + 1) * GLA_DV]
                                 + pv[hh * c:(hh + 1) * c, hh * GLA_DV:(hh + 1) * GLA_DV]
                                 for hh in range(GLA_HEADS)], axis=1)
            o_ref[rs, :] = finish(o, sr_ref[rs, :].astype(F32)).astype(o_ref.dtype)
            update_state(k, v, b)

    @pl.when(steep)
    def _():
        rows = 16

        def tile(i, carry):
            off = pl.multiple_of(i * rows, rows)
            a = la_ref[pl.ds(off, rows), :]
            q = q_ref[pl.ds(off, rows), :].astype(F32)
            k = k_ref[pl.ds(off, rows), :].astype(F32)
            v = v_ref[pl.ds(off, rows), :].astype(F32)
            outs = []
            for r in range(rows):
                row = lambda x: x[r:r + 1].astype(BF16)
                upd = lax.dot_general(row(v), row(k), (((0,), (0,)), ((), ())),
                                      preferred_element_type=F32)
                st = jnp.where(head_mask, jnp.exp(a[r:r + 1]) * st_sc[...] + upd, 0.0)
                st_sc[...] = st
                outs.append(lax.dot_general(row(q), st.astype(BF16), (((1,), (1,)), ((), ())),
                                            preferred_element_type=F32))
            o = jnp.concatenate(outs, axis=0)
            o_ref[pl.ds(off, rows), :] = finish(o, sr_ref[pl.ds(off, rows), :].astype(F32)).astype(o_ref.dtype)
            return carry

        lax.fori_loop(0, GLA_GROUP // rows, tile, 0)


def _gla(kf, vf, laf, q, k, v, la, sr, gn):
    bsz, seq, _ = q.shape
    t = GLA_GROUP
    kw = GLA_HEADS * GLA_DK
    vw = GLA_HEADS * GLA_DV
    last = FRONT // GLA_CHUNK - 1
    fr = lambda w: pl.BlockSpec((GLA_CHUNK, w), lambda b, g: (last, 0))
    xs = lambda w: pl.BlockSpec((None, t, w), lambda b, g: (b, g, 0))
    return pl.pallas_call(
        _gla_body,
        grid=(bsz, seq // t),
        in_specs=[_const_spec((1, GLA_DV)), fr(kw), fr(vw), fr(kw), xs(kw), xs(kw), xs(vw), xs(kw), xs(vw)],
        out_specs=xs(vw),
        out_shape=jax.ShapeDtypeStruct((bsz, seq, vw), BF16),
        scratch_shapes=[pltpu.VMEM((vw, kw), F32), pltpu.VMEM((t, kw), F32)],
        compiler_params=_cparams(("parallel", "arbitrary")),
        name="gla",
    )(gn, kf, vf, laf, q, k, v, la, sr)


def _merge_body(oa_ref, ob_ref, gate_ref, u_ref, wb0_ref, wb1_ref, wout_ref, gffn_ref, wr_ref, br_ref,
                u1_out, h2_out, info_out, cnt_out, cnt_sc):
    i = pl.program_id(0)
    tm = ROW_TILE

    @pl.when(i == 0)
    def _():
        cnt_sc[...] = jnp.zeros(cnt_sc.shape, F32)

    ya = jnp.dot(oa_ref[...], wb0_ref[...], preferred_element_type=F32)
    yb = jnp.dot(ob_ref[...], wb1_ref[...], preferred_element_type=F32)
    gate = gate_ref[...].astype(F32)
    merged = gate[:, :D_MODEL] * ya + gate[:, D_MODEL:] * yb
    u1 = u_ref[...] + jnp.dot(merged.astype(BF16), wout_ref[...], preferred_element_type=F32)
    u1_out[...] = u1
    ms = jnp.mean(u1 * u1, axis=-1, keepdims=True)
    h2f = u1 * lax.rsqrt(ms + EPS) * gffn_ref[...]
    _store_row_tiles(h2_out, h2f)
    h2 = h2f.astype(BF16)

    logits = jnp.dot(h2, wr_ref[...], preferred_element_type=F32) + br_ref[...]
    lane = lax.broadcasted_iota(jnp.int32, (tm, LANES), 1)
    is_group = lane < N_GROUPS
    gl = jnp.where(is_group, logits, NEG_BIG)
    gmax = jnp.max(gl, axis=-1, keepdims=True)
    g_idx = jnp.min(jnp.where(gl == gmax, lane, LANES), axis=-1, keepdims=True)
    g_w = 1.0 / jnp.sum(jnp.where(is_group, jnp.exp(gl - gmax), 0.0), axis=-1, keepdims=True)
    lo = N_GROUPS + EXPERTS_PER_GROUP * g_idx
    el = jnp.where((lane >= lo) & (lane < lo + EXPERTS_PER_GROUP), logits, NEG_BIG)
    v1 = jnp.max(el, axis=-1, keepdims=True)
    i1 = jnp.min(jnp.where(el == v1, lane, LANES), axis=-1, keepdims=True)
    el2 = jnp.where(lane == i1, NEG_BIG, el)
    v2 = jnp.max(el2, axis=-1, keepdims=True)
    i2 = jnp.min(jnp.where(el2 == v2, lane, LANES), axis=-1, keepdims=True)
    e21 = jnp.exp(v2 - v1)
    w1 = g_w / (1.0 + e21)
    w2 = w1 * e21
    id1 = i1 - N_GROUPS
    id2 = i2 - N_GROUPS

    onehot = ((lane == id1) | (lane == id2))
    ti = lax.broadcasted_iota(jnp.int32, (tm, tm), 0)
    si = lax.broadcasted_iota(jnp.int32, (tm, tm), 1)
    strict = (si < ti).astype(BF16)
    before = jnp.dot(strict, onehot.astype(BF16), preferred_element_type=F32) + cnt_sc[...]
    r1 = jnp.sum(jnp.where(lane == id1, before, 0.0), axis=-1, keepdims=True)
    r2 = jnp.sum(jnp.where(lane == id2, before, 0.0), axis=-1, keepdims=True)
    cnt = cnt_sc[...] + jnp.sum(onehot.astype(F32), axis=0, keepdims=True)
    cnt_sc[...] = cnt
    cnt_out[...] = jnp.broadcast_to(cnt, cnt_out.shape)

    info = (jnp.where(lane == 0, id1.astype(F32), 0.0) + jnp.where(lane == 1, id2.astype(F32), 0.0)
            + jnp.where(lane == 2, w1, 0.0) + jnp.where(lane == 3, w2, 0.0)
            + jnp.where(lane == 4, r1, 0.0) + jnp.where(lane == 5, r2, 0.0))
    info_out[...] = jnp.transpose(info)[:8, :]


def _merge(oa, ob, gates, u, p):
    n = u.shape[0]
    tm = ROW_TILE
    row = lambda w: pl.BlockSpec((tm, w), lambda i: (i, 0))
    return pl.pallas_call(
        _merge_body,
        grid=(n // tm,),
        in_specs=[row(512), row(512), row(2 * D_MODEL), row(D_MODEL),
                  _const_spec((512, D_MODEL)), _const_spec((512, D_MODEL)), _const_spec((D_MODEL, D_MODEL)),
                  _const_spec((1, D_MODEL)), _const_spec((D_MODEL, LANES)), _const_spec((1, LANES))],
        out_specs=[row(D_MODEL), _row_tile_spec(tm, lambda i: (i, 0)), pl.BlockSpec((8, tm), lambda i: (0, i)),
                   _const_spec((8, LANES))],
        out_shape=[jax.ShapeDtypeStruct((n, D_MODEL), F32), jax.ShapeDtypeStruct((n * RT, LANES), F32),
                   jax.ShapeDtypeStruct((8, n), F32), jax.ShapeDtypeStruct((8, LANES), F32)],
        scratch_shapes=[pltpu.VMEM((1, LANES), F32)],
        compiler_params=_cparams(("arbitrary",)),
        name="merge_router",
    )(oa, ob, gates, u, p['wb0'], p['wb1'], p['wout'], p['gffn'], p['wr'], p['br'])


def _dispatch_body(tail_ref, nb_ref, dest_ref, h2_ref, xs_hbm, zero_sc, sem, zsem):
    i = pl.program_id(0)
    blk_rows = MOE_TILE * RT
    n_blocks = xs_hbm.shape[0] // blk_rows

    def zero_copy(blk):
        dst = xs_hbm.at[pl.ds(pl.multiple_of(blk * blk_rows, blk_rows), blk_rows)]
        return pltpu.make_async_copy(zero_sc, dst, zsem)

    @pl.when(i == 0)
    def _():
        zero_sc[...] = jnp.zeros(zero_sc.shape, F32)

        def tails(fn):
            def body(e, carry):
                @pl.when(tail_ref[e] >= 0)
                def _():
                    fn(zero_copy(tail_ref[e]))
                return carry
            lax.fori_loop(0, N_EXPERTS, body, 0)

        def unused(fn):
            def body(b, carry):
                fn(zero_copy(b))
                return carry
            lax.fori_loop(nb_ref[0], n_blocks, body, 0)

        tails(lambda cp: cp.start())
        unused(lambda cp: cp.start())
        tails(lambda cp: cp.wait())
        unused(lambda cp: cp.wait())

    def start(r, carry):
        src = _token_rows(h2_ref, r)
        pltpu.make_async_copy(src, _token_rows(xs_hbm, dest_ref[0, 0, r]), sem).start()
        pltpu.make_async_copy(src, _token_rows(xs_hbm, dest_ref[0, 1, r]), sem).start()
        return carry

    lax.fori_loop(0, DMA_TILE, start, 0)
    for _ in range(2):
        pltpu.make_async_copy(h2_ref, xs_hbm.at[pl.ds(0, DMA_TILE * RT)], sem).wait()


def _dispatch(tail_blocks, n_used, dest, h2, n_slots):
    n = h2.shape[0] // RT
    grid_spec = pltpu.PrefetchScalarGridSpec(
        num_scalar_prefetch=2,
        grid=(n // DMA_TILE,),
        in_specs=[pl.BlockSpec((1, 2, DMA_TILE), lambda i, tb, nb: (i, 0, 0), memory_space=pltpu.SMEM),
                  _row_tile_spec(DMA_TILE, lambda i, tb, nb: (i, 0))],
        out_specs=pl.BlockSpec(memory_space=pl.ANY),
        scratch_shapes=[pltpu.VMEM((MOE_TILE * RT, LANES), F32), pltpu.SemaphoreType.DMA(()),
                        pltpu.SemaphoreType.DMA(())],
    )
    return pl.pallas_call(
        _dispatch_body,
        grid_spec=grid_spec,
        out_shape=jax.ShapeDtypeStruct((n_slots * RT, LANES), F32),
        compiler_params=_cparams(("arbitrary",)),
        name="dispatch",
    )(tail_blocks, n_used, dest, h2)


def _experts_body(be_ref, nb_ref, x_ref, wg_ref, wu_ref, wd_ref, y_ref, wg_sc, wu_sc, wd_sc):
    i = pl.program_id(0)
    prev = be_ref[jnp.maximum(i - 1, 0)]
    fresh = (i == 0) | (be_ref[i] != prev)

    @pl.when(fresh)
    def _():
        wg_sc[...] = wg_ref[...].astype(BF16)
        wu_sc[...] = wu_ref[...].astype(BF16)
        wd_sc[...] = wd_ref[...].astype(BF16)

    @pl.when(i < nb_ref[0])
    def _():
        x = _load_row_tiles(x_ref, MOE_TILE).astype(BF16)
        y = jnp.zeros((MOE_TILE, D_MODEL), F32)
        for j in range(D_EXPERT // EXPERT_CHUNK):
            cs = slice(j * EXPERT_CHUNK, (j + 1) * EXPERT_CHUNK)
            gp = jnp.dot(x, wg_sc[:, cs], preferred_element_type=F32)
            up = jnp.dot(x, wu_sc[:, cs], preferred_element_type=F32)
            hid = (gp * _sigmoid(gp) * up).astype(BF16)
            y = y + jnp.dot(hid, wd_sc[cs, :], preferred_element_type=F32)
        _store_row_tiles(y_ref, y)

    @pl.when(i >= nb_ref[0])
    def _():
        y_ref[...] = jnp.zeros(y_ref.shape, F32)


def _experts(block_e, n_used, xs, wg, wu, wd):
    n_slots = xs.shape[0] // RT
    n_blocks = n_slots // MOE_TILE
    xmap = lambda i, be, nb: (jnp.minimum(i, nb[0] - 1), 0)
    wmap = lambda i, be, nb: (be[i], 0, 0)
    grid_spec = pltpu.PrefetchScalarGridSpec(
        num_scalar_prefetch=2,
        grid=(n_blocks,),
        in_specs=[_row_tile_spec(MOE_TILE, xmap),
                  pl.BlockSpec((None, D_MODEL, D_EXPERT), wmap),
                  pl.BlockSpec((None, D_MODEL, D_EXPERT), wmap),
                  pl.BlockSpec((None, D_EXPERT, D_MODEL), wmap)],
        out_specs=_row_tile_spec(MOE_TILE, lambda i, be, nb: (i, 0)),
        scratch_shapes=[pltpu.VMEM((D_MODEL, D_EXPERT), BF16), pltpu.VMEM((D_MODEL, D_EXPERT), BF16),
                        pltpu.VMEM((D_EXPERT, D_MODEL), BF16)],
    )
    return pl.pallas_call(
        _experts_body,
        grid_spec=grid_spec,
        out_shape=jax.ShapeDtypeStruct((n_slots * RT, LANES), F32),
        compiler_params=_cparams(("arbitrary",)),
        name="experts",
    )(block_e, n_used, xs, wg, wu, wd)


def _combine_body(dest_ref, dest_next_ref, w_ref, u1_ref, ys_hbm, o_ref, ybuf, sems):
    i = pl.program_id(0)
    n_steps = pl.num_programs(0)
    t = DMA_TILE

    def gather(d_ref, slot):
        buf = ybuf.at[slot]

        def start(r, carry):
            pltpu.make_async_copy(_token_rows(ys_hbm, d_ref[0, 0, r]), _token_rows(buf, r),
                                  sems.at[slot]).start()
            pltpu.make_async_copy(_token_rows(ys_hbm, d_ref[0, 1, r]), _token_rows(buf, t + r),
                                  sems.at[slot]).start()
            return carry

        lax.fori_loop(0, t, start, 0)

    slot = i % 2

    @pl.when(i == 0)
    def _():
        gather(dest_ref, 0)

    @pl.when(i + 1 < n_steps)
    def _():
        gather(dest_next_ref, 1 - slot)

    buf = ybuf.at[slot]
    pltpu.make_async_copy(ys_hbm.at[pl.ds(0, buf.shape[0])], buf, sems.at[slot]).wait()
    w = w_ref[...]
    o_ref[...] = (u1_ref[...] + w[:, 0:1] * _load_row_tiles(buf, t) + w[:, 1:2] * _load_row_tiles(buf, t, t))


def _combine(dest, w, u1, ys):
    n = u1.shape[0]
    t = DMA_TILE
    n_steps = n // t
    return pl.pallas_call(
        _combine_body,
        grid=(n_steps,),
        in_specs=[pl.BlockSpec((1, 2, t), lambda i: (i, 0, 0), memory_space=pltpu.SMEM),
                  pl.BlockSpec((1, 2, t), lambda i: (jnp.minimum(i + 1, n_steps - 1), 0, 0),
                               memory_space=pltpu.SMEM),
                  pl.BlockSpec((t, 2), lambda i: (i, 0)),
                  pl.BlockSpec((t, D_MODEL), lambda i: (i, 0)),
                  pl.BlockSpec(memory_space=pl.ANY)],
        out_specs=pl.BlockSpec((t, D_MODEL), lambda i: (i, 0)),
        out_shape=jax.ShapeDtypeStruct((n, D_MODEL), F32),
        scratch_shapes=[pltpu.VMEM((2, 2 * t * RT, LANES), F32), pltpu.SemaphoreType.DMA((2,))],
        compiler_params=_cparams(("arbitrary",)),
        name="combine",
    )(dest, dest, w, u1, ys)


def _rope_tables(pos):
    half = DA_HEAD_DIM // 2
    inv_freq = jnp.power(ROPE_THETA, -jnp.arange(half, dtype=F32) * 2.0 / DA_HEAD_DIM)
    ang = pos[:, None] * inv_freq[None, :]
    cos, sin = jnp.cos(ang), jnp.sin(ang)
    cos_t = jnp.tile(cos, (1, LANES // half))
    sin_t = jnp.tile(jnp.concatenate([-sin, sin], axis=1), (1, LANES // DA_HEAD_DIM))
    return cos_t, sin_t


def _layer(x, meta_tokens, l, g_mix_norm, w_in, g_q_norm, g_k_norm, lambda_q1, lambda_k1, lambda_q2, lambda_k2,
           g_diff_subln, w_gla_gate_up, b_gla_gate, g_gla_norm, w_branch, b_merge_gate, w_out, g_ffn_norm,
           w_router_group, b_router_group, w_router_expert, b_router_expert, w_exp_gate, w_exp_up, w_exp_down):
    bsz, seq, _ = x.shape
    n = bsz * seq

    wi = w_in[l]
    offs = [0]
    for s in (512, 512, 512, 256, 256, 512, 512, GLA_RANK, 2 * D_MODEL):
        offs.append(offs[-1] + s)
    sec = lambda j: wi[:, offs[j]:offs[j + 1]].astype(BF16)
    p = {
        'gmix': g_mix_norm[l][None, :],
        'gqn': jnp.tile(g_q_norm[l], LANES // DA_HEAD_DIM)[None, :],
        'gkn': jnp.tile(g_k_norm[l], LANES // DA_HEAD_DIM)[None, :],
        'wq': sec(0), 'wk': sec(1), 'wv': sec(2), 'wgq': sec(3), 'wgk': sec(4), 'wgv': sec(5), 'wgr': sec(6),
        'wgg': jnp.pad(sec(7), ((0, 0), (0, LANES - GLA_RANK))),
        'wup': jnp.pad(w_gla_gate_up[l].astype(BF16), ((0, LANES - GLA_RANK), (0, 0))),
        'bup': b_gla_gate[l][None, :],
        'wgate': sec(8),
        'bgate': b_merge_gate[l].reshape(1, 2 * D_MODEL),
        'wb0': w_branch[l, 0].astype(BF16), 'wb1': w_branch[l, 1].astype(BF16),
        'wout': w_out[l].astype(BF16),
        'gffn': g_ffn_norm[l][None, :],
        'wr': jnp.pad(jnp.concatenate([w_router_group[l], w_router_expert[l].reshape(D_MODEL, N_EXPERTS)],
                                      axis=1).astype(BF16), ((0, 0), (0, LANES - N_GROUPS - N_EXPERTS))),
        'br': jnp.pad(jnp.concatenate([b_router_group[l], b_router_expert[l].reshape(N_EXPERTS)]),
                      (0, LANES - N_GROUPS - N_EXPERTS))[None, :],
    }

    u_front = jnp.concatenate([jnp.zeros((FRONT - N_META, D_MODEL), F32), meta_tokens.astype(F32)], axis=0)
    cos_f, sin_f = _rope_tables(jnp.arange(FRONT, dtype=F32) - (FRONT - N_META))
    cos_x, sin_x = _rope_tables(jnp.arange(seq, dtype=F32) + N_META)
    front = _inproj(u_front, FRONT, cos_f, sin_f, p)
    xin = _inproj(x.reshape(n, D_MODEL), ROW_TILE, cos_x, sin_x, p)
    q, k, v, gq, gk, gv, sr, la, gates = [a.reshape(bsz, seq, a.shape[-1]) for a in xin]
    _, kf, vf, _, gkf, gvf, _, laf, _ = front

    lam_init = 0.8 - 0.6 * math.exp(-0.3 * l)
    lam_vecs = [a[l][None, :] for a in (lambda_q1, lambda_k1, lambda_q2, lambda_k2)]
    o_a = _diff_attn(q, kf, vf, k, v, lam_vecs, g_diff_subln[l][None, :], lam_init)
    o_b = _gla(gkf, gvf, laf, gq, gk, gv, la, sr, g_gla_norm[l][None, :])

    u1, h2, info, cnt = _merge(o_a.reshape(n, -1), o_b.reshape(n, -1), gates.reshape(n, -1),
                               x.reshape(n, D_MODEL), p)

    ids = info[0:2].astype(jnp.int32)
    wts = info[2:4]
    rank = info[4:6].astype(jnp.int32)
    counts = cnt[0, :N_EXPERTS].astype(jnp.int32)
    padded = (counts + MOE_TILE - 1) // MOE_TILE * MOE_TILE
    pends = jnp.cumsum(padded)
    pstarts = pends - padded
    expert = jnp.arange(N_EXPERTS, dtype=jnp.int32)
    dest = jnp.sum(jnp.where(ids[..., None] == expert, pstarts, 0), axis=-1) + rank
    n_slots = (2 * n // MOE_TILE + N_EXPERTS) * MOE_TILE
    n_blocks = n_slots // MOE_TILE
    n_used = (pends[-1] // MOE_TILE).astype(jnp.int32)
    blk = jnp.minimum(jnp.arange(n_blocks, dtype=jnp.int32), n_used - 1) * MOE_TILE
    block_e = jnp.minimum(jnp.sum(pends[None, :] <= blk[:, None], axis=1), N_EXPERTS - 1).astype(jnp.int32)
    tail_blocks = jnp.where(counts > 0, pends // MOE_TILE - 1, -1).astype(jnp.int32)
    dest_t = dest.reshape(2, n // DMA_TILE, DMA_TILE).transpose(1, 0, 2)

    xs = _dispatch(tail_blocks, n_used[None], dest_t, h2, n_slots)
    ys = _experts(block_e, n_used[None], xs, w_exp_gate[l], w_exp_up[l], w_exp_down[l])
    out = _combine(dest_t, wts.T, u1, ys)
    return out.reshape(bsz, seq, D_MODEL)


def kernel(x, meta_tokens, g_mix_norm, w_in, g_q_norm, g_k_norm, lambda_q1, lambda_k1, lambda_q2, lambda_k2,
           g_diff_subln, w_gla_gate_up, b_gla_gate, g_gla_norm, w_branch, b_merge_gate, w_out, g_ffn_norm,
           w_router_group, b_router_group, w_router_expert, b_router_expert, w_exp_gate, w_exp_up, w_exp_down):
    depth = w_in.shape[0]
    assert depth == 1, "meta tokens are only carried through a single layer in this implementation"
    assert x.shape[1] % ROW_TILE == 0 and x.shape[2] == D_MODEL
    return _layer(x, meta_tokens, 0, g_mix_norm, w_in, g_q_norm, g_k_norm, lambda_q1, lambda_k1, lambda_q2,
                  lambda_k2, g_diff_subln, w_gla_gate_up, b_gla_gate, g_gla_norm, w_branch, b_merge_gate, w_out,
                  g_ffn_norm, w_router_group, b_router_group, w_router_expert, b_router_expert,
                  w_exp_gate, w_exp_up, w_exp_down)
```

```python
import functools
import math

import jax
import jax.numpy as jnp
from jax import lax
from jax.experimental import pallas as pl
from jax.experimental.pallas import tpu as pltpu

F32 = jnp.float32
BF16 = jnp.bfloat16

D_MODEL = 1024
N_META = 16
EPS = 1e-6
ROPE_THETA = 10000.0

DA_HEADS = 4
DA_HEAD_DIM = 64
DA_V_DIM = 128
Q_SCALE = DA_HEAD_DIM ** -0.5 * math.log2(math.e)
ATT_BOUND_MARGIN = 1.02
ATT_SAFE_BOUND = 60.0
GLA_HEADS = 4
GLA_DK = 64
GLA_DV = 128
GLA_RANK = 16
GLA_TAU = 16.0
GLA_CHUNK = 64
GLA_SAFE_DECAY = 60.0
N_GROUPS = 4
EXPERTS_PER_GROUP = 8
N_EXPERTS = 32
D_EXPERT = 512
ROUTER_ROWS = 48

LANES = 128
FRONT = 256
ATT_TILE = 256
ROW_TILE = 512
GLA_GROUP = 512
MOE_TILE = 512
EXPERT_CHUNK = 256
DMA_TILE = 256
NEG_BIG = -1e30
VMEM_LIMIT = 56 * 1024 * 1024


def _cparams(sem):
    return pltpu.CompilerParams(dimension_semantics=sem, vmem_limit_bytes=VMEM_LIMIT)


def _const_spec(shape):
    nd = len(shape)
    return pl.BlockSpec(shape, lambda *_: (0,) * nd)


RT = D_MODEL // LANES


def _row_tile_spec(rows, index_map):
    return pl.BlockSpec((rows * RT, LANES), index_map)


def _token_rows(ref, tok):
    return ref.at[pl.ds(pl.multiple_of(tok * RT, RT), RT)]


def _load_row_tiles(ref, rows, first=0):
    return jnp.concatenate([ref[pl.ds(first * RT + c, rows, stride=RT), :] for c in range(RT)], axis=1)


def _store_row_tiles(ref, val):
    for c in range(RT):
        ref[pl.ds(c, val.shape[0], stride=RT), :] = val[:, c * LANES:(c + 1) * LANES]


def _sigmoid(x):
    return 0.5 * jnp.tanh(0.5 * x) + 0.5


def _log_sigmoid(x):
    return jnp.minimum(x, 0.0) - jnp.log1p(jnp.exp(-jnp.abs(x)))


def _inproj_body(u_ref, gmix_ref, cos_ref, sin_ref, gqn_ref, gkn_ref,
                 wq_ref, wk_ref, wv_ref, wgq_ref, wgk_ref, wgv_ref, wgr_ref, wgg_ref,
                 wup_ref, bup_ref, wgate_ref, bgate_ref,
                 q_out, k_out, v_out, gq_out, gk_out, gv_out, sr_out, la_out, gate_out):
    x = u_ref[...]
    ms = jnp.mean(x * x, axis=-1, keepdims=True)
    h = (x * lax.rsqrt(ms + EPS) * gmix_ref[...]).astype(BF16)

    cos = cos_ref[...]
    sin = sin_ref[...]
    lane = lax.broadcasted_iota(jnp.int32, (1, LANES), 1)
    first_half = (lane % DA_HEAD_DIM) < (DA_HEAD_DIM // 2)
    gi = lax.broadcasted_iota(jnp.int32, (LANES, LANES), 0) // DA_HEAD_DIM
    gj = lax.broadcasted_iota(jnp.int32, (LANES, LANES), 1) // DA_HEAD_DIM
    group_sum = (gi == gj).astype(BF16)

    def norm_rope(w_ref, gain_ref, out_ref, scale):
        z = jnp.dot(h, w_ref[...], preferred_element_type=F32)
        for hh in range(DA_HEADS):
            zh = z[:, hh * LANES:(hh + 1) * LANES]
            ssq = jnp.dot((zh * zh).astype(BF16), group_sum, preferred_element_type=F32)
            zn = zh * lax.rsqrt(ssq * (1.0 / DA_HEAD_DIM) + EPS) * gain_ref[...]
            rot = jnp.where(first_half,
                            pltpu.roll(zn, LANES - DA_HEAD_DIM // 2, 1),
                            pltpu.roll(zn, DA_HEAD_DIM // 2, 1))
            zr = zn * cos + rot * sin
            out_ref[:, hh * LANES:(hh + 1) * LANES] = (zr * scale).astype(out_ref.dtype)

    norm_rope(wq_ref, gqn_ref, q_out, Q_SCALE)
    norm_rope(wk_ref, gkn_ref, k_out, 1.0)
    v_out[...] = jnp.dot(h, wv_ref[...], preferred_element_type=F32).astype(v_out.dtype)

    gq_out[...] = (jnp.dot(h, wgq_ref[...], preferred_element_type=F32) * (GLA_DK ** -0.5)).astype(gq_out.dtype)
    gk_out[...] = jnp.dot(h, wgk_ref[...], preferred_element_type=F32).astype(gk_out.dtype)
    gv_out[...] = jnp.dot(h, wgv_ref[...], preferred_element_type=F32).astype(gv_out.dtype)
    r = jnp.dot(h, wgr_ref[...], preferred_element_type=F32)
    sr_out[...] = (r * _sigmoid(r)).astype(sr_out.dtype)

    g_lr = jnp.dot(h, wgg_ref[...], preferred_element_type=F32)
    pre = jnp.dot(g_lr.astype(BF16), wup_ref[...], preferred_element_type=F32) + bup_ref[...]
    la_out[...] = _log_sigmoid(pre) * (1.0 / GLA_TAU)

    gl = jnp.dot(h, wgate_ref[...], preferred_element_type=F32) + bgate_ref[...]
    gate_out[...] = _sigmoid(gl).astype(gate_out.dtype)


def _inproj(u, tm, cos, sin, p):
    rows = u.shape[0]
    n_tab = cos.shape[0] // tm
    row = lambda w: pl.BlockSpec((tm, w), lambda i: (i, 0))
    tab = pl.BlockSpec((tm, LANES), lambda i: (i % n_tab, 0))
    weights = [p['wq'], p['wk'], p['wv'], p['wgq'], p['wgk'], p['wgv'], p['wgr'], p['wgg'],
               p['wup'], p['bup'], p['wgate'], p['bgate']]
    out_widths = [(512, BF16), (512, BF16), (512, BF16), (256, BF16), (256, BF16), (512, BF16),
                  (512, BF16), (256, F32), (2048, BF16)]
    return pl.pallas_call(
        _inproj_body,
        grid=(rows // tm,),
        in_specs=[row(D_MODEL), _const_spec((1, D_MODEL)), tab, tab,
                  _const_spec((1, LANES)), _const_spec((1, LANES))]
                 + [_const_spec(w.shape) for w in weights],
        out_specs=[row(w) for w, _ in out_widths],
        out_shape=[jax.ShapeDtypeStruct((rows, w), dt) for w, dt in out_widths],
        compiler_params=_cparams(("parallel",)),
        name="inproj",
    )(u, p['gmix'], cos, sin, p['gqn'], p['gkn'], *weights)


def _diff_attn_body(bound_ref, lq1_ref, lk1_ref, lq2_ref, lk2_ref, gsub_ref,
                    q_ref, kf_ref, vf_ref, kx_ref, vx_ref, o_ref, vt_sc, qs_sc, s_sc, cmax_sc, m_sc, l_sc, acc_sc,
                    *, lam_init):
    qi = pl.program_id(1)
    tq = ATT_TILE
    n_kv = kx_ref.shape[0] // tq
    heads = range(DA_HEADS)
    hs = lambda h: slice(h * LANES, (h + 1) * LANES)

    @pl.when(qi == 0)
    def _():
        def tr(j, carry):
            off = pl.multiple_of(j * tq, tq)
            for h in heads:
                vt_sc[h, j] = jnp.transpose(vx_ref[pl.ds(off, tq), hs(h)].astype(F32)).astype(BF16)
            return carry

        lax.fori_loop(0, n_kv, tr, 0)

    d = lax.broadcasted_iota(jnp.int32, (LANES, 1), 0)
    for h in heads:
        qt = jnp.transpose(q_ref[:, hs(h)].astype(F32))
        qs_sc[h] = jnp.concatenate([jnp.where(d < DA_HEAD_DIM, qt, 0.0),
                                    jnp.where(d >= DA_HEAD_DIM, qt, 0.0)], axis=1).astype(BF16)

    meta = slice(FRONT - N_META, FRONT)
    key = lax.broadcasted_iota(jnp.int32, (tq, 2 * tq), 0)
    qry = lax.broadcasted_iota(jnp.int32, (tq, 2 * tq), 1) % tq
    causal = key <= qry
    bound = bound_ref[0]

    def meta_scores(h):
        s = jnp.dot(kf_ref[meta, hs(h)], qs_sc[h], preferred_element_type=F32)
        vt = jnp.transpose(vf_ref[meta, hs(h)].astype(F32)).astype(BF16)
        return s, vt

    def block_scores(j, h, diagonal):
        off = pl.multiple_of(j * tq, tq)
        s = jnp.dot(kx_ref[pl.ds(off, tq), hs(h)], qs_sc[h], preferred_element_type=F32)
        return jnp.where(causal, s, NEG_BIG) if diagonal else s

    @pl.when(bound <= ATT_SAFE_BOUND)
    def _():
        for h in heads:
            s, vt = meta_scores(h)
            pr = jnp.exp2(s - bound)
            l_sc[h] = jnp.sum(pr, axis=0, keepdims=True)
            acc_sc[h] = jnp.dot(vt, pr.astype(BF16), preferred_element_type=F32)

        def blocks(*js, diagonal_last=False):
            work = [(j, h, diagonal_last and j is js[-1]) for j in js for h in heads]
            ss = [block_scores(j, h, dg) for j, h, dg in work]
            ps = [jnp.exp2(s - bound) for s in ss]
            for (j, h, _), pr in zip(work, ps):
                l_sc[h] += jnp.sum(pr, axis=0, keepdims=True)
            for (j, h, _), pr in zip(work, ps):
                acc_sc[h] += jnp.dot(vt_sc[h, j], pr.astype(BF16), preferred_element_type=F32)

        def pair(u, carry):
            blocks(2 * u, 2 * u + 1)
            return carry

        lax.fori_loop(0, qi // 2, pair, 0)

        @pl.when(qi % 2 == 1)
        def _():
            blocks(qi - 1, qi, diagonal_last=True)

        @pl.when(qi % 2 == 0)
        def _():
            blocks(qi, diagonal_last=True)

    @pl.when(bound > ATT_SAFE_BOUND)
    def _():
        for h in heads:
            s, vt = meta_scores(h)
            m0 = jnp.max(s, axis=0, keepdims=True)
            pr = jnp.exp2(s - m0)
            m_sc[h] = m0
            l_sc[h] = jnp.sum(pr, axis=0, keepdims=True)
            acc_sc[h] = jnp.dot(vt, pr.astype(BF16), preferred_element_type=F32)

        def scores(j, slot, diagonal=False):
            for h in heads:
                s = block_scores(j, h, diagonal)
                s_sc[h, slot] = s
                cmax_sc[h, slot] = jnp.max(s, axis=0, keepdims=True)

        def accumulate(j, slot):
            for h in heads:
                m_old = m_sc[h]
                m_new = jnp.maximum(m_old, cmax_sc[h, slot])
                alpha = jnp.exp2(m_old - m_new)
                pr = jnp.exp2(s_sc[h, slot] - m_new)
                l_sc[h] = alpha * l_sc[h] + jnp.sum(pr, axis=0, keepdims=True)
                acc_sc[h] = alpha * acc_sc[h] + jnp.dot(vt_sc[h, j], pr.astype(BF16),
                                                        preferred_element_type=F32)
                m_sc[h] = m_new

        @pl.when(qi == 0)
        def _():
            scores(0, 0, diagonal=True)
            accumulate(0, 0)

        @pl.when(qi > 0)
        def _():
            scores(0, 0)

            def pair(u, carry):
                j = 2 * u
                scores(j + 1, 1)
                accumulate(j, 0)
                scores(j + 2, 0)
                accumulate(j + 1, 1)
                return carry

            lax.fori_loop(0, (qi - 1) // 2, pair, 0)

            @pl.when(qi % 2 == 1)
            def _():
                scores(qi, 1, diagonal=True)
                accumulate(qi - 1, 0)
                accumulate(qi, 1)

            @pl.when(qi % 2 == 0)
            def _():
                scores(qi - 1, 1)
                accumulate(qi - 2, 0)
                scores(qi, 0, diagonal=True)
                accumulate(qi - 1, 1)
                accumulate(qi, 0)

    lam = (jnp.exp(jnp.sum(lq1_ref[...] * lk1_ref[...], axis=-1, keepdims=True))
           - jnp.exp(jnp.sum(lq2_ref[...] * lk2_ref[...], axis=-1, keepdims=True)) + lam_init)
    for h in heads:
        acc = acc_sc[h]
        inv_l = 1.0 / l_sc[h]
        ot = acc[:, :tq] * inv_l[:, :tq] - lam * (acc[:, tq:] * inv_l[:, tq:])
        ms = jnp.mean(ot * ot, axis=0, keepdims=True)
        o = jnp.transpose(ot * lax.rsqrt(ms + EPS)) * gsub_ref[...] * (1.0 - lam_init)
        o_ref[:, hs(h)] = o.astype(o_ref.dtype)


def _diff_attn(bound, q, kf, vf, kx, vx, lam_vecs, gsub, lam_init):
    bsz, seq, _ = q.shape
    tq = ATT_TILE
    vec = _const_spec((1, DA_HEAD_DIM))
    width = DA_HEADS * LANES
    return pl.pallas_call(
        functools.partial(_diff_attn_body, lam_init=lam_init),
        grid=(bsz, seq // tq),
        in_specs=[pl.BlockSpec(memory_space=pltpu.SMEM), vec, vec, vec, vec, _const_spec((1, DA_V_DIM)),
                  pl.BlockSpec((None, tq, width), lambda b, i: (b, i, 0)),
                  _const_spec((FRONT, width)), _const_spec((FRONT, width)),
                  pl.BlockSpec((None, seq, width), lambda b, i: (b, 0, 0)),
                  pl.BlockSpec((None, seq, width), lambda b, i: (b, 0, 0))],
        out_specs=pl.BlockSpec((None, tq, width), lambda b, i: (b, i, 0)),
        out_shape=jax.ShapeDtypeStruct((bsz, seq, width), BF16),
        scratch_shapes=[pltpu.VMEM((DA_HEADS, seq // tq, DA_V_DIM, tq), BF16),
                        pltpu.VMEM((DA_HEADS, LANES, 2 * tq), BF16),
                        pltpu.VMEM((DA_HEADS, 2, tq, 2 * tq), F32), pltpu.VMEM((DA_HEADS, 2, 1, 2 * tq), F32),
                        pltpu.VMEM((DA_HEADS, 1, 2 * tq), F32), pltpu.VMEM((DA_HEADS, 1, 2 * tq), F32),
                        pltpu.VMEM((DA_HEADS, DA_V_DIM, 2 * tq), F32)],
        compiler_params=_cparams(("parallel", "arbitrary")),
        name="diff_attn",
    )(bound, *lam_vecs, gsub, q, kf, vf, kx, vx)


def _split3(a):
    a1 = a.astype(BF16)
    r1 = a - a1.astype(F32)
    a2 = r1.astype(BF16)
    a3 = (r1 - a2.astype(F32)).astype(BF16)
    return a1, a2, a3


def _gla_body(gn_ref, kf_ref, vf_ref, laf_ref, q_ref, k_ref, v_ref, la_ref, sr_ref, o_ref, st_sc, b_sc):
    g = pl.program_id(1)
    c = GLA_CHUNK
    kw = GLA_HEADS * GLA_DK
    vw = GLA_HEADS * GLA_DV

    ti = lax.broadcasted_iota(jnp.int32, (c, c), 0)
    si = lax.broadcasted_iota(jnp.int32, (c, c), 1)
    tri = (si <= ti).astype(BF16)
    hv = lax.broadcasted_iota(jnp.int32, (vw, kw), 0) // GLA_DV
    hk = lax.broadcasted_iota(jnp.int32, (vw, kw), 1) // GLA_DK
    head_mask = hv == hk
    lane_head = lax.broadcasted_iota(jnp.int32, (1, kw), 1) // GLA_DK
    causal = lax.broadcasted_iota(jnp.int32, (GLA_HEADS * c, c), 0) % c >= \
        lax.broadcasted_iota(jnp.int32, (GLA_HEADS * c, c), 1)

    def cumsum(a):
        a1, a2, a3 = _split3(a)
        return (jnp.dot(tri, a1, preferred_element_type=F32)
                + jnp.dot(tri, a2, preferred_element_type=F32)
                + jnp.dot(tri, a3, preferred_element_type=F32))

    def update_state(k, v, b):
        b_last = b[c - 1:c, :]
        kd = (k * jnp.exp(b_last - b)).astype(BF16)
        upd = lax.dot_general(v, kd, (((0,), (0,)), ((), ())), preferred_element_type=F32)
        st_sc[...] = jnp.where(head_mask, jnp.exp(b_last) * st_sc[...] + upd, 0.0)

    @pl.when(g == 0)
    def _():
        st_sc[...] = jnp.zeros(st_sc.shape, F32)
        update_state(kf_ref[...].astype(F32), vf_ref[...], cumsum(laf_ref[...]))

    gn = gn_ref[...]

    def finish(o, sr):
        outs = []
        for hh in range(GLA_HEADS):
            cs = slice(hh * GLA_DV, (hh + 1) * GLA_DV)
            oh = o[:, cs]
            ms = jnp.mean(oh * oh, axis=-1, keepdims=True)
            outs.append(oh * lax.rsqrt(ms + EPS) * gn * sr[:, cs])
        return jnp.concatenate(outs, axis=1)

    n_chunks = GLA_GROUP // c
    for ci in range(n_chunks):
        rs = slice(ci * c, (ci + 1) * c)
        b_sc[rs, :] = cumsum(la_ref[rs, :])
    steep = jnp.min(b_sc[...]) < -GLA_SAFE_DECAY

    @pl.when(jnp.logical_not(steep))
    def _():
        for ci in range(n_chunks):
            rs = slice(ci * c, (ci + 1) * c)
            b = b_sc[rs, :]
            q = q_ref[rs, :].astype(F32)
            k = k_ref[rs, :].astype(F32)
            v = v_ref[rs, :]
            qe = q * jnp.exp(b)
            ke = (k * jnp.exp(-b)).astype(BF16)
            inter = lax.dot_general(qe.astype(BF16), st_sc[...].astype(BF16), (((1,), (1,)), ((), ())),
                                    preferred_element_type=F32)
            qstack = jnp.concatenate([jnp.where(lane_head == hh, qe, 0.0) for hh in range(GLA_HEADS)],
                                     axis=0).astype(BF16)
            sc = lax.dot_general(qstack, ke, (((1,), (1,)), ((), ())), preferred_element_type=F32)
            sc = jnp.where(causal, sc, 0.0).astype(BF16)
            pv = jnp.dot(sc, v, preferred_element_type=F32)
            o = jnp.concatenate([inter[:, hh * GLA_DV:(hh + 1) * GLA_DV]
                                 + pv[hh * c:(hh + 1) * c, hh * GLA_DV:(hh + 1) * GLA_DV]
                                 for hh in range(GLA_HEADS)], axis=1)
            o_ref[rs, :] = finish(o, sr_ref[rs, :].astype(F32)).astype(o_ref.dtype)
            update_state(k, v, b)

    @pl.when(steep)
    def _():
        rows = 16

        def tile(i, carry):
            off = pl.multiple_of(i * rows, rows)
            a = la_ref[pl.ds(off, rows), :]
            q = q_ref[pl.ds(off, rows), :].astype(F32)
            k = k_ref[pl.ds(off, rows), :].astype(F32)
            v = v_ref[pl.ds(off, rows), :].astype(F32)
            outs = []
            for r in range(rows):
                row = lambda x: x[r:r + 1].astype(BF16)
                upd = lax.dot_general(row(v), row(k), (((0,), (0,)), ((), ())),
                                      preferred_element_type=F32)
                st = jnp.where(head_mask, jnp.exp(a[r:r + 1]) * st_sc[...] + upd, 0.0)
                st_sc[...] = st
                outs.append(lax.dot_general(row(q), st.astype(BF16), (((1,), (1,)), ((), ())),
                                            preferred_element_type=F32))
            o = jnp.concatenate(outs, axis=0)
            o_ref[pl.ds(off, rows), :] = finish(o, sr_ref[pl.ds(off, rows), :].astype(F32)).astype(o_ref.dtype)
            return carry

        lax.fori_loop(0, GLA_GROUP // rows, tile, 0)


def _gla(kf, vf, laf, q, k, v, la, sr, gn):
    bsz, seq, _ = q.shape
    t = GLA_GROUP
    kw = GLA_HEADS * GLA_DK
    vw = GLA_HEADS * GLA_DV
    last = FRONT // GLA_CHUNK - 1
    fr = lambda w: pl.BlockSpec((GLA_CHUNK, w), lambda b, g: (last, 0))
    xs = lambda w: pl.BlockSpec((None, t, w), lambda b, g: (b, g, 0))
    return pl.pallas_call(
        _gla_body,
        grid=(bsz, seq // t),
        in_specs=[_const_spec((1, GLA_DV)), fr(kw), fr(vw), fr(kw), xs(kw), xs(kw), xs(vw), xs(kw), xs(vw)],
        out_specs=xs(vw),
        out_shape=jax.ShapeDtypeStruct((bsz, seq, vw), BF16),
        scratch_shapes=[pltpu.VMEM((vw, kw), F32), pltpu.VMEM((t, kw), F32)],
        compiler_params=_cparams(("parallel", "arbitrary")),
        name="gla",
    )(gn, kf, vf, laf, q, k, v, la, sr)


def _merge_body(oa_ref, ob_ref, gate_ref, u_ref, wb0_ref, wb1_ref, wout_ref, gffn_ref, wr_ref, br_ref,
                u1_out, h2_out, info_out, cnt_out, cnt_sc):
    i = pl.program_id(0)
    tm = ROW_TILE

    @pl.when(i == 0)
    def _():
        cnt_sc[...] = jnp.zeros(cnt_sc.shape, F32)

    ya = jnp.dot(oa_ref[...], wb0_ref[...], preferred_element_type=F32)
    yb = jnp.dot(ob_ref[...], wb1_ref[...], preferred_element_type=F32)
    gate = gate_ref[...].astype(F32)
    merged = gate[:, :D_MODEL] * ya + gate[:, D_MODEL:] * yb
    u1 = u_ref[...] + jnp.dot(merged.astype(BF16), wout_ref[...], preferred_element_type=F32)
    u1_out[...] = u1
    ms = jnp.mean(u1 * u1, axis=-1, keepdims=True)
    h2f = u1 * lax.rsqrt(ms + EPS) * gffn_ref[...]
    _store_row_tiles(h2_out, h2f)
    h2 = h2f.astype(BF16)

    logits = lax.dot_general(wr_ref[...], h2, (((1,), (1,)), ((), ())), preferred_element_type=F32) + br_ref[...]
    row = lax.broadcasted_iota(jnp.int32, (ROUTER_ROWS, tm), 0)
    is_group = row < N_GROUPS
    gl = jnp.where(is_group, logits, NEG_BIG)
    gmax = jnp.max(gl, axis=0, keepdims=True)
    g_idx = jnp.min(jnp.where(gl == gmax, row, ROUTER_ROWS), axis=0, keepdims=True)
    g_w = 1.0 / jnp.sum(jnp.where(is_group, jnp.exp(gl - gmax), 0.0), axis=0, keepdims=True)
    lo = N_GROUPS + EXPERTS_PER_GROUP * g_idx
    el = jnp.where((row >= lo) & (row < lo + EXPERTS_PER_GROUP), logits, NEG_BIG)
    v1 = jnp.max(el, axis=0, keepdims=True)
    i1 = jnp.min(jnp.where(el == v1, row, ROUTER_ROWS), axis=0, keepdims=True)
    el2 = jnp.where(row == i1, NEG_BIG, el)
    v2 = jnp.max(el2, axis=0, keepdims=True)
    i2 = jnp.min(jnp.where(el2 == v2, row, ROUTER_ROWS), axis=0, keepdims=True)
    e21 = jnp.exp(v2 - v1)
    w1 = g_w / (1.0 + e21)
    w2 = w1 * e21

    onehot = ((row == i1) | (row == i2)).astype(BF16)
    si = lax.broadcasted_iota(jnp.int32, (tm, tm), 0)
    ti = lax.broadcasted_iota(jnp.int32, (tm, tm), 1)
    earlier = (si < ti).astype(BF16)
    cnt = cnt_sc[...]
    before = jnp.dot(onehot, earlier, preferred_element_type=F32) + jnp.tile(cnt, (1, tm // LANES))
    r1 = jnp.sum(jnp.where(row == i1, before, 0.0), axis=0, keepdims=True)
    r2 = jnp.sum(jnp.where(row == i2, before, 0.0), axis=0, keepdims=True)
    cnt = cnt + jnp.dot(onehot, jnp.ones((tm, LANES), BF16), preferred_element_type=F32)
    cnt_sc[...] = cnt
    cnt_out[...] = cnt

    zero = jnp.zeros_like(w1)
    info_out[...] = jnp.concatenate([(i1 - N_GROUPS).astype(F32), (i2 - N_GROUPS).astype(F32),
                                     w1, w2, r1, r2, zero, zero], axis=0)


def _merge(oa, ob, gates, u, p):
    n = u.shape[0]
    tm = ROW_TILE
    row = lambda w: pl.BlockSpec((tm, w), lambda i: (i, 0))
    return pl.pallas_call(
        _merge_body,
        grid=(n // tm,),
        in_specs=[row(512), row(512), row(2 * D_MODEL), row(D_MODEL),
                  _const_spec((512, D_MODEL)), _const_spec((512, D_MODEL)), _const_spec((D_MODEL, D_MODEL)),
                  _const_spec((1, D_MODEL)), _const_spec((ROUTER_ROWS, D_MODEL)), _const_spec((ROUTER_ROWS, tm))],
        out_specs=[row(D_MODEL), _row_tile_spec(tm, lambda i: (i, 0)), pl.BlockSpec((8, tm), lambda i: (0, i)),
                   _const_spec((ROUTER_ROWS, LANES))],
        out_shape=[jax.ShapeDtypeStruct((n, D_MODEL), F32), jax.ShapeDtypeStruct((n * RT, LANES), F32),
                   jax.ShapeDtypeStruct((8, n), F32), jax.ShapeDtypeStruct((ROUTER_ROWS, LANES), F32)],
        scratch_shapes=[pltpu.VMEM((ROUTER_ROWS, LANES), F32)],
        compiler_params=_cparams(("arbitrary",)),
        name="merge_router",
    )(oa, ob, gates, u, p['wb0'], p['wb1'], p['wout'], p['gffn'], p['wr'], p['br'])


def _dispatch_body(tail_ref, nb_ref, dest_ref, h2_ref, xs_hbm, zero_sc, sem, zsem):
    i = pl.program_id(0)
    blk_rows = MOE_TILE * RT
    n_blocks = xs_hbm.shape[0] // blk_rows

    def zero_copy(blk):
        dst = xs_hbm.at[pl.ds(pl.multiple_of(blk * blk_rows, blk_rows), blk_rows)]
        return pltpu.make_async_copy(zero_sc, dst, zsem)

    @pl.when(i == 0)
    def _():
        zero_sc[...] = jnp.zeros(zero_sc.shape, F32)

        def tails(fn):
            def body(e, carry):
                @pl.when(tail_ref[e] >= 0)
                def _():
                    fn(zero_copy(tail_ref[e]))
                return carry
            lax.fori_loop(0, N_EXPERTS, body, 0)

        def unused(fn):
            def body(b, carry):
                fn(zero_copy(b))
                return carry
            lax.fori_loop(nb_ref[0], n_blocks, body, 0)

        tails(lambda cp: cp.start())
        unused(lambda cp: cp.start())
        tails(lambda cp: cp.wait())
        unused(lambda cp: cp.wait())

    def start(r, carry):
        src = _token_rows(h2_ref, r)
        pltpu.make_async_copy(src, _token_rows(xs_hbm, dest_ref[0, 0, r]), sem).start()
        pltpu.make_async_copy(src, _token_rows(xs_hbm, dest_ref[0, 1, r]), sem).start()
        return carry

    lax.fori_loop(0, DMA_TILE, start, 0)
    for _ in range(2):
        pltpu.make_async_copy(h2_ref, xs_hbm.at[pl.ds(0, DMA_TILE * RT)], sem).wait()


def _dispatch(tail_blocks, n_used, dest, h2, n_slots):
    n = h2.shape[0] // RT
    grid_spec = pltpu.PrefetchScalarGridSpec(
        num_scalar_prefetch=2,
        grid=(n // DMA_TILE,),
        in_specs=[pl.BlockSpec((1, 2, DMA_TILE), lambda i, tb, nb: (i, 0, 0), memory_space=pltpu.SMEM),
                  _row_tile_spec(DMA_TILE, lambda i, tb, nb: (i, 0))],
        out_specs=pl.BlockSpec(memory_space=pl.ANY),
        scratch_shapes=[pltpu.VMEM((MOE_TILE * RT, LANES), F32), pltpu.SemaphoreType.DMA(()),
                        pltpu.SemaphoreType.DMA(())],
    )
    return pl.pallas_call(
        _dispatch_body,
        grid_spec=grid_spec,
        out_shape=jax.ShapeDtypeStruct((n_slots * RT, LANES), F32),
        compiler_params=_cparams(("arbitrary",)),
        name="dispatch",
    )(tail_blocks, n_used, dest, h2)


def _experts_body(be_ref, nb_ref, x_ref, wg_ref, wu_ref, wd_ref, y_ref, wg_sc, wu_sc, wd_sc):
    i = pl.program_id(0)
    prev = be_ref[jnp.maximum(i - 1, 0)]
    fresh = (i == 0) | (be_ref[i] != prev)

    @pl.when(fresh)
    def _():
        wg_sc[...] = wg_ref[...].astype(BF16)
        wu_sc[...] = wu_ref[...].astype(BF16)
        wd_sc[...] = wd_ref[...].astype(BF16)

    @pl.when(i < nb_ref[0])
    def _():
        x = _load_row_tiles(x_ref, MOE_TILE).astype(BF16)
        y = jnp.zeros((MOE_TILE, D_MODEL), F32)
        for j in range(D_EXPERT // EXPERT_CHUNK):
            cs = slice(j * EXPERT_CHUNK, (j + 1) * EXPERT_CHUNK)
            gp = jnp.dot(x, wg_sc[:, cs], preferred_element_type=F32)
            up = jnp.dot(x, wu_sc[:, cs], preferred_element_type=F32)
            hid = (gp * _sigmoid(gp) * up).astype(BF16)
            y = y + jnp.dot(hid, wd_sc[cs, :], preferred_element_type=F32)
        _store_row_tiles(y_ref, y)

    @pl.when(i >= nb_ref[0])
    def _():
        y_ref[...] = jnp.zeros(y_ref.shape, F32)


def _experts(block_e, n_used, xs, wg, wu, wd):
    n_slots = xs.shape[0] // RT
    n_blocks = n_slots // MOE_TILE
    xmap = lambda i, be, nb: (jnp.minimum(i, nb[0] - 1), 0)
    wmap = lambda i, be, nb: (be[i], 0, 0)
    grid_spec = pltpu.PrefetchScalarGridSpec(
        num_scalar_prefetch=2,
        grid=(n_blocks,),
        in_specs=[_row_tile_spec(MOE_TILE, xmap),
                  pl.BlockSpec((None, D_MODEL, D_EXPERT), wmap),
                  pl.BlockSpec((None, D_MODEL, D_EXPERT), wmap),
                  pl.BlockSpec((None, D_EXPERT, D_MODEL), wmap)],
        out_specs=_row_tile_spec(MOE_TILE, lambda i, be, nb: (i, 0)),
        scratch_shapes=[pltpu.VMEM((D_MODEL, D_EXPERT), BF16), pltpu.VMEM((D_MODEL, D_EXPERT), BF16),
                        pltpu.VMEM((D_EXPERT, D_MODEL), BF16)],
    )
    return pl.pallas_call(
        _experts_body,
        grid_spec=grid_spec,
        out_shape=jax.ShapeDtypeStruct((n_slots * RT, LANES), F32),
        compiler_params=_cparams(("arbitrary",)),
        name="experts",
    )(block_e, n_used, xs, wg, wu, wd)


def _combine_body(dest_ref, dest_next_ref, w_ref, u1_ref, ys_hbm, o_ref, ybuf, sems):
    i = pl.program_id(0)
    n_steps = pl.num_programs(0)
    t = DMA_TILE

    def gather(d_ref, slot):
        buf = ybuf.at[slot]

        def start(r, carry):
            pltpu.make_async_copy(_token_rows(ys_hbm, d_ref[0, 0, r]), _token_rows(buf, r),
                                  sems.at[slot]).start()
            pltpu.make_async_copy(_token_rows(ys_hbm, d_ref[0, 1, r]), _token_rows(buf, t + r),
                                  sems.at[slot]).start()
            return carry

        lax.fori_loop(0, t, start, 0)

    slot = i % 2

    @pl.when(i == 0)
    def _():
        gather(dest_ref, 0)

    @pl.when(i + 1 < n_steps)
    def _():
        gather(dest_next_ref, 1 - slot)

    buf = ybuf.at[slot]
    pltpu.make_async_copy(ys_hbm.at[pl.ds(0, buf.shape[0])], buf, sems.at[slot]).wait()
    w = w_ref[...]
    o_ref[...] = (u1_ref[...] + w[:, 0:1] * _load_row_tiles(buf, t) + w[:, 1:2] * _load_row_tiles(buf, t, t))


def _combine(dest, w, u1, ys):
    n = u1.shape[0]
    t = DMA_TILE
    n_steps = n // t
    return pl.pallas_call(
        _combine_body,
        grid=(n_steps,),
        in_specs=[pl.BlockSpec((1, 2, t), lambda i: (i, 0, 0), memory_space=pltpu.SMEM),
                  pl.BlockSpec((1, 2, t), lambda i: (jnp.minimum(i + 1, n_steps - 1), 0, 0),
                               memory_space=pltpu.SMEM),
                  pl.BlockSpec((t, 2), lambda i: (i, 0)),
                  pl.BlockSpec((t, D_MODEL), lambda i: (i, 0)),
                  pl.BlockSpec(memory_space=pl.ANY)],
        out_specs=pl.BlockSpec((t, D_MODEL), lambda i: (i, 0)),
        out_shape=jax.ShapeDtypeStruct((n, D_MODEL), F32),
        scratch_shapes=[pltpu.VMEM((2, 2 * t * RT, LANES), F32), pltpu.SemaphoreType.DMA((2,))],
        compiler_params=_cparams(("arbitrary",)),
        name="combine",
    )(dest, dest, w, u1, ys)


def _rope_tables(pos):
    half = DA_HEAD_DIM // 2
    inv_freq = jnp.power(ROPE_THETA, -jnp.arange(half, dtype=F32) * 2.0 / DA_HEAD_DIM)
    ang = pos[:, None] * inv_freq[None, :]
    cos, sin = jnp.cos(ang), jnp.sin(ang)
    cos_t = jnp.tile(cos, (1, LANES // half))
    sin_t = jnp.tile(jnp.concatenate([-sin, sin], axis=1), (1, LANES // DA_HEAD_DIM))
    return cos_t, sin_t


def _layer(x, meta_tokens, l, g_mix_norm, w_in, g_q_norm, g_k_norm, lambda_q1, lambda_k1, lambda_q2, lambda_k2,
           g_diff_subln, w_gla_gate_up, b_gla_gate, g_gla_norm, w_branch, b_merge_gate, w_out, g_ffn_norm,
           w_router_group, b_router_group, w_router_expert, b_router_expert, w_exp_gate, w_exp_up, w_exp_down):
    bsz, seq, _ = x.shape
    n = bsz * seq

    wi = w_in[l]
    offs = [0]
    for s in (512, 512, 512, 256, 256, 512, 512, GLA_RANK, 2 * D_MODEL):
        offs.append(offs[-1] + s)
    sec = lambda j: wi[:, offs[j]:offs[j + 1]].astype(BF16)
    p = {
        'gmix': g_mix_norm[l][None, :],
        'gqn': jnp.tile(g_q_norm[l], LANES // DA_HEAD_DIM)[None, :],
        'gkn': jnp.tile(g_k_norm[l], LANES // DA_HEAD_DIM)[None, :],
        'wq': sec(0), 'wk': sec(1), 'wv': sec(2), 'wgq': sec(3), 'wgk': sec(4), 'wgv': sec(5), 'wgr': sec(6),
        'wgg': jnp.pad(sec(7), ((0, 0), (0, LANES - GLA_RANK))),
        'wup': jnp.pad(w_gla_gate_up[l].astype(BF16), ((0, LANES - GLA_RANK), (0, 0))),
        'bup': b_gla_gate[l][None, :],
        'wgate': sec(8),
        'bgate': b_merge_gate[l].reshape(1, 2 * D_MODEL),
        'wb0': w_branch[l, 0].astype(BF16), 'wb1': w_branch[l, 1].astype(BF16),
        'wout': w_out[l].astype(BF16),
        'gffn': g_ffn_norm[l][None, :],
        'wr': jnp.pad(jnp.concatenate([w_router_group[l], w_router_expert[l].reshape(D_MODEL, N_EXPERTS)],
                                      axis=1).T.astype(BF16), ((0, ROUTER_ROWS - N_GROUPS - N_EXPERTS), (0, 0))),
        'br': jnp.broadcast_to(
            jnp.pad(jnp.concatenate([b_router_group[l], b_router_expert[l].reshape(N_EXPERTS)]),
                    (0, ROUTER_ROWS - N_GROUPS - N_EXPERTS))[:, None], (ROUTER_ROWS, ROW_TILE)),
    }

    u_front = jnp.concatenate([jnp.zeros((FRONT - N_META, D_MODEL), F32), meta_tokens.astype(F32)], axis=0)
    cos_f, sin_f = _rope_tables(jnp.arange(FRONT, dtype=F32) - (FRONT - N_META))
    cos_x, sin_x = _rope_tables(jnp.arange(seq, dtype=F32) + N_META)
    front = _inproj(u_front, FRONT, cos_f, sin_f, p)
    xin = _inproj(x.reshape(n, D_MODEL), ROW_TILE, cos_x, sin_x, p)
    q, k, v, gq, gk, gv, sr, la, gates = [a.reshape(bsz, seq, a.shape[-1]) for a in xin]
    _, kf, vf, _, gkf, gvf, _, laf, _ = front

    lam_init = 0.8 - 0.6 * math.exp(-0.3 * l)
    lam_vecs = [a[l][None, :] for a in (lambda_q1, lambda_k1, lambda_q2, lambda_k2)]
    score_bound = (ATT_BOUND_MARGIN * DA_HEAD_DIM * Q_SCALE
                   * jnp.max(jnp.abs(g_q_norm[l])) * jnp.max(jnp.abs(g_k_norm[l]))).reshape(1).astype(F32)
    o_a = _diff_attn(score_bound, q, kf, vf, k, v, lam_vecs, g_diff_subln[l][None, :], lam_init)
    o_b = _gla(gkf, gvf, laf, gq, gk, gv, la, sr, g_gla_norm[l][None, :])

    u1, h2, info, cnt = _merge(o_a.reshape(n, -1), o_b.reshape(n, -1), gates.reshape(n, -1),
                               x.reshape(n, D_MODEL), p)

    ids = info[0:2].astype(jnp.int32)
    wts = info[2:4]
    rank = info[4:6].astype(jnp.int32)
    counts = cnt[N_GROUPS:N_GROUPS + N_EXPERTS, 0].astype(jnp.int32)
    padded = (counts + MOE_TILE - 1) // MOE_TILE * MOE_TILE
    pends = jnp.cumsum(padded)
    pstarts = pends - padded
    expert = jnp.arange(N_EXPERTS, dtype=jnp.int32)
    dest = jnp.sum(jnp.where(ids[..., None] == expert, pstarts, 0), axis=-1) + rank
    n_slots = (2 * n // MOE_TILE + N_EXPERTS) * MOE_TILE
    n_blocks = n_slots // MOE_TILE
    n_used = (pends[-1] // MOE_TILE).astype(jnp.int32)
    blk = jnp.minimum(jnp.arange(n_blocks, dtype=jnp.int32), n_used - 1) * MOE_TILE
    block_e = jnp.minimum(jnp.sum(pends[None, :] <= blk[:, None], axis=1), N_EXPERTS - 1).astype(jnp.int32)
    tail_blocks = jnp.where(counts > 0, pends // MOE_TILE - 1, -1).astype(jnp.int32)
    dest_t = dest.reshape(2, n // DMA_TILE, DMA_TILE).transpose(1, 0, 2)

    xs = _dispatch(tail_blocks, n_used[None], dest_t, h2, n_slots)
    ys = _experts(block_e, n_used[None], xs, w_exp_gate[l], w_exp_up[l], w_exp_down[l])
    out = _combine(dest_t, wts.T, u1, ys)
    return out.reshape(bsz, seq, D_MODEL)


def kernel(x, meta_tokens, g_mix_norm, w_in, g_q_norm, g_k_norm, lambda_q1, lambda_k1, lambda_q2, lambda_k2,
           g_diff_subln, w_gla_gate_up, b_gla_gate, g_gla_norm, w_branch, b_merge_gate, w_out, g_ffn_norm,
           w_router_group, b_router_group, w_router_expert, b_router_expert, w_exp_gate, w_exp_up, w_exp_down):
    depth = w_in.shape[0]
    assert depth == 1, "meta tokens are only carried through a single layer in this implementation"
    assert x.shape[1] % ROW_TILE == 0 and x.shape[2] == D_MODEL
    return _layer(x, meta_tokens, 0, g_mix_norm, w_in, g_q_norm, g_k_norm, lambda_q1, lambda_k1, lambda_q2,
                  lambda_k2, g_diff_subln, w_gla_gate_up, b_gla_gate, g_gla_norm, w_branch, b_merge_gate, w_out,
                  g_ffn_norm, w_router_group, b_router_group, w_router_expert, b_router_expert,
                  w_exp_gate, w_exp_up, w_exp_down)
```

```python
import functools
import math

import jax
import jax.numpy as jnp
from jax import lax
from jax.experimental import pallas as pl
from jax.experimental.pallas import tpu as pltpu

F32 = jnp.float32
BF16 = jnp.bfloat16

D_MODEL = 1024
N_META = 16
EPS = 1e-6
ROPE_THETA = 10000.0

DA_HEADS = 4
DA_HEAD_DIM = 64
DA_V_DIM = 128
Q_SCALE = DA_HEAD_DIM ** -0.5 * math.log2(math.e)
ATT_BOUND_MARGIN = 1.02
ATT_SAFE_BOUND = 60.0
GLA_HEADS = 4
GLA_DK = 64
GLA_DV = 128
GLA_RANK = 16
GLA_TAU = 16.0
GLA_CHUNK = 64
GLA_SAFE_DECAY = 60.0
N_GROUPS = 4
EXPERTS_PER_GROUP = 8
N_EXPERTS = 32
D_EXPERT = 512
ROUTER_ROWS = 48

LANES = 128
FRONT = 256
ATT_TILE = 256
ROW_TILE = 512
GLA_GROUP = 512
MOE_TILE = 512
EXPERT_CHUNK = 256
DMA_TILE = 256
NEG_BIG = -1e30
VMEM_LIMIT = 56 * 1024 * 1024


def _cparams(sem):
    return pltpu.CompilerParams(dimension_semantics=sem, vmem_limit_bytes=VMEM_LIMIT)


def _const_spec(shape):
    nd = len(shape)
    return pl.BlockSpec(shape, lambda *_: (0,) * nd)


RT = D_MODEL // LANES


def _row_tile_spec(rows, index_map):
    return pl.BlockSpec((rows * RT, LANES), index_map)


def _token_rows(ref, tok):
    return ref.at[pl.ds(pl.multiple_of(tok * RT, RT), RT)]


def _load_row_tiles(ref, rows, first=0):
    return jnp.concatenate([ref[pl.ds(first * RT + c, rows, stride=RT), :] for c in range(RT)], axis=1)


def _store_row_tiles(ref, val):
    for c in range(RT):
        ref[pl.ds(c, val.shape[0], stride=RT), :] = val[:, c * LANES:(c + 1) * LANES]


def _sigmoid(x):
    return 0.5 * jnp.tanh(0.5 * x) + 0.5


def _log_sigmoid(x):
    return jnp.minimum(x, 0.0) - jnp.log1p(jnp.exp(-jnp.abs(x)))


def _inproj_body(u_ref, gmix_ref, cos_ref, sin_ref, gqn_ref, gkn_ref,
                 wq_ref, wk_ref, wv_ref, wgq_ref, wgk_ref, wgv_ref, wgr_ref, wgg_ref,
                 wup_ref, bup_ref, wgate_ref, bgate_ref,
                 q_out, k_out, v_out, gq_out, gk_out, gv_out, sr_out, la_out, gate_out):
    x = u_ref[...]
    ms = jnp.mean(x * x, axis=-1, keepdims=True)
    h = (x * lax.rsqrt(ms + EPS) * gmix_ref[...]).astype(BF16)

    cos = cos_ref[...]
    sin = sin_ref[...]
    lane = lax.broadcasted_iota(jnp.int32, (1, LANES), 1)
    first_half = (lane % DA_HEAD_DIM) < (DA_HEAD_DIM // 2)
    gi = lax.broadcasted_iota(jnp.int32, (LANES, LANES), 0) // DA_HEAD_DIM
    gj = lax.broadcasted_iota(jnp.int32, (LANES, LANES), 1) // DA_HEAD_DIM
    group_sum = (gi == gj).astype(BF16)

    def norm_rope(w_ref, gain_ref, out_ref, scale):
        z = jnp.dot(h, w_ref[...], preferred_element_type=F32)
        for hh in range(DA_HEADS):
            zh = z[:, hh * LANES:(hh + 1) * LANES]
            ssq = jnp.dot((zh * zh).astype(BF16), group_sum, preferred_element_type=F32)
            zn = zh * lax.rsqrt(ssq * (1.0 / DA_HEAD_DIM) + EPS) * gain_ref[...]
            rot = jnp.where(first_half,
                            pltpu.roll(zn, LANES - DA_HEAD_DIM // 2, 1),
                            pltpu.roll(zn, DA_HEAD_DIM // 2, 1))
            zr = zn * cos + rot * sin
            out_ref[:, hh * LANES:(hh + 1) * LANES] = (zr * scale).astype(out_ref.dtype)

    norm_rope(wq_ref, gqn_ref, q_out, Q_SCALE)
    norm_rope(wk_ref, gkn_ref, k_out, 1.0)
    v_out[...] = jnp.dot(h, wv_ref[...], preferred_element_type=F32).astype(v_out.dtype)

    gq_out[...] = (jnp.dot(h, wgq_ref[...], preferred_element_type=F32) * (GLA_DK ** -0.5)).astype(gq_out.dtype)
    gk_out[...] = jnp.dot(h, wgk_ref[...], preferred_element_type=F32).astype(gk_out.dtype)
    gv_out[...] = jnp.dot(h, wgv_ref[...], preferred_element_type=F32).astype(gv_out.dtype)
    r = jnp.dot(h, wgr_ref[...], preferred_element_type=F32)
    sr_out[...] = (r * _sigmoid(r)).astype(sr_out.dtype)

    g_lr = jnp.dot(h, wgg_ref[...], preferred_element_type=F32)
    pre = jnp.dot(g_lr.astype(BF16), wup_ref[...], preferred_element_type=F32) + bup_ref[...]
    la_out[...] = _log_sigmoid(pre) * (1.0 / GLA_TAU)

    gl = jnp.dot(h, wgate_ref[...], preferred_element_type=F32) + bgate_ref[...]
    gate_out[...] = _sigmoid(gl).astype(gate_out.dtype)


def _inproj(u, tm, cos, sin, p):
    rows = u.shape[0]
    n_tab = cos.shape[0] // tm
    row = lambda w: pl.BlockSpec((tm, w), lambda i: (i, 0))
    tab = pl.BlockSpec((tm, LANES), lambda i: (i % n_tab, 0))
    weights = [p['wq'], p['wk'], p['wv'], p['wgq'], p['wgk'], p['wgv'], p['wgr'], p['wgg'],
               p['wup'], p['bup'], p['wgate'], p['bgate']]
    out_widths = [(512, BF16), (512, BF16), (512, BF16), (256, BF16), (256, BF16), (512, BF16),
                  (512, BF16), (256, F32), (2048, BF16)]
    return pl.pallas_call(
        _inproj_body,
        grid=(rows // tm,),
        in_specs=[row(D_MODEL), _const_spec((1, D_MODEL)), tab, tab,
                  _const_spec((1, LANES)), _const_spec((1, LANES))]
                 + [_const_spec(w.shape) for w in weights],
        out_specs=[row(w) for w, _ in out_widths],
        out_shape=[jax.ShapeDtypeStruct((rows, w), dt) for w, dt in out_widths],
        compiler_params=_cparams(("parallel",)),
        name="inproj",
    )(u, p['gmix'], cos, sin, p['gqn'], p['gkn'], *weights)


def _diff_attn_body(bound_ref, lq1_ref, lk1_ref, lq2_ref, lk2_ref, gsub_ref,
                    q_ref, kf_ref, vf_ref, kx_ref, vx_ref, o_ref, vt_sc, qs_sc, s_sc, cmax_sc, m_sc, l_sc, acc_sc,
                    *, lam_init):
    qi = pl.program_id(1)
    tq = ATT_TILE
    n_kv = kx_ref.shape[0] // tq
    heads = range(DA_HEADS)
    hs = lambda h: slice(h * LANES, (h + 1) * LANES)

    @pl.when(qi == 0)
    def _():
        def tr(j, carry):
            off = pl.multiple_of(j * tq, tq)
            for h in heads:
                vt_sc[h, j] = jnp.transpose(vx_ref[pl.ds(off, tq), hs(h)].astype(F32)).astype(BF16)
            return carry

        lax.fori_loop(0, n_kv, tr, 0)

    d = lax.broadcasted_iota(jnp.int32, (LANES, 1), 0)
    for h in heads:
        qt = jnp.transpose(q_ref[:, hs(h)].astype(F32))
        qs_sc[h] = jnp.concatenate([jnp.where(d < DA_HEAD_DIM, qt, 0.0),
                                    jnp.where(d >= DA_HEAD_DIM, qt, 0.0)], axis=1).astype(BF16)

    meta = slice(FRONT - N_META, FRONT)
    key = lax.broadcasted_iota(jnp.int32, (tq, 2 * tq), 0)
    qry = lax.broadcasted_iota(jnp.int32, (tq, 2 * tq), 1) % tq
    causal = key <= qry
    bound = bound_ref[0]

    def meta_scores(h):
        s = jnp.dot(kf_ref[meta, hs(h)], qs_sc[h], preferred_element_type=F32)
        vt = jnp.transpose(vf_ref[meta, hs(h)].astype(F32)).astype(BF16)
        return s, vt

    def block_scores(j, h, diagonal):
        off = pl.multiple_of(j * tq, tq)
        s = jnp.dot(kx_ref[pl.ds(off, tq), hs(h)], qs_sc[h], preferred_element_type=F32)
        return jnp.where(causal, s, NEG_BIG) if diagonal else s

    @pl.when(bound <= ATT_SAFE_BOUND)
    def _():
        for h in heads:
            s, vt = meta_scores(h)
            pr = jnp.exp2(s - bound)
            l_sc[h] = jnp.sum(pr, axis=0, keepdims=True)
            acc_sc[h] = jnp.dot(vt, pr.astype(BF16), preferred_element_type=F32)

        def blocks(*js, diagonal_last=False):
            work = [(j, h, diagonal_last and j is js[-1]) for j in js for h in heads]
            ss = [block_scores(j, h, dg) for j, h, dg in work]
            ps = [jnp.exp2(s - bound) for s in ss]
            for (j, h, _), pr in zip(work, ps):
                l_sc[h] += jnp.sum(pr, axis=0, keepdims=True)
            for (j, h, _), pr in zip(work, ps):
                acc_sc[h] += jnp.dot(vt_sc[h, j], pr.astype(BF16), preferred_element_type=F32)

        def pair(u, carry):
            blocks(2 * u, 2 * u + 1)
            return carry

        lax.fori_loop(0, qi // 2, pair, 0)

        @pl.when(qi % 2 == 1)
        def _():
            blocks(qi - 1, qi, diagonal_last=True)

        @pl.when(qi % 2 == 0)
        def _():
            blocks(qi, diagonal_last=True)

    @pl.when(bound > ATT_SAFE_BOUND)
    def _():
        for h in heads:
            s, vt = meta_scores(h)
            m0 = jnp.max(s, axis=0, keepdims=True)
            pr = jnp.exp2(s - m0)
            m_sc[h] = m0
            l_sc[h] = jnp.sum(pr, axis=0, keepdims=True)
            acc_sc[h] = jnp.dot(vt, pr.astype(BF16), preferred_element_type=F32)

        def scores(j, slot, diagonal=False):
            for h in heads:
                s = block_scores(j, h, diagonal)
                s_sc[h, slot] = s
                cmax_sc[h, slot] = jnp.max(s, axis=0, keepdims=True)

        def accumulate(j, slot):
            for h in heads:
                m_old = m_sc[h]
                m_new = jnp.maximum(m_old, cmax_sc[h, slot])
                alpha = jnp.exp2(m_old - m_new)
                pr = jnp.exp2(s_sc[h, slot] - m_new)
                l_sc[h] = alpha * l_sc[h] + jnp.sum(pr, axis=0, keepdims=True)
                acc_sc[h] = alpha * acc_sc[h] + jnp.dot(vt_sc[h, j], pr.astype(BF16),
                                                        preferred_element_type=F32)
                m_sc[h] = m_new

        @pl.when(qi == 0)
        def _():
            scores(0, 0, diagonal=True)
            accumulate(0, 0)

        @pl.when(qi > 0)
        def _():
            scores(0, 0)

            def pair(u, carry):
                j = 2 * u
                scores(j + 1, 1)
                accumulate(j, 0)
                scores(j + 2, 0)
                accumulate(j + 1, 1)
                return carry

            lax.fori_loop(0, (qi - 1) // 2, pair, 0)

            @pl.when(qi % 2 == 1)
            def _():
                scores(qi, 1, diagonal=True)
                accumulate(qi - 1, 0)
                accumulate(qi, 1)

            @pl.when(qi % 2 == 0)
            def _():
                scores(qi - 1, 1)
                accumulate(qi - 2, 0)
                scores(qi, 0, diagonal=True)
                accumulate(qi - 1, 1)
                accumulate(qi, 0)

    lam = (jnp.exp(jnp.sum(lq1_ref[...] * lk1_ref[...], axis=-1, keepdims=True))
           - jnp.exp(jnp.sum(lq2_ref[...] * lk2_ref[...], axis=-1, keepdims=True)) + lam_init)
    for h in heads:
        acc = acc_sc[h]
        inv_l = 1.0 / l_sc[h]
        ot = acc[:, :tq] * inv_l[:, :tq] - lam * (acc[:, tq:] * inv_l[:, tq:])
        ms = jnp.mean(ot * ot, axis=0, keepdims=True)
        o = jnp.transpose(ot * lax.rsqrt(ms + EPS)) * gsub_ref[...] * (1.0 - lam_init)
        o_ref[:, hs(h)] = o.astype(o_ref.dtype)


def _diff_attn(bound, q, kf, vf, kx, vx, lam_vecs, gsub, lam_init):
    bsz, seq, _ = q.shape
    tq = ATT_TILE
    vec = _const_spec((1, DA_HEAD_DIM))
    width = DA_HEADS * LANES
    return pl.pallas_call(
        functools.partial(_diff_attn_body, lam_init=lam_init),
        grid=(bsz, seq // tq),
        in_specs=[pl.BlockSpec(memory_space=pltpu.SMEM), vec, vec, vec, vec, _const_spec((1, DA_V_DIM)),
                  pl.BlockSpec((None, tq, width), lambda b, i: (b, i, 0)),
                  _const_spec((FRONT, width)), _const_spec((FRONT, width)),
                  pl.BlockSpec((None, seq, width), lambda b, i: (b, 0, 0)),
                  pl.BlockSpec((None, seq, width), lambda b, i: (b, 0, 0))],
        out_specs=pl.BlockSpec((None, tq, width), lambda b, i: (b, i, 0)),
        out_shape=jax.ShapeDtypeStruct((bsz, seq, width), BF16),
        scratch_shapes=[pltpu.VMEM((DA_HEADS, seq // tq, DA_V_DIM, tq), BF16),
                        pltpu.VMEM((DA_HEADS, LANES, 2 * tq), BF16),
                        pltpu.VMEM((DA_HEADS, 2, tq, 2 * tq), F32), pltpu.VMEM((DA_HEADS, 2, 1, 2 * tq), F32),
                        pltpu.VMEM((DA_HEADS, 1, 2 * tq), F32), pltpu.VMEM((DA_HEADS, 1, 2 * tq), F32),
                        pltpu.VMEM((DA_HEADS, DA_V_DIM, 2 * tq), F32)],
        compiler_params=_cparams(("parallel", "arbitrary")),
        name="diff_attn",
    )(bound, *lam_vecs, gsub, q, kf, vf, kx, vx)


def _split3(a):
    a1 = a.astype(BF16)
    r1 = a - a1.astype(F32)
    a2 = r1.astype(BF16)
    a3 = (r1 - a2.astype(F32)).astype(BF16)
    return a1, a2, a3


def _gla_body(gn_ref, kf_ref, vf_ref, laf_ref, q_ref, k_ref, v_ref, la_ref, sr_ref, o_ref, st_sc, b_sc):
    g = pl.program_id(1)
    c = GLA_CHUNK
    kw = GLA_HEADS * GLA_DK
    vw = GLA_HEADS * GLA_DV

    ti = lax.broadcasted_iota(jnp.int32, (c, c), 0)
    si = lax.broadcasted_iota(jnp.int32, (c, c), 1)
    tri = (si <= ti).astype(BF16)
    hv = lax.broadcasted_iota(jnp.int32, (vw, kw), 0) // GLA_DV
    hk = lax.broadcasted_iota(jnp.int32, (vw, kw), 1) // GLA_DK
    head_mask = hv == hk
    lane_head = lax.broadcasted_iota(jnp.int32, (1, kw), 1) // GLA_DK
    causal = lax.broadcasted_iota(jnp.int32, (GLA_HEADS * c, c), 0) % c >= \
        lax.broadcasted_iota(jnp.int32, (GLA_HEADS * c, c), 1)

    def cumsum(a):
        a1, a2, a3 = _split3(a)
        return (jnp.dot(tri, a1, preferred_element_type=F32)
                + jnp.dot(tri, a2, preferred_element_type=F32)
                + jnp.dot(tri, a3, preferred_element_type=F32))

    def update_state(k, v, b):
        b_last = b[c - 1:c, :]
        kd = (k * jnp.exp(b_last - b)).astype(BF16)
        upd = lax.dot_general(v, kd, (((0,), (0,)), ((), ())), preferred_element_type=F32)
        st_sc[...] = jnp.where(head_mask, jnp.exp(b_last) * st_sc[...] + upd, 0.0)

    @pl.when(g == 0)
    def _():
        st_sc[...] = jnp.zeros(st_sc.shape, F32)
        update_state(kf_ref[...].astype(F32), vf_ref[...], cumsum(laf_ref[...]))

    gn = gn_ref[...]

    def finish(o, sr):
        outs = []
        for hh in range(GLA_HEADS):
            cs = slice(hh * GLA_DV, (hh + 1) * GLA_DV)
            oh = o[:, cs]
            ms = jnp.mean(oh * oh, axis=-1, keepdims=True)
            outs.append(oh * lax.rsqrt(ms + EPS) * gn * sr[:, cs])
        return jnp.concatenate(outs, axis=1)

    n_chunks = GLA_GROUP // c
    for ci in range(n_chunks):
        rs = slice(ci * c, (ci + 1) * c)
        b_sc[rs, :] = cumsum(la_ref[rs, :])
    steep = jnp.min(b_sc[...]) < -GLA_SAFE_DECAY

    @pl.when(jnp.logical_not(steep))
    def _():
        for ci in range(n_chunks):
            rs = slice(ci * c, (ci + 1) * c)
            b = b_sc[rs, :]
            q = q_ref[rs, :].astype(F32)
            k = k_ref[rs, :].astype(F32)
            v = v_ref[rs, :]
            qe = q * jnp.exp(b)
            ke = (k * jnp.exp(-b)).astype(BF16)
            inter = lax.dot_general(qe.astype(BF16), st_sc[...].astype(BF16), (((1,), (1,)), ((), ())),
                                    preferred_element_type=F32)
            qstack = jnp.concatenate([jnp.where(lane_head == hh, qe, 0.0) for hh in range(GLA_HEADS)],
                                     axis=0).astype(BF16)
            sc = lax.dot_general(qstack, ke, (((1,), (1,)), ((), ())), preferred_element_type=F32)
            sc = jnp.where(causal, sc, 0.0).astype(BF16)
            pv = jnp.dot(sc, v, preferred_element_type=F32)
            o = jnp.concatenate([inter[:, hh * GLA_DV:(hh + 1) * GLA_DV]
                                 + pv[hh * c:(hh + 1) * c, hh * GLA_DV:(hh + 1) * GLA_DV]
                                 for hh in range(GLA_HEADS)], axis=1)
            o_ref[rs, :] = finish(o, sr_ref[rs, :].astype(F32)).astype(o_ref.dtype)
            update_state(k, v, b)

    @pl.when(steep)
    def _():
        rows = 16

        def tile(i, carry):
            off = pl.multiple_of(i * rows, rows)
            a = la_ref[pl.ds(off, rows), :]
            q = q_ref[pl.ds(off, rows), :].astype(F32)
            k = k_ref[pl.ds(off, rows), :].astype(F32)
            v = v_ref[pl.ds(off, rows), :].astype(F32)
            outs = []
            for r in range(rows):
                row = lambda x: x[r:r + 1].astype(BF16)
                upd = lax.dot_general(row(v), row(k), (((0,), (0,)), ((), ())),
                                      preferred_element_type=F32)
                st = jnp.where(head_mask, jnp.exp(a[r:r + 1]) * st_sc[...] + upd, 0.0)
                st_sc[...] = st
                outs.append(lax.dot_general(row(q), st.astype(BF16), (((1,), (1,)), ((), ())),
                                            preferred_element_type=F32))
            o = jnp.concatenate(outs, axis=0)
            o_ref[pl.ds(off, rows), :] = finish(o, sr_ref[pl.ds(off, rows), :].astype(F32)).astype(o_ref.dtype)
            return carry

        lax.fori_loop(0, GLA_GROUP // rows, tile, 0)


def _gla(kf, vf, laf, q, k, v, la, sr, gn):
    bsz, seq, _ = q.shape
    t = GLA_GROUP
    kw = GLA_HEADS * GLA_DK
    vw = GLA_HEADS * GLA_DV
    last = FRONT // GLA_CHUNK - 1
    fr = lambda w: pl.BlockSpec((GLA_CHUNK, w), lambda b, g: (last, 0))
    xs = lambda w: pl.BlockSpec((None, t, w), lambda b, g: (b, g, 0))
    return pl.pallas_call(
        _gla_body,
        grid=(bsz, seq // t),
        in_specs=[_const_spec((1, GLA_DV)), fr(kw), fr(vw), fr(kw), xs(kw), xs(kw), xs(vw), xs(kw), xs(vw)],
        out_specs=xs(vw),
        out_shape=jax.ShapeDtypeStruct((bsz, seq, vw), BF16),
        scratch_shapes=[pltpu.VMEM((vw, kw), F32), pltpu.VMEM((t, kw), F32)],
        compiler_params=_cparams(("parallel", "arbitrary")),
        name="gla",
    )(gn, kf, vf, laf, q, k, v, la, sr)


def _merge_body(oa_ref, ob_ref, gate_ref, u_ref, wb0_ref, wb1_ref, wout_ref, gffn_ref, wr_ref, br_ref,
                u1_out, h2_out, info_out, cnt_out, cnt_sc, *, part_steps):
    i = pl.program_id(0)
    tm = ROW_TILE

    @pl.when(i % part_steps == 0)
    def _():
        cnt_sc[...] = jnp.zeros(cnt_sc.shape, F32)

    ya = jnp.dot(oa_ref[...], wb0_ref[...], preferred_element_type=F32)
    yb = jnp.dot(ob_ref[...], wb1_ref[...], preferred_element_type=F32)
    gate = gate_ref[...].astype(F32)
    merged = gate[:, :D_MODEL] * ya + gate[:, D_MODEL:] * yb
    u1 = u_ref[...] + jnp.dot(merged.astype(BF16), wout_ref[...], preferred_element_type=F32)
    u1_out[...] = u1
    ms = jnp.mean(u1 * u1, axis=-1, keepdims=True)
    h2f = u1 * lax.rsqrt(ms + EPS) * gffn_ref[...]
    _store_row_tiles(h2_out, h2f)
    h2 = h2f.astype(BF16)

    logits = lax.dot_general(wr_ref[...], h2, (((1,), (1,)), ((), ())), preferred_element_type=F32) + br_ref[...]
    row = lax.broadcasted_iota(jnp.int32, (ROUTER_ROWS, tm), 0)
    is_group = row < N_GROUPS
    gl = jnp.where(is_group, logits, NEG_BIG)
    gmax = jnp.max(gl, axis=0, keepdims=True)
    g_idx = jnp.min(jnp.where(gl == gmax, row, ROUTER_ROWS), axis=0, keepdims=True)
    g_w = 1.0 / jnp.sum(jnp.where(is_group, jnp.exp(gl - gmax), 0.0), axis=0, keepdims=True)
    lo = N_GROUPS + EXPERTS_PER_GROUP * g_idx
    el = jnp.where((row >= lo) & (row < lo + EXPERTS_PER_GROUP), logits, NEG_BIG)
    v1 = jnp.max(el, axis=0, keepdims=True)
    i1 = jnp.min(jnp.where(el == v1, row, ROUTER_ROWS), axis=0, keepdims=True)
    el2 = jnp.where(row == i1, NEG_BIG, el)
    v2 = jnp.max(el2, axis=0, keepdims=True)
    i2 = jnp.min(jnp.where(el2 == v2, row, ROUTER_ROWS), axis=0, keepdims=True)
    e21 = jnp.exp(v2 - v1)
    w1 = g_w / (1.0 + e21)
    w2 = w1 * e21

    onehot = ((row == i1) | (row == i2)).astype(BF16)
    si = lax.broadcasted_iota(jnp.int32, (tm, tm), 0)
    ti = lax.broadcasted_iota(jnp.int32, (tm, tm), 1)
    earlier = (si < ti).astype(BF16)
    cnt = cnt_sc[...]
    before = jnp.dot(onehot, earlier, preferred_element_type=F32) + jnp.tile(cnt, (1, tm // LANES))
    r1 = jnp.sum(jnp.where(row == i1, before, 0.0), axis=0, keepdims=True)
    r2 = jnp.sum(jnp.where(row == i2, before, 0.0), axis=0, keepdims=True)
    cnt = cnt + jnp.dot(onehot, jnp.ones((tm, LANES), BF16), preferred_element_type=F32)
    cnt_sc[...] = cnt
    cnt_out[...] = cnt

    zero = jnp.zeros_like(w1)
    info_out[...] = jnp.concatenate([(i1 - N_GROUPS).astype(F32), (i2 - N_GROUPS).astype(F32),
                                     w1, w2, r1, r2, zero, zero], axis=0)


def _merge(oa, ob, gates, u, p, n_parts):
    n = u.shape[0]
    tm = ROW_TILE
    part_steps = n // tm // n_parts
    row = lambda w: pl.BlockSpec((tm, w), lambda i: (i, 0))
    return pl.pallas_call(
        functools.partial(_merge_body, part_steps=part_steps),
        grid=(n // tm,),
        in_specs=[row(512), row(512), row(2 * D_MODEL), row(D_MODEL),
                  _const_spec((512, D_MODEL)), _const_spec((512, D_MODEL)), _const_spec((D_MODEL, D_MODEL)),
                  _const_spec((1, D_MODEL)), _const_spec((ROUTER_ROWS, D_MODEL)), _const_spec((ROUTER_ROWS, tm))],
        out_specs=[row(D_MODEL), _row_tile_spec(tm, lambda i: (i, 0)), pl.BlockSpec((8, tm), lambda i: (0, i)),
                   pl.BlockSpec((None, ROUTER_ROWS, LANES), lambda i: (i // part_steps, 0, 0))],
        out_shape=[jax.ShapeDtypeStruct((n, D_MODEL), F32), jax.ShapeDtypeStruct((n * RT, LANES), F32),
                   jax.ShapeDtypeStruct((8, n), F32), jax.ShapeDtypeStruct((n_parts, ROUTER_ROWS, LANES), F32)],
        scratch_shapes=[pltpu.VMEM((ROUTER_ROWS, LANES), F32)],
        compiler_params=_cparams(("arbitrary",)),
        name="merge_router",
    )(oa, ob, gates, u, p['wb0'], p['wb1'], p['wout'], p['gffn'], p['wr'], p['br'])


def _dispatch_issue(i, tail_ref, nb_ref, dest_ref, h2_ref, xs_hbm, zero_sc, sem, zsem):
    blk_rows = MOE_TILE * RT
    n_blocks = xs_hbm.shape[0] // blk_rows

    def zero_copy(blk):
        dst = xs_hbm.at[pl.ds(pl.multiple_of(blk * blk_rows, blk_rows), blk_rows)]
        return pltpu.make_async_copy(zero_sc, dst, zsem)

    @pl.when(i == 0)
    def _():
        zero_sc[...] = jnp.zeros(zero_sc.shape, F32)

        def tails(fn):
            def body(e, carry):
                @pl.when(tail_ref[e] >= 0)
                def _():
                    fn(zero_copy(tail_ref[e]))
                return carry
            lax.fori_loop(0, N_EXPERTS, body, 0)

        def unused(fn):
            def body(b, carry):
                fn(zero_copy(b))
                return carry
            lax.fori_loop(nb_ref[0], n_blocks, body, 0)

        tails(lambda cp: cp.start())
        unused(lambda cp: cp.start())
        tails(lambda cp: cp.wait())
        unused(lambda cp: cp.wait())

    def start(r, carry):
        src = _token_rows(h2_ref, r)
        pltpu.make_async_copy(src, _token_rows(xs_hbm, dest_ref[0, 0, r]), sem).start()
        pltpu.make_async_copy(src, _token_rows(xs_hbm, dest_ref[0, 1, r]), sem).start()
        return carry

    lax.fori_loop(0, DMA_TILE, start, 0)


def _dispatch_wait(h2_ref, xs_hbm, sem):
    for _ in range(2):
        pltpu.make_async_copy(h2_ref, xs_hbm.at[pl.ds(0, DMA_TILE * RT)], sem).wait()


def _experts_step(i, n_blocks, be_ref, nb_ref, x_ref, wg_ref, wu_ref, wd_ref, y_ref, wg_sc, wu_sc, wd_sc):
    last = n_blocks - 1
    fresh = (i == 0) | ((i <= last) & (be_ref[jnp.minimum(i, last)] != be_ref[jnp.clip(i - 1, 0, last)]))

    @pl.when(fresh)
    def _():
        wg_sc[...] = wg_ref[...].astype(BF16)
        wu_sc[...] = wu_ref[...].astype(BF16)
        wd_sc[...] = wd_ref[...].astype(BF16)

    @pl.when(i < nb_ref[0])
    def _():
        x = _load_row_tiles(x_ref, MOE_TILE).astype(BF16)
        y = jnp.zeros((MOE_TILE, D_MODEL), F32)
        for j in range(D_EXPERT // EXPERT_CHUNK):
            cs = slice(j * EXPERT_CHUNK, (j + 1) * EXPERT_CHUNK)
            gp = jnp.dot(x, wg_sc[:, cs], preferred_element_type=F32)
            up = jnp.dot(x, wu_sc[:, cs], preferred_element_type=F32)
            hid = (gp * _sigmoid(gp) * up).astype(BF16)
            y = y + jnp.dot(hid, wd_sc[cs, :], preferred_element_type=F32)
        _store_row_tiles(y_ref, y)

    @pl.when((i >= nb_ref[0]) & (i <= last))
    def _():
        y_ref[...] = jnp.zeros(y_ref.shape, F32)


def _combine_issue(i, n_steps, dest_ref, dest_next_ref, ys_hbm, ybuf, sems):
    t = DMA_TILE

    def gather(d_ref, slot):
        buf = ybuf.at[slot]

        def start(r, carry):
            pltpu.make_async_copy(_token_rows(ys_hbm, d_ref[0, 0, r]), _token_rows(buf, r),
                                  sems.at[slot]).start()
            pltpu.make_async_copy(_token_rows(ys_hbm, d_ref[0, 1, r]), _token_rows(buf, t + r),
                                  sems.at[slot]).start()
            return carry

        lax.fori_loop(0, t, start, 0)

    @pl.when(i == 0)
    def _():
        gather(dest_ref, 0)

    @pl.when(i + 1 < n_steps)
    def _():
        gather(dest_next_ref, 1 - i % 2)


def _combine_finish(i, w_ref, u1_ref, ys_hbm, o_ref, ybuf, sems):
    t = DMA_TILE
    slot = i % 2
    buf = ybuf.at[slot]
    pltpu.make_async_copy(ys_hbm.at[pl.ds(0, buf.shape[0])], buf, sems.at[slot]).wait()
    w = w_ref[...]
    o_ref[...] = (u1_ref[...] + w[:, 0:1] * _load_row_tiles(buf, t) + w[:, 1:2] * _load_row_tiles(buf, t, t))


def _moe_stage(name, experts=None, dispatch=None, combine=None):
    none = jnp.zeros((1,), jnp.int32)
    prefetch = [experts['block_e'] if experts else none, experts['n_used'] if experts else none,
                dispatch['tail_blocks'] if dispatch else none, dispatch['n_used'] if dispatch else none]
    inputs, in_specs, out_shapes, out_specs, scratch, aliases, steps = [], [], [], [], [], {}, []
    smem_tile = lambda fn: pl.BlockSpec((1, 2, DMA_TILE), fn, memory_space=pltpu.SMEM)

    if experts:
        n_blocks = experts['xs'].shape[0] // (MOE_TILE * RT)
        steps.append(n_blocks)
        blk = lambda i: jnp.minimum(i, n_blocks - 1)
        wmap = lambda i, be, nb, tb, nd: (be[blk(i)], 0, 0)
        inputs += [experts['xs'], experts['wg'], experts['wu'], experts['wd']]
        in_specs += [_row_tile_spec(MOE_TILE, lambda i, be, nb, tb, nd: (jnp.minimum(blk(i), nb[0] - 1), 0)),
                     pl.BlockSpec((None, D_MODEL, D_EXPERT), wmap),
                     pl.BlockSpec((None, D_MODEL, D_EXPERT), wmap),
                     pl.BlockSpec((None, D_EXPERT, D_MODEL), wmap)]
        out_shapes.append(jax.ShapeDtypeStruct(experts['xs'].shape, F32))
        out_specs.append(_row_tile_spec(MOE_TILE, lambda i, be, nb, tb, nd: (blk(i), 0)))
        scratch += [pltpu.VMEM((D_MODEL, D_EXPERT), BF16), pltpu.VMEM((D_MODEL, D_EXPERT), BF16),
                    pltpu.VMEM((D_EXPERT, D_MODEL), BF16)]
    if dispatch:
        d_tiles = dispatch['dest'].shape[0]
        d_row0 = dispatch['row0'] // DMA_TILE
        steps.append(d_tiles)
        dmap = lambda i: jnp.minimum(i, d_tiles - 1)
        inputs += [dispatch['dest'], dispatch['h2']]
        in_specs += [smem_tile(lambda i, be, nb, tb, nd: (dmap(i), 0, 0)),
                     _row_tile_spec(DMA_TILE, lambda i, be, nb, tb, nd: (d_row0 + dmap(i), 0))]
        out_shapes.append(jax.ShapeDtypeStruct((dispatch['n_slots'] * RT, LANES), F32))
        out_specs.append(pl.BlockSpec(memory_space=pl.ANY))
        scratch += [pltpu.VMEM((MOE_TILE * RT, LANES), F32), pltpu.SemaphoreType.DMA(()),
                    pltpu.SemaphoreType.DMA(())]
    if combine:
        c_tiles = combine['dest'].shape[0]
        c_row0 = combine['row0'] // DMA_TILE
        steps.append(c_tiles)
        cmap = lambda i: jnp.minimum(i, c_tiles - 1)
        tok_tile = lambda i, be, nb, tb, nd: (c_row0 + cmap(i), 0)
        aliases[len(prefetch) + len(inputs) + 3] = len(out_shapes)
        inputs += [combine['dest'], combine['dest'], combine['w'], combine['u1'], combine['ys']]
        in_specs += [smem_tile(lambda i, be, nb, tb, nd: (cmap(i), 0, 0)),
                     smem_tile(lambda i, be, nb, tb, nd: (cmap(i + 1), 0, 0)),
                     pl.BlockSpec((DMA_TILE, 2), tok_tile), pl.BlockSpec((DMA_TILE, D_MODEL), tok_tile),
                     pl.BlockSpec(memory_space=pl.ANY)]
        out_shapes.append(jax.ShapeDtypeStruct(combine['u1'].shape, F32))
        out_specs.append(pl.BlockSpec((DMA_TILE, D_MODEL), tok_tile))
        scratch += [pltpu.VMEM((2, 2 * DMA_TILE * RT, LANES), F32), pltpu.SemaphoreType.DMA((2,))]

    n_in, n_out = len(inputs), len(out_shapes)

    def body(*refs):
        i = pl.program_id(0)
        be_ref, nbe_ref, tail_ref, nbd_ref = refs[:4]
        ins, outs, scr = list(refs[4:4 + n_in]), list(refs[4 + n_in:4 + n_in + n_out]), list(refs[4 + n_in + n_out:])
        e_args = d_args = c_args = None
        if experts:
            e_args = ins[:4] + [outs.pop(0)] + scr[:3]
            ins, scr = ins[4:], scr[3:]
        if dispatch:
            d_args = ins[:2] + [outs.pop(0)] + scr[:3]
            ins, scr = ins[2:], scr[3:]
        if combine:
            c_args = ins[:5] + [outs.pop(0)] + scr[:2]

        if dispatch:
            ddest, h2_ref, xs_hbm, zero_sc, sem, zsem = d_args

            @pl.when(i < d_tiles)
            def _():
                _dispatch_issue(i, tail_ref, nbd_ref, ddest, h2_ref, xs_hbm, zero_sc, sem, zsem)
        if combine:
            cdest, cnext, w_ref, u1_ref, ys_hbm, o_ref, ybuf, sems = c_args

            @pl.when(i < c_tiles)
            def _():
                _combine_issue(i, c_tiles, cdest, cnext, ys_hbm, ybuf, sems)
        if experts:
            x_ref, wg_ref, wu_ref, wd_ref, y_ref, wg_sc, wu_sc, wd_sc = e_args
            _experts_step(i, n_blocks, be_ref, nbe_ref, x_ref, wg_ref, wu_ref, wd_ref, y_ref, wg_sc, wu_sc, wd_sc)
        if combine:
            @pl.when(i < c_tiles)
            def _():
                _combine_finish(i, w_ref, u1_ref, ys_hbm, o_ref, ybuf, sems)
        if dispatch:
            @pl.when(i < d_tiles)
            def _():
                _dispatch_wait(h2_ref, xs_hbm, sem)

    grid_spec = pltpu.PrefetchScalarGridSpec(
        num_scalar_prefetch=len(prefetch), grid=(max(steps),),
        in_specs=in_specs, out_specs=out_specs, scratch_shapes=scratch)
    outs = pl.pallas_call(
        body, grid_spec=grid_spec, out_shape=out_shapes, input_output_aliases=aliases,
        compiler_params=_cparams(("arbitrary",)), name=name,
    )(*prefetch, *inputs)
    return outs


def _rope_tables(pos):
    half = DA_HEAD_DIM // 2
    inv_freq = jnp.power(ROPE_THETA, -jnp.arange(half, dtype=F32) * 2.0 / DA_HEAD_DIM)
    ang = pos[:, None] * inv_freq[None, :]
    cos, sin = jnp.cos(ang), jnp.sin(ang)
    cos_t = jnp.tile(cos, (1, LANES // half))
    sin_t = jnp.tile(jnp.concatenate([-sin, sin], axis=1), (1, LANES // DA_HEAD_DIM))
    return cos_t, sin_t


def _layer(x, meta_tokens, l, g_mix_norm, w_in, g_q_norm, g_k_norm, lambda_q1, lambda_k1, lambda_q2, lambda_k2,
           g_diff_subln, w_gla_gate_up, b_gla_gate, g_gla_norm, w_branch, b_merge_gate, w_out, g_ffn_norm,
           w_router_group, b_router_group, w_router_expert, b_router_expert, w_exp_gate, w_exp_up, w_exp_down):
    bsz, seq, _ = x.shape
    n = bsz * seq

    wi = w_in[l]
    offs = [0]
    for s in (512, 512, 512, 256, 256, 512, 512, GLA_RANK, 2 * D_MODEL):
        offs.append(offs[-1] + s)
    sec = lambda j: wi[:, offs[j]:offs[j + 1]].astype(BF16)
    p = {
        'gmix': g_mix_norm[l][None, :],
        'gqn': jnp.tile(g_q_norm[l], LANES // DA_HEAD_DIM)[None, :],
        'gkn': jnp.tile(g_k_norm[l], LANES // DA_HEAD_DIM)[None, :],
        'wq': sec(0), 'wk': sec(1), 'wv': sec(2), 'wgq': sec(3), 'wgk': sec(4), 'wgv': sec(5), 'wgr': sec(6),
        'wgg': jnp.pad(sec(7), ((0, 0), (0, LANES - GLA_RANK))),
        'wup': jnp.pad(w_gla_gate_up[l].astype(BF16), ((0, LANES - GLA_RANK), (0, 0))),
        'bup': b_gla_gate[l][None, :],
        'wgate': sec(8),
        'bgate': b_merge_gate[l].reshape(1, 2 * D_MODEL),
        'wb0': w_branch[l, 0].astype(BF16), 'wb1': w_branch[l, 1].astype(BF16),
        'wout': w_out[l].astype(BF16),
        'gffn': g_ffn_norm[l][None, :],
        'wr': jnp.pad(jnp.concatenate([w_router_group[l], w_router_expert[l].reshape(D_MODEL, N_EXPERTS)],
                                      axis=1).T.astype(BF16), ((0, ROUTER_ROWS - N_GROUPS - N_EXPERTS), (0, 0))),
        'br': jnp.broadcast_to(
            jnp.pad(jnp.concatenate([b_router_group[l], b_router_expert[l].reshape(N_EXPERTS)]),
                    (0, ROUTER_ROWS - N_GROUPS - N_EXPERTS))[:, None], (ROUTER_ROWS, ROW_TILE)),
    }

    u_front = jnp.concatenate([jnp.zeros((FRONT - N_META, D_MODEL), F32), meta_tokens.astype(F32)], axis=0)
    cos_f, sin_f = _rope_tables(jnp.arange(FRONT, dtype=F32) - (FRONT - N_META))
    cos_x, sin_x = _rope_tables(jnp.arange(seq, dtype=F32) + N_META)
    front = _inproj(u_front, FRONT, cos_f, sin_f, p)
    xin = _inproj(x.reshape(n, D_MODEL), ROW_TILE, cos_x, sin_x, p)
    q, k, v, gq, gk, gv, sr, la, gates = [a.reshape(bsz, seq, a.shape[-1]) for a in xin]
    _, kf, vf, _, gkf, gvf, _, laf, _ = front

    lam_init = 0.8 - 0.6 * math.exp(-0.3 * l)
    lam_vecs = [a[l][None, :] for a in (lambda_q1, lambda_k1, lambda_q2, lambda_k2)]
    score_bound = (ATT_BOUND_MARGIN * DA_HEAD_DIM * Q_SCALE
                   * jnp.max(jnp.abs(g_q_norm[l])) * jnp.max(jnp.abs(g_k_norm[l]))).reshape(1).astype(F32)
    o_a = _diff_attn(score_bound, q, kf, vf, k, v, lam_vecs, g_diff_subln[l][None, :], lam_init)
    o_b = _gla(gkf, gvf, laf, gq, gk, gv, la, sr, g_gla_norm[l][None, :])

    n_parts = 2 if n % (2 * max(ROW_TILE, MOE_TILE)) == 0 else 1
    m = n // n_parts
    u1, h2, info, cnt = _merge(o_a.reshape(n, -1), o_b.reshape(n, -1), gates.reshape(n, -1),
                               x.reshape(n, D_MODEL), p, n_parts)
    wts = info[2:4].T
    expert = jnp.arange(N_EXPERTS, dtype=jnp.int32)
    n_slots = (2 * m // MOE_TILE + N_EXPERTS) * MOE_TILE
    n_blocks = n_slots // MOE_TILE

    def tables(part):
        cols = slice(part * m, (part + 1) * m)
        ids = info[0:2, cols].astype(jnp.int32)
        rank = info[4:6, cols].astype(jnp.int32)
        counts = cnt[part, N_GROUPS:N_GROUPS + N_EXPERTS, 0].astype(jnp.int32)
        padded = (counts + MOE_TILE - 1) // MOE_TILE * MOE_TILE
        pends = jnp.cumsum(padded)
        pstarts = pends - padded
        dest = jnp.sum(jnp.where(ids[..., None] == expert, pstarts, 0), axis=-1) + rank
        n_used = (pends[-1] // MOE_TILE).astype(jnp.int32)
        blk = jnp.minimum(jnp.arange(n_blocks, dtype=jnp.int32), n_used - 1) * MOE_TILE
        block_e = jnp.minimum(jnp.sum(pends[None, :] <= blk[:, None], axis=1), N_EXPERTS - 1).astype(jnp.int32)
        return dict(dest=dest.reshape(2, m // DMA_TILE, DMA_TILE).transpose(1, 0, 2), n_used=n_used[None],
                    block_e=block_e, tail_blocks=jnp.where(counts > 0, pends // MOE_TILE - 1, -1).astype(jnp.int32))

    tabs = [tables(part) for part in range(n_parts)]
    dispatch_of = lambda part: dict(tail_blocks=tabs[part]['tail_blocks'], n_used=tabs[part]['n_used'],
                                    dest=tabs[part]['dest'], h2=h2, row0=part * m, n_slots=n_slots)
    xs = _moe_stage("dispatch", dispatch=dispatch_of(0))[0]
    ys_prev = None
    for part in range(n_parts):
        experts = dict(block_e=tabs[part]['block_e'], n_used=tabs[part]['n_used'], xs=xs,
                       wg=w_exp_gate[l], wu=w_exp_up[l], wd=w_exp_down[l])
        nxt = dispatch_of(part + 1) if part + 1 < n_parts else None
        prev = (dict(dest=tabs[part - 1]['dest'], w=wts, u1=u1, row0=(part - 1) * m, ys=ys_prev)
                if part > 0 else None)
        outs = list(_moe_stage("experts", experts=experts, dispatch=nxt, combine=prev))
        ys_prev = outs.pop(0)
        if nxt:
            xs = outs.pop(0)
        if prev:
            u1 = outs.pop(0)
    u1 = _moe_stage("combine", combine=dict(dest=tabs[-1]['dest'], w=wts, u1=u1, row0=(n_parts - 1) * m,
                                            ys=ys_prev))[0]
    return u1.reshape(bsz, seq, D_MODEL)


def kernel(x, meta_tokens, g_mix_norm, w_in, g_q_norm, g_k_norm, lambda_q1, lambda_k1, lambda_q2, lambda_k2,
           g_diff_subln, w_gla_gate_up, b_gla_gate, g_gla_norm, w_branch, b_merge_gate, w_out, g_ffn_norm,
           w_router_group, b_router_group, w_router_expert, b_router_expert, w_exp_gate, w_exp_up, w_exp_down):
    depth = w_in.shape[0]
    assert depth == 1, "meta tokens are only carried through a single layer in this implementation"
    assert x.shape[1] % ROW_TILE == 0 and x.shape[2] == D_MODEL
    return _layer(x, meta_tokens, 0, g_mix_norm, w_in, g_q_norm, g_k_norm, lambda_q1, lambda_k1, lambda_q2,
                  lambda_k2, g_diff_subln, w_gla_gate_up, b_gla_gate, g_gla_norm, w_branch, b_merge_gate, w_out,
                  g_ffn_norm, w_router_group, b_router_group, w_router_expert, b_router_expert,
                  w_exp_gate, w_exp_up, w_exp_down)
```

```python
import functools
import math

import jax
import jax.numpy as jnp
from jax import lax
from jax.experimental import pallas as pl
from jax.experimental.pallas import tpu as pltpu

F32 = jnp.float32
BF16 = jnp.bfloat16

D_MODEL = 1024
N_META = 16
EPS = 1e-6
ROPE_THETA = 10000.0

DA_HEADS = 4
DA_HEAD_DIM = 64
DA_V_DIM = 128
Q_SCALE = DA_HEAD_DIM ** -0.5 * math.log2(math.e)
ATT_BOUND_MARGIN = 1.02
ATT_SAFE_BOUND = 60.0
GLA_HEADS = 4
GLA_DK = 64
GLA_DV = 128
GLA_RANK = 16
GLA_TAU = 16.0
GLA_CHUNK = 64
GLA_SAFE_DECAY = 60.0
N_GROUPS = 4
EXPERTS_PER_GROUP = 8
N_EXPERTS = 32
D_EXPERT = 512
ROUTER_ROWS = 48

LANES = 128
FRONT = 256
ATT_TILE = 256
ROW_TILE = 512
GLA_GROUP = 512
GLA_BATCH = 4
MOE_TILE = 512
EXPERT_CHUNK = 256
DMA_TILE = 256
NEG_BIG = -1e30
VMEM_LIMIT = 56 * 1024 * 1024


def _cparams(sem):
    return pltpu.CompilerParams(dimension_semantics=sem, vmem_limit_bytes=VMEM_LIMIT)


def _const_spec(shape):
    nd = len(shape)
    return pl.BlockSpec(shape, lambda *_: (0,) * nd)


RT = D_MODEL // LANES


def _row_tile_spec(rows, index_map):
    return pl.BlockSpec((rows * RT, LANES), index_map)


def _token_rows(ref, tok):
    return ref.at[pl.ds(pl.multiple_of(tok * RT, RT), RT)]


def _load_row_tiles(ref, rows, first=0):
    return jnp.concatenate([ref[pl.ds(first * RT + c, rows, stride=RT), :] for c in range(RT)], axis=1)


def _store_row_tiles(ref, val):
    for c in range(RT):
        ref[pl.ds(c, val.shape[0], stride=RT), :] = val[:, c * LANES:(c + 1) * LANES]


def _sigmoid(x):
    return 0.5 * jnp.tanh(0.5 * x) + 0.5


def _log_sigmoid(x):
    return jnp.minimum(x, 0.0) - jnp.log1p(jnp.exp(-jnp.abs(x)))


def _inproj_body(u_ref, gmix_ref, cos_ref, sin_ref, gqn_ref, gkn_ref,
                 wq_ref, wk_ref, wv_ref, wgq_ref, wgk_ref, wgv_ref, wgr_ref, wgg_ref,
                 wup_ref, bup_ref,
                 q_out, k_out, v_out, gq_out, gk_out, gv_out, sr_out, la_out):
    x = u_ref[...]
    ms = jnp.mean(x * x, axis=-1, keepdims=True)
    h = (x * lax.rsqrt(ms + EPS) * gmix_ref[...]).astype(BF16)

    cos = cos_ref[...]
    sin = sin_ref[...]
    lane = lax.broadcasted_iota(jnp.int32, (1, LANES), 1)
    first_half = (lane % DA_HEAD_DIM) < (DA_HEAD_DIM // 2)
    gi = lax.broadcasted_iota(jnp.int32, (LANES, LANES), 0) // DA_HEAD_DIM
    gj = lax.broadcasted_iota(jnp.int32, (LANES, LANES), 1) // DA_HEAD_DIM
    group_sum = (gi == gj).astype(BF16)

    def norm_rope(w_ref, gain_ref, out_ref, scale):
        z = jnp.dot(h, w_ref[...], preferred_element_type=F32)
        for hh in range(DA_HEADS):
            zh = z[:, hh * LANES:(hh + 1) * LANES]
            ssq = jnp.dot((zh * zh).astype(BF16), group_sum, preferred_element_type=F32)
            zn = zh * lax.rsqrt(ssq * (1.0 / DA_HEAD_DIM) + EPS) * gain_ref[...]
            rot = jnp.where(first_half,
                            pltpu.roll(zn, LANES - DA_HEAD_DIM // 2, 1),
                            pltpu.roll(zn, DA_HEAD_DIM // 2, 1))
            zr = zn * cos + rot * sin
            out_ref[:, hh * LANES:(hh + 1) * LANES] = (zr * scale).astype(out_ref.dtype)

    norm_rope(wq_ref, gqn_ref, q_out, Q_SCALE)
    norm_rope(wk_ref, gkn_ref, k_out, 1.0)
    v_out[...] = jnp.dot(h, wv_ref[...], preferred_element_type=F32).astype(v_out.dtype)

    gq_out[...] = (jnp.dot(h, wgq_ref[...], preferred_element_type=F32) * (GLA_DK ** -0.5)).astype(gq_out.dtype)
    gk_out[...] = jnp.dot(h, wgk_ref[...], preferred_element_type=F32).astype(gk_out.dtype)
    gv_out[...] = jnp.dot(h, wgv_ref[...], preferred_element_type=F32).astype(gv_out.dtype)
    r = jnp.dot(h, wgr_ref[...], preferred_element_type=F32)
    sr_out[...] = (r * _sigmoid(r)).astype(sr_out.dtype)

    g_lr = jnp.dot(h, wgg_ref[...], preferred_element_type=F32)
    pre = jnp.dot(g_lr.astype(BF16), wup_ref[...], preferred_element_type=F32) + bup_ref[...]
    la_out[...] = _log_sigmoid(pre) * (1.0 / GLA_TAU)


def _inproj(u, tm, cos, sin, p):
    rows = u.shape[0]
    n_tab = cos.shape[0] // tm
    row = lambda w: pl.BlockSpec((tm, w), lambda i: (i, 0))
    tab = pl.BlockSpec((tm, LANES), lambda i: (i % n_tab, 0))
    weights = [p['wq'], p['wk'], p['wv'], p['wgq'], p['wgk'], p['wgv'], p['wgr'], p['wgg'],
               p['wup'], p['bup']]
    out_widths = [(512, BF16), (512, BF16), (512, BF16), (256, BF16), (256, BF16), (512, BF16),
                  (512, BF16), (256, F32)]
    return pl.pallas_call(
        _inproj_body,
        grid=(rows // tm,),
        in_specs=[row(D_MODEL), _const_spec((1, D_MODEL)), tab, tab,
                  _const_spec((1, LANES)), _const_spec((1, LANES))]
                 + [_const_spec(w.shape) for w in weights],
        out_specs=[row(w) for w, _ in out_widths],
        out_shape=[jax.ShapeDtypeStruct((rows, w), dt) for w, dt in out_widths],
        compiler_params=_cparams(("parallel",)),
        name="inproj",
    )(u, p['gmix'], cos, sin, p['gqn'], p['gkn'], *weights)


def _diff_attn_body(bound_ref, lq1_ref, lk1_ref, lq2_ref, lk2_ref, gsub_ref,
                    q_ref, kf_ref, vf_ref, kx_ref, vx_ref, o_ref, vt_sc, qs_sc, s_sc, cmax_sc, m_sc, l_sc, acc_sc,
                    *, lam_init):
    qi = pl.program_id(1)
    tq = ATT_TILE
    n_kv = kx_ref.shape[0] // tq
    heads = range(DA_HEADS)
    hs = lambda h: slice(h * LANES, (h + 1) * LANES)

    @pl.when(qi == 0)
    def _():
        def tr(j, carry):
            off = pl.multiple_of(j * tq, tq)
            for h in heads:
                vt_sc[h, j] = jnp.transpose(vx_ref[pl.ds(off, tq), hs(h)].astype(F32)).astype(BF16)
            return carry

        lax.fori_loop(0, n_kv, tr, 0)

    d = lax.broadcasted_iota(jnp.int32, (LANES, 1), 0)
    for h in heads:
        qt = jnp.transpose(q_ref[:, hs(h)].astype(F32))
        qs_sc[h] = jnp.concatenate([jnp.where(d < DA_HEAD_DIM, qt, 0.0),
                                    jnp.where(d >= DA_HEAD_DIM, qt, 0.0)], axis=1).astype(BF16)

    meta = slice(FRONT - N_META, FRONT)
    key = lax.broadcasted_iota(jnp.int32, (tq, 2 * tq), 0)
    qry = lax.broadcasted_iota(jnp.int32, (tq, 2 * tq), 1) % tq
    causal = key <= qry
    bound = bound_ref[0]

    def meta_scores(h):
        s = jnp.dot(kf_ref[meta, hs(h)], qs_sc[h], preferred_element_type=F32)
        vt = jnp.transpose(vf_ref[meta, hs(h)].astype(F32)).astype(BF16)
        return s, vt

    def block_scores(j, h, diagonal):
        off = pl.multiple_of(j * tq, tq)
        s = jnp.dot(kx_ref[pl.ds(off, tq), hs(h)], qs_sc[h], preferred_element_type=F32)
        return jnp.where(causal, s, NEG_BIG) if diagonal else s

    @pl.when(bound <= ATT_SAFE_BOUND)
    def _():
        for h in heads:
            s, vt = meta_scores(h)
            pr = jnp.exp2(s - bound)
            l_sc[h] = jnp.sum(pr, axis=0, keepdims=True)
            acc_sc[h] = jnp.dot(vt, pr.astype(BF16), preferred_element_type=F32)

        def blocks(*js, diagonal_last=False):
            work = [(j, h, diagonal_last and j is js[-1]) for j in js for h in heads]
            ss = [block_scores(j, h, dg) for j, h, dg in work]
            ps = [jnp.exp2(s - bound) for s in ss]
            for (j, h, _), pr in zip(work, ps):
                l_sc[h] += jnp.sum(pr, axis=0, keepdims=True)
            for (j, h, _), pr in zip(work, ps):
                acc_sc[h] += jnp.dot(vt_sc[h, j], pr.astype(BF16), preferred_element_type=F32)

        def pair(u, carry):
            blocks(2 * u, 2 * u + 1)
            return carry

        lax.fori_loop(0, qi // 2, pair, 0)

        @pl.when(qi % 2 == 1)
        def _():
            blocks(qi - 1, qi, diagonal_last=True)

        @pl.when(qi % 2 == 0)
        def _():
            blocks(qi, diagonal_last=True)

    @pl.when(bound > ATT_SAFE_BOUND)
    def _():
        for h in heads:
            s, vt = meta_scores(h)
            m0 = jnp.max(s, axis=0, keepdims=True)
            pr = jnp.exp2(s - m0)
            m_sc[h] = m0
            l_sc[h] = jnp.sum(pr, axis=0, keepdims=True)
            acc_sc[h] = jnp.dot(vt, pr.astype(BF16), preferred_element_type=F32)

        def scores(j, slot, diagonal=False):
            for h in heads:
                s = block_scores(j, h, diagonal)
                s_sc[h, slot] = s
                cmax_sc[h, slot] = jnp.max(s, axis=0, keepdims=True)

        def accumulate(j, slot):
            for h in heads:
                m_old = m_sc[h]
                m_new = jnp.maximum(m_old, cmax_sc[h, slot])
                alpha = jnp.exp2(m_old - m_new)
                pr = jnp.exp2(s_sc[h, slot] - m_new)
                l_sc[h] = alpha * l_sc[h] + jnp.sum(pr, axis=0, keepdims=True)
                acc_sc[h] = alpha * acc_sc[h] + jnp.dot(vt_sc[h, j], pr.astype(BF16),
                                                        preferred_element_type=F32)
                m_sc[h] = m_new

        @pl.when(qi == 0)
        def _():
            scores(0, 0, diagonal=True)
            accumulate(0, 0)

        @pl.when(qi > 0)
        def _():
            scores(0, 0)

            def pair(u, carry):
                j = 2 * u
                scores(j + 1, 1)
                accumulate(j, 0)
                scores(j + 2, 0)
                accumulate(j + 1, 1)
                return carry

            lax.fori_loop(0, (qi - 1) // 2, pair, 0)

            @pl.when(qi % 2 == 1)
            def _():
                scores(qi, 1, diagonal=True)
                accumulate(qi - 1, 0)
                accumulate(qi, 1)

            @pl.when(qi % 2 == 0)
            def _():
                scores(qi - 1, 1)
                accumulate(qi - 2, 0)
                scores(qi, 0, diagonal=True)
                accumulate(qi - 1, 1)
                accumulate(qi, 0)

    lam = (jnp.exp(jnp.sum(lq1_ref[...] * lk1_ref[...], axis=-1, keepdims=True))
           - jnp.exp(jnp.sum(lq2_ref[...] * lk2_ref[...], axis=-1, keepdims=True)) + lam_init)
    for h in heads:
        acc = acc_sc[h]
        inv_l = 1.0 / l_sc[h]
        ot = acc[:, :tq] * inv_l[:, :tq] - lam * (acc[:, tq:] * inv_l[:, tq:])
        ms = jnp.mean(ot * ot, axis=0, keepdims=True)
        o = jnp.transpose(ot * lax.rsqrt(ms + EPS)) * gsub_ref[...] * (1.0 - lam_init)
        o_ref[:, hs(h)] = o.astype(o_ref.dtype)


def _diff_attn(bound, q, kf, vf, kx, vx, lam_vecs, gsub, lam_init):
    bsz, seq, _ = q.shape
    tq = ATT_TILE
    vec = _const_spec((1, DA_HEAD_DIM))
    width = DA_HEADS * LANES
    return pl.pallas_call(
        functools.partial(_diff_attn_body, lam_init=lam_init),
        grid=(bsz, seq // tq),
        in_specs=[pl.BlockSpec(memory_space=pltpu.SMEM), vec, vec, vec, vec, _const_spec((1, DA_V_DIM)),
                  pl.BlockSpec((None, tq, width), lambda b, i: (b, i, 0)),
                  _const_spec((FRONT, width)), _const_spec((FRONT, width)),
                  pl.BlockSpec((None, seq, width), lambda b, i: (b, 0, 0)),
                  pl.BlockSpec((None, seq, width), lambda b, i: (b, 0, 0))],
        out_specs=pl.BlockSpec((None, tq, width), lambda b, i: (b, i, 0)),
        out_shape=jax.ShapeDtypeStruct((bsz, seq, width), BF16),
        scratch_shapes=[pltpu.VMEM((DA_HEADS, seq // tq, DA_V_DIM, tq), BF16),
                        pltpu.VMEM((DA_HEADS, LANES, 2 * tq), BF16),
                        pltpu.VMEM((DA_HEADS, 2, tq, 2 * tq), F32), pltpu.VMEM((DA_HEADS, 2, 1, 2 * tq), F32),
                        pltpu.VMEM((DA_HEADS, 1, 2 * tq), F32), pltpu.VMEM((DA_HEADS, 1, 2 * tq), F32),
                        pltpu.VMEM((DA_HEADS, DA_V_DIM, 2 * tq), F32)],
        compiler_params=_cparams(("parallel", "arbitrary")),
        name="diff_attn",
    )(bound, *lam_vecs, gsub, q, kf, vf, kx, vx)


def _split3(a):
    a1 = a.astype(BF16)
    r1 = a - a1.astype(F32)
    a2 = r1.astype(BF16)
    a3 = (r1 - a2.astype(F32)).astype(BF16)
    return a1, a2, a3


def _gla_body(gn_ref, kf_ref, vf_ref, laf_ref, q_ref, k_ref, v_ref, la_ref, sr_ref, o_ref, st_sc, b_sc):
    g = pl.program_id(1)
    c = GLA_CHUNK
    kw = GLA_HEADS * GLA_DK
    vw = GLA_HEADS * GLA_DV

    ti = lax.broadcasted_iota(jnp.int32, (c, c), 0)
    si = lax.broadcasted_iota(jnp.int32, (c, c), 1)
    tri = (si <= ti).astype(BF16)
    hv = lax.broadcasted_iota(jnp.int32, (vw, kw), 0) // GLA_DV
    hk = lax.broadcasted_iota(jnp.int32, (vw, kw), 1) // GLA_DK
    head_mask = hv == hk
    lane_head = lax.broadcasted_iota(jnp.int32, (1, kw), 1) // GLA_DK
    causal = lax.broadcasted_iota(jnp.int32, (GLA_HEADS * c, c), 0) % c >= \
        lax.broadcasted_iota(jnp.int32, (GLA_HEADS * c, c), 1)

    def cumsum(a):
        a1, a2, a3 = _split3(a)
        return (jnp.dot(tri, a1, preferred_element_type=F32)
                + jnp.dot(tri, a2, preferred_element_type=F32)
                + jnp.dot(tri, a3, preferred_element_type=F32))

    batch = range(GLA_BATCH)

    def next_state(st, k, v, b):
        b_last = b[c - 1:c, :]
        kd = (k * jnp.exp(b_last - b)).astype(BF16)
        upd = lax.dot_general(v, kd, (((0,), (0,)), ((), ())), preferred_element_type=F32)
        return jnp.where(head_mask, jnp.exp(b_last) * st + upd, 0.0)

    @pl.when(g == 0)
    def _():
        st0 = next_state(jnp.zeros((vw, kw), F32), kf_ref[...].astype(F32), vf_ref[...], cumsum(laf_ref[...]))
        for bb in batch:
            st_sc[bb] = st0

    gn = gn_ref[...]

    def finish(o, sr):
        outs = []
        for hh in range(GLA_HEADS):
            cs = slice(hh * GLA_DV, (hh + 1) * GLA_DV)
            oh = o[:, cs]
            ms = jnp.mean(oh * oh, axis=-1, keepdims=True)
            outs.append(oh * lax.rsqrt(ms + EPS) * gn * sr[:, cs])
        return jnp.concatenate(outs, axis=1)

    n_chunks = GLA_GROUP // c
    for ci in range(n_chunks):
        rs = slice(ci * c, (ci + 1) * c)
        for bb in batch:
            b_sc[bb, rs, :] = cumsum(la_ref[bb, rs, :])
    steep = jnp.min(b_sc[...]) < -GLA_SAFE_DECAY

    @pl.when(jnp.logical_not(steep))
    def _():
        nt = (((1,), (1,)), ((), ()))
        sts = [st_sc[bb] for bb in batch]
        for ci in range(n_chunks):
            rs = slice(ci * c, (ci + 1) * c)
            bs = [b_sc[bb, rs, :] for bb in batch]
            ks = [k_ref[bb, rs, :].astype(F32) for bb in batch]
            vs = [v_ref[bb, rs, :] for bb in batch]
            qes = [q_ref[bb, rs, :].astype(F32) * jnp.exp(bs[bb]) for bb in batch]
            nxt = [next_state(sts[bb], ks[bb], vs[bb], bs[bb]) for bb in batch]
            inters = [lax.dot_general(qes[bb].astype(BF16), sts[bb].astype(BF16), nt,
                                      preferred_element_type=F32) for bb in batch]
            scs = []
            for bb in batch:
                ke = (ks[bb] * jnp.exp(-bs[bb])).astype(BF16)
                qstack = jnp.concatenate([jnp.where(lane_head == hh, qes[bb], 0.0) for hh in range(GLA_HEADS)],
                                         axis=0).astype(BF16)
                sc = lax.dot_general(qstack, ke, nt, preferred_element_type=F32)
                scs.append(jnp.where(causal, sc, 0.0).astype(BF16))
            pvs = [jnp.dot(scs[bb], vs[bb], preferred_element_type=F32) for bb in batch]
            for bb in batch:
                o = jnp.concatenate([inters[bb][:, hh * GLA_DV:(hh + 1) * GLA_DV]
                                     + pvs[bb][hh * c:(hh + 1) * c, hh * GLA_DV:(hh + 1) * GLA_DV]
                                     for hh in range(GLA_HEADS)], axis=1)
                o_ref[bb, rs, :] = finish(o, sr_ref[bb, rs, :].astype(F32)).astype(o_ref.dtype)
            sts = nxt
        for bb in batch:
            st_sc[bb] = sts[bb]

    @pl.when(steep)
    def _():
        rows = 16

        def tile(i, carry):
            off = pl.multiple_of(i * rows, rows)
            for bb in batch:
                a = la_ref[bb, pl.ds(off, rows), :]
                q = q_ref[bb, pl.ds(off, rows), :].astype(F32)
                k = k_ref[bb, pl.ds(off, rows), :].astype(F32)
                v = v_ref[bb, pl.ds(off, rows), :].astype(F32)
                outs = []
                for r in range(rows):
                    row = lambda x: x[r:r + 1].astype(BF16)
                    upd = lax.dot_general(row(v), row(k), (((0,), (0,)), ((), ())),
                                          preferred_element_type=F32)
                    st = jnp.where(head_mask, jnp.exp(a[r:r + 1]) * st_sc[bb] + upd, 0.0)
                    st_sc[bb] = st
                    outs.append(lax.dot_general(row(q), st.astype(BF16), (((1,), (1,)), ((), ())),
                                                preferred_element_type=F32))
                o = jnp.concatenate(outs, axis=0)
                o_ref[bb, pl.ds(off, rows), :] = finish(
                    o, sr_ref[bb, pl.ds(off, rows), :].astype(F32)).astype(o_ref.dtype)
            return carry

        lax.fori_loop(0, GLA_GROUP // rows, tile, 0)


def _gla(kf, vf, laf, q, k, v, la, sr, gn):
    bsz, seq, _ = q.shape
    t = GLA_GROUP
    kw = GLA_HEADS * GLA_DK
    vw = GLA_HEADS * GLA_DV
    last = FRONT // GLA_CHUNK - 1
    nb = GLA_BATCH
    assert bsz % nb == 0
    fr = lambda w: pl.BlockSpec((GLA_CHUNK, w), lambda b, g: (last, 0))
    xs = lambda w: pl.BlockSpec((nb, t, w), lambda b, g: (b, g, 0))
    return pl.pallas_call(
        _gla_body,
        grid=(bsz // nb, seq // t),
        in_specs=[_const_spec((1, GLA_DV)), fr(kw), fr(vw), fr(kw), xs(kw), xs(kw), xs(vw), xs(kw), xs(vw)],
        out_specs=xs(vw),
        out_shape=jax.ShapeDtypeStruct((bsz, seq, vw), BF16),
        scratch_shapes=[pltpu.VMEM((nb, vw, kw), F32), pltpu.VMEM((nb, t, kw), F32)],
        compiler_params=_cparams(("parallel", "arbitrary")),
        name="gla",
    )(gn, kf, vf, laf, q, k, v, la, sr)


def _merge_body(oa_ref, ob_ref, u_ref, gmix_ref, wgate_ref, bgate_ref, wb0_ref, wb1_ref, wout_ref, gffn_ref,
                wr_ref, br_ref,
                u1_out, h2_out, info_out, cnt_out, cnt_sc):
    i = pl.program_id(0)
    tm = ROW_TILE

    @pl.when(i == 0)
    def _():
        cnt_sc[...] = jnp.zeros(cnt_sc.shape, F32)

    ya = jnp.dot(oa_ref[...], wb0_ref[...], preferred_element_type=F32)
    yb = jnp.dot(ob_ref[...], wb1_ref[...], preferred_element_type=F32)
    x = u_ref[...]
    h = (x * lax.rsqrt(jnp.mean(x * x, axis=-1, keepdims=True) + EPS) * gmix_ref[...]).astype(BF16)
    gate = _sigmoid(jnp.dot(h, wgate_ref[...], preferred_element_type=F32) + bgate_ref[...])
    merged = gate[:, :D_MODEL] * ya + gate[:, D_MODEL:] * yb
    u1 = x + jnp.dot(merged.astype(BF16), wout_ref[...], preferred_element_type=F32)
    u1_out[...] = u1
    ms = jnp.mean(u1 * u1, axis=-1, keepdims=True)
    h2f = u1 * lax.rsqrt(ms + EPS) * gffn_ref[...]
    _store_row_tiles(h2_out, h2f)
    h2 = h2f.astype(BF16)

    logits = lax.dot_general(wr_ref[...], h2, (((1,), (1,)), ((), ())), preferred_element_type=F32) + br_ref[...]
    row = lax.broadcasted_iota(jnp.int32, (ROUTER_ROWS, tm), 0)
    is_group = row < N_GROUPS
    gl = jnp.where(is_group, logits, NEG_BIG)
    gmax = jnp.max(gl, axis=0, keepdims=True)
    g_idx = jnp.min(jnp.where(gl == gmax, row, ROUTER_ROWS), axis=0, keepdims=True)
    g_w = 1.0 / jnp.sum(jnp.where(is_group, jnp.exp(gl - gmax), 0.0), axis=0, keepdims=True)
    lo = N_GROUPS + EXPERTS_PER_GROUP * g_idx
    el = jnp.where((row >= lo) & (row < lo + EXPERTS_PER_GROUP), logits, NEG_BIG)
    v1 = jnp.max(el, axis=0, keepdims=True)
    i1 = jnp.min(jnp.where(el == v1, row, ROUTER_ROWS), axis=0, keepdims=True)
    el2 = jnp.where(row == i1, NEG_BIG, el)
    v2 = jnp.max(el2, axis=0, keepdims=True)
    i2 = jnp.min(jnp.where(el2 == v2, row, ROUTER_ROWS), axis=0, keepdims=True)
    e21 = jnp.exp(v2 - v1)
    w1 = g_w / (1.0 + e21)
    w2 = w1 * e21

    onehot = ((row == i1) | (row == i2)).astype(BF16)
    si = lax.broadcasted_iota(jnp.int32, (tm, tm), 0)
    ti = lax.broadcasted_iota(jnp.int32, (tm, tm), 1)
    earlier = (si < ti).astype(BF16)
    cnt = cnt_sc[...]
    before = jnp.dot(onehot, earlier, preferred_element_type=F32) + jnp.tile(cnt, (1, tm // LANES))
    r1 = jnp.sum(jnp.where(row == i1, before, 0.0), axis=0, keepdims=True)
    r2 = jnp.sum(jnp.where(row == i2, before, 0.0), axis=0, keepdims=True)
    cnt = cnt + jnp.dot(onehot, jnp.ones((tm, LANES), BF16), preferred_element_type=F32)
    cnt_sc[...] = cnt
    cnt_out[...] = cnt

    zero = jnp.zeros_like(w1)
    info_out[...] = jnp.concatenate([(i1 - N_GROUPS).astype(F32), (i2 - N_GROUPS).astype(F32),
                                     w1, w2, r1, r2, zero, zero], axis=0)


def _merge(oa, ob, u, p):
    n = u.shape[0]
    tm = ROW_TILE
    row = lambda w: pl.BlockSpec((tm, w), lambda i: (i, 0))
    return pl.pallas_call(
        _merge_body,
        grid=(n // tm,),
        in_specs=[row(512), row(512), row(D_MODEL),
                  _const_spec((1, D_MODEL)), _const_spec((D_MODEL, 2 * D_MODEL)), _const_spec((1, 2 * D_MODEL)),
                  _const_spec((512, D_MODEL)), _const_spec((512, D_MODEL)), _const_spec((D_MODEL, D_MODEL)),
                  _const_spec((1, D_MODEL)), _const_spec((ROUTER_ROWS, D_MODEL)), _const_spec((ROUTER_ROWS, tm))],
        out_specs=[row(D_MODEL), _row_tile_spec(tm, lambda i: (i, 0)), pl.BlockSpec((8, tm), lambda i: (0, i)),
                   _const_spec((ROUTER_ROWS, LANES))],
        out_shape=[jax.ShapeDtypeStruct((n, D_MODEL), F32), jax.ShapeDtypeStruct((n * RT, LANES), F32),
                   jax.ShapeDtypeStruct((8, n), F32), jax.ShapeDtypeStruct((ROUTER_ROWS, LANES), F32)],
        scratch_shapes=[pltpu.VMEM((ROUTER_ROWS, LANES), F32)],
        compiler_params=_cparams(("arbitrary",)),
        name="merge_router",
    )(oa, ob, u, p['gmix'], p['wgate'], p['bgate'], p['wb0'], p['wb1'], p['wout'], p['gffn'], p['wr'], p['br'])


def _dispatch_body(tail_ref, nb_ref, dest_ref, h2_ref, xs_hbm, zero_sc, sem, zsem):
    i = pl.program_id(0)
    blk_rows = MOE_TILE * RT
    n_blocks = xs_hbm.shape[0] // blk_rows

    def zero_copy(blk):
        dst = xs_hbm.at[pl.ds(pl.multiple_of(blk * blk_rows, blk_rows), blk_rows)]
        return pltpu.make_async_copy(zero_sc, dst, zsem)

    @pl.when(i == 0)
    def _():
        zero_sc[...] = jnp.zeros(zero_sc.shape, F32)

        def tails(fn):
            def body(e, carry):
                @pl.when(tail_ref[e] >= 0)
                def _():
                    fn(zero_copy(tail_ref[e]))
                return carry
            lax.fori_loop(0, N_EXPERTS, body, 0)

        def unused(fn):
            def body(b, carry):
                fn(zero_copy(b))
                return carry
            lax.fori_loop(nb_ref[0], n_blocks, body, 0)

        tails(lambda cp: cp.start())
        unused(lambda cp: cp.start())
        tails(lambda cp: cp.wait())
        unused(lambda cp: cp.wait())

    def start(r, carry):
        src = _token_rows(h2_ref, r)
        pltpu.make_async_copy(src, _token_rows(xs_hbm, dest_ref[0, 0, r]), sem).start()
        pltpu.make_async_copy(src, _token_rows(xs_hbm, dest_ref[0, 1, r]), sem).start()
        return carry

    lax.fori_loop(0, DMA_TILE, start, 0)
    for _ in range(2):
        pltpu.make_async_copy(h2_ref, xs_hbm.at[pl.ds(0, DMA_TILE * RT)], sem).wait()


def _dispatch(tail_blocks, n_used, dest, h2, n_slots):
    n = h2.shape[0] // RT
    grid_spec = pltpu.PrefetchScalarGridSpec(
        num_scalar_prefetch=2,
        grid=(n // DMA_TILE,),
        in_specs=[pl.BlockSpec((1, 2, DMA_TILE), lambda i, tb, nb: (i, 0, 0), memory_space=pltpu.SMEM),
                  _row_tile_spec(DMA_TILE, lambda i, tb, nb: (i, 0))],
        out_specs=pl.BlockSpec(memory_space=pl.ANY),
        scratch_shapes=[pltpu.VMEM((MOE_TILE * RT, LANES), F32), pltpu.SemaphoreType.DMA(()),
                        pltpu.SemaphoreType.DMA(())],
    )
    return pl.pallas_call(
        _dispatch_body,
        grid_spec=grid_spec,
        out_shape=jax.ShapeDtypeStruct((n_slots * RT, LANES), F32),
        compiler_params=_cparams(("arbitrary",)),
        name="dispatch",
    )(tail_blocks, n_used, dest, h2)


def _experts_body(be_ref, nb_ref, x_ref, wg_ref, wu_ref, wd_ref, y_ref, wg_sc, wu_sc, wd_sc):
    i = pl.program_id(0)
    prev = be_ref[jnp.maximum(i - 1, 0)]
    fresh = (i == 0) | (be_ref[i] != prev)

    @pl.when(fresh)
    def _():
        wg_sc[...] = wg_ref[...].astype(BF16)
        wu_sc[...] = wu_ref[...].astype(BF16)
        wd_sc[...] = wd_ref[...].astype(BF16)

    @pl.when(i < nb_ref[0])
    def _():
        x = _load_row_tiles(x_ref, MOE_TILE).astype(BF16)
        y = jnp.zeros((MOE_TILE, D_MODEL), F32)
        for j in range(D_EXPERT // EXPERT_CHUNK):
            cs = slice(j * EXPERT_CHUNK, (j + 1) * EXPERT_CHUNK)
            gp = jnp.dot(x, wg_sc[:, cs], preferred_element_type=F32)
            up = jnp.dot(x, wu_sc[:, cs], preferred_element_type=F32)
            hid = (gp * _sigmoid(gp) * up).astype(BF16)
            y = y + jnp.dot(hid, wd_sc[cs, :], preferred_element_type=F32)
        _store_row_tiles(y_ref, y)

    @pl.when(i >= nb_ref[0])
    def _():
        y_ref[...] = jnp.zeros(y_ref.shape, F32)


def _experts(block_e, n_used, xs, wg, wu, wd):
    n_slots = xs.shape[0] // RT
    n_blocks = n_slots // MOE_TILE
    xmap = lambda i, be, nb: (jnp.minimum(i, nb[0] - 1), 0)
    wmap = lambda i, be, nb: (be[i], 0, 0)
    grid_spec = pltpu.PrefetchScalarGridSpec(
        num_scalar_prefetch=2,
        grid=(n_blocks,),
        in_specs=[_row_tile_spec(MOE_TILE, xmap),
                  pl.BlockSpec((None, D_MODEL, D_EXPERT), wmap),
                  pl.BlockSpec((None, D_MODEL, D_EXPERT), wmap),
                  pl.BlockSpec((None, D_EXPERT, D_MODEL), wmap)],
        out_specs=_row_tile_spec(MOE_TILE, lambda i, be, nb: (i, 0)),
        scratch_shapes=[pltpu.VMEM((D_MODEL, D_EXPERT), BF16), pltpu.VMEM((D_MODEL, D_EXPERT), BF16),
                        pltpu.VMEM((D_EXPERT, D_MODEL), BF16)],
    )
    return pl.pallas_call(
        _experts_body,
        grid_spec=grid_spec,
        out_shape=jax.ShapeDtypeStruct((n_slots * RT, LANES), F32),
        compiler_params=_cparams(("arbitrary",)),
        name="experts",
    )(block_e, n_used, xs, wg, wu, wd)


def _combine_body(dest_ref, dest_next_ref, w_ref, u1_ref, ys_hbm, o_ref, ybuf, sems):
    i = pl.program_id(0)
    n_steps = pl.num_programs(0)
    t = DMA_TILE

    def gather(d_ref, slot):
        buf = ybuf.at[slot]

        def start(r, carry):
            pltpu.make_async_copy(_token_rows(ys_hbm, d_ref[0, 0, r]), _token_rows(buf, r),
                                  sems.at[slot]).start()
            pltpu.make_async_copy(_token_rows(ys_hbm, d_ref[0, 1, r]), _token_rows(buf, t + r),
                                  sems.at[slot]).start()
            return carry

        lax.fori_loop(0, t, start, 0)

    slot = i % 2

    @pl.when(i == 0)
    def _():
        gather(dest_ref, 0)

    @pl.when(i + 1 < n_steps)
    def _():
        gather(dest_next_ref, 1 - slot)

    buf = ybuf.at[slot]
    pltpu.make_async_copy(ys_hbm.at[pl.ds(0, buf.shape[0])], buf, sems.at[slot]).wait()
    w = w_ref[...]
    o_ref[...] = (u1_ref[...] + w[:, 0:1] * _load_row_tiles(buf, t) + w[:, 1:2] * _load_row_tiles(buf, t, t))


def _combine(dest, w, u1, ys):
    n = u1.shape[0]
    t = DMA_TILE
    n_steps = n // t
    return pl.pallas_call(
        _combine_body,
        grid=(n_steps,),
        in_specs=[pl.BlockSpec((1, 2, t), lambda i: (i, 0, 0), memory_space=pltpu.SMEM),
                  pl.BlockSpec((1, 2, t), lambda i: (jnp.minimum(i + 1, n_steps - 1), 0, 0),
                               memory_space=pltpu.SMEM),
                  pl.BlockSpec((t, 2), lambda i: (i, 0)),
                  pl.BlockSpec((t, D_MODEL), lambda i: (i, 0)),
                  pl.BlockSpec(memory_space=pl.ANY)],
        out_specs=pl.BlockSpec((t, D_MODEL), lambda i: (i, 0)),
        out_shape=jax.ShapeDtypeStruct((n, D_MODEL), F32),
        scratch_shapes=[pltpu.VMEM((2, 2 * t * RT, LANES), F32), pltpu.SemaphoreType.DMA((2,))],
        compiler_params=_cparams(("arbitrary",)),
        name="combine",
    )(dest, dest, w, u1, ys)


def _rope_tables(pos):
    half = DA_HEAD_DIM // 2
    inv_freq = jnp.power(ROPE_THETA, -jnp.arange(half, dtype=F32) * 2.0 / DA_HEAD_DIM)
    ang = pos[:, None] * inv_freq[None, :]
    cos, sin = jnp.cos(ang), jnp.sin(ang)
    cos_t = jnp.tile(cos, (1, LANES // half))
    sin_t = jnp.tile(jnp.concatenate([-sin, sin], axis=1), (1, LANES // DA_HEAD_DIM))
    return cos_t, sin_t


def _layer(x, meta_tokens, l, g_mix_norm, w_in, g_q_norm, g_k_norm, lambda_q1, lambda_k1, lambda_q2, lambda_k2,
           g_diff_subln, w_gla_gate_up, b_gla_gate, g_gla_norm, w_branch, b_merge_gate, w_out, g_ffn_norm,
           w_router_group, b_router_group, w_router_expert, b_router_expert, w_exp_gate, w_exp_up, w_exp_down):
    bsz, seq, _ = x.shape
    n = bsz * seq

    wi = w_in[l]
    offs = [0]
    for s in (512, 512, 512, 256, 256, 512, 512, GLA_RANK, 2 * D_MODEL):
        offs.append(offs[-1] + s)
    sec = lambda j: wi[:, offs[j]:offs[j + 1]].astype(BF16)
    p = {
        'gmix': g_mix_norm[l][None, :],
        'gqn': jnp.tile(g_q_norm[l], LANES // DA_HEAD_DIM)[None, :],
        'gkn': jnp.tile(g_k_norm[l], LANES // DA_HEAD_DIM)[None, :],
        'wq': sec(0), 'wk': sec(1), 'wv': sec(2), 'wgq': sec(3), 'wgk': sec(4), 'wgv': sec(5), 'wgr': sec(6),
        'wgg': jnp.pad(sec(7), ((0, 0), (0, LANES - GLA_RANK))),
        'wup': jnp.pad(w_gla_gate_up[l].astype(BF16), ((0, LANES - GLA_RANK), (0, 0))),
        'bup': b_gla_gate[l][None, :],
        'wgate': sec(8),
        'bgate': b_merge_gate[l].reshape(1, 2 * D_MODEL),
        'wb0': w_branch[l, 0].astype(BF16), 'wb1': w_branch[l, 1].astype(BF16),
        'wout': w_out[l].astype(BF16),
        'gffn': g_ffn_norm[l][None, :],
        'wr': jnp.pad(jnp.concatenate([w_router_group[l], w_router_expert[l].reshape(D_MODEL, N_EXPERTS)],
                                      axis=1).T.astype(BF16), ((0, ROUTER_ROWS - N_GROUPS - N_EXPERTS), (0, 0))),
        'br': jnp.broadcast_to(
            jnp.pad(jnp.concatenate([b_router_group[l], b_router_expert[l].reshape(N_EXPERTS)]),
                    (0, ROUTER_ROWS - N_GROUPS - N_EXPERTS))[:, None], (ROUTER_ROWS, ROW_TILE)),
    }

    u_front = jnp.concatenate([jnp.zeros((FRONT - N_META, D_MODEL), F32), meta_tokens.astype(F32)], axis=0)
    cos_f, sin_f = _rope_tables(jnp.arange(FRONT, dtype=F32) - (FRONT - N_META))
    cos_x, sin_x = _rope_tables(jnp.arange(seq, dtype=F32) + N_META)
    front = _inproj(u_front, FRONT, cos_f, sin_f, p)
    xin = _inproj(x.reshape(n, D_MODEL), ROW_TILE, cos_x, sin_x, p)
    q, k, v, gq, gk, gv, sr, la = [a.reshape(bsz, seq, a.shape[-1]) for a in xin]
    _, kf, vf, _, gkf, gvf, _, laf = front

    lam_init = 0.8 - 0.6 * math.exp(-0.3 * l)
    lam_vecs = [a[l][None, :] for a in (lambda_q1, lambda_k1, lambda_q2, lambda_k2)]
    score_bound = (ATT_BOUND_MARGIN * DA_HEAD_DIM * Q_SCALE
                   * jnp.max(jnp.abs(g_q_norm[l])) * jnp.max(jnp.abs(g_k_norm[l]))).reshape(1).astype(F32)
    o_a = _diff_attn(score_bound, q, kf, vf, k, v, lam_vecs, g_diff_subln[l][None, :], lam_init)
    o_b = _gla(gkf, gvf, laf, gq, gk, gv, la, sr, g_gla_norm[l][None, :])

    u1, h2, info, cnt = _merge(o_a.reshape(n, -1), o_b.reshape(n, -1), x.reshape(n, D_MODEL), p)

    ids = info[0:2].astype(jnp.int32)
    wts = info[2:4]
    rank = info[4:6].astype(jnp.int32)
    counts = cnt[N_GROUPS:N_GROUPS + N_EXPERTS, 0].astype(jnp.int32)
    padded = (counts + MOE_TILE - 1) // MOE_TILE * MOE_TILE
    pends = jnp.cumsum(padded)
    pstarts = pends - padded
    expert = jnp.arange(N_EXPERTS, dtype=jnp.int32)
    dest = jnp.sum(jnp.where(ids[..., None] == expert, pstarts, 0), axis=-1) + rank
    n_slots = (2 * n // MOE_TILE + N_EXPERTS) * MOE_TILE
    n_blocks = n_slots // MOE_TILE
    n_used = (pends[-1] // MOE_TILE).astype(jnp.int32)
    blk = jnp.minimum(jnp.arange(n_blocks, dtype=jnp.int32), n_used - 1) * MOE_TILE
    block_e = jnp.minimum(jnp.sum(pends[None, :] <= blk[:, None], axis=1), N_EXPERTS - 1).astype(jnp.int32)
    tail_blocks = jnp.where(counts > 0, pends // MOE_TILE - 1, -1).astype(jnp.int32)
    dest_t = dest.reshape(2, n // DMA_TILE, DMA_TILE).transpose(1, 0, 2)

    xs = _dispatch(tail_blocks, n_used[None], dest_t, h2, n_slots)
    ys = _experts(block_e, n_used[None], xs, w_exp_gate[l], w_exp_up[l], w_exp_down[l])
    out = _combine(dest_t, wts.T, u1, ys)
    return out.reshape(bsz, seq, D_MODEL)


def kernel(x, meta_tokens, g_mix_norm, w_in, g_q_norm, g_k_norm, lambda_q1, lambda_k1, lambda_q2, lambda_k2,
           g_diff_subln, w_gla_gate_up, b_gla_gate, g_gla_norm, w_branch, b_merge_gate, w_out, g_ffn_norm,
           w_router_group, b_router_group, w_router_expert, b_router_expert, w_exp_gate, w_exp_up, w_exp_down):
    depth = w_in.shape[0]
    assert depth == 1, "meta tokens are only carried through a single layer in this implementation"
    assert x.shape[1] % ROW_TILE == 0 and x.shape[2] == D_MODEL
    return _layer(x, meta_tokens, 0, g_mix_norm, w_in, g_q_norm, g_k_norm, lambda_q1, lambda_k1, lambda_q2,
                  lambda_k2, g_diff_subln, w_gla_gate_up, b_gla_gate, g_gla_norm, w_branch, b_merge_gate, w_out,
                  g_ffn_norm, w_router_group, b_router_group, w_router_expert, b_router_expert,
                  w_exp_gate, w_exp_up, w_exp_down)
```

```python
import functools
import math

import jax
import jax.numpy as jnp
from jax import lax
from jax.experimental import pallas as pl
from jax.experimental.pallas import tpu as pltpu

F32 = jnp.float32
BF16 = jnp.bfloat16

D_MODEL = 1024
N_META = 16
EPS = 1e-6
ROPE_THETA = 10000.0

DA_HEADS = 4
DA_HEAD_DIM = 64
DA_V_DIM = 128
Q_SCALE = DA_HEAD_DIM ** -0.5 * math.log2(math.e)
ATT_BOUND_MARGIN = 1.02
ATT_SAFE_BOUND = 60.0
GLA_HEADS = 4
GLA_DK = 64
GLA_DV = 128
GLA_RANK = 16
GLA_TAU = 16.0
GLA_CHUNK = 64
GLA_SAFE_DECAY = 60.0
N_GROUPS = 4
EXPERTS_PER_GROUP = 8
N_EXPERTS = 32
D_EXPERT = 512
ROUTER_ROWS = 48

LANES = 128
FRONT = 256
ATT_TILE = 256
ROW_TILE = 512
GLA_GROUP = 512
GLA_BATCH = 4
MOE_TILE = 512
EXPERT_CHUNK = 256
DMA_TILE = 512
NEG_BIG = -1e30
VMEM_LIMIT = 56 * 1024 * 1024


def _cparams(sem):
    return pltpu.CompilerParams(dimension_semantics=sem, vmem_limit_bytes=VMEM_LIMIT)


def _const_spec(shape):
    nd = len(shape)
    return pl.BlockSpec(shape, lambda *_: (0,) * nd)


RT = D_MODEL // LANES


def _row_tile_spec(rows, index_map):
    return pl.BlockSpec((rows * RT, LANES), index_map)


def _token_rows(ref, tok):
    return ref.at[pl.ds(pl.multiple_of(tok * RT, RT), RT)]


def _load_row_tiles(ref, rows, first=0):
    return jnp.concatenate([ref[pl.ds(first * RT + c, rows, stride=RT), :] for c in range(RT)], axis=1)


def _store_row_tiles(ref, val):
    for c in range(RT):
        ref[pl.ds(c, val.shape[0], stride=RT), :] = val[:, c * LANES:(c + 1) * LANES]


def _sigmoid(x):
    return 0.5 * jnp.tanh(0.5 * x) + 0.5


def _log_sigmoid(x):
    return jnp.minimum(x, 0.0) - jnp.log1p(jnp.exp(-jnp.abs(x)))


def _inproj_body(u_ref, gmix_ref, cos_ref, sin_ref, gqn_ref, gkn_ref,
                 wq_ref, wk_ref, wv_ref, wgq_ref, wgk_ref, wgv_ref, wgr_ref, wgg_ref,
                 wup_ref, bup_ref,
                 q_out, k_out, v_out, gq_out, gk_out, gv_out, sr_out, la_out):
    x = u_ref[...]
    ms = jnp.mean(x * x, axis=-1, keepdims=True)
    h = (x * lax.rsqrt(ms + EPS) * gmix_ref[...]).astype(BF16)

    cos = cos_ref[...]
    sin = sin_ref[...]
    lane = lax.broadcasted_iota(jnp.int32, (1, LANES), 1)
    first_half = (lane % DA_HEAD_DIM) < (DA_HEAD_DIM // 2)
    gi = lax.broadcasted_iota(jnp.int32, (LANES, LANES), 0) // DA_HEAD_DIM
    gj = lax.broadcasted_iota(jnp.int32, (LANES, LANES), 1) // DA_HEAD_DIM
    group_sum = (gi == gj).astype(BF16)

    def norm_rope(w_ref, gain_ref, out_ref, scale):
        z = jnp.dot(h, w_ref[...], preferred_element_type=F32)
        for hh in range(DA_HEADS):
            zh = z[:, hh * LANES:(hh + 1) * LANES]
            ssq = jnp.dot((zh * zh).astype(BF16), group_sum, preferred_element_type=F32)
            zn = zh * lax.rsqrt(ssq * (1.0 / DA_HEAD_DIM) + EPS) * gain_ref[...]
            rot = jnp.where(first_half,
                            pltpu.roll(zn, LANES - DA_HEAD_DIM // 2, 1),
                            pltpu.roll(zn, DA_HEAD_DIM // 2, 1))
            zr = zn * cos + rot * sin
            out_ref[:, hh * LANES:(hh + 1) * LANES] = (zr * scale).astype(out_ref.dtype)

    norm_rope(wq_ref, gqn_ref, q_out, Q_SCALE)
    norm_rope(wk_ref, gkn_ref, k_out, 1.0)
    v_out[...] = jnp.dot(h, wv_ref[...], preferred_element_type=F32).astype(v_out.dtype)

    gq_out[...] = (jnp.dot(h, wgq_ref[...], preferred_element_type=F32) * (GLA_DK ** -0.5)).astype(gq_out.dtype)
    gk_out[...] = jnp.dot(h, wgk_ref[...], preferred_element_type=F32).astype(gk_out.dtype)
    gv_out[...] = jnp.dot(h, wgv_ref[...], preferred_element_type=F32).astype(gv_out.dtype)
    r = jnp.dot(h, wgr_ref[...], preferred_element_type=F32)
    sr_out[...] = (r * _sigmoid(r)).astype(sr_out.dtype)

    g_lr = jnp.dot(h, wgg_ref[...], preferred_element_type=F32)
    pre = jnp.dot(g_lr.astype(BF16), wup_ref[...], preferred_element_type=F32) + bup_ref[...]
    la_out[...] = _log_sigmoid(pre) * (1.0 / GLA_TAU)


def _inproj(u, tm, cos, sin, p):
    rows = u.shape[0]
    n_tab = cos.shape[0] // tm
    row = lambda w: pl.BlockSpec((tm, w), lambda i: (i, 0))
    tab = pl.BlockSpec((tm, LANES), lambda i: (i % n_tab, 0))
    weights = [p['wq'], p['wk'], p['wv'], p['wgq'], p['wgk'], p['wgv'], p['wgr'], p['wgg'],
               p['wup'], p['bup']]
    out_widths = [(512, BF16), (512, BF16), (512, BF16), (256, BF16), (256, BF16), (512, BF16),
                  (512, BF16), (256, F32)]
    return pl.pallas_call(
        _inproj_body,
        grid=(rows // tm,),
        in_specs=[row(D_MODEL), _const_spec((1, D_MODEL)), tab, tab,
                  _const_spec((1, LANES)), _const_spec((1, LANES))]
                 + [_const_spec(w.shape) for w in weights],
        out_specs=[row(w) for w, _ in out_widths],
        out_shape=[jax.ShapeDtypeStruct((rows, w), dt) for w, dt in out_widths],
        compiler_params=_cparams(("parallel",)),
        name="inproj",
    )(u, p['gmix'], cos, sin, p['gqn'], p['gkn'], *weights)


def _diff_attn_body(bound_ref, lq1_ref, lk1_ref, lq2_ref, lk2_ref, gsub_ref,
                    q_ref, kf_ref, vf_ref, kx_ref, vx_ref, o_ref, vt_sc, qs_sc, s_sc, cmax_sc, m_sc, l_sc, acc_sc,
                    *, lam_init):
    qi = pl.program_id(1)
    tq = ATT_TILE
    n_kv = kx_ref.shape[0] // tq
    heads = range(DA_HEADS)
    hs = lambda h: slice(h * LANES, (h + 1) * LANES)

    @pl.when(qi == 0)
    def _():
        def tr(j, carry):
            off = pl.multiple_of(j * tq, tq)
            for h in heads:
                vt_sc[h, j] = jnp.transpose(vx_ref[pl.ds(off, tq), hs(h)].astype(F32)).astype(BF16)
            return carry

        lax.fori_loop(0, n_kv, tr, 0)

    d = lax.broadcasted_iota(jnp.int32, (LANES, 1), 0)
    for h in heads:
        qt = jnp.transpose(q_ref[:, hs(h)].astype(F32))
        qs_sc[h] = jnp.concatenate([jnp.where(d < DA_HEAD_DIM, qt, 0.0),
                                    jnp.where(d >= DA_HEAD_DIM, qt, 0.0)], axis=1).astype(BF16)

    meta = slice(FRONT - N_META, FRONT)
    key = lax.broadcasted_iota(jnp.int32, (tq, 2 * tq), 0)
    qry = lax.broadcasted_iota(jnp.int32, (tq, 2 * tq), 1) % tq
    causal = key <= qry
    bound = bound_ref[0]

    def meta_scores(h):
        s = jnp.dot(kf_ref[meta, hs(h)], qs_sc[h], preferred_element_type=F32)
        vt = jnp.transpose(vf_ref[meta, hs(h)].astype(F32)).astype(BF16)
        return s, vt

    def block_scores(j, h, diagonal):
        off = pl.multiple_of(j * tq, tq)
        s = jnp.dot(kx_ref[pl.ds(off, tq), hs(h)], qs_sc[h], preferred_element_type=F32)
        return jnp.where(causal, s, NEG_BIG) if diagonal else s

    @pl.when(bound <= ATT_SAFE_BOUND)
    def _():
        l_sc[...] = jnp.zeros(l_sc.shape, F32)
        acc_sc[...] = jnp.zeros(acc_sc.shape, F32)

        def blocks(*js, last=False):
            work = [(h, block_scores(j, h, last and j is js[-1]), (h, j)) for j in js for h in heads]
            if last:
                work += [(h,) + meta_scores(h) for h in heads]
            ps = [(h, jnp.exp2(s - bound), vt) for h, s, vt in work]
            for h, pr, _ in ps:
                l_sc[h] += jnp.sum(pr, axis=0, keepdims=True)
            for h, pr, vt in ps:
                vt = vt_sc[vt] if isinstance(vt, tuple) else vt
                acc_sc[h] += jnp.dot(vt, pr.astype(BF16), preferred_element_type=F32)

        def pair(u, carry):
            blocks(2 * u, 2 * u + 1)
            return carry

        lax.fori_loop(0, qi // 2, pair, 0)

        @pl.when(qi % 2 == 1)
        def _():
            blocks(qi - 1, qi, last=True)

        @pl.when(qi % 2 == 0)
        def _():
            blocks(qi, last=True)

    @pl.when(bound > ATT_SAFE_BOUND)
    def _():
        for h in heads:
            s, vt = meta_scores(h)
            m0 = jnp.max(s, axis=0, keepdims=True)
            pr = jnp.exp2(s - m0)
            m_sc[h] = m0
            l_sc[h] = jnp.sum(pr, axis=0, keepdims=True)
            acc_sc[h] = jnp.dot(vt, pr.astype(BF16), preferred_element_type=F32)

        def scores(j, slot, diagonal=False):
            for h in heads:
                s = block_scores(j, h, diagonal)
                s_sc[h, slot] = s
                cmax_sc[h, slot] = jnp.max(s, axis=0, keepdims=True)

        def accumulate(j, slot):
            for h in heads:
                m_old = m_sc[h]
                m_new = jnp.maximum(m_old, cmax_sc[h, slot])
                alpha = jnp.exp2(m_old - m_new)
                pr = jnp.exp2(s_sc[h, slot] - m_new)
                l_sc[h] = alpha * l_sc[h] + jnp.sum(pr, axis=0, keepdims=True)
                acc_sc[h] = alpha * acc_sc[h] + jnp.dot(vt_sc[h, j], pr.astype(BF16),
                                                        preferred_element_type=F32)
                m_sc[h] = m_new

        @pl.when(qi == 0)
        def _():
            scores(0, 0, diagonal=True)
            accumulate(0, 0)

        @pl.when(qi > 0)
        def _():
            scores(0, 0)

            def pair(u, carry):
                j = 2 * u
                scores(j + 1, 1)
                accumulate(j, 0)
                scores(j + 2, 0)
                accumulate(j + 1, 1)
                return carry

            lax.fori_loop(0, (qi - 1) // 2, pair, 0)

            @pl.when(qi % 2 == 1)
            def _():
                scores(qi, 1, diagonal=True)
                accumulate(qi - 1, 0)
                accumulate(qi, 1)

            @pl.when(qi % 2 == 0)
            def _():
                scores(qi - 1, 1)
                accumulate(qi - 2, 0)
                scores(qi, 0, diagonal=True)
                accumulate(qi - 1, 1)
                accumulate(qi, 0)

    lam = (jnp.exp(jnp.sum(lq1_ref[...] * lk1_ref[...], axis=-1, keepdims=True))
           - jnp.exp(jnp.sum(lq2_ref[...] * lk2_ref[...], axis=-1, keepdims=True)) + lam_init)
    for h in heads:
        acc = acc_sc[h]
        inv_l = 1.0 / l_sc[h]
        ot = acc[:, :tq] * inv_l[:, :tq] - lam * (acc[:, tq:] * inv_l[:, tq:])
        ms = jnp.mean(ot * ot, axis=0, keepdims=True)
        o = jnp.transpose(ot * lax.rsqrt(ms + EPS)) * gsub_ref[...] * (1.0 - lam_init)
        o_ref[:, hs(h)] = o.astype(o_ref.dtype)


def _diff_attn(bound, q, kf, vf, kx, vx, lam_vecs, gsub, lam_init):
    bsz, seq, _ = q.shape
    tq = ATT_TILE
    vec = _const_spec((1, DA_HEAD_DIM))
    width = DA_HEADS * LANES
    return pl.pallas_call(
        functools.partial(_diff_attn_body, lam_init=lam_init),
        grid=(bsz, seq // tq),
        in_specs=[pl.BlockSpec(memory_space=pltpu.SMEM), vec, vec, vec, vec, _const_spec((1, DA_V_DIM)),
                  pl.BlockSpec((None, tq, width), lambda b, i: (b, i, 0)),
                  _const_spec((FRONT, width)), _const_spec((FRONT, width)),
                  pl.BlockSpec((None, seq, width), lambda b, i: (b, 0, 0)),
                  pl.BlockSpec((None, seq, width), lambda b, i: (b, 0, 0))],
        out_specs=pl.BlockSpec((None, tq, width), lambda b, i: (b, i, 0)),
        out_shape=jax.ShapeDtypeStruct((bsz, seq, width), BF16),
        scratch_shapes=[pltpu.VMEM((DA_HEADS, seq // tq, DA_V_DIM, tq), BF16),
                        pltpu.VMEM((DA_HEADS, LANES, 2 * tq), BF16),
                        pltpu.VMEM((DA_HEADS, 2, tq, 2 * tq), F32), pltpu.VMEM((DA_HEADS, 2, 1, 2 * tq), F32),
                        pltpu.VMEM((DA_HEADS, 1, 2 * tq), F32), pltpu.VMEM((DA_HEADS, 1, 2 * tq), F32),
                        pltpu.VMEM((DA_HEADS, DA_V_DIM, 2 * tq), F32)],
        compiler_params=_cparams(("parallel", "arbitrary")),
        name="diff_attn",
    )(bound, *lam_vecs, gsub, q, kf, vf, kx, vx)


def _split3(a):
    a1 = a.astype(BF16)
    r1 = a - a1.astype(F32)
    a2 = r1.astype(BF16)
    a3 = (r1 - a2.astype(F32)).astype(BF16)
    return a1, a2, a3


def _gla_body(gn_ref, kf_ref, vf_ref, laf_ref, q_ref, k_ref, v_ref, la_ref, sr_ref, o_ref, st_sc, b_sc):
    g = pl.program_id(1)
    c = GLA_CHUNK
    kw = GLA_HEADS * GLA_DK
    vw = GLA_HEADS * GLA_DV

    ti = lax.broadcasted_iota(jnp.int32, (c, c), 0)
    si = lax.broadcasted_iota(jnp.int32, (c, c), 1)
    tri = (si <= ti).astype(BF16)
    hv = lax.broadcasted_iota(jnp.int32, (vw, kw), 0) // GLA_DV
    hk = lax.broadcasted_iota(jnp.int32, (vw, kw), 1) // GLA_DK
    head_mask = hv == hk
    lane_head = lax.broadcasted_iota(jnp.int32, (1, kw), 1) // GLA_DK
    causal = lax.broadcasted_iota(jnp.int32, (GLA_HEADS * c, c), 0) % c >= \
        lax.broadcasted_iota(jnp.int32, (GLA_HEADS * c, c), 1)

    def cumsum(a):
        a1, a2, a3 = _split3(a)
        return (jnp.dot(tri, a1, preferred_element_type=F32)
                + jnp.dot(tri, a2, preferred_element_type=F32)
                + jnp.dot(tri, a3, preferred_element_type=F32))

    batch = range(GLA_BATCH)

    def next_state(st, k, v, b):
        b_last = b[c - 1:c, :]
        kd = (k * jnp.exp(b_last - b)).astype(BF16)
        upd = lax.dot_general(v, kd, (((0,), (0,)), ((), ())), preferred_element_type=F32)
        return jnp.where(head_mask, jnp.exp(b_last) * st + upd, 0.0)

    @pl.when(g == 0)
    def _():
        st0 = next_state(jnp.zeros((vw, kw), F32), kf_ref[...].astype(F32), vf_ref[...], cumsum(laf_ref[...]))
        for bb in batch:
            st_sc[bb] = st0

    gn = gn_ref[...]

    def finish(o, sr):
        outs = []
        for hh in range(GLA_HEADS):
            cs = slice(hh * GLA_DV, (hh + 1) * GLA_DV)
            oh = o[:, cs]
            ms = jnp.mean(oh * oh, axis=-1, keepdims=True)
            outs.append(oh * lax.rsqrt(ms + EPS) * gn * sr[:, cs])
        return jnp.concatenate(outs, axis=1)

    n_chunks = GLA_GROUP // c
    for ci in range(n_chunks):
        rs = slice(ci * c, (ci + 1) * c)
        for bb in batch:
            b_sc[bb, rs, :] = cumsum(la_ref[bb, rs, :])
    steep = jnp.min(b_sc[...]) < -GLA_SAFE_DECAY

    @pl.when(jnp.logical_not(steep))
    def _():
        nt = (((1,), (1,)), ((), ()))
        sts = [st_sc[bb] for bb in batch]
        for ci in range(n_chunks):
            rs = slice(ci * c, (ci + 1) * c)
            bs = [b_sc[bb, rs, :] for bb in batch]
            ks = [k_ref[bb, rs, :].astype(F32) for bb in batch]
            vs = [v_ref[bb, rs, :] for bb in batch]
            qes = [q_ref[bb, rs, :].astype(F32) * jnp.exp(bs[bb]) for bb in batch]
            nxt = [next_state(sts[bb], ks[bb], vs[bb], bs[bb]) for bb in batch]
            inters = [lax.dot_general(qes[bb].astype(BF16), sts[bb].astype(BF16), nt,
                                      preferred_element_type=F32) for bb in batch]
            scs = []
            for bb in batch:
                ke = (ks[bb] * jnp.exp(-bs[bb])).astype(BF16)
                qstack = jnp.concatenate([jnp.where(lane_head == hh, qes[bb], 0.0) for hh in range(GLA_HEADS)],
                                         axis=0).astype(BF16)
                sc = lax.dot_general(qstack, ke, nt, preferred_element_type=F32)
                scs.append(jnp.where(causal, sc, 0.0).astype(BF16))
            pvs = [jnp.dot(scs[bb], vs[bb], preferred_element_type=F32) for bb in batch]
            for bb in batch:
                o = jnp.concatenate([inters[bb][:, hh * GLA_DV:(hh + 1) * GLA_DV]
                                     + pvs[bb][hh * c:(hh + 1) * c, hh * GLA_DV:(hh + 1) * GLA_DV]
                                     for hh in range(GLA_HEADS)], axis=1)
                o_ref[bb, rs, :] = finish(o, sr_ref[bb, rs, :].astype(F32)).astype(o_ref.dtype)
            sts = nxt
        for bb in batch:
            st_sc[bb] = sts[bb]

    @pl.when(steep)
    def _():
        rows = 16

        def tile(i, carry):
            off = pl.multiple_of(i * rows, rows)
            for bb in batch:
                a = la_ref[bb, pl.ds(off, rows), :]
                q = q_ref[bb, pl.ds(off, rows), :].astype(F32)
                k = k_ref[bb, pl.ds(off, rows), :].astype(F32)
                v = v_ref[bb, pl.ds(off, rows), :].astype(F32)
                outs = []
                for r in range(rows):
                    row = lambda x: x[r:r + 1].astype(BF16)
                    upd = lax.dot_general(row(v), row(k), (((0,), (0,)), ((), ())),
                                          preferred_element_type=F32)
                    st = jnp.where(head_mask, jnp.exp(a[r:r + 1]) * st_sc[bb] + upd, 0.0)
                    st_sc[bb] = st
                    outs.append(lax.dot_general(row(q), st.astype(BF16), (((1,), (1,)), ((), ())),
                                                preferred_element_type=F32))
                o = jnp.concatenate(outs, axis=0)
                o_ref[bb, pl.ds(off, rows), :] = finish(
                    o, sr_ref[bb, pl.ds(off, rows), :].astype(F32)).astype(o_ref.dtype)
            return carry

        lax.fori_loop(0, GLA_GROUP // rows, tile, 0)


def _gla(kf, vf, laf, q, k, v, la, sr, gn):
    bsz, seq, _ = q.shape
    t = GLA_GROUP
    kw = GLA_HEADS * GLA_DK
    vw = GLA_HEADS * GLA_DV
    last = FRONT // GLA_CHUNK - 1
    nb = GLA_BATCH
    assert bsz % nb == 0
    fr = lambda w: pl.BlockSpec((GLA_CHUNK, w), lambda b, g: (last, 0))
    xs = lambda w: pl.BlockSpec((nb, t, w), lambda b, g: (b, g, 0))
    return pl.pallas_call(
        _gla_body,
        grid=(bsz // nb, seq // t),
        in_specs=[_const_spec((1, GLA_DV)), fr(kw), fr(vw), fr(kw), xs(kw), xs(kw), xs(vw), xs(kw), xs(vw)],
        out_specs=xs(vw),
        out_shape=jax.ShapeDtypeStruct((bsz, seq, vw), BF16),
        scratch_shapes=[pltpu.VMEM((nb, vw, kw), F32), pltpu.VMEM((nb, t, kw), F32)],
        compiler_params=_cparams(("parallel", "arbitrary")),
        name="gla",
    )(gn, kf, vf, laf, q, k, v, la, sr)


def _merge_body(oa_ref, ob_ref, u_ref, gmix_ref, wgate_ref, bgate_ref, wb0_ref, wb1_ref, wout_ref, gffn_ref,
                wr_ref, br_ref,
                u1_out, h2_out, info_out, cnt_out, cnt_sc):
    i = pl.program_id(0)
    tm = ROW_TILE

    @pl.when(i == 0)
    def _():
        cnt_sc[...] = jnp.zeros(cnt_sc.shape, F32)

    ya = jnp.dot(oa_ref[...], wb0_ref[...], preferred_element_type=F32)
    yb = jnp.dot(ob_ref[...], wb1_ref[...], preferred_element_type=F32)
    x = u_ref[...]
    h = (x * lax.rsqrt(jnp.mean(x * x, axis=-1, keepdims=True) + EPS) * gmix_ref[...]).astype(BF16)
    gate = _sigmoid(jnp.dot(h, wgate_ref[...], preferred_element_type=F32) + bgate_ref[...])
    merged = gate[:, :D_MODEL] * ya + gate[:, D_MODEL:] * yb
    u1 = x + jnp.dot(merged.astype(BF16), wout_ref[...], preferred_element_type=F32)
    u1_out[...] = u1
    ms = jnp.mean(u1 * u1, axis=-1, keepdims=True)
    h2f = u1 * lax.rsqrt(ms + EPS) * gffn_ref[...]
    _store_row_tiles(h2_out, h2f)
    h2 = h2f.astype(BF16)

    logits = lax.dot_general(wr_ref[...], h2, (((1,), (1,)), ((), ())), preferred_element_type=F32) + br_ref[...]
    row = lax.broadcasted_iota(jnp.int32, (ROUTER_ROWS, tm), 0)
    is_group = row < N_GROUPS
    gl = jnp.where(is_group, logits, NEG_BIG)
    gmax = jnp.max(gl, axis=0, keepdims=True)
    g_idx = jnp.min(jnp.where(gl == gmax, row, ROUTER_ROWS), axis=0, keepdims=True)
    g_w = 1.0 / jnp.sum(jnp.where(is_group, jnp.exp(gl - gmax), 0.0), axis=0, keepdims=True)
    lo = N_GROUPS + EXPERTS_PER_GROUP * g_idx
    el = jnp.where((row >= lo) & (row < lo + EXPERTS_PER_GROUP), logits, NEG_BIG)
    v1 = jnp.max(el, axis=0, keepdims=True)
    i1 = jnp.min(jnp.where(el == v1, row, ROUTER_ROWS), axis=0, keepdims=True)
    el2 = jnp.where(row == i1, NEG_BIG, el)
    v2 = jnp.max(el2, axis=0, keepdims=True)
    i2 = jnp.min(jnp.where(el2 == v2, row, ROUTER_ROWS), axis=0, keepdims=True)
    e21 = jnp.exp(v2 - v1)
    w1 = g_w / (1.0 + e21)
    w2 = w1 * e21

    onehot = ((row == i1) | (row == i2)).astype(BF16)
    si = lax.broadcasted_iota(jnp.int32, (tm, tm), 0)
    ti = lax.broadcasted_iota(jnp.int32, (tm, tm), 1)
    earlier = (si < ti).astype(BF16)
    cnt = cnt_sc[...]
    before = jnp.dot(onehot, earlier, preferred_element_type=F32) + jnp.tile(cnt, (1, tm // LANES))
    r1 = jnp.sum(jnp.where(row == i1, before, 0.0), axis=0, keepdims=True)
    r2 = jnp.sum(jnp.where(row == i2, before, 0.0), axis=0, keepdims=True)
    cnt = cnt + jnp.dot(onehot, jnp.ones((tm, LANES), BF16), preferred_element_type=F32)
    cnt_sc[...] = cnt
    cnt_out[...] = cnt

    zero = jnp.zeros_like(w1)
    info_out[...] = jnp.concatenate([(i1 - N_GROUPS).astype(F32), (i2 - N_GROUPS).astype(F32),
                                     w1, w2, r1, r2, zero, zero], axis=0)


def _merge(oa, ob, u, p):
    n = u.shape[0]
    tm = ROW_TILE
    row = lambda w: pl.BlockSpec((tm, w), lambda i: (i, 0))
    return pl.pallas_call(
        _merge_body,
        grid=(n // tm,),
        in_specs=[row(512), row(512), row(D_MODEL),
                  _const_spec((1, D_MODEL)), _const_spec((D_MODEL, 2 * D_MODEL)), _const_spec((1, 2 * D_MODEL)),
                  _const_spec((512, D_MODEL)), _const_spec((512, D_MODEL)), _const_spec((D_MODEL, D_MODEL)),
                  _const_spec((1, D_MODEL)), _const_spec((ROUTER_ROWS, D_MODEL)), _const_spec((ROUTER_ROWS, tm))],
        out_specs=[row(D_MODEL), _row_tile_spec(tm, lambda i: (i, 0)), pl.BlockSpec((8, tm), lambda i: (0, i)),
                   _const_spec((ROUTER_ROWS, LANES))],
        out_shape=[jax.ShapeDtypeStruct((n, D_MODEL), F32), jax.ShapeDtypeStruct((n * RT, LANES), F32),
                   jax.ShapeDtypeStruct((8, n), F32), jax.ShapeDtypeStruct((ROUTER_ROWS, LANES), F32)],
        scratch_shapes=[pltpu.VMEM((ROUTER_ROWS, LANES), F32)],
        compiler_params=_cparams(("arbitrary",)),
        name="merge_router",
    )(oa, ob, u, p['gmix'], p['wgate'], p['bgate'], p['wb0'], p['wb1'], p['wout'], p['gffn'], p['wr'], p['br'])


def _dispatch_body(tail_ref, nb_ref, dest_ref, h2_ref, xs_hbm, zero_sc, sem, zsem):
    i = pl.program_id(0)
    blk_rows = MOE_TILE * RT
    n_blocks = xs_hbm.shape[0] // blk_rows

    def zero_copy(blk):
        dst = xs_hbm.at[pl.ds(pl.multiple_of(blk * blk_rows, blk_rows), blk_rows)]
        return pltpu.make_async_copy(zero_sc, dst, zsem)

    @pl.when(i == 0)
    def _():
        zero_sc[...] = jnp.zeros(zero_sc.shape, F32)

        def tails(fn):
            def body(e, carry):
                @pl.when(tail_ref[e] >= 0)
                def _():
                    fn(zero_copy(tail_ref[e]))
                return carry
            lax.fori_loop(0, N_EXPERTS, body, 0)

        def unused(fn):
            def body(b, carry):
                fn(zero_copy(b))
                return carry
            lax.fori_loop(nb_ref[0], n_blocks, body, 0)

        tails(lambda cp: cp.start())
        unused(lambda cp: cp.start())
        tails(lambda cp: cp.wait())
        unused(lambda cp: cp.wait())

    def start(r, carry):
        src = _token_rows(h2_ref, r)
        pltpu.make_async_copy(src, _token_rows(xs_hbm, dest_ref[0, 0, r]), sem).start()
        pltpu.make_async_copy(src, _token_rows(xs_hbm, dest_ref[0, 1, r]), sem).start()
        return carry

    lax.fori_loop(0, DMA_TILE, start, 0)
    for _ in range(2):
        pltpu.make_async_copy(h2_ref, xs_hbm.at[pl.ds(0, DMA_TILE * RT)], sem).wait()


def _dispatch(tail_blocks, n_used, dest, h2, n_slots):
    n = h2.shape[0] // RT
    grid_spec = pltpu.PrefetchScalarGridSpec(
        num_scalar_prefetch=2,
        grid=(n // DMA_TILE,),
        in_specs=[pl.BlockSpec((1, 2, DMA_TILE), lambda i, tb, nb: (i, 0, 0), memory_space=pltpu.SMEM),
                  _row_tile_spec(DMA_TILE, lambda i, tb, nb: (i, 0))],
        out_specs=pl.BlockSpec(memory_space=pl.ANY),
        scratch_shapes=[pltpu.VMEM((MOE_TILE * RT, LANES), F32), pltpu.SemaphoreType.DMA(()),
                        pltpu.SemaphoreType.DMA(())],
    )
    return pl.pallas_call(
        _dispatch_body,
        grid_spec=grid_spec,
        out_shape=jax.ShapeDtypeStruct((n_slots * RT, LANES), F32),
        compiler_params=_cparams(("arbitrary",)),
        name="dispatch",
    )(tail_blocks, n_used, dest, h2)


def _experts_body(be_ref, nb_ref, x_ref, wg_ref, wu_ref, wd_ref, y_ref, wg_sc, wu_sc, wd_sc):
    i = pl.program_id(0)
    prev = be_ref[jnp.maximum(i - 1, 0)]
    fresh = (i == 0) | (be_ref[i] != prev)

    @pl.when(fresh)
    def _():
        wg_sc[...] = wg_ref[...].astype(BF16)
        wu_sc[...] = wu_ref[...].astype(BF16)
        wd_sc[...] = wd_ref[...].astype(BF16)

    @pl.when(i < nb_ref[0])
    def _():
        x = _load_row_tiles(x_ref, MOE_TILE).astype(BF16)
        y = jnp.zeros((MOE_TILE, D_MODEL), F32)
        for j in range(D_EXPERT // EXPERT_CHUNK):
            cs = slice(j * EXPERT_CHUNK, (j + 1) * EXPERT_CHUNK)
            gp = jnp.dot(x, wg_sc[:, cs], preferred_element_type=F32)
            up = jnp.dot(x, wu_sc[:, cs], preferred_element_type=F32)
            hid = (gp * _sigmoid(gp) * up).astype(BF16)
            y = y + jnp.dot(hid, wd_sc[cs, :], preferred_element_type=F32)
        _store_row_tiles(y_ref, y)

    @pl.when(i >= nb_ref[0])
    def _():
        y_ref[...] = jnp.zeros(y_ref.shape, F32)


def _experts(block_e, n_used, xs, wg, wu, wd):
    n_slots = xs.shape[0] // RT
    n_blocks = n_slots // MOE_TILE
    xmap = lambda i, be, nb: (jnp.minimum(i, nb[0] - 1), 0)
    wmap = lambda i, be, nb: (be[i], 0, 0)
    grid_spec = pltpu.PrefetchScalarGridSpec(
        num_scalar_prefetch=2,
        grid=(n_blocks,),
        in_specs=[_row_tile_spec(MOE_TILE, xmap),
                  pl.BlockSpec((None, D_MODEL, D_EXPERT), wmap),
                  pl.BlockSpec((None, D_MODEL, D_EXPERT), wmap),
                  pl.BlockSpec((None, D_EXPERT, D_MODEL), wmap)],
        out_specs=_row_tile_spec(MOE_TILE, lambda i, be, nb: (i, 0)),
        scratch_shapes=[pltpu.VMEM((D_MODEL, D_EXPERT), BF16), pltpu.VMEM((D_MODEL, D_EXPERT), BF16),
                        pltpu.VMEM((D_EXPERT, D_MODEL), BF16)],
    )
    return pl.pallas_call(
        _experts_body,
        grid_spec=grid_spec,
        out_shape=jax.ShapeDtypeStruct((n_slots * RT, LANES), F32),
        compiler_params=_cparams(("arbitrary",)),
        name="experts",
    )(block_e, n_used, xs, wg, wu, wd)


def _combine_body(dest_ref, dest_next_ref, w_ref, u1_ref, ys_hbm, o_ref, ybuf, sems):
    i = pl.program_id(0)
    n_steps = pl.num_programs(0)
    t = DMA_TILE

    def gather(d_ref, slot):
        buf = ybuf.at[slot]

        def start(r, carry):
            pltpu.make_async_copy(_token_rows(ys_hbm, d_ref[0, 0, r]), _token_rows(buf, r),
                                  sems.at[slot]).start()
            pltpu.make_async_copy(_token_rows(ys_hbm, d_ref[0, 1, r]), _token_rows(buf, t + r),
                                  sems.at[slot]).start()
            return carry

        lax.fori_loop(0, t, start, 0)

    slot = i % 2

    @pl.when(i == 0)
    def _():
        gather(dest_ref, 0)

    @pl.when(i + 1 < n_steps)
    def _():
        gather(dest_next_ref, 1 - slot)

    buf = ybuf.at[slot]
    pltpu.make_async_copy(ys_hbm.at[pl.ds(0, buf.shape[0])], buf, sems.at[slot]).wait()
    w = w_ref[...]
    o_ref[...] = (u1_ref[...] + w[:, 0:1] * _load_row_tiles(buf, t) + w[:, 1:2] * _load_row_tiles(buf, t, t))


def _combine(dest, w, u1, ys):
    n = u1.shape[0]
    t = DMA_TILE
    n_steps = n // t
    return pl.pallas_call(
        _combine_body,
        grid=(n_steps,),
        in_specs=[pl.BlockSpec((1, 2, t), lambda i: (i, 0, 0), memory_space=pltpu.SMEM),
                  pl.BlockSpec((1, 2, t), lambda i: (jnp.minimum(i + 1, n_steps - 1), 0, 0),
                               memory_space=pltpu.SMEM),
                  pl.BlockSpec((t, 2), lambda i: (i, 0)),
                  pl.BlockSpec((t, D_MODEL), lambda i: (i, 0)),
                  pl.BlockSpec(memory_space=pl.ANY)],
        out_specs=pl.BlockSpec((t, D_MODEL), lambda i: (i, 0)),
        out_shape=jax.ShapeDtypeStruct((n, D_MODEL), F32),
        scratch_shapes=[pltpu.VMEM((2, 2 * t * RT, LANES), F32), pltpu.SemaphoreType.DMA((2,))],
        compiler_params=_cparams(("arbitrary",)),
        name="combine",
    )(dest, dest, w, u1, ys)


def _rope_tables(pos):
    half = DA_HEAD_DIM // 2
    inv_freq = jnp.power(ROPE_THETA, -jnp.arange(half, dtype=F32) * 2.0 / DA_HEAD_DIM)
    ang = pos[:, None] * inv_freq[None, :]
    cos, sin = jnp.cos(ang), jnp.sin(ang)
    cos_t = jnp.tile(cos, (1, LANES // half))
    sin_t = jnp.tile(jnp.concatenate([-sin, sin], axis=1), (1, LANES // DA_HEAD_DIM))
    return cos_t, sin_t


def _layer(x, meta_tokens, l, g_mix_norm, w_in, g_q_norm, g_k_norm, lambda_q1, lambda_k1, lambda_q2, lambda_k2,
           g_diff_subln, w_gla_gate_up, b_gla_gate, g_gla_norm, w_branch, b_merge_gate, w_out, g_ffn_norm,
           w_router_group, b_router_group, w_router_expert, b_router_expert, w_exp_gate, w_exp_up, w_exp_down):
    bsz, seq, _ = x.shape
    n = bsz * seq

    wi = w_in[l]
    offs = [0]
    for s in (512, 512, 512, 256, 256, 512, 512, GLA_RANK, 2 * D_MODEL):
        offs.append(offs[-1] + s)
    sec = lambda j: wi[:, offs[j]:offs[j + 1]].astype(BF16)
    p = {
        'gmix': g_mix_norm[l][None, :],
        'gqn': jnp.tile(g_q_norm[l], LANES // DA_HEAD_DIM)[None, :],
        'gkn': jnp.tile(g_k_norm[l], LANES // DA_HEAD_DIM)[None, :],
        'wq': sec(0), 'wk': sec(1), 'wv': sec(2), 'wgq': sec(3), 'wgk': sec(4), 'wgv': sec(5), 'wgr': sec(6),
        'wgg': jnp.pad(sec(7), ((0, 0), (0, LANES - GLA_RANK))),
        'wup': jnp.pad(w_gla_gate_up[l].astype(BF16), ((0, LANES - GLA_RANK), (0, 0))),
        'bup': b_gla_gate[l][None, :],
        'wgate': sec(8),
        'bgate': b_merge_gate[l].reshape(1, 2 * D_MODEL),
        'wb0': w_branch[l, 0].astype(BF16), 'wb1': w_branch[l, 1].astype(BF16),
        'wout': w_out[l].astype(BF16),
        'gffn': g_ffn_norm[l][None, :],
        'wr': jnp.pad(jnp.concatenate([w_router_group[l], w_router_expert[l].reshape(D_MODEL, N_EXPERTS)],
                                      axis=1).T.astype(BF16), ((0, ROUTER_ROWS - N_GROUPS - N_EXPERTS), (0, 0))),
        'br': jnp.broadcast_to(
            jnp.pad(jnp.concatenate([b_router_group[l], b_router_expert[l].reshape(N_EXPERTS)]),
                    (0, ROUTER_ROWS - N_GROUPS - N_EXPERTS))[:, None], (ROUTER_ROWS, ROW_TILE)),
    }

    u_front = jnp.concatenate([jnp.zeros((FRONT - N_META, D_MODEL), F32), meta_tokens.astype(F32)], axis=0)
    cos_f, sin_f = _rope_tables(jnp.arange(FRONT, dtype=F32) - (FRONT - N_META))
    cos_x, sin_x = _rope_tables(jnp.arange(seq, dtype=F32) + N_META)
    front = _inproj(u_front, FRONT, cos_f, sin_f, p)
    xin = _inproj(x.reshape(n, D_MODEL), ROW_TILE, cos_x, sin_x, p)
    q, k, v, gq, gk, gv, sr, la = [a.reshape(bsz, seq, a.shape[-1]) for a in xin]
    _, kf, vf, _, gkf, gvf, _, laf = front

    lam_init = 0.8 - 0.6 * math.exp(-0.3 * l)
    lam_vecs = [a[l][None, :] for a in (lambda_q1, lambda_k1, lambda_q2, lambda_k2)]
    score_bound = (ATT_BOUND_MARGIN * DA_HEAD_DIM * Q_SCALE
                   * jnp.max(jnp.abs(g_q_norm[l])) * jnp.max(jnp.abs(g_k_norm[l]))).reshape(1).astype(F32)
    o_a = _diff_attn(score_bound, q, kf, vf, k, v, lam_vecs, g_diff_subln[l][None, :], lam_init)
    o_b = _gla(gkf, gvf, laf, gq, gk, gv, la, sr, g_gla_norm[l][None, :])

    u1, h2, info, cnt = _merge(o_a.reshape(n, -1), o_b.reshape(n, -1), x.reshape(n, D_MODEL), p)

    ids = info[0:2].astype(jnp.int32)
    wts = info[2:4]
    rank = info[4:6].astype(jnp.int32)
    counts = cnt[N_GROUPS:N_GROUPS + N_EXPERTS, 0].astype(jnp.int32)
    padded = (counts + MOE_TILE - 1) // MOE_TILE * MOE_TILE
    pends = jnp.cumsum(padded)
    pstarts = pends - padded
    expert = jnp.arange(N_EXPERTS, dtype=jnp.int32)
    dest = jnp.sum(jnp.where(ids[..., None] == expert, pstarts, 0), axis=-1) + rank
    n_slots = (2 * n // MOE_TILE + N_EXPERTS) * MOE_TILE
    n_blocks = n_slots // MOE_TILE
    n_used = (pends[-1] // MOE_TILE).astype(jnp.int32)
    blk = jnp.minimum(jnp.arange(n_blocks, dtype=jnp.int32), n_used - 1) * MOE_TILE
    block_e = jnp.minimum(jnp.sum(pends[None, :] <= blk[:, None], axis=1), N_EXPERTS - 1).astype(jnp.int32)
    tail_blocks = jnp.where(counts > 0, pends // MOE_TILE - 1, -1).astype(jnp.int32)
    dest_t = dest.reshape(2, n // DMA_TILE, DMA_TILE).transpose(1, 0, 2)

    xs = _dispatch(tail_blocks, n_used[None], dest_t, h2, n_slots)
    ys = _experts(block_e, n_used[None], xs, w_exp_gate[l], w_exp_up[l], w_exp_down[l])
    out = _combine(dest_t, wts.T, u1, ys)
    return out.reshape(bsz, seq, D_MODEL)


def kernel(x, meta_tokens, g_mix_norm, w_in, g_q_norm, g_k_norm, lambda_q1, lambda_k1, lambda_q2, lambda_k2,
           g_diff_subln, w_gla_gate_up, b_gla_gate, g_gla_norm, w_branch, b_merge_gate, w_out, g_ffn_norm,
           w_router_group, b_router_group, w_router_expert, b_router_expert, w_exp_gate, w_exp_up, w_exp_down):
    depth = w_in.shape[0]
    assert depth == 1, "meta tokens are only carried through a single layer in this implementation"
    assert x.shape[1] % ROW_TILE == 0 and x.shape[2] == D_MODEL
    return _layer(x, meta_tokens, 0, g_mix_norm, w_in, g_q_norm, g_k_norm, lambda_q1, lambda_k1, lambda_q2,
                  lambda_k2, g_diff_subln, w_gla_gate_up, b_gla_gate, g_gla_norm, w_branch, b_merge_gate, w_out,
                  g_ffn_norm, w_router_group, b_router_group, w_router_expert, b_router_expert,
                  w_exp_gate, w_exp_up, w_exp_down)
```

```python
import functools
import math

import jax
import jax.numpy as jnp
from jax import lax
from jax.experimental import pallas as pl
from jax.experimental.pallas import tpu as pltpu

F32 = jnp.float32
BF16 = jnp.bfloat16

D_MODEL = 1024
N_META = 16
EPS = 1e-6
ROPE_THETA = 10000.0

DA_HEADS = 4
DA_HEAD_DIM = 64
DA_V_DIM = 128
Q_SCALE = DA_HEAD_DIM ** -0.5 * math.log2(math.e)
ATT_BOUND_MARGIN = 1.02
ATT_SAFE_BOUND = 60.0
GLA_HEADS = 4
GLA_DK = 64
GLA_DV = 128
GLA_RANK = 16
GLA_TAU = 16.0
GLA_CHUNK = 64
GLA_SAFE_DECAY = 60.0
N_GROUPS = 4
EXPERTS_PER_GROUP = 8
N_EXPERTS = 32
D_EXPERT = 512
ROUTER_ROWS = 48
MIXER_SECTIONS = ((0, 512), (512, 512), (1024, 512), (1536, 256), (1792, 256), (2048, 512), (2560, 512),
                  (3072, 128))
GATE_OFFSET = 3072 + GLA_RANK

LANES = 128
FRONT = 256
ATT_TILE = 256
ROW_TILE = 512
GLA_GROUP = 512
GLA_BATCH = 4
MOE_TILE = 512
EXPERT_CHUNK = 256
DMA_TILE = 512
NEG_BIG = -1e30
VMEM_LIMIT = 56 * 1024 * 1024


def _cparams(sem):
    return pltpu.CompilerParams(dimension_semantics=sem, vmem_limit_bytes=VMEM_LIMIT)


def _const_spec(shape):
    nd = len(shape)
    return pl.BlockSpec(shape, lambda *_: (0,) * nd)


RT = D_MODEL // LANES


def _row_tile_spec(rows, index_map):
    return pl.BlockSpec((rows * RT, LANES), index_map)


def _token_rows(ref, tok):
    return ref.at[pl.ds(pl.multiple_of(tok * RT, RT), RT)]


def _load_row_tiles(ref, rows, first=0):
    return jnp.concatenate([ref[pl.ds(first * RT + c, rows, stride=RT), :] for c in range(RT)], axis=1)


def _store_row_tiles(ref, val):
    for c in range(RT):
        ref[pl.ds(c, val.shape[0], stride=RT), :] = val[:, c * LANES:(c + 1) * LANES]


def _sigmoid(x):
    return 0.5 * jnp.tanh(0.5 * x) + 0.5


def _log_sigmoid(x):
    return jnp.minimum(x, 0.0) - jnp.log1p(jnp.exp(-jnp.abs(x)))


def _inproj_body(u_ref, gmix_ref, cos_ref, sin_ref, gqn_ref, gkn_ref,
                 wq_ref, wk_ref, wv_ref, wgq_ref, wgk_ref, wgv_ref, wgr_ref, wgg_ref,
                 wup_ref, bup_ref,
                 q_out, k_out, v_out, gq_out, gk_out, gv_out, sr_out, la_out):
    x = u_ref[...]
    ms = jnp.mean(x * x, axis=-1, keepdims=True)
    h = (x * lax.rsqrt(ms + EPS) * gmix_ref[...]).astype(BF16)

    cos = cos_ref[...]
    sin = sin_ref[...]
    lane = lax.broadcasted_iota(jnp.int32, (1, LANES), 1)
    first_half = (lane % DA_HEAD_DIM) < (DA_HEAD_DIM // 2)
    gi = lax.broadcasted_iota(jnp.int32, (LANES, LANES), 0) // DA_HEAD_DIM
    gj = lax.broadcasted_iota(jnp.int32, (LANES, LANES), 1) // DA_HEAD_DIM
    group_sum = (gi == gj).astype(BF16)

    def norm_rope(w_ref, gain_ref, out_ref, scale, transposed):
        z = jnp.dot(h, w_ref[...], preferred_element_type=F32)
        for hh in range(DA_HEADS):
            hs = slice(hh * LANES, (hh + 1) * LANES)
            zh = z[:, hs]
            ssq = jnp.dot((zh * zh).astype(BF16), group_sum, preferred_element_type=F32)
            zn = zh * lax.rsqrt(ssq * (1.0 / DA_HEAD_DIM) + EPS) * gain_ref[...]
            rot = jnp.where(first_half,
                            pltpu.roll(zn, LANES - DA_HEAD_DIM // 2, 1),
                            pltpu.roll(zn, DA_HEAD_DIM // 2, 1))
            zr = (zn * cos + rot * sin) * scale
            if transposed:
                out_ref[hs, :] = jnp.transpose(zr).astype(out_ref.dtype)
            else:
                out_ref[:, hs] = zr.astype(out_ref.dtype)

    norm_rope(wq_ref, gqn_ref, q_out, Q_SCALE, transposed=True)
    norm_rope(wk_ref, gkn_ref, k_out, 1.0, transposed=False)
    v_out[...] = jnp.dot(h, wv_ref[...], preferred_element_type=F32).astype(v_out.dtype)

    gq_out[...] = (jnp.dot(h, wgq_ref[...], preferred_element_type=F32) * (GLA_DK ** -0.5)).astype(gq_out.dtype)
    gk_out[...] = jnp.dot(h, wgk_ref[...], preferred_element_type=F32).astype(gk_out.dtype)
    gv_out[...] = jnp.dot(h, wgv_ref[...], preferred_element_type=F32).astype(gv_out.dtype)
    r = jnp.dot(h, wgr_ref[...], preferred_element_type=F32)
    sr_out[...] = (r * _sigmoid(r)).astype(sr_out.dtype)

    g_lr = jnp.dot(h, wgg_ref[...], preferred_element_type=F32)
    pre = jnp.dot(g_lr.astype(BF16), wup_ref[...], preferred_element_type=F32) + bup_ref[...]
    la_out[...] = _log_sigmoid(pre) * (1.0 / GLA_TAU)


def _inproj(u, tm, cos, sin, p):
    rows = u.shape[0]
    n_tab = cos.shape[0] // tm
    row = lambda w: pl.BlockSpec((tm, w), lambda i: (i, 0))
    tab = pl.BlockSpec((tm, LANES), lambda i: (i % n_tab, 0))
    sections = [pl.BlockSpec((D_MODEL, w), lambda i, j=off // w: (0, j)) for off, w in MIXER_SECTIONS]
    assert all(off % w == 0 for off, w in MIXER_SECTIONS)
    out_widths = [(512, BF16), (512, BF16), (512, BF16), (256, BF16), (256, BF16), (512, BF16),
                  (512, BF16), (256, F32)]
    return pl.pallas_call(
        _inproj_body,
        grid=(rows // tm,),
        in_specs=[row(D_MODEL), _const_spec((1, D_MODEL)), tab, tab,
                  _const_spec((1, LANES)), _const_spec((1, LANES))]
                 + sections + [_const_spec(p['wup'].shape), _const_spec(p['bup'].shape)],
        out_specs=[pl.BlockSpec((out_widths[0][0], tm), lambda i: (0, i))] + [row(w) for w, _ in out_widths[1:]],
        out_shape=[jax.ShapeDtypeStruct((out_widths[0][0], rows), BF16)]
                  + [jax.ShapeDtypeStruct((rows, w), dt) for w, dt in out_widths[1:]],
        compiler_params=_cparams(("parallel",)),
        name="inproj",
    )(u, p['gmix'], cos, sin, p['gqn'], p['gkn'], *([p['w_in']] * len(MIXER_SECTIONS)), p['wup'], p['bup'])


def _diff_attn_body(bound_ref, lq1_ref, lk1_ref, lq2_ref, lk2_ref, gsub_ref,
                    q_ref, kf_ref, vf_ref, kx_ref, vx_ref, o_ref, vt_sc, qs_sc, s_sc, cmax_sc, m_sc, l_sc, acc_sc,
                    *, lam_init):
    qi = pl.program_id(1)
    tq = ATT_TILE
    n_kv = kx_ref.shape[0] // tq
    heads = range(DA_HEADS)
    hs = lambda h: slice(h * LANES, (h + 1) * LANES)

    @pl.when(qi == 0)
    def _():
        def tr(j, carry):
            off = pl.multiple_of(j * tq, tq)
            for h in heads:
                vt_sc[h, j] = jnp.transpose(vx_ref[pl.ds(off, tq), hs(h)].astype(F32)).astype(BF16)
            return carry

        lax.fori_loop(0, n_kv, tr, 0)

    d = lax.broadcasted_iota(jnp.int32, (LANES, tq), 0)
    for h in heads:
        qt = q_ref[hs(h), :]
        zero = jnp.zeros_like(qt)
        qs_sc[h] = jnp.concatenate([jnp.where(d < DA_HEAD_DIM, qt, zero),
                                    jnp.where(d >= DA_HEAD_DIM, qt, zero)], axis=1)

    meta = slice(FRONT - N_META, FRONT)
    key = lax.broadcasted_iota(jnp.int32, (tq, 2 * tq), 0)
    qry = lax.broadcasted_iota(jnp.int32, (tq, 2 * tq), 1) % tq
    causal = key <= qry
    bound = bound_ref[0]

    def meta_scores(h):
        s = jnp.dot(kf_ref[meta, hs(h)], qs_sc[h], preferred_element_type=F32)
        vt = jnp.transpose(vf_ref[meta, hs(h)].astype(F32)).astype(BF16)
        return s, vt

    def block_scores(j, h, diagonal):
        off = pl.multiple_of(j * tq, tq)
        s = jnp.dot(kx_ref[pl.ds(off, tq), hs(h)], qs_sc[h], preferred_element_type=F32)
        return jnp.where(causal, s, NEG_BIG) if diagonal else s

    @pl.when(bound <= ATT_SAFE_BOUND)
    def _():
        l_sc[...] = jnp.zeros(l_sc.shape, F32)
        acc_sc[...] = jnp.zeros(acc_sc.shape, F32)

        def blocks(*js, last=False):
            work = [(h, block_scores(j, h, last and j is js[-1]), (h, j)) for j in js for h in heads]
            if last:
                work += [(h,) + meta_scores(h) for h in heads]
            ps = [(h, jnp.exp2(s - bound), vt) for h, s, vt in work]
            for h, pr, _ in ps:
                l_sc[h] += jnp.sum(pr, axis=0, keepdims=True)
            for h, pr, vt in ps:
                vt = vt_sc[vt] if isinstance(vt, tuple) else vt
                acc_sc[h] += jnp.dot(vt, pr.astype(BF16), preferred_element_type=F32)

        def pair(u, carry):
            blocks(2 * u, 2 * u + 1)
            return carry

        lax.fori_loop(0, qi // 2, pair, 0)

        @pl.when(qi % 2 == 1)
        def _():
            blocks(qi - 1, qi, last=True)

        @pl.when(qi % 2 == 0)
        def _():
            blocks(qi, last=True)

    @pl.when(bound > ATT_SAFE_BOUND)
    def _():
        for h in heads:
            s, vt = meta_scores(h)
            m0 = jnp.max(s, axis=0, keepdims=True)
            pr = jnp.exp2(s - m0)
            m_sc[h] = m0
            l_sc[h] = jnp.sum(pr, axis=0, keepdims=True)
            acc_sc[h] = jnp.dot(vt, pr.astype(BF16), preferred_element_type=F32)

        def scores(j, slot, diagonal=False):
            for h in heads:
                s = block_scores(j, h, diagonal)
                s_sc[h, slot] = s
                cmax_sc[h, slot] = jnp.max(s, axis=0, keepdims=True)

        def accumulate(j, slot):
            for h in heads:
                m_old = m_sc[h]
                m_new = jnp.maximum(m_old, cmax_sc[h, slot])
                alpha = jnp.exp2(m_old - m_new)
                pr = jnp.exp2(s_sc[h, slot] - m_new)
                l_sc[h] = alpha * l_sc[h] + jnp.sum(pr, axis=0, keepdims=True)
                acc_sc[h] = alpha * acc_sc[h] + jnp.dot(vt_sc[h, j], pr.astype(BF16),
                                                        preferred_element_type=F32)
                m_sc[h] = m_new

        @pl.when(qi == 0)
        def _():
            scores(0, 0, diagonal=True)
            accumulate(0, 0)

        @pl.when(qi > 0)
        def _():
            scores(0, 0)

            def pair(u, carry):
                j = 2 * u
                scores(j + 1, 1)
                accumulate(j, 0)
                scores(j + 2, 0)
                accumulate(j + 1, 1)
                return carry

            lax.fori_loop(0, (qi - 1) // 2, pair, 0)

            @pl.when(qi % 2 == 1)
            def _():
                scores(qi, 1, diagonal=True)
                accumulate(qi - 1, 0)
                accumulate(qi, 1)

            @pl.when(qi % 2 == 0)
            def _():
                scores(qi - 1, 1)
                accumulate(qi - 2, 0)
                scores(qi, 0, diagonal=True)
                accumulate(qi - 1, 1)
                accumulate(qi, 0)

    lam = (jnp.exp(jnp.sum(lq1_ref[...] * lk1_ref[...], axis=-1, keepdims=True))
           - jnp.exp(jnp.sum(lq2_ref[...] * lk2_ref[...], axis=-1, keepdims=True)) + lam_init)
    for h in heads:
        acc = acc_sc[h]
        inv_l = 1.0 / l_sc[h]
        ot = acc[:, :tq] * inv_l[:, :tq] - lam * (acc[:, tq:] * inv_l[:, tq:])
        ms = jnp.mean(ot * ot, axis=0, keepdims=True)
        o_ref[hs(h), :] = (ot * lax.rsqrt(ms + EPS) * gsub_ref[...] * (1.0 - lam_init)).astype(o_ref.dtype)


def _diff_attn(bound, q_t, kf, vf, kx, vx, lam_vecs, gsub_t, lam_init):
    bsz, seq, _ = kx.shape
    tq = ATT_TILE
    vec = _const_spec((1, DA_HEAD_DIM))
    width = DA_HEADS * LANES
    tok_tile = pl.BlockSpec((width, tq), lambda b, i: (0, b * (seq // tq) + i))
    return pl.pallas_call(
        functools.partial(_diff_attn_body, lam_init=lam_init),
        grid=(bsz, seq // tq),
        in_specs=[pl.BlockSpec(memory_space=pltpu.SMEM), vec, vec, vec, vec, _const_spec((DA_V_DIM, tq)),
                  tok_tile,
                  _const_spec((FRONT, width)), _const_spec((FRONT, width)),
                  pl.BlockSpec((None, seq, width), lambda b, i: (b, 0, 0)),
                  pl.BlockSpec((None, seq, width), lambda b, i: (b, 0, 0))],
        out_specs=tok_tile,
        out_shape=jax.ShapeDtypeStruct((width, bsz * seq), BF16),
        scratch_shapes=[pltpu.VMEM((DA_HEADS, seq // tq, DA_V_DIM, tq), BF16),
                        pltpu.VMEM((DA_HEADS, LANES, 2 * tq), BF16),
                        pltpu.VMEM((DA_HEADS, 2, tq, 2 * tq), F32), pltpu.VMEM((DA_HEADS, 2, 1, 2 * tq), F32),
                        pltpu.VMEM((DA_HEADS, 1, 2 * tq), F32), pltpu.VMEM((DA_HEADS, 1, 2 * tq), F32),
                        pltpu.VMEM((DA_HEADS, DA_V_DIM, 2 * tq), F32)],
        compiler_params=_cparams(("parallel", "arbitrary")),
        name="diff_attn",
    )(bound, *lam_vecs, gsub_t, q_t, kf, vf, kx, vx)


def _split3(a):
    a1 = a.astype(BF16)
    r1 = a - a1.astype(F32)
    a2 = r1.astype(BF16)
    a3 = (r1 - a2.astype(F32)).astype(BF16)
    return a1, a2, a3


def _gla_body(gn_ref, kf_ref, vf_ref, laf_ref, q_ref, k_ref, v_ref, la_ref, sr_ref, o_ref, st_sc, b_sc):
    g = pl.program_id(1)
    c = GLA_CHUNK
    kw = GLA_HEADS * GLA_DK
    vw = GLA_HEADS * GLA_DV

    ti = lax.broadcasted_iota(jnp.int32, (c, c), 0)
    si = lax.broadcasted_iota(jnp.int32, (c, c), 1)
    tri = (si <= ti).astype(BF16)
    hv = lax.broadcasted_iota(jnp.int32, (vw, kw), 0) // GLA_DV
    hk = lax.broadcasted_iota(jnp.int32, (vw, kw), 1) // GLA_DK
    head_mask = hv == hk
    lane_head = lax.broadcasted_iota(jnp.int32, (1, kw), 1) // GLA_DK
    causal = lax.broadcasted_iota(jnp.int32, (GLA_HEADS * c, c), 0) % c >= \
        lax.broadcasted_iota(jnp.int32, (GLA_HEADS * c, c), 1)

    def cumsum(a):
        a1, a2, a3 = _split3(a)
        return (jnp.dot(tri, a1, preferred_element_type=F32)
                + jnp.dot(tri, a2, preferred_element_type=F32)
                + jnp.dot(tri, a3, preferred_element_type=F32))

    batch = range(GLA_BATCH)

    def next_state(st, k, v, b):
        b_last = b[c - 1:c, :]
        kd = (k * jnp.exp(b_last - b)).astype(BF16)
        upd = lax.dot_general(v, kd, (((0,), (0,)), ((), ())), preferred_element_type=F32)
        return jnp.where(head_mask, jnp.exp(b_last) * st + upd, 0.0)

    @pl.when(g == 0)
    def _():
        st0 = next_state(jnp.zeros((vw, kw), F32), kf_ref[...].astype(F32), vf_ref[...], cumsum(laf_ref[...]))
        for bb in batch:
            st_sc[bb] = st0

    gn = gn_ref[...]

    def finish(o, sr):
        outs = []
        for hh in range(GLA_HEADS):
            cs = slice(hh * GLA_DV, (hh + 1) * GLA_DV)
            oh = o[:, cs]
            ms = jnp.mean(oh * oh, axis=-1, keepdims=True)
            outs.append(oh * lax.rsqrt(ms + EPS) * gn * sr[:, cs])
        return jnp.concatenate(outs, axis=1)

    n_chunks = GLA_GROUP // c
    for ci in range(n_chunks):
        rs = slice(ci * c, (ci + 1) * c)
        for bb in batch:
            b_sc[bb, rs, :] = cumsum(la_ref[bb, rs, :])
    steep = jnp.min(b_sc[...]) < -GLA_SAFE_DECAY

    @pl.when(jnp.logical_not(steep))
    def _():
        nt = (((1,), (1,)), ((), ()))
        sts = [st_sc[bb] for bb in batch]
        for ci in range(n_chunks):
            rs = slice(ci * c, (ci + 1) * c)
            bs = [b_sc[bb, rs, :] for bb in batch]
            ks = [k_ref[bb, rs, :].astype(F32) for bb in batch]
            vs = [v_ref[bb, rs, :] for bb in batch]
            qes = [q_ref[bb, rs, :].astype(F32) * jnp.exp(bs[bb]) for bb in batch]
            nxt = [next_state(sts[bb], ks[bb], vs[bb], bs[bb]) for bb in batch]
            inters = [lax.dot_general(qes[bb].astype(BF16), sts[bb].astype(BF16), nt,
                                      preferred_element_type=F32) for bb in batch]
            scs = []
            for bb in batch:
                ke = (ks[bb] * jnp.exp(-bs[bb])).astype(BF16)
                qstack = jnp.concatenate([jnp.where(lane_head == hh, qes[bb], 0.0) for hh in range(GLA_HEADS)],
                                         axis=0).astype(BF16)
                sc = lax.dot_general(qstack, ke, nt, preferred_element_type=F32)
                scs.append(jnp.where(causal, sc, 0.0).astype(BF16))
            pvs = [jnp.dot(scs[bb], vs[bb], preferred_element_type=F32) for bb in batch]
            for bb in batch:
                o = jnp.concatenate([inters[bb][:, hh * GLA_DV:(hh + 1) * GLA_DV]
                                     + pvs[bb][hh * c:(hh + 1) * c, hh * GLA_DV:(hh + 1) * GLA_DV]
                                     for hh in range(GLA_HEADS)], axis=1)
                o_ref[bb, rs, :] = finish(o, sr_ref[bb, rs, :].astype(F32)).astype(o_ref.dtype)
            sts = nxt
        for bb in batch:
            st_sc[bb] = sts[bb]

    @pl.when(steep)
    def _():
        rows = 16

        def tile(i, carry):
            off = pl.multiple_of(i * rows, rows)
            for bb in batch:
                a = la_ref[bb, pl.ds(off, rows), :]
                q = q_ref[bb, pl.ds(off, rows), :].astype(F32)
                k = k_ref[bb, pl.ds(off, rows), :].astype(F32)
                v = v_ref[bb, pl.ds(off, rows), :].astype(F32)
                outs = []
                for r in range(rows):
                    row = lambda x: x[r:r + 1].astype(BF16)
                    upd = lax.dot_general(row(v), row(k), (((0,), (0,)), ((), ())),
                                          preferred_element_type=F32)
                    st = jnp.where(head_mask, jnp.exp(a[r:r + 1]) * st_sc[bb] + upd, 0.0)
                    st_sc[bb] = st
                    outs.append(lax.dot_general(row(q), st.astype(BF16), (((1,), (1,)), ((), ())),
                                                preferred_element_type=F32))
                o = jnp.concatenate(outs, axis=0)
                o_ref[bb, pl.ds(off, rows), :] = finish(
                    o, sr_ref[bb, pl.ds(off, rows), :].astype(F32)).astype(o_ref.dtype)
            return carry

        lax.fori_loop(0, GLA_GROUP // rows, tile, 0)


def _gla(kf, vf, laf, q, k, v, la, sr, gn):
    bsz, seq, _ = q.shape
    t = GLA_GROUP
    kw = GLA_HEADS * GLA_DK
    vw = GLA_HEADS * GLA_DV
    last = FRONT // GLA_CHUNK - 1
    nb = GLA_BATCH
    assert bsz % nb == 0
    fr = lambda w: pl.BlockSpec((GLA_CHUNK, w), lambda b, g: (last, 0))
    xs = lambda w: pl.BlockSpec((nb, t, w), lambda b, g: (b, g, 0))
    return pl.pallas_call(
        _gla_body,
        grid=(bsz // nb, seq // t),
        in_specs=[_const_spec((1, GLA_DV)), fr(kw), fr(vw), fr(kw), xs(kw), xs(kw), xs(vw), xs(kw), xs(vw)],
        out_specs=xs(vw),
        out_shape=jax.ShapeDtypeStruct((bsz, seq, vw), BF16),
        scratch_shapes=[pltpu.VMEM((nb, vw, kw), F32), pltpu.VMEM((nb, t, kw), F32)],
        compiler_params=_cparams(("parallel", "arbitrary")),
        name="gla",
    )(gn, kf, vf, laf, q, k, v, la, sr)


def _merge_body(oa_ref, ob_ref, u_ref, gmix_ref, wgate_ref, bgate_ref, wb0_ref, wb1_ref, wout_ref, gffn_ref,
                wr_ref, br_ref,
                u1_out, h2_out, info_out, cnt_out, cnt_sc):
    i = pl.program_id(0)
    tm = ROW_TILE

    @pl.when(i == 0)
    def _():
        cnt_sc[...] = jnp.zeros(cnt_sc.shape, F32)

    ya = lax.dot_general(oa_ref[...], wb0_ref[...], (((0,), (0,)), ((), ())), preferred_element_type=F32)
    yb = jnp.dot(ob_ref[...], wb1_ref[...], preferred_element_type=F32)
    x = u_ref[...]
    h = (x * lax.rsqrt(jnp.mean(x * x, axis=-1, keepdims=True) + EPS) * gmix_ref[...]).astype(BF16)
    gate = _sigmoid(jnp.dot(h, wgate_ref[...], preferred_element_type=F32) + bgate_ref[...])
    merged = gate[:, :D_MODEL] * ya + gate[:, D_MODEL:] * yb
    u1 = x + jnp.dot(merged.astype(BF16), wout_ref[...], preferred_element_type=F32)
    u1_out[...] = u1
    ms = jnp.mean(u1 * u1, axis=-1, keepdims=True)
    h2f = u1 * lax.rsqrt(ms + EPS) * gffn_ref[...]
    _store_row_tiles(h2_out, h2f)
    h2 = h2f.astype(BF16)

    logits = lax.dot_general(wr_ref[...], h2, (((1,), (1,)), ((), ())), preferred_element_type=F32) + br_ref[...]
    row = lax.broadcasted_iota(jnp.int32, (ROUTER_ROWS, tm), 0)
    is_group = row < N_GROUPS
    gl = jnp.where(is_group, logits, NEG_BIG)
    gmax = jnp.max(gl, axis=0, keepdims=True)
    g_idx = jnp.min(jnp.where(gl == gmax, row, ROUTER_ROWS), axis=0, keepdims=True)
    g_w = 1.0 / jnp.sum(jnp.where(is_group, jnp.exp(gl - gmax), 0.0), axis=0, keepdims=True)
    lo = N_GROUPS + EXPERTS_PER_GROUP * g_idx
    el = jnp.where((row >= lo) & (row < lo + EXPERTS_PER_GROUP), logits, NEG_BIG)
    v1 = jnp.max(el, axis=0, keepdims=True)
    i1 = jnp.min(jnp.where(el == v1, row, ROUTER_ROWS), axis=0, keepdims=True)
    el2 = jnp.where(row == i1, NEG_BIG, el)
    v2 = jnp.max(el2, axis=0, keepdims=True)
    i2 = jnp.min(jnp.where(el2 == v2, row, ROUTER_ROWS), axis=0, keepdims=True)
    e21 = jnp.exp(v2 - v1)
    w1 = g_w / (1.0 + e21)
    w2 = w1 * e21

    onehot = ((row == i1) | (row == i2)).astype(BF16)
    si = lax.broadcasted_iota(jnp.int32, (tm, tm), 0)
    ti = lax.broadcasted_iota(jnp.int32, (tm, tm), 1)
    earlier = (si < ti).astype(BF16)
    cnt = cnt_sc[...]
    before = jnp.dot(onehot, earlier, preferred_element_type=F32) + jnp.tile(cnt, (1, tm // LANES))
    r1 = jnp.sum(jnp.where(row == i1, before, 0.0), axis=0, keepdims=True)
    r2 = jnp.sum(jnp.where(row == i2, before, 0.0), axis=0, keepdims=True)
    cnt = cnt + jnp.dot(onehot, jnp.ones((tm, LANES), BF16), preferred_element_type=F32)
    cnt_sc[...] = cnt
    cnt_out[...] = cnt

    zero = jnp.zeros_like(w1)
    info_out[...] = jnp.concatenate([(i1 - N_GROUPS).astype(F32), (i2 - N_GROUPS).astype(F32),
                                     w1, w2, r1, r2, zero, zero], axis=0)


def _merge(oa, ob, u, p):
    n = u.shape[0]
    tm = ROW_TILE
    row = lambda w: pl.BlockSpec((tm, w), lambda i: (i, 0))
    return pl.pallas_call(
        _merge_body,
        grid=(n // tm,),
        in_specs=[pl.BlockSpec((512, tm), lambda i: (0, i)), row(512), row(D_MODEL),
                  _const_spec((1, D_MODEL)), _const_spec((D_MODEL, 2 * D_MODEL)), _const_spec((1, 2 * D_MODEL)),
                  _const_spec((512, D_MODEL)), _const_spec((512, D_MODEL)), _const_spec((D_MODEL, D_MODEL)),
                  _const_spec((1, D_MODEL)), _const_spec((ROUTER_ROWS, D_MODEL)), _const_spec((ROUTER_ROWS, tm))],
        out_specs=[row(D_MODEL), _row_tile_spec(tm, lambda i: (i, 0)), pl.BlockSpec((8, tm), lambda i: (0, i)),
                   _const_spec((ROUTER_ROWS, LANES))],
        out_shape=[jax.ShapeDtypeStruct((n, D_MODEL), F32), jax.ShapeDtypeStruct((n * RT, LANES), F32),
                   jax.ShapeDtypeStruct((8, n), F32), jax.ShapeDtypeStruct((ROUTER_ROWS, LANES), F32)],
        scratch_shapes=[pltpu.VMEM((ROUTER_ROWS, LANES), F32)],
        compiler_params=_cparams(("arbitrary",)),
        name="merge_router",
    )(oa, ob, u, p['gmix'], p['wgate'], p['bgate'], p['wb0'], p['wb1'], p['wout'], p['gffn'], p['wr'], p['br'])


def _dispatch_body(tail_ref, nb_ref, dest_ref, h2_ref, xs_hbm, zero_sc, sem, zsem):
    i = pl.program_id(0)
    blk_rows = MOE_TILE * RT
    n_blocks = xs_hbm.shape[0] // blk_rows

    def zero_copy(blk):
        dst = xs_hbm.at[pl.ds(pl.multiple_of(blk * blk_rows, blk_rows), blk_rows)]
        return pltpu.make_async_copy(zero_sc, dst, zsem)

    @pl.when(i == 0)
    def _():
        zero_sc[...] = jnp.zeros(zero_sc.shape, F32)

        def tails(fn):
            def body(e, carry):
                @pl.when(tail_ref[e] >= 0)
                def _():
                    fn(zero_copy(tail_ref[e]))
                return carry
            lax.fori_loop(0, N_EXPERTS, body, 0)

        def unused(fn):
            def body(b, carry):
                fn(zero_copy(b))
                return carry
            lax.fori_loop(nb_ref[0], n_blocks, body, 0)

        tails(lambda cp: cp.start())
        unused(lambda cp: cp.start())
        tails(lambda cp: cp.wait())
        unused(lambda cp: cp.wait())

    def start(r, carry):
        src = _token_rows(h2_ref, r)
        pltpu.make_async_copy(src, _token_rows(xs_hbm, dest_ref[0, 0, r]), sem).start()
        pltpu.make_async_copy(src, _token_rows(xs_hbm, dest_ref[0, 1, r]), sem).start()
        return carry

    lax.fori_loop(0, DMA_TILE, start, 0)
    for _ in range(2):
        pltpu.make_async_copy(h2_ref, xs_hbm.at[pl.ds(0, DMA_TILE * RT)], sem).wait()


def _dispatch(tail_blocks, n_used, dest, h2, n_slots):
    n = h2.shape[0] // RT
    grid_spec = pltpu.PrefetchScalarGridSpec(
        num_scalar_prefetch=2,
        grid=(n // DMA_TILE,),
        in_specs=[pl.BlockSpec((1, 2, DMA_TILE), lambda i, tb, nb: (i, 0, 0), memory_space=pltpu.SMEM),
                  _row_tile_spec(DMA_TILE, lambda i, tb, nb: (i, 0))],
        out_specs=pl.BlockSpec(memory_space=pl.ANY),
        scratch_shapes=[pltpu.VMEM((MOE_TILE * RT, LANES), F32), pltpu.SemaphoreType.DMA(()),
                        pltpu.SemaphoreType.DMA(())],
    )
    return pl.pallas_call(
        _dispatch_body,
        grid_spec=grid_spec,
        out_shape=jax.ShapeDtypeStruct((n_slots * RT, LANES), F32),
        compiler_params=_cparams(("arbitrary",)),
        name="dispatch",
    )(tail_blocks, n_used, dest, h2)


def _experts_body(be_ref, nb_ref, x_ref, wg_ref, wu_ref, wd_ref, y_ref, wg_sc, wu_sc, wd_sc):
    i = pl.program_id(0)
    prev = be_ref[jnp.maximum(i - 1, 0)]
    fresh = (i == 0) | (be_ref[i] != prev)

    @pl.when(fresh)
    def _():
        wg_sc[...] = wg_ref[...].astype(BF16)
        wu_sc[...] = wu_ref[...].astype(BF16)
        wd_sc[...] = wd_ref[...].astype(BF16)

    @pl.when(i < nb_ref[0])
    def _():
        x = _load_row_tiles(x_ref, MOE_TILE).astype(BF16)
        y = jnp.zeros((MOE_TILE, D_MODEL), F32)
        for j in range(D_EXPERT // EXPERT_CHUNK):
            cs = slice(j * EXPERT_CHUNK, (j + 1) * EXPERT_CHUNK)
            gp = jnp.dot(x, wg_sc[:, cs], preferred_element_type=F32)
            up = jnp.dot(x, wu_sc[:, cs], preferred_element_type=F32)
            hid = (gp * _sigmoid(gp) * up).astype(BF16)
            y = y + jnp.dot(hid, wd_sc[cs, :], preferred_element_type=F32)
        _store_row_tiles(y_ref, y)

    @pl.when(i >= nb_ref[0])
    def _():
        y_ref[...] = jnp.zeros(y_ref.shape, F32)


def _experts(block_e, n_used, xs, wg, wu, wd):
    n_slots = xs.shape[0] // RT
    n_blocks = n_slots // MOE_TILE
    xmap = lambda i, be, nb: (jnp.minimum(i, nb[0] - 1), 0)
    wmap = lambda i, be, nb: (be[i], 0, 0)
    grid_spec = pltpu.PrefetchScalarGridSpec(
        num_scalar_prefetch=2,
        grid=(n_blocks,),
        in_specs=[_row_tile_spec(MOE_TILE, xmap),
                  pl.BlockSpec((None, D_MODEL, D_EXPERT), wmap),
                  pl.BlockSpec((None, D_MODEL, D_EXPERT), wmap),
                  pl.BlockSpec((None, D_EXPERT, D_MODEL), wmap)],
        out_specs=_row_tile_spec(MOE_TILE, lambda i, be, nb: (i, 0)),
        scratch_shapes=[pltpu.VMEM((D_MODEL, D_EXPERT), BF16), pltpu.VMEM((D_MODEL, D_EXPERT), BF16),
                        pltpu.VMEM((D_EXPERT, D_MODEL), BF16)],
    )
    return pl.pallas_call(
        _experts_body,
        grid_spec=grid_spec,
        out_shape=jax.ShapeDtypeStruct((n_slots * RT, LANES), F32),
        compiler_params=_cparams(("arbitrary",)),
        name="experts",
    )(block_e, n_used, xs, wg, wu, wd)


def _combine_body(dest_ref, dest_next_ref, w_ref, u1_ref, ys_hbm, o_ref, ybuf, sems):
    i = pl.program_id(0)
    n_steps = pl.num_programs(0)
    t = DMA_TILE

    def gather(d_ref, slot):
        buf = ybuf.at[slot]

        def start(r, carry):
            pltpu.make_async_copy(_token_rows(ys_hbm, d_ref[0, 0, r]), _token_rows(buf, r),
                                  sems.at[slot]).start()
            pltpu.make_async_copy(_token_rows(ys_hbm, d_ref[0, 1, r]), _token_rows(buf, t + r),
                                  sems.at[slot]).start()
            return carry

        lax.fori_loop(0, t, start, 0)

    slot = i % 2

    @pl.when(i == 0)
    def _():
        gather(dest_ref, 0)

    @pl.when(i + 1 < n_steps)
    def _():
        gather(dest_next_ref, 1 - slot)

    buf = ybuf.at[slot]
    pltpu.make_async_copy(ys_hbm.at[pl.ds(0, buf.shape[0])], buf, sems.at[slot]).wait()
    w = w_ref[...]
    o_ref[...] = (u1_ref[...] + w[:, 0:1] * _load_row_tiles(buf, t) + w[:, 1:2] * _load_row_tiles(buf, t, t))


def _combine(dest, w, u1, ys):
    n = u1.shape[0]
    t = DMA_TILE
    n_steps = n // t
    return pl.pallas_call(
        _combine_body,
        grid=(n_steps,),
        in_specs=[pl.BlockSpec((1, 2, t), lambda i: (i, 0, 0), memory_space=pltpu.SMEM),
                  pl.BlockSpec((1, 2, t), lambda i: (jnp.minimum(i + 1, n_steps - 1), 0, 0),
                               memory_space=pltpu.SMEM),
                  pl.BlockSpec((t, 2), lambda i: (i, 0)),
                  pl.BlockSpec((t, D_MODEL), lambda i: (i, 0)),
                  pl.BlockSpec(memory_space=pl.ANY)],
        out_specs=pl.BlockSpec((t, D_MODEL), lambda i: (i, 0)),
        out_shape=jax.ShapeDtypeStruct((n, D_MODEL), F32),
        scratch_shapes=[pltpu.VMEM((2, 2 * t * RT, LANES), F32), pltpu.SemaphoreType.DMA((2,))],
        compiler_params=_cparams(("arbitrary",)),
        name="combine",
    )(dest, dest, w, u1, ys)


def _rope_tables(pos):
    half = DA_HEAD_DIM // 2
    inv_freq = jnp.power(ROPE_THETA, -jnp.arange(half, dtype=F32) * 2.0 / DA_HEAD_DIM)
    ang = pos[:, None] * inv_freq[None, :]
    cos, sin = jnp.cos(ang), jnp.sin(ang)
    cos_t = jnp.tile(cos, (1, LANES // half))
    sin_t = jnp.tile(jnp.concatenate([-sin, sin], axis=1), (1, LANES // DA_HEAD_DIM))
    return cos_t, sin_t


def _layer(x, meta_tokens, l, g_mix_norm, w_in, g_q_norm, g_k_norm, lambda_q1, lambda_k1, lambda_q2, lambda_k2,
           g_diff_subln, w_gla_gate_up, b_gla_gate, g_gla_norm, w_branch, b_merge_gate, w_out, g_ffn_norm,
           w_router_group, b_router_group, w_router_expert, b_router_expert, w_exp_gate, w_exp_up, w_exp_down):
    bsz, seq, _ = x.shape
    n = bsz * seq

    w_in_bf = w_in[l].astype(BF16)
    p = {
        'gmix': g_mix_norm[l][None, :],
        'gqn': jnp.tile(g_q_norm[l], LANES // DA_HEAD_DIM)[None, :],
        'gkn': jnp.tile(g_k_norm[l], LANES // DA_HEAD_DIM)[None, :],
        'w_in': w_in_bf,
        'wup': jnp.pad(w_gla_gate_up[l].astype(BF16), ((0, LANES - GLA_RANK), (0, 0))),
        'bup': b_gla_gate[l][None, :],
        'wgate': w_in_bf[:, GATE_OFFSET:GATE_OFFSET + 2 * D_MODEL],
        'bgate': b_merge_gate[l].reshape(1, 2 * D_MODEL),
        'wb0': w_branch[l, 0].astype(BF16), 'wb1': w_branch[l, 1].astype(BF16),
        'wout': w_out[l].astype(BF16),
        'gffn': g_ffn_norm[l][None, :],
        'wr': jnp.pad(jnp.concatenate([w_router_group[l], w_router_expert[l].reshape(D_MODEL, N_EXPERTS)],
                                      axis=1).T.astype(BF16), ((0, ROUTER_ROWS - N_GROUPS - N_EXPERTS), (0, 0))),
        'br': jnp.broadcast_to(
            jnp.pad(jnp.concatenate([b_router_group[l], b_router_expert[l].reshape(N_EXPERTS)]),
                    (0, ROUTER_ROWS - N_GROUPS - N_EXPERTS))[:, None], (ROUTER_ROWS, ROW_TILE)),
    }

    u_front = jnp.concatenate([jnp.zeros((FRONT - N_META, D_MODEL), F32), meta_tokens.astype(F32)], axis=0)
    cos_f, sin_f = _rope_tables(jnp.arange(FRONT, dtype=F32) - (FRONT - N_META))
    cos_x, sin_x = _rope_tables(jnp.arange(seq, dtype=F32) + N_META)
    front = _inproj(u_front, FRONT, cos_f, sin_f, p)
    xin = _inproj(x.reshape(n, D_MODEL), ROW_TILE, cos_x, sin_x, p)
    q_t = xin[0]
    k, v, gq, gk, gv, sr, la = [a.reshape(bsz, seq, a.shape[-1]) for a in xin[1:]]
    _, kf, vf, _, gkf, gvf, _, laf = front

    lam_init = 0.8 - 0.6 * math.exp(-0.3 * l)
    lam_vecs = [a[l][None, :] for a in (lambda_q1, lambda_k1, lambda_q2, lambda_k2)]
    score_bound = (ATT_BOUND_MARGIN * DA_HEAD_DIM * Q_SCALE
                   * jnp.max(jnp.abs(g_q_norm[l])) * jnp.max(jnp.abs(g_k_norm[l]))).reshape(1).astype(F32)
    gsub_t = jnp.broadcast_to(g_diff_subln[l][:, None], (DA_V_DIM, ATT_TILE))
    o_a_t = _diff_attn(score_bound, q_t, kf, vf, k, v, lam_vecs, gsub_t, lam_init)
    o_b = _gla(gkf, gvf, laf, gq, gk, gv, la, sr, g_gla_norm[l][None, :])

    u1, h2, info, cnt = _merge(o_a_t, o_b.reshape(n, -1), x.reshape(n, D_MODEL), p)

    ids = info[0:2].astype(jnp.int32)
    wts = info[2:4]
    rank = info[4:6].astype(jnp.int32)
    counts = cnt[N_GROUPS:N_GROUPS + N_EXPERTS, 0].astype(jnp.int32)
    padded = (counts + MOE_TILE - 1) // MOE_TILE * MOE_TILE
    pends = jnp.cumsum(padded)
    pstarts = pends - padded
    expert = jnp.arange(N_EXPERTS, dtype=jnp.int32)
    dest = jnp.sum(jnp.where(ids[..., None] == expert, pstarts, 0), axis=-1) + rank
    n_slots = (2 * n // MOE_TILE + N_EXPERTS) * MOE_TILE
    n_blocks = n_slots // MOE_TILE
    n_used = (pends[-1] // MOE_TILE).astype(jnp.int32)
    blk = jnp.minimum(jnp.arange(n_blocks, dtype=jnp.int32), n_used - 1) * MOE_TILE
    block_e = jnp.minimum(jnp.sum(pends[None, :] <= blk[:, None], axis=1), N_EXPERTS - 1).astype(jnp.int32)
    tail_blocks = jnp.where(counts > 0, pends // MOE_TILE - 1, -1).astype(jnp.int32)
    dest_t = dest.reshape(2, n // DMA_TILE, DMA_TILE).transpose(1, 0, 2)

    xs = _dispatch(tail_blocks, n_used[None], dest_t, h2, n_slots)
    ys = _experts(block_e, n_used[None], xs, w_exp_gate[l], w_exp_up[l], w_exp_down[l])
    out = _combine(dest_t, wts.T, u1, ys)
    return out.reshape(bsz, seq, D_MODEL)


def kernel(x, meta_tokens, g_mix_norm, w_in, g_q_norm, g_k_norm, lambda_q1, lambda_k1, lambda_q2, lambda_k2,
           g_diff_subln, w_gla_gate_up, b_gla_gate, g_gla_norm, w_branch, b_merge_gate, w_out, g_ffn_norm,
           w_router_group, b_router_group, w_router_expert, b_router_expert, w_exp_gate, w_exp_up, w_exp_down):
    depth = w_in.shape[0]
    assert depth == 1, "meta tokens are only carried through a single layer in this implementation"
    assert x.shape[1] % ROW_TILE == 0 and x.shape[2] == D_MODEL
    return _layer(x, meta_tokens, 0, g_mix_norm, w_in, g_q_norm, g_k_norm, lambda_q1, lambda_k1, lambda_q2,
                  lambda_k2, g_diff_subln, w_gla_gate_up, b_gla_gate, g_gla_norm, w_branch, b_merge_gate, w_out,
                  g_ffn_norm, w_router_group, b_router_group, w_router_expert, b_router_expert,
                  w_exp_gate, w_exp_up, w_exp_down)
```

```python
import functools
import math

import jax
import jax.numpy as jnp
from jax import lax
from jax.experimental import pallas as pl
from jax.experimental.pallas import tpu as pltpu

F32 = jnp.float32
BF16 = jnp.bfloat16

D_MODEL = 1024
N_META = 16
EPS = 1e-6
ROPE_THETA = 10000.0

DA_HEADS = 4
DA_HEAD_DIM = 64
DA_V_DIM = 128
Q_SCALE = DA_HEAD_DIM ** -0.5 * math.log2(math.e)
ATT_BOUND_MARGIN = 1.02
ATT_SAFE_BOUND = 60.0
GLA_HEADS = 4
GLA_DK = 64
GLA_DV = 128
GLA_RANK = 16
GLA_TAU = 16.0
GLA_CHUNK = 64
GLA_SAFE_DECAY = 60.0
N_GROUPS = 4
EXPERTS_PER_GROUP = 8
N_EXPERTS = 32
D_EXPERT = 512
ROUTER_ROWS = 48
MIXER_SECTIONS = ((0, 512), (512, 512), (1024, 512), (1536, 256), (1792, 256), (2048, 512), (2560, 512),
                  (3072, 128))
GATE_OFFSET = 3072 + GLA_RANK

LANES = 128
FRONT = 256
ATT_TILE = 256
ROW_TILE = 512
GLA_GROUP = 512
GLA_BATCH = 4
MOE_TILE = 512
EXPERT_CHUNK = 256
DMA_TILE = 512
NEG_BIG = -1e30
VMEM_LIMIT = 56 * 1024 * 1024


def _cparams(sem):
    return pltpu.CompilerParams(dimension_semantics=sem, vmem_limit_bytes=VMEM_LIMIT)


def _const_spec(shape):
    nd = len(shape)
    return pl.BlockSpec(shape, lambda *_: (0,) * nd)


RT = D_MODEL // LANES


def _row_tile_spec(rows, index_map):
    return pl.BlockSpec((rows * RT, LANES), index_map)


def _token_rows(ref, tok):
    return ref.at[pl.ds(pl.multiple_of(tok * RT, RT), RT)]


def _load_row_tiles(ref, rows, first=0):
    return jnp.concatenate([ref[pl.ds(first * RT + c, rows, stride=RT), :] for c in range(RT)], axis=1)


def _store_row_tiles(ref, val):
    for c in range(RT):
        ref[pl.ds(c, val.shape[0], stride=RT), :] = val[:, c * LANES:(c + 1) * LANES]


def _sigmoid(x):
    return 0.5 * jnp.tanh(0.5 * x) + 0.5


def _log_sigmoid(x):
    return jnp.minimum(x, 0.0) - jnp.log1p(jnp.exp(-jnp.abs(x)))


def _inproj_body(u_ref, gmix_ref, cos_ref, sin_ref, gqn_ref, gkn_ref,
                 wq_ref, wk_ref, wv_ref, wgq_ref, wgk_ref, wgv_ref, wgr_ref, wgg_ref,
                 wup_ref, bup_ref,
                 q_out, k_out, v_out, gq_out, gk_out, gv_out, sr_out, la_out):
    x = u_ref[...]
    ms = jnp.mean(x * x, axis=-1, keepdims=True)
    h = (x * lax.rsqrt(ms + EPS) * gmix_ref[...]).astype(BF16)

    cos = cos_ref[...]
    sin = sin_ref[...]
    lane = lax.broadcasted_iota(jnp.int32, (1, LANES), 1)
    first_half = (lane % DA_HEAD_DIM) < (DA_HEAD_DIM // 2)
    gi = lax.broadcasted_iota(jnp.int32, (LANES, LANES), 0) // DA_HEAD_DIM
    gj = lax.broadcasted_iota(jnp.int32, (LANES, LANES), 1) // DA_HEAD_DIM
    group_sum = (gi == gj).astype(BF16)

    def norm_rope(w_ref, gain_ref, out_ref, scale, transposed):
        z = jnp.dot(h, w_ref[...], preferred_element_type=F32)
        for hh in range(DA_HEADS):
            hs = slice(hh * LANES, (hh + 1) * LANES)
            zh = z[:, hs]
            ssq = jnp.dot((zh * zh).astype(BF16), group_sum, preferred_element_type=F32)
            zn = zh * lax.rsqrt(ssq * (1.0 / DA_HEAD_DIM) + EPS) * gain_ref[...]
            rot = jnp.where(first_half,
                            pltpu.roll(zn, LANES - DA_HEAD_DIM // 2, 1),
                            pltpu.roll(zn, DA_HEAD_DIM // 2, 1))
            zr = (zn * cos + rot * sin) * scale
            if transposed:
                out_ref[hs, :] = jnp.transpose(zr).astype(out_ref.dtype)
            else:
                out_ref[:, hs] = zr.astype(out_ref.dtype)

    norm_rope(wq_ref, gqn_ref, q_out, Q_SCALE, transposed=True)
    norm_rope(wk_ref, gkn_ref, k_out, 1.0, transposed=False)
    v_out[...] = jnp.dot(h, wv_ref[...], preferred_element_type=F32).astype(v_out.dtype)

    gq_out[...] = (jnp.dot(h, wgq_ref[...], preferred_element_type=F32) * (GLA_DK ** -0.5)).astype(gq_out.dtype)
    gk_out[...] = jnp.dot(h, wgk_ref[...], preferred_element_type=F32).astype(gk_out.dtype)
    gv_out[...] = jnp.dot(h, wgv_ref[...], preferred_element_type=F32).astype(gv_out.dtype)
    r = jnp.dot(h, wgr_ref[...], preferred_element_type=F32)
    sr_out[...] = (r * _sigmoid(r)).astype(sr_out.dtype)

    g_lr = jnp.dot(h, wgg_ref[...], preferred_element_type=F32)
    pre = jnp.dot(g_lr.astype(BF16), wup_ref[...], preferred_element_type=F32) + bup_ref[...]
    la_out[...] = _log_sigmoid(pre) * (1.0 / GLA_TAU)


def _inproj(u, tm, cos, sin, p):
    rows = u.shape[0]
    n_tab = cos.shape[0] // tm
    row = lambda w: pl.BlockSpec((tm, w), lambda i: (i, 0))
    tab = pl.BlockSpec((tm, LANES), lambda i: (i % n_tab, 0))
    sections = [pl.BlockSpec((D_MODEL, w), lambda i, j=off // w: (0, j)) for off, w in MIXER_SECTIONS]
    assert all(off % w == 0 for off, w in MIXER_SECTIONS)
    out_widths = [(512, BF16), (512, BF16), (512, BF16), (256, BF16), (256, BF16), (512, BF16),
                  (512, BF16), (256, F32)]
    return pl.pallas_call(
        _inproj_body,
        grid=(rows // tm,),
        in_specs=[row(D_MODEL), _const_spec((1, D_MODEL)), tab, tab,
                  _const_spec((1, LANES)), _const_spec((1, LANES))]
                 + sections + [_const_spec(p['wup'].shape), _const_spec(p['bup'].shape)],
        out_specs=[pl.BlockSpec((out_widths[0][0], tm), lambda i: (0, i))] + [row(w) for w, _ in out_widths[1:]],
        out_shape=[jax.ShapeDtypeStruct((out_widths[0][0], rows), BF16)]
                  + [jax.ShapeDtypeStruct((rows, w), dt) for w, dt in out_widths[1:]],
        compiler_params=_cparams(("parallel",)),
        name="inproj",
    )(u, p['gmix'], cos, sin, p['gqn'], p['gkn'], *([p['w_in']] * len(MIXER_SECTIONS)), p['wup'], p['bup'])


def _diff_attn_body(bound_ref, lq1_ref, lk1_ref, lq2_ref, lk2_ref, gsub_ref,
                    q_ref, kf_ref, vf_ref, kx_ref, vx_ref, o_ref, vt_sc, qs_sc, s_sc, cmax_sc, m_sc, l_sc, acc_sc,
                    *, lam_init):
    qi = pl.program_id(1)
    tq = ATT_TILE
    n_kv = kx_ref.shape[0] // tq
    heads = range(DA_HEADS)
    hs = lambda h: slice(h * LANES, (h + 1) * LANES)

    @pl.when(qi == 0)
    def _():
        def tr(j, carry):
            off = pl.multiple_of(j * tq, tq)
            for h in heads:
                vt_sc[h, j] = jnp.transpose(vx_ref[pl.ds(off, tq), hs(h)].astype(F32)).astype(BF16)
            return carry

        lax.fori_loop(0, n_kv, tr, 0)

    d = lax.broadcasted_iota(jnp.int32, (LANES, tq), 0)
    for h in heads:
        qt = q_ref[hs(h), :]
        zero = jnp.zeros_like(qt)
        qs_sc[h] = jnp.concatenate([jnp.where(d < DA_HEAD_DIM, qt, zero),
                                    jnp.where(d >= DA_HEAD_DIM, qt, zero)], axis=1)

    meta = slice(FRONT - N_META, FRONT)
    key = lax.broadcasted_iota(jnp.int32, (tq, 2 * tq), 0)
    qry = lax.broadcasted_iota(jnp.int32, (tq, 2 * tq), 1) % tq
    causal = key <= qry
    bound = bound_ref[0]

    def meta_scores(h):
        s = jnp.dot(kf_ref[meta, hs(h)], qs_sc[h], preferred_element_type=F32)
        vt = jnp.transpose(vf_ref[meta, hs(h)].astype(F32)).astype(BF16)
        return s, vt

    def block_scores(j, h, diagonal):
        off = pl.multiple_of(j * tq, tq)
        s = jnp.dot(kx_ref[pl.ds(off, tq), hs(h)], qs_sc[h], preferred_element_type=F32)
        return jnp.where(causal, s, NEG_BIG) if diagonal else s

    @pl.when(bound <= ATT_SAFE_BOUND)
    def _():
        l_sc[...] = jnp.zeros(l_sc.shape, F32)
        acc_sc[...] = jnp.zeros(acc_sc.shape, F32)

        def blocks(*js, last=False):
            work = [(h, block_scores(j, h, last and j is js[-1]), (h, j)) for j in js for h in heads]
            if last:
                work += [(h,) + meta_scores(h) for h in heads]
            ps = [(h, jnp.exp2(s - bound), vt) for h, s, vt in work]
            for h, pr, _ in ps:
                l_sc[h] += jnp.sum(pr, axis=0, keepdims=True)
            for h, pr, vt in ps:
                vt = vt_sc[vt] if isinstance(vt, tuple) else vt
                acc_sc[h] += jnp.dot(vt, pr.astype(BF16), preferred_element_type=F32)

        def pair(u, carry):
            blocks(2 * u, 2 * u + 1)
            return carry

        lax.fori_loop(0, qi // 2, pair, 0)

        @pl.when(qi % 2 == 1)
        def _():
            blocks(qi - 1, qi, last=True)

        @pl.when(qi % 2 == 0)
        def _():
            blocks(qi, last=True)

    @pl.when(bound > ATT_SAFE_BOUND)
    def _():
        for h in heads:
            s, vt = meta_scores(h)
            m0 = jnp.max(s, axis=0, keepdims=True)
            pr = jnp.exp2(s - m0)
            m_sc[h] = m0
            l_sc[h] = jnp.sum(pr, axis=0, keepdims=True)
            acc_sc[h] = jnp.dot(vt, pr.astype(BF16), preferred_element_type=F32)

        def scores(j, slot, diagonal=False):
            for h in heads:
                s = block_scores(j, h, diagonal)
                s_sc[h, slot] = s
                cmax_sc[h, slot] = jnp.max(s, axis=0, keepdims=True)

        def accumulate(j, slot):
            for h in heads:
                m_old = m_sc[h]
                m_new = jnp.maximum(m_old, cmax_sc[h, slot])
                alpha = jnp.exp2(m_old - m_new)
                pr = jnp.exp2(s_sc[h, slot] - m_new)
                l_sc[h] = alpha * l_sc[h] + jnp.sum(pr, axis=0, keepdims=True)
                acc_sc[h] = alpha * acc_sc[h] + jnp.dot(vt_sc[h, j], pr.astype(BF16),
                                                        preferred_element_type=F32)
                m_sc[h] = m_new

        @pl.when(qi == 0)
        def _():
            scores(0, 0, diagonal=True)
            accumulate(0, 0)

        @pl.when(qi > 0)
        def _():
            scores(0, 0)

            def pair(u, carry):
                j = 2 * u
                scores(j + 1, 1)
                accumulate(j, 0)
                scores(j + 2, 0)
                accumulate(j + 1, 1)
                return carry

            lax.fori_loop(0, (qi - 1) // 2, pair, 0)

            @pl.when(qi % 2 == 1)
            def _():
                scores(qi, 1, diagonal=True)
                accumulate(qi - 1, 0)
                accumulate(qi, 1)

            @pl.when(qi % 2 == 0)
            def _():
                scores(qi - 1, 1)
                accumulate(qi - 2, 0)
                scores(qi, 0, diagonal=True)
                accumulate(qi - 1, 1)
                accumulate(qi, 0)

    lam = (jnp.exp(jnp.sum(lq1_ref[...] * lk1_ref[...], axis=-1, keepdims=True))
           - jnp.exp(jnp.sum(lq2_ref[...] * lk2_ref[...], axis=-1, keepdims=True)) + lam_init)
    for h in heads:
        acc = acc_sc[h]
        inv_l = 1.0 / l_sc[h]
        ot = acc[:, :tq] * inv_l[:, :tq] - lam * (acc[:, tq:] * inv_l[:, tq:])
        ms = jnp.mean(ot * ot, axis=0, keepdims=True)
        o_ref[hs(h), :] = (ot * lax.rsqrt(ms + EPS) * gsub_ref[...] * (1.0 - lam_init)).astype(o_ref.dtype)


def _diff_attn(bound, q_t, kf, vf, kx, vx, lam_vecs, gsub_t, lam_init):
    bsz, seq, _ = kx.shape
    tq = ATT_TILE
    vec = _const_spec((1, DA_HEAD_DIM))
    width = DA_HEADS * LANES
    tok_tile = pl.BlockSpec((width, tq), lambda b, i: (0, b * (seq // tq) + i))
    return pl.pallas_call(
        functools.partial(_diff_attn_body, lam_init=lam_init),
        grid=(bsz, seq // tq),
        in_specs=[pl.BlockSpec(memory_space=pltpu.SMEM), vec, vec, vec, vec, _const_spec((DA_V_DIM, tq)),
                  tok_tile,
                  _const_spec((FRONT, width)), _const_spec((FRONT, width)),
                  pl.BlockSpec((None, seq, width), lambda b, i: (b, 0, 0)),
                  pl.BlockSpec((None, seq, width), lambda b, i: (b, 0, 0))],
        out_specs=tok_tile,
        out_shape=jax.ShapeDtypeStruct((width, bsz * seq), BF16),
        scratch_shapes=[pltpu.VMEM((DA_HEADS, seq // tq, DA_V_DIM, tq), BF16),
                        pltpu.VMEM((DA_HEADS, LANES, 2 * tq), BF16),
                        pltpu.VMEM((DA_HEADS, 2, tq, 2 * tq), F32), pltpu.VMEM((DA_HEADS, 2, 1, 2 * tq), F32),
                        pltpu.VMEM((DA_HEADS, 1, 2 * tq), F32), pltpu.VMEM((DA_HEADS, 1, 2 * tq), F32),
                        pltpu.VMEM((DA_HEADS, DA_V_DIM, 2 * tq), F32)],
        compiler_params=_cparams(("parallel", "arbitrary")),
        name="diff_attn",
    )(bound, *lam_vecs, gsub_t, q_t, kf, vf, kx, vx)


def _split3(a):
    a1 = a.astype(BF16)
    r1 = a - a1.astype(F32)
    a2 = r1.astype(BF16)
    a3 = (r1 - a2.astype(F32)).astype(BF16)
    return a1, a2, a3


def _gla_body(gn_ref, kf_ref, vf_ref, laf_ref, q_ref, k_ref, v_ref, la_ref, sr_ref, o_ref, st_sc, b_sc):
    g = pl.program_id(1)
    c = GLA_CHUNK
    kw = GLA_HEADS * GLA_DK
    vw = GLA_HEADS * GLA_DV

    ti = lax.broadcasted_iota(jnp.int32, (c, c), 0)
    si = lax.broadcasted_iota(jnp.int32, (c, c), 1)
    tri = (si <= ti).astype(BF16)
    hv = lax.broadcasted_iota(jnp.int32, (vw, kw), 0) // GLA_DV
    hk = lax.broadcasted_iota(jnp.int32, (vw, kw), 1) // GLA_DK
    head_mask = hv == hk
    lane_head = lax.broadcasted_iota(jnp.int32, (1, kw), 1) // GLA_DK
    causal = lax.broadcasted_iota(jnp.int32, (GLA_HEADS * c, c), 0) % c >= \
        lax.broadcasted_iota(jnp.int32, (GLA_HEADS * c, c), 1)

    def cumsum(a):
        a1, a2, a3 = _split3(a)
        return (jnp.dot(tri, a1, preferred_element_type=F32)
                + jnp.dot(tri, a2, preferred_element_type=F32)
                + jnp.dot(tri, a3, preferred_element_type=F32))

    batch = range(GLA_BATCH)

    def next_state(st, k, v, b):
        b_last = b[c - 1:c, :]
        kd = (k * jnp.exp(b_last - b)).astype(BF16)
        upd = lax.dot_general(v, kd, (((0,), (0,)), ((), ())), preferred_element_type=F32)
        return jnp.where(head_mask, jnp.exp(b_last) * st + upd, 0.0)

    @pl.when(g == 0)
    def _():
        st0 = next_state(jnp.zeros((vw, kw), F32), kf_ref[...].astype(F32), vf_ref[...], cumsum(laf_ref[...]))
        for bb in batch:
            st_sc[bb] = st0

    gn = gn_ref[...]

    def finish(o, sr):
        outs = []
        for hh in range(GLA_HEADS):
            cs = slice(hh * GLA_DV, (hh + 1) * GLA_DV)
            oh = o[:, cs]
            ms = jnp.mean(oh * oh, axis=-1, keepdims=True)
            outs.append(oh * lax.rsqrt(ms + EPS) * gn * sr[:, cs])
        return jnp.concatenate(outs, axis=1)

    n_chunks = GLA_GROUP // c
    for ci in range(n_chunks):
        rs = slice(ci * c, (ci + 1) * c)
        for bb in batch:
            b_sc[bb, rs, :] = cumsum(la_ref[bb, rs, :])
    steep = jnp.min(b_sc[...]) < -GLA_SAFE_DECAY

    @pl.when(jnp.logical_not(steep))
    def _():
        nt = (((1,), (1,)), ((), ()))
        sts = [st_sc[bb] for bb in batch]
        for ci in range(n_chunks):
            rs = slice(ci * c, (ci + 1) * c)
            bs = [b_sc[bb, rs, :] for bb in batch]
            ks = [k_ref[bb, rs, :].astype(F32) for bb in batch]
            vs = [v_ref[bb, rs, :] for bb in batch]
            qes = [q_ref[bb, rs, :].astype(F32) * jnp.exp(bs[bb]) for bb in batch]
            nxt = [next_state(sts[bb], ks[bb], vs[bb], bs[bb]) for bb in batch]
            inters = [lax.dot_general(qes[bb].astype(BF16), sts[bb].astype(BF16), nt,
                                      preferred_element_type=F32) for bb in batch]
            scs = []
            for bb in batch:
                ke = (ks[bb] * jnp.exp(-bs[bb])).astype(BF16)
                qstack = jnp.concatenate([jnp.where(lane_head == hh, qes[bb], 0.0) for hh in range(GLA_HEADS)],
                                         axis=0).astype(BF16)
                sc = lax.dot_general(qstack, ke, nt, preferred_element_type=F32)
                scs.append(jnp.where(causal, sc, 0.0).astype(BF16))
            pvs = [jnp.dot(scs[bb], vs[bb], preferred_element_type=F32) for bb in batch]
            for bb in batch:
                o = jnp.concatenate([inters[bb][:, hh * GLA_DV:(hh + 1) * GLA_DV]
                                     + pvs[bb][hh * c:(hh + 1) * c, hh * GLA_DV:(hh + 1) * GLA_DV]
                                     for hh in range(GLA_HEADS)], axis=1)
                o_ref[bb, rs, :] = finish(o, sr_ref[bb, rs, :].astype(F32)).astype(o_ref.dtype)
            sts = nxt
        for bb in batch:
            st_sc[bb] = sts[bb]

    @pl.when(steep)
    def _():
        rows = 16

        def tile(i, carry):
            off = pl.multiple_of(i * rows, rows)
            for bb in batch:
                a = la_ref[bb, pl.ds(off, rows), :]
                q = q_ref[bb, pl.ds(off, rows), :].astype(F32)
                k = k_ref[bb, pl.ds(off, rows), :].astype(F32)
                v = v_ref[bb, pl.ds(off, rows), :].astype(F32)
                outs = []
                for r in range(rows):
                    row = lambda x: x[r:r + 1].astype(BF16)
                    upd = lax.dot_general(row(v), row(k), (((0,), (0,)), ((), ())),
                                          preferred_element_type=F32)
                    st = jnp.where(head_mask, jnp.exp(a[r:r + 1]) * st_sc[bb] + upd, 0.0)
                    st_sc[bb] = st
                    outs.append(lax.dot_general(row(q), st.astype(BF16), (((1,), (1,)), ((), ())),
                                                preferred_element_type=F32))
                o = jnp.concatenate(outs, axis=0)
                o_ref[bb, pl.ds(off, rows), :] = finish(
                    o, sr_ref[bb, pl.ds(off, rows), :].astype(F32)).astype(o_ref.dtype)
            return carry

        lax.fori_loop(0, GLA_GROUP // rows, tile, 0)


def _gla(kf, vf, laf, q, k, v, la, sr, gn):
    bsz, seq, _ = q.shape
    t = GLA_GROUP
    kw = GLA_HEADS * GLA_DK
    vw = GLA_HEADS * GLA_DV
    last = FRONT // GLA_CHUNK - 1
    nb = GLA_BATCH
    assert bsz % nb == 0
    fr = lambda w: pl.BlockSpec((GLA_CHUNK, w), lambda b, g: (last, 0))
    xs = lambda w: pl.BlockSpec((nb, t, w), lambda b, g: (b, g, 0))
    return pl.pallas_call(
        _gla_body,
        grid=(bsz // nb, seq // t),
        in_specs=[_const_spec((1, GLA_DV)), fr(kw), fr(vw), fr(kw), xs(kw), xs(kw), xs(vw), xs(kw), xs(vw)],
        out_specs=xs(vw),
        out_shape=jax.ShapeDtypeStruct((bsz, seq, vw), BF16),
        scratch_shapes=[pltpu.VMEM((nb, vw, kw), F32), pltpu.VMEM((nb, t, kw), F32)],
        compiler_params=_cparams(("parallel", "arbitrary")),
        name="gla",
    )(gn, kf, vf, laf, q, k, v, la, sr)


def _merge_body(oa_ref, ob_ref, u_ref, gmix_ref, wgate_ref, bgate_ref, wb0_ref, wb1_ref, wout_ref, gffn_ref,
                wr_ref, br_ref,
                u1_out, h2_out, info_out, cnt_out, cnt_sc):
    i = pl.program_id(0)
    tm = ROW_TILE

    @pl.when(i == 0)
    def _():
        cnt_sc[...] = jnp.zeros(cnt_sc.shape, F32)

    ya = lax.dot_general(oa_ref[...], wb0_ref[...], (((0,), (0,)), ((), ())), preferred_element_type=F32)
    yb = jnp.dot(ob_ref[...], wb1_ref[...], preferred_element_type=F32)
    x = u_ref[...]
    h = (x * lax.rsqrt(jnp.mean(x * x, axis=-1, keepdims=True) + EPS) * gmix_ref[...]).astype(BF16)
    gate = _sigmoid(jnp.dot(h, wgate_ref[...], preferred_element_type=F32) + bgate_ref[...])
    merged = gate[:, :D_MODEL] * ya + gate[:, D_MODEL:] * yb
    u1 = x + jnp.dot(merged.astype(BF16), wout_ref[...], preferred_element_type=F32)
    u1_out[...] = u1
    ms = jnp.mean(u1 * u1, axis=-1, keepdims=True)
    h2f = u1 * lax.rsqrt(ms + EPS) * gffn_ref[...]
    _store_row_tiles(h2_out, h2f)
    h2 = h2f.astype(BF16)

    logits = lax.dot_general(wr_ref[...], h2, (((1,), (1,)), ((), ())), preferred_element_type=F32) + br_ref[...]
    row = lax.broadcasted_iota(jnp.int32, (ROUTER_ROWS, tm), 0)
    is_group = row < N_GROUPS
    gl = jnp.where(is_group, logits, NEG_BIG)
    gmax = jnp.max(gl, axis=0, keepdims=True)
    g_idx = jnp.min(jnp.where(gl == gmax, row, ROUTER_ROWS), axis=0, keepdims=True)
    g_w = 1.0 / jnp.sum(jnp.where(is_group, jnp.exp(gl - gmax), 0.0), axis=0, keepdims=True)
    lo = N_GROUPS + EXPERTS_PER_GROUP * g_idx
    el = jnp.where((row >= lo) & (row < lo + EXPERTS_PER_GROUP), logits, NEG_BIG)
    v1 = jnp.max(el, axis=0, keepdims=True)
    i1 = jnp.min(jnp.where(el == v1, row, ROUTER_ROWS), axis=0, keepdims=True)
    el2 = jnp.where(row == i1, NEG_BIG, el)
    v2 = jnp.max(el2, axis=0, keepdims=True)
    i2 = jnp.min(jnp.where(el2 == v2, row, ROUTER_ROWS), axis=0, keepdims=True)
    e21 = jnp.exp(v2 - v1)
    w1 = g_w / (1.0 + e21)
    w2 = w1 * e21

    onehot = ((row == i1) | (row == i2)).astype(BF16)
    si = lax.broadcasted_iota(jnp.int32, (tm, tm), 0)
    ti = lax.broadcasted_iota(jnp.int32, (tm, tm), 1)
    earlier = (si < ti).astype(BF16)
    cnt = cnt_sc[...]
    before = jnp.dot(onehot, earlier, preferred_element_type=F32) + jnp.tile(cnt, (1, tm // LANES))
    r1 = jnp.sum(jnp.where(row == i1, before, 0.0), axis=0, keepdims=True)
    r2 = jnp.sum(jnp.where(row == i2, before, 0.0), axis=0, keepdims=True)
    cnt = cnt + jnp.dot(onehot, jnp.ones((tm, LANES), BF16), preferred_element_type=F32)
    cnt_sc[...] = cnt
    cnt_out[...] = cnt

    zero = jnp.zeros_like(w1)
    info_out[...] = jnp.concatenate([(i1 - N_GROUPS).astype(F32), (i2 - N_GROUPS).astype(F32),
                                     w1, w2, r1, r2, zero, zero], axis=0)


def _merge(oa, ob, u, p):
    n = u.shape[0]
    tm = ROW_TILE
    row = lambda w: pl.BlockSpec((tm, w), lambda i: (i, 0))
    return pl.pallas_call(
        _merge_body,
        grid=(n // tm,),
        in_specs=[pl.BlockSpec((512, tm), lambda i: (0, i)), row(512), row(D_MODEL),
                  _const_spec((1, D_MODEL)), _const_spec((D_MODEL, 2 * D_MODEL)), _const_spec((1, 2 * D_MODEL)),
                  _const_spec((512, D_MODEL)), _const_spec((512, D_MODEL)), _const_spec((D_MODEL, D_MODEL)),
                  _const_spec((1, D_MODEL)), _const_spec((ROUTER_ROWS, D_MODEL)), _const_spec((ROUTER_ROWS, tm))],
        out_specs=[row(D_MODEL), _row_tile_spec(tm, lambda i: (i, 0)), pl.BlockSpec((8, tm), lambda i: (0, i)),
                   _const_spec((ROUTER_ROWS, LANES))],
        out_shape=[jax.ShapeDtypeStruct((n, D_MODEL), F32), jax.ShapeDtypeStruct((n * RT, LANES), F32),
                   jax.ShapeDtypeStruct((8, n), F32), jax.ShapeDtypeStruct((ROUTER_ROWS, LANES), F32)],
        scratch_shapes=[pltpu.VMEM((ROUTER_ROWS, LANES), F32)],
        compiler_params=_cparams(("arbitrary",)),
        name="merge_router",
    )(oa, ob, u, p['gmix'], p['wgate'], p['bgate'], p['wb0'], p['wb1'], p['wout'], p['gffn'], p['wr'], p['br'])


def _dispatch_body(tail_ref, nb_ref, dest_ref, h2_ref, xs_hbm, zero_sc, sem, zsem):
    i = pl.program_id(0)
    blk_rows = MOE_TILE * RT
    n_blocks = xs_hbm.shape[0] // blk_rows

    def zero_copy(blk):
        dst = xs_hbm.at[pl.ds(pl.multiple_of(blk * blk_rows, blk_rows), blk_rows)]
        return pltpu.make_async_copy(zero_sc, dst, zsem)

    @pl.when(i == 0)
    def _():
        zero_sc[...] = jnp.zeros(zero_sc.shape, F32)

        def tails(fn):
            def body(e, carry):
                @pl.when(tail_ref[e] >= 0)
                def _():
                    fn(zero_copy(tail_ref[e]))
                return carry
            lax.fori_loop(0, N_EXPERTS, body, 0)

        def unused(fn):
            def body(b, carry):
                fn(zero_copy(b))
                return carry
            lax.fori_loop(nb_ref[0], n_blocks, body, 0)

        tails(lambda cp: cp.start())
        unused(lambda cp: cp.start())
        tails(lambda cp: cp.wait())
        unused(lambda cp: cp.wait())

    def start(r, carry):
        src = _token_rows(h2_ref, r)
        pltpu.make_async_copy(src, _token_rows(xs_hbm, dest_ref[0, 0, r]), sem).start()
        pltpu.make_async_copy(src, _token_rows(xs_hbm, dest_ref[0, 1, r]), sem).start()
        return carry

    lax.fori_loop(0, DMA_TILE, start, 0)
    for _ in range(2):
        pltpu.make_async_copy(h2_ref, xs_hbm.at[pl.ds(0, DMA_TILE * RT)], sem).wait()


def _dispatch(tail_blocks, n_used, dest, h2, n_slots):
    n = h2.shape[0] // RT
    grid_spec = pltpu.PrefetchScalarGridSpec(
        num_scalar_prefetch=2,
        grid=(n // DMA_TILE,),
        in_specs=[pl.BlockSpec((1, 2, DMA_TILE), lambda i, tb, nb: (i, 0, 0), memory_space=pltpu.SMEM),
                  _row_tile_spec(DMA_TILE, lambda i, tb, nb: (i, 0))],
        out_specs=pl.BlockSpec(memory_space=pl.ANY),
        scratch_shapes=[pltpu.VMEM((MOE_TILE * RT, LANES), F32), pltpu.SemaphoreType.DMA(()),
                        pltpu.SemaphoreType.DMA(())],
    )
    return pl.pallas_call(
        _dispatch_body,
        grid_spec=grid_spec,
        out_shape=jax.ShapeDtypeStruct((n_slots * RT, LANES), F32),
        compiler_params=_cparams(("arbitrary",)),
        name="dispatch",
    )(tail_blocks, n_used, dest, h2)


def _experts_body(be_ref, nb_ref, x_ref, wg_ref, wu_ref, wd_ref, y_ref, wg_sc, wu_sc, wd_sc):
    i = pl.program_id(0)
    prev = be_ref[jnp.maximum(i - 1, 0)]
    fresh = (i == 0) | (be_ref[i] != prev)

    @pl.when(fresh)
    def _():
        wg_sc[...] = wg_ref[...].astype(BF16)
        wu_sc[...] = wu_ref[...].astype(BF16)
        wd_sc[...] = wd_ref[...].astype(BF16)

    @pl.when(i < nb_ref[0])
    def _():
        x = _load_row_tiles(x_ref, MOE_TILE).astype(BF16)
        y = jnp.zeros((MOE_TILE, D_MODEL), F32)
        for j in range(D_EXPERT // EXPERT_CHUNK):
            cs = slice(j * EXPERT_CHUNK, (j + 1) * EXPERT_CHUNK)
            gp = jnp.dot(x, wg_sc[:, cs], preferred_element_type=F32)
            up = jnp.dot(x, wu_sc[:, cs], preferred_element_type=F32)
            hid = (gp * _sigmoid(gp) * up).astype(BF16)
            y = y + jnp.dot(hid, wd_sc[cs, :], preferred_element_type=F32)
        _store_row_tiles(y_ref, y)

    @pl.when(i >= nb_ref[0])
    def _():
        y_ref[...] = jnp.zeros(y_ref.shape, F32)


def _experts(block_e, n_used, xs, wg, wu, wd):
    n_slots = xs.shape[0] // RT
    n_blocks = n_slots // MOE_TILE
    xmap = lambda i, be, nb: (jnp.minimum(i, nb[0] - 1), 0)
    wmap = lambda i, be, nb: (be[i], 0, 0)
    grid_spec = pltpu.PrefetchScalarGridSpec(
        num_scalar_prefetch=2,
        grid=(n_blocks,),
        in_specs=[_row_tile_spec(MOE_TILE, xmap),
                  pl.BlockSpec((None, D_MODEL, D_EXPERT), wmap),
                  pl.BlockSpec((None, D_MODEL, D_EXPERT), wmap),
                  pl.BlockSpec((None, D_EXPERT, D_MODEL), wmap)],
        out_specs=_row_tile_spec(MOE_TILE, lambda i, be, nb: (i, 0)),
        scratch_shapes=[pltpu.VMEM((D_MODEL, D_EXPERT), BF16), pltpu.VMEM((D_MODEL, D_EXPERT), BF16),
                        pltpu.VMEM((D_EXPERT, D_MODEL), BF16)],
    )
    return pl.pallas_call(
        _experts_body,
        grid_spec=grid_spec,
        out_shape=jax.ShapeDtypeStruct((n_slots * RT, LANES), F32),
        compiler_params=_cparams(("arbitrary",)),
        name="experts",
    )(block_e, n_used, xs, wg, wu, wd)


def _combine_body(dest_ref, dest_next_ref, w_ref, u1_ref, ys_hbm, o_ref, ybuf, sems):
    i = pl.program_id(0)
    n_steps = pl.num_programs(0)
    t = DMA_TILE
    group = 8
    slot = i % 2

    def start(d_ref, s, r):
        buf = ybuf.at[s]
        pltpu.make_async_copy(_token_rows(ys_hbm, d_ref[0, 0, r]), _token_rows(buf, r), sems.at[s]).start()
        pltpu.make_async_copy(_token_rows(ys_hbm, d_ref[0, 1, r]), _token_rows(buf, t + r), sems.at[s]).start()

    @pl.when(i == 0)
    def _():
        def first(r, carry):
            start(dest_ref, 0, r)
            return carry

        lax.fori_loop(0, t, first, 0)

    buf = ybuf.at[slot]
    pltpu.make_async_copy(ys_hbm.at[pl.ds(0, buf.shape[0])], buf, sems.at[slot]).wait()

    def combine(j, prefetch):
        base = pl.multiple_of(j * group, group)
        if prefetch:
            for r in range(group):
                start(dest_next_ref, 1 - slot, base + r)
        rows = pl.ds(base, group)
        w = w_ref[rows, :]
        o_ref[rows, :] = (u1_ref[rows, :] + w[:, 0:1] * _load_row_tiles(buf, group, base)
                          + w[:, 1:2] * _load_row_tiles(buf, group, t + base))

    def loop(prefetch):
        def body(j, carry):
            combine(j, prefetch)
            return carry

        lax.fori_loop(0, t // group, body, 0)

    @pl.when(i + 1 < n_steps)
    def _():
        loop(True)

    @pl.when(i + 1 >= n_steps)
    def _():
        loop(False)


def _combine(dest, w, u1, ys):
    n = u1.shape[0]
    t = DMA_TILE
    n_steps = n // t
    return pl.pallas_call(
        _combine_body,
        grid=(n_steps,),
        in_specs=[pl.BlockSpec((1, 2, t), lambda i: (i, 0, 0), memory_space=pltpu.SMEM),
                  pl.BlockSpec((1, 2, t), lambda i: (jnp.minimum(i + 1, n_steps - 1), 0, 0),
                               memory_space=pltpu.SMEM),
                  pl.BlockSpec((t, 2), lambda i: (i, 0)),
                  pl.BlockSpec((t, D_MODEL), lambda i: (i, 0)),
                  pl.BlockSpec(memory_space=pl.ANY)],
        out_specs=pl.BlockSpec((t, D_MODEL), lambda i: (i, 0)),
        out_shape=jax.ShapeDtypeStruct((n, D_MODEL), F32),
        scratch_shapes=[pltpu.VMEM((2, 2 * t * RT, LANES), F32), pltpu.SemaphoreType.DMA((2,))],
        compiler_params=_cparams(("arbitrary",)),
        name="combine",
    )(dest, dest, w, u1, ys)


def _rope_tables(pos):
    half = DA_HEAD_DIM // 2
    inv_freq = jnp.power(ROPE_THETA, -jnp.arange(half, dtype=F32) * 2.0 / DA_HEAD_DIM)
    ang = pos[:, None] * inv_freq[None, :]
    cos, sin = jnp.cos(ang), jnp.sin(ang)
    cos_t = jnp.tile(cos, (1, LANES // half))
    sin_t = jnp.tile(jnp.concatenate([-sin, sin], axis=1), (1, LANES // DA_HEAD_DIM))
    return cos_t, sin_t


def _layer(x, meta_tokens, l, g_mix_norm, w_in, g_q_norm, g_k_norm, lambda_q1, lambda_k1, lambda_q2, lambda_k2,
           g_diff_subln, w_gla_gate_up, b_gla_gate, g_gla_norm, w_branch, b_merge_gate, w_out, g_ffn_norm,
           w_router_group, b_router_group, w_router_expert, b_router_expert, w_exp_gate, w_exp_up, w_exp_down):
    bsz, seq, _ = x.shape
    n = bsz * seq

    w_in_bf = w_in[l].astype(BF16)
    p = {
        'gmix': g_mix_norm[l][None, :],
        'gqn': jnp.tile(g_q_norm[l], LANES // DA_HEAD_DIM)[None, :],
        'gkn': jnp.tile(g_k_norm[l], LANES // DA_HEAD_DIM)[None, :],
        'w_in': w_in_bf,
        'wup': jnp.pad(w_gla_gate_up[l].astype(BF16), ((0, LANES - GLA_RANK), (0, 0))),
        'bup': b_gla_gate[l][None, :],
        'wgate': w_in_bf[:, GATE_OFFSET:GATE_OFFSET + 2 * D_MODEL],
        'bgate': b_merge_gate[l].reshape(1, 2 * D_MODEL),
        'wb0': w_branch[l, 0].astype(BF16), 'wb1': w_branch[l, 1].astype(BF16),
        'wout': w_out[l].astype(BF16),
        'gffn': g_ffn_norm[l][None, :],
        'wr': jnp.pad(jnp.concatenate([w_router_group[l], w_router_expert[l].reshape(D_MODEL, N_EXPERTS)],
                                      axis=1).T.astype(BF16), ((0, ROUTER_ROWS - N_GROUPS - N_EXPERTS), (0, 0))),
        'br': jnp.broadcast_to(
            jnp.pad(jnp.concatenate([b_router_group[l], b_router_expert[l].reshape(N_EXPERTS)]),
                    (0, ROUTER_ROWS - N_GROUPS - N_EXPERTS))[:, None], (ROUTER_ROWS, ROW_TILE)),
    }

    u_front = jnp.concatenate([jnp.zeros((FRONT - N_META, D_MODEL), F32), meta_tokens.astype(F32)], axis=0)
    cos_f, sin_f = _rope_tables(jnp.arange(FRONT, dtype=F32) - (FRONT - N_META))
    cos_x, sin_x = _rope_tables(jnp.arange(seq, dtype=F32) + N_META)
    front = _inproj(u_front, FRONT, cos_f, sin_f, p)
    xin = _inproj(x.reshape(n, D_MODEL), ROW_TILE, cos_x, sin_x, p)
    q_t = xin[0]
    k, v, gq, gk, gv, sr, la = [a.reshape(bsz, seq, a.shape[-1]) for a in xin[1:]]
    _, kf, vf, _, gkf, gvf, _, laf = front

    lam_init = 0.8 - 0.6 * math.exp(-0.3 * l)
    lam_vecs = [a[l][None, :] for a in (lambda_q1, lambda_k1, lambda_q2, lambda_k2)]
    score_bound = (ATT_BOUND_MARGIN * DA_HEAD_DIM * Q_SCALE
                   * jnp.max(jnp.abs(g_q_norm[l])) * jnp.max(jnp.abs(g_k_norm[l]))).reshape(1).astype(F32)
    gsub_t = jnp.broadcast_to(g_diff_subln[l][:, None], (DA_V_DIM, ATT_TILE))
    o_a_t = _diff_attn(score_bound, q_t, kf, vf, k, v, lam_vecs, gsub_t, lam_init)
    o_b = _gla(gkf, gvf, laf, gq, gk, gv, la, sr, g_gla_norm[l][None, :])

    u1, h2, info, cnt = _merge(o_a_t, o_b.reshape(n, -1), x.reshape(n, D_MODEL), p)

    ids = info[0:2].astype(jnp.int32)
    wts = info[2:4]
    rank = info[4:6].astype(jnp.int32)
    counts = cnt[N_GROUPS:N_GROUPS + N_EXPERTS, 0].astype(jnp.int32)
    padded = (counts + MOE_TILE - 1) // MOE_TILE * MOE_TILE
    pends = jnp.cumsum(padded)
    pstarts = pends - padded
    expert = jnp.arange(N_EXPERTS, dtype=jnp.int32)
    dest = jnp.sum(jnp.where(ids[..., None] == expert, pstarts, 0), axis=-1) + rank
    n_slots = (2 * n // MOE_TILE + N_EXPERTS) * MOE_TILE
    n_blocks = n_slots // MOE_TILE
    n_used = (pends[-1] // MOE_TILE).astype(jnp.int32)
    blk = jnp.minimum(jnp.arange(n_blocks, dtype=jnp.int32), n_used - 1) * MOE_TILE
    block_e = jnp.minimum(jnp.sum(pends[None, :] <= blk[:, None], axis=1), N_EXPERTS - 1).astype(jnp.int32)
    tail_blocks = jnp.where(counts > 0, pends // MOE_TILE - 1, -1).astype(jnp.int32)
    dest_t = dest.reshape(2, n // DMA_TILE, DMA_TILE).transpose(1, 0, 2)

    xs = _dispatch(tail_blocks, n_used[None], dest_t, h2, n_slots)
    ys = _experts(block_e, n_used[None], xs, w_exp_gate[l], w_exp_up[l], w_exp_down[l])
    out = _combine(dest_t, wts.T, u1, ys)
    return out.reshape(bsz, seq, D_MODEL)


def kernel(x, meta_tokens, g_mix_norm, w_in, g_q_norm, g_k_norm, lambda_q1, lambda_k1, lambda_q2, lambda_k2,
           g_diff_subln, w_gla_gate_up, b_gla_gate, g_gla_norm, w_branch, b_merge_gate, w_out, g_ffn_norm,
           w_router_group, b_router_group, w_router_expert, b_router_expert, w_exp_gate, w_exp_up, w_exp_down):
    depth = w_in.shape[0]
    assert depth == 1, "meta tokens are only carried through a single layer in this implementation"
    assert x.shape[1] % ROW_TILE == 0 and x.shape[2] == D_MODEL
    return _layer(x, meta_tokens, 0, g_mix_norm, w_in, g_q_norm, g_k_norm, lambda_q1, lambda_k1, lambda_q2,
                  lambda_k2, g_diff_subln, w_gla_gate_up, b_gla_gate, g_gla_norm, w_branch, b_merge_gate, w_out,
                  g_ffn_norm, w_router_group, b_router_group, w_router_expert, b_router_expert,
                  w_exp_gate, w_exp_up, w_exp_down)
```

```python
import functools
import math

import jax
import jax.numpy as jnp
from jax import lax
from jax.experimental import pallas as pl
from jax.experimental.pallas import tpu as pltpu

F32 = jnp.float32
BF16 = jnp.bfloat16

D_MODEL = 1024
N_META = 16
EPS = 1e-6
ROPE_THETA = 10000.0

DA_HEADS = 4
DA_HEAD_DIM = 64
DA_V_DIM = 128
Q_SCALE = DA_HEAD_DIM ** -0.5 * math.log2(math.e)
ATT_BOUND_MARGIN = 1.02
ATT_SAFE_BOUND = 60.0
GLA_HEADS = 4
GLA_DK = 64
GLA_DV = 128
GLA_RANK = 16
GLA_TAU = 16.0
GLA_CHUNK = 64
GLA_SAFE_DECAY = 60.0
N_GROUPS = 4
EXPERTS_PER_GROUP = 8
N_EXPERTS = 32
D_EXPERT = 512
ROUTER_ROWS = 48
MIXER_SECTIONS = ((0, 512), (512, 512), (1024, 512), (1536, 256), (1792, 256), (2048, 512), (2560, 512),
                  (3072, 128))
GATE_OFFSET = 3072 + GLA_RANK

LANES = 128
FRONT = 256
ATT_TILE = 256
ROW_TILE = 512
GLA_GROUP = 512
GLA_BATCH = 4
MOE_TILE = 512
EXPERT_CHUNK = 256
DMA_TILE = 512
NEG_BIG = -1e30
VMEM_LIMIT = 56 * 1024 * 1024


def _cparams(sem):
    return pltpu.CompilerParams(dimension_semantics=sem, vmem_limit_bytes=VMEM_LIMIT)


def _const_spec(shape):
    nd = len(shape)
    return pl.BlockSpec(shape, lambda *_: (0,) * nd)


RT = D_MODEL // LANES


def _row_tile_spec(rows, index_map):
    return pl.BlockSpec((rows * RT, LANES), index_map)


def _token_rows(ref, tok):
    return ref.at[pl.ds(pl.multiple_of(tok * RT, RT), RT)]


def _load_row_tiles(ref, rows, first=0):
    return jnp.concatenate([ref[pl.ds(first * RT + c, rows, stride=RT), :] for c in range(RT)], axis=1)


def _store_row_tiles(ref, val):
    for c in range(RT):
        ref[pl.ds(c, val.shape[0], stride=RT), :] = val[:, c * LANES:(c + 1) * LANES]


def _sigmoid(x):
    return 0.5 * jnp.tanh(0.5 * x) + 0.5


def _log_sigmoid(x):
    return jnp.minimum(x, 0.0) - jnp.log1p(jnp.exp(-jnp.abs(x)))


def _inproj_body(u_ref, gmix_ref, cos_ref, sin_ref, gqn_ref, gkn_ref,
                 wq_ref, wk_ref, wv_ref, wgq_ref, wgk_ref, wgv_ref, wgr_ref, wgg_ref,
                 wup_ref, bup_ref,
                 q_out, k_out, v_out, gq_out, gk_out, gv_out, sr_out, la_out):
    x = u_ref[...]
    ms = jnp.mean(x * x, axis=-1, keepdims=True)
    h = (x * lax.rsqrt(ms + EPS) * gmix_ref[...]).astype(BF16)

    cos = cos_ref[...]
    sin = sin_ref[...]
    lane = lax.broadcasted_iota(jnp.int32, (1, LANES), 1)
    first_half = (lane % DA_HEAD_DIM) < (DA_HEAD_DIM // 2)
    gi = lax.broadcasted_iota(jnp.int32, (LANES, LANES), 0) // DA_HEAD_DIM
    gj = lax.broadcasted_iota(jnp.int32, (LANES, LANES), 1) // DA_HEAD_DIM
    group_sum = (gi == gj).astype(BF16)

    def norm_rope(w_ref, gain_ref, out_ref, scale, transposed):
        z = jnp.dot(h, w_ref[...], preferred_element_type=F32)
        for hh in range(DA_HEADS):
            hs = slice(hh * LANES, (hh + 1) * LANES)
            zh = z[:, hs]
            ssq = jnp.dot((zh * zh).astype(BF16), group_sum, preferred_element_type=F32)
            zn = zh * lax.rsqrt(ssq * (1.0 / DA_HEAD_DIM) + EPS) * gain_ref[...]
            rot = jnp.where(first_half,
                            pltpu.roll(zn, LANES - DA_HEAD_DIM // 2, 1),
                            pltpu.roll(zn, DA_HEAD_DIM // 2, 1))
            zr = (zn * cos + rot * sin) * scale
            if transposed:
                out_ref[hs, :] = jnp.transpose(zr).astype(out_ref.dtype)
            else:
                out_ref[:, hs] = zr.astype(out_ref.dtype)

    norm_rope(wq_ref, gqn_ref, q_out, Q_SCALE, transposed=True)
    norm_rope(wk_ref, gkn_ref, k_out, 1.0, transposed=False)
    v_out[...] = jnp.dot(h, wv_ref[...], preferred_element_type=F32).astype(v_out.dtype)

    gq_out[...] = (jnp.dot(h, wgq_ref[...], preferred_element_type=F32) * (GLA_DK ** -0.5)).astype(gq_out.dtype)
    gk_out[...] = jnp.dot(h, wgk_ref[...], preferred_element_type=F32).astype(gk_out.dtype)
    gv_out[...] = jnp.dot(h, wgv_ref[...], preferred_element_type=F32).astype(gv_out.dtype)
    r = jnp.dot(h, wgr_ref[...], preferred_element_type=F32)
    sr_out[...] = (r * _sigmoid(r)).astype(sr_out.dtype)

    g_lr = jnp.dot(h, wgg_ref[...], preferred_element_type=F32)
    pre = jnp.dot(g_lr.astype(BF16), wup_ref[...], preferred_element_type=F32) + bup_ref[...]
    la_out[...] = _log_sigmoid(pre) * (1.0 / GLA_TAU)


def _inproj(u, tm, cos, sin, p):
    rows = u.shape[0]
    n_tab = cos.shape[0] // tm
    row = lambda w: pl.BlockSpec((tm, w), lambda i: (i, 0))
    tab = pl.BlockSpec((tm, LANES), lambda i: (i % n_tab, 0))
    sections = [pl.BlockSpec((D_MODEL, w), lambda i, j=off // w: (0, j)) for off, w in MIXER_SECTIONS]
    assert all(off % w == 0 for off, w in MIXER_SECTIONS)
    out_widths = [(512, BF16), (512, BF16), (512, BF16), (256, BF16), (256, BF16), (512, BF16),
                  (512, BF16), (256, F32)]
    return pl.pallas_call(
        _inproj_body,
        grid=(rows // tm,),
        in_specs=[row(D_MODEL), _const_spec((1, D_MODEL)), tab, tab,
                  _const_spec((1, LANES)), _const_spec((1, LANES))]
                 + sections + [_const_spec(p['wup'].shape), _const_spec(p['bup'].shape)],
        out_specs=[pl.BlockSpec((out_widths[0][0], tm), lambda i: (0, i))] + [row(w) for w, _ in out_widths[1:]],
        out_shape=[jax.ShapeDtypeStruct((out_widths[0][0], rows), BF16)]
                  + [jax.ShapeDtypeStruct((rows, w), dt) for w, dt in out_widths[1:]],
        compiler_params=_cparams(("parallel",)),
        name="inproj",
    )(u, p['gmix'], cos, sin, p['gqn'], p['gkn'], *([p['w_in']] * len(MIXER_SECTIONS)), p['wup'], p['bup'])


def _diff_attn_body(bound_ref, lq1_ref, lk1_ref, lq2_ref, lk2_ref, gsub_ref,
                    q_ref, kf_ref, vf_ref, kx_ref, vx_ref, o_ref, vt_sc, qs_sc, s_sc, cmax_sc, m_sc, l_sc, acc_sc,
                    *, lam_init):
    qi = pl.program_id(1)
    tq = ATT_TILE
    n_kv = kx_ref.shape[0] // tq
    heads = range(DA_HEADS)
    hs = lambda h: slice(h * LANES, (h + 1) * LANES)

    @pl.when(qi == 0)
    def _():
        def tr(j, carry):
            off = pl.multiple_of(j * tq, tq)
            for h in heads:
                vt_sc[h, j] = jnp.transpose(vx_ref[pl.ds(off, tq), hs(h)].astype(F32)).astype(BF16)
            return carry

        lax.fori_loop(0, n_kv, tr, 0)

    d = lax.broadcasted_iota(jnp.int32, (LANES, tq), 0)
    for h in heads:
        qt = q_ref[hs(h), :]
        zero = jnp.zeros_like(qt)
        qs_sc[h] = jnp.concatenate([jnp.where(d < DA_HEAD_DIM, qt, zero),
                                    jnp.where(d >= DA_HEAD_DIM, qt, zero)], axis=1)

    meta = slice(FRONT - N_META, FRONT)
    key = lax.broadcasted_iota(jnp.int32, (tq, 2 * tq), 0)
    qry = lax.broadcasted_iota(jnp.int32, (tq, 2 * tq), 1) % tq
    causal = key <= qry
    bound = bound_ref[0]

    def meta_scores(h):
        s = jnp.dot(kf_ref[meta, hs(h)], qs_sc[h], preferred_element_type=F32)
        vt = jnp.transpose(vf_ref[meta, hs(h)].astype(F32)).astype(BF16)
        return s, vt

    def block_scores(j, h, diagonal):
        off = pl.multiple_of(j * tq, tq)
        s = jnp.dot(kx_ref[pl.ds(off, tq), hs(h)], qs_sc[h], preferred_element_type=F32)
        return jnp.where(causal, s, NEG_BIG) if diagonal else s

    @pl.when(bound <= ATT_SAFE_BOUND)
    def _():
        l_sc[...] = jnp.zeros(l_sc.shape, F32)
        acc_sc[...] = jnp.zeros(acc_sc.shape, F32)

        def blocks(*js, last=False):
            work = [(h, block_scores(j, h, last and j is js[-1]), (h, j)) for j in js for h in heads]
            if last:
                work += [(h,) + meta_scores(h) for h in heads]
            ps = [(h, jnp.exp2(s - bound), vt) for h, s, vt in work]
            for h, pr, _ in ps:
                l_sc[h] += jnp.sum(pr, axis=0, keepdims=True)
            for h, pr, vt in ps:
                vt = vt_sc[vt] if isinstance(vt, tuple) else vt
                acc_sc[h] += jnp.dot(vt, pr.astype(BF16), preferred_element_type=F32)

        def triple(u, carry):
            blocks(3 * u, 3 * u + 1, 3 * u + 2)
            return carry

        lax.fori_loop(0, qi // 3, triple, 0)
        for rem in range(3):
            pl.when(qi % 3 == rem)(functools.partial(blocks, *[qi - d for d in range(rem, -1, -1)], last=True))

    @pl.when(bound > ATT_SAFE_BOUND)
    def _():
        for h in heads:
            s, vt = meta_scores(h)
            m0 = jnp.max(s, axis=0, keepdims=True)
            pr = jnp.exp2(s - m0)
            m_sc[h] = m0
            l_sc[h] = jnp.sum(pr, axis=0, keepdims=True)
            acc_sc[h] = jnp.dot(vt, pr.astype(BF16), preferred_element_type=F32)

        def scores(j, slot, diagonal=False):
            for h in heads:
                s = block_scores(j, h, diagonal)
                s_sc[h, slot] = s
                cmax_sc[h, slot] = jnp.max(s, axis=0, keepdims=True)

        def accumulate(j, slot):
            for h in heads:
                m_old = m_sc[h]
                m_new = jnp.maximum(m_old, cmax_sc[h, slot])
                alpha = jnp.exp2(m_old - m_new)
                pr = jnp.exp2(s_sc[h, slot] - m_new)
                l_sc[h] = alpha * l_sc[h] + jnp.sum(pr, axis=0, keepdims=True)
                acc_sc[h] = alpha * acc_sc[h] + jnp.dot(vt_sc[h, j], pr.astype(BF16),
                                                        preferred_element_type=F32)
                m_sc[h] = m_new

        @pl.when(qi == 0)
        def _():
            scores(0, 0, diagonal=True)
            accumulate(0, 0)

        @pl.when(qi > 0)
        def _():
            scores(0, 0)

            def pair(u, carry):
                j = 2 * u
                scores(j + 1, 1)
                accumulate(j, 0)
                scores(j + 2, 0)
                accumulate(j + 1, 1)
                return carry

            lax.fori_loop(0, (qi - 1) // 2, pair, 0)

            @pl.when(qi % 2 == 1)
            def _():
                scores(qi, 1, diagonal=True)
                accumulate(qi - 1, 0)
                accumulate(qi, 1)

            @pl.when(qi % 2 == 0)
            def _():
                scores(qi - 1, 1)
                accumulate(qi - 2, 0)
                scores(qi, 0, diagonal=True)
                accumulate(qi - 1, 1)
                accumulate(qi, 0)

    lam = (jnp.exp(jnp.sum(lq1_ref[...] * lk1_ref[...], axis=-1, keepdims=True))
           - jnp.exp(jnp.sum(lq2_ref[...] * lk2_ref[...], axis=-1, keepdims=True)) + lam_init)
    for h in heads:
        acc = acc_sc[h]
        inv_l = 1.0 / l_sc[h]
        ot = acc[:, :tq] * inv_l[:, :tq] - lam * (acc[:, tq:] * inv_l[:, tq:])
        ms = jnp.mean(ot * ot, axis=0, keepdims=True)
        o_ref[hs(h), :] = (ot * lax.rsqrt(ms + EPS) * gsub_ref[...] * (1.0 - lam_init)).astype(o_ref.dtype)


def _diff_attn(bound, q_t, kf, vf, kx, vx, lam_vecs, gsub_t, lam_init):
    bsz, seq, _ = kx.shape
    tq = ATT_TILE
    vec = _const_spec((1, DA_HEAD_DIM))
    width = DA_HEADS * LANES
    tok_tile = pl.BlockSpec((width, tq), lambda b, i: (0, b * (seq // tq) + i))
    return pl.pallas_call(
        functools.partial(_diff_attn_body, lam_init=lam_init),
        grid=(bsz, seq // tq),
        in_specs=[pl.BlockSpec(memory_space=pltpu.SMEM), vec, vec, vec, vec, _const_spec((DA_V_DIM, tq)),
                  tok_tile,
                  _const_spec((FRONT, width)), _const_spec((FRONT, width)),
                  pl.BlockSpec((None, seq, width), lambda b, i: (b, 0, 0)),
                  pl.BlockSpec((None, seq, width), lambda b, i: (b, 0, 0))],
        out_specs=tok_tile,
        out_shape=jax.ShapeDtypeStruct((width, bsz * seq), BF16),
        scratch_shapes=[pltpu.VMEM((DA_HEADS, seq // tq, DA_V_DIM, tq), BF16),
                        pltpu.VMEM((DA_HEADS, LANES, 2 * tq), BF16),
                        pltpu.VMEM((DA_HEADS, 2, tq, 2 * tq), F32), pltpu.VMEM((DA_HEADS, 2, 1, 2 * tq), F32),
                        pltpu.VMEM((DA_HEADS, 1, 2 * tq), F32), pltpu.VMEM((DA_HEADS, 1, 2 * tq), F32),
                        pltpu.VMEM((DA_HEADS, DA_V_DIM, 2 * tq), F32)],
        compiler_params=_cparams(("parallel", "arbitrary")),
        name="diff_attn",
    )(bound, *lam_vecs, gsub_t, q_t, kf, vf, kx, vx)


def _split3(a):
    a1 = a.astype(BF16)
    r1 = a - a1.astype(F32)
    a2 = r1.astype(BF16)
    a3 = (r1 - a2.astype(F32)).astype(BF16)
    return a1, a2, a3


def _gla_body(gn_ref, kf_ref, vf_ref, laf_ref, q_ref, k_ref, v_ref, la_ref, sr_ref, o_ref, st_sc, b_sc):
    g = pl.program_id(1)
    c = GLA_CHUNK
    kw = GLA_HEADS * GLA_DK
    vw = GLA_HEADS * GLA_DV

    ti = lax.broadcasted_iota(jnp.int32, (c, c), 0)
    si = lax.broadcasted_iota(jnp.int32, (c, c), 1)
    tri = (si <= ti).astype(BF16)
    hv = lax.broadcasted_iota(jnp.int32, (vw, kw), 0) // GLA_DV
    hk = lax.broadcasted_iota(jnp.int32, (vw, kw), 1) // GLA_DK
    head_mask = hv == hk
    lane_head = lax.broadcasted_iota(jnp.int32, (1, kw), 1) // GLA_DK
    causal = lax.broadcasted_iota(jnp.int32, (GLA_HEADS * c, c), 0) % c >= \
        lax.broadcasted_iota(jnp.int32, (GLA_HEADS * c, c), 1)

    def cumsum(a):
        a1, a2, a3 = _split3(a)
        return (jnp.dot(tri, a1, preferred_element_type=F32)
                + jnp.dot(tri, a2, preferred_element_type=F32)
                + jnp.dot(tri, a3, preferred_element_type=F32))

    batch = range(GLA_BATCH)

    def next_state(st, k, v, b):
        b_last = b[c - 1:c, :]
        kd = (k * jnp.exp(b_last - b)).astype(BF16)
        upd = lax.dot_general(v, kd, (((0,), (0,)), ((), ())), preferred_element_type=F32)
        return jnp.where(head_mask, jnp.exp(b_last) * st + upd, 0.0)

    @pl.when(g == 0)
    def _():
        st0 = next_state(jnp.zeros((vw, kw), F32), kf_ref[...].astype(F32), vf_ref[...], cumsum(laf_ref[...]))
        for bb in batch:
            st_sc[bb] = st0

    gn = gn_ref[...]

    def finish(o, sr):
        outs = []
        for hh in range(GLA_HEADS):
            cs = slice(hh * GLA_DV, (hh + 1) * GLA_DV)
            oh = o[:, cs]
            ms = jnp.mean(oh * oh, axis=-1, keepdims=True)
            outs.append(oh * lax.rsqrt(ms + EPS) * gn * sr[:, cs])
        return jnp.concatenate(outs, axis=1)

    n_chunks = GLA_GROUP // c
    for ci in range(n_chunks):
        rs = slice(ci * c, (ci + 1) * c)
        for bb in batch:
            b_sc[bb, rs, :] = cumsum(la_ref[bb, rs, :])
    steep = jnp.min(b_sc[...]) < -GLA_SAFE_DECAY

    @pl.when(jnp.logical_not(steep))
    def _():
        nt = (((1,), (1,)), ((), ()))
        sts = [st_sc[bb] for bb in batch]
        for ci in range(n_chunks):
            rs = slice(ci * c, (ci + 1) * c)
            bs = [b_sc[bb, rs, :] for bb in batch]
            ks = [k_ref[bb, rs, :].astype(F32) for bb in batch]
            vs = [v_ref[bb, rs, :] for bb in batch]
            qes = [q_ref[bb, rs, :].astype(F32) * jnp.exp(bs[bb]) for bb in batch]
            nxt = [next_state(sts[bb], ks[bb], vs[bb], bs[bb]) for bb in batch]
            inters = [lax.dot_general(qes[bb].astype(BF16), sts[bb].astype(BF16), nt,
                                      preferred_element_type=F32) for bb in batch]
            scs = []
            for bb in batch:
                ke = (ks[bb] * jnp.exp(-bs[bb])).astype(BF16)
                qstack = jnp.concatenate([jnp.where(lane_head == hh, qes[bb], 0.0) for hh in range(GLA_HEADS)],
                                         axis=0).astype(BF16)
                sc = lax.dot_general(qstack, ke, nt, preferred_element_type=F32)
                scs.append(jnp.where(causal, sc, 0.0).astype(BF16))
            pvs = [jnp.dot(scs[bb], vs[bb], preferred_element_type=F32) for bb in batch]
            for bb in batch:
                o = jnp.concatenate([inters[bb][:, hh * GLA_DV:(hh + 1) * GLA_DV]
                                     + pvs[bb][hh * c:(hh + 1) * c, hh * GLA_DV:(hh + 1) * GLA_DV]
                                     for hh in range(GLA_HEADS)], axis=1)
                o_ref[bb, rs, :] = finish(o, sr_ref[bb, rs, :].astype(F32)).astype(o_ref.dtype)
            sts = nxt
        for bb in batch:
            st_sc[bb] = sts[bb]

    @pl.when(steep)
    def _():
        rows = 16

        def tile(i, carry):
            off = pl.multiple_of(i * rows, rows)
            for bb in batch:
                a = la_ref[bb, pl.ds(off, rows), :]
                q = q_ref[bb, pl.ds(off, rows), :].astype(F32)
                k = k_ref[bb, pl.ds(off, rows), :].astype(F32)
                v = v_ref[bb, pl.ds(off, rows), :].astype(F32)
                outs = []
                for r in range(rows):
                    row = lambda x: x[r:r + 1].astype(BF16)
                    upd = lax.dot_general(row(v), row(k), (((0,), (0,)), ((), ())),
                                          preferred_element_type=F32)
                    st = jnp.where(head_mask, jnp.exp(a[r:r + 1]) * st_sc[bb] + upd, 0.0)
                    st_sc[bb] = st
                    outs.append(lax.dot_general(row(q), st.astype(BF16), (((1,), (1,)), ((), ())),
                                                preferred_element_type=F32))
                o = jnp.concatenate(outs, axis=0)
                o_ref[bb, pl.ds(off, rows), :] = finish(
                    o, sr_ref[bb, pl.ds(off, rows), :].astype(F32)).astype(o_ref.dtype)
            return carry

        lax.fori_loop(0, GLA_GROUP // rows, tile, 0)


def _gla(kf, vf, laf, q, k, v, la, sr, gn):
    bsz, seq, _ = q.shape
    t = GLA_GROUP
    kw = GLA_HEADS * GLA_DK
    vw = GLA_HEADS * GLA_DV
    last = FRONT // GLA_CHUNK - 1
    nb = GLA_BATCH
    assert bsz % nb == 0
    fr = lambda w: pl.BlockSpec((GLA_CHUNK, w), lambda b, g: (last, 0))
    xs = lambda w: pl.BlockSpec((nb, t, w), lambda b, g: (b, g, 0))
    return pl.pallas_call(
        _gla_body,
        grid=(bsz // nb, seq // t),
        in_specs=[_const_spec((1, GLA_DV)), fr(kw), fr(vw), fr(kw), xs(kw), xs(kw), xs(vw), xs(kw), xs(vw)],
        out_specs=xs(vw),
        out_shape=jax.ShapeDtypeStruct((bsz, seq, vw), BF16),
        scratch_shapes=[pltpu.VMEM((nb, vw, kw), F32), pltpu.VMEM((nb, t, kw), F32)],
        compiler_params=_cparams(("parallel", "arbitrary")),
        name="gla",
    )(gn, kf, vf, laf, q, k, v, la, sr)


def _merge_body(oa_ref, ob_ref, u_ref, gmix_ref, wgate_ref, bgate_ref, wb0_ref, wb1_ref, wout_ref, gffn_ref,
                wr_ref, br_ref,
                u1_out, h2_out, info_out, cnt_out, cnt_sc):
    i = pl.program_id(0)
    tm = ROW_TILE

    @pl.when(i == 0)
    def _():
        cnt_sc[...] = jnp.zeros(cnt_sc.shape, F32)

    ya = lax.dot_general(oa_ref[...], wb0_ref[...], (((0,), (0,)), ((), ())), preferred_element_type=F32)
    yb = jnp.dot(ob_ref[...], wb1_ref[...], preferred_element_type=F32)
    x = u_ref[...]
    h = (x * lax.rsqrt(jnp.mean(x * x, axis=-1, keepdims=True) + EPS) * gmix_ref[...]).astype(BF16)
    gate = _sigmoid(jnp.dot(h, wgate_ref[...], preferred_element_type=F32) + bgate_ref[...])
    merged = gate[:, :D_MODEL] * ya + gate[:, D_MODEL:] * yb
    u1 = x + jnp.dot(merged.astype(BF16), wout_ref[...], preferred_element_type=F32)
    u1_out[...] = u1
    ms = jnp.mean(u1 * u1, axis=-1, keepdims=True)
    h2f = u1 * lax.rsqrt(ms + EPS) * gffn_ref[...]
    _store_row_tiles(h2_out, h2f)
    h2 = h2f.astype(BF16)

    logits = lax.dot_general(wr_ref[...], h2, (((1,), (1,)), ((), ())), preferred_element_type=F32) + br_ref[...]
    row = lax.broadcasted_iota(jnp.int32, (ROUTER_ROWS, tm), 0)
    is_group = row < N_GROUPS
    gl = jnp.where(is_group, logits, NEG_BIG)
    gmax = jnp.max(gl, axis=0, keepdims=True)
    g_idx = jnp.min(jnp.where(gl == gmax, row, ROUTER_ROWS), axis=0, keepdims=True)
    g_w = 1.0 / jnp.sum(jnp.where(is_group, jnp.exp(gl - gmax), 0.0), axis=0, keepdims=True)
    lo = N_GROUPS + EXPERTS_PER_GROUP * g_idx
    el = jnp.where((row >= lo) & (row < lo + EXPERTS_PER_GROUP), logits, NEG_BIG)
    v1 = jnp.max(el, axis=0, keepdims=True)
    i1 = jnp.min(jnp.where(el == v1, row, ROUTER_ROWS), axis=0, keepdims=True)
    el2 = jnp.where(row == i1, NEG_BIG, el)
    v2 = jnp.max(el2, axis=0, keepdims=True)
    i2 = jnp.min(jnp.where(el2 == v2, row, ROUTER_ROWS), axis=0, keepdims=True)
    e21 = jnp.exp(v2 - v1)
    w1 = g_w / (1.0 + e21)
    w2 = w1 * e21

    onehot = ((row == i1) | (row == i2)).astype(BF16)
    si = lax.broadcasted_iota(jnp.int32, (tm, tm), 0)
    ti = lax.broadcasted_iota(jnp.int32, (tm, tm), 1)
    earlier = (si < ti).astype(BF16)
    cnt = cnt_sc[...]
    before = jnp.dot(onehot, earlier, preferred_element_type=F32) + jnp.tile(cnt, (1, tm // LANES))
    r1 = jnp.sum(jnp.where(row == i1, before, 0.0), axis=0, keepdims=True)
    r2 = jnp.sum(jnp.where(row == i2, before, 0.0), axis=0, keepdims=True)
    cnt = cnt + jnp.dot(onehot, jnp.ones((tm, LANES), BF16), preferred_element_type=F32)
    cnt_sc[...] = cnt
    cnt_out[...] = cnt

    zero = jnp.zeros_like(w1)
    info_out[...] = jnp.concatenate([(i1 - N_GROUPS).astype(F32), (i2 - N_GROUPS).astype(F32),
                                     w1, w2, r1, r2, zero, zero], axis=0)


def _merge(oa, ob, u, p):
    n = u.shape[0]
    tm = ROW_TILE
    row = lambda w: pl.BlockSpec((tm, w), lambda i: (i, 0))
    return pl.pallas_call(
        _merge_body,
        grid=(n // tm,),
        in_specs=[pl.BlockSpec((512, tm), lambda i: (0, i)), row(512), row(D_MODEL),
                  _const_spec((1, D_MODEL)), _const_spec((D_MODEL, 2 * D_MODEL)), _const_spec((1, 2 * D_MODEL)),
                  _const_spec((512, D_MODEL)), _const_spec((512, D_MODEL)), _const_spec((D_MODEL, D_MODEL)),
                  _const_spec((1, D_MODEL)), _const_spec((ROUTER_ROWS, D_MODEL)), _const_spec((ROUTER_ROWS, tm))],
        out_specs=[row(D_MODEL), _row_tile_spec(tm, lambda i: (i, 0)), pl.BlockSpec((8, tm), lambda i: (0, i)),
                   _const_spec((ROUTER_ROWS, LANES))],
        out_shape=[jax.ShapeDtypeStruct((n, D_MODEL), F32), jax.ShapeDtypeStruct((n * RT, LANES), F32),
                   jax.ShapeDtypeStruct((8, n), F32), jax.ShapeDtypeStruct((ROUTER_ROWS, LANES), F32)],
        scratch_shapes=[pltpu.VMEM((ROUTER_ROWS, LANES), F32)],
        compiler_params=_cparams(("arbitrary",)),
        name="merge_router",
    )(oa, ob, u, p['gmix'], p['wgate'], p['bgate'], p['wb0'], p['wb1'], p['wout'], p['gffn'], p['wr'], p['br'])


def _dispatch_body(tail_ref, nb_ref, dest_ref, h2_ref, xs_hbm, zero_sc, sem, zsem):
    i = pl.program_id(0)
    blk_rows = MOE_TILE * RT
    n_blocks = xs_hbm.shape[0] // blk_rows

    def zero_copy(blk):
        dst = xs_hbm.at[pl.ds(pl.multiple_of(blk * blk_rows, blk_rows), blk_rows)]
        return pltpu.make_async_copy(zero_sc, dst, zsem)

    @pl.when(i == 0)
    def _():
        zero_sc[...] = jnp.zeros(zero_sc.shape, F32)

        def tails(fn):
            def body(e, carry):
                @pl.when(tail_ref[e] >= 0)
                def _():
                    fn(zero_copy(tail_ref[e]))
                return carry
            lax.fori_loop(0, N_EXPERTS, body, 0)

        def unused(fn):
            def body(b, carry):
                fn(zero_copy(b))
                return carry
            lax.fori_loop(nb_ref[0], n_blocks, body, 0)

        tails(lambda cp: cp.start())
        unused(lambda cp: cp.start())
        tails(lambda cp: cp.wait())
        unused(lambda cp: cp.wait())

    def start(r, carry):
        src = _token_rows(h2_ref, r)
        pltpu.make_async_copy(src, _token_rows(xs_hbm, dest_ref[0, 0, r]), sem).start()
        pltpu.make_async_copy(src, _token_rows(xs_hbm, dest_ref[0, 1, r]), sem).start()
        return carry

    lax.fori_loop(0, DMA_TILE, start, 0)
    for _ in range(2):
        pltpu.make_async_copy(h2_ref, xs_hbm.at[pl.ds(0, DMA_TILE * RT)], sem).wait()


def _dispatch(tail_blocks, n_used, dest, h2, n_slots):
    n = h2.shape[0] // RT
    grid_spec = pltpu.PrefetchScalarGridSpec(
        num_scalar_prefetch=2,
        grid=(n // DMA_TILE,),
        in_specs=[pl.BlockSpec((1, 2, DMA_TILE), lambda i, tb, nb: (i, 0, 0), memory_space=pltpu.SMEM),
                  _row_tile_spec(DMA_TILE, lambda i, tb, nb: (i, 0))],
        out_specs=pl.BlockSpec(memory_space=pl.ANY),
        scratch_shapes=[pltpu.VMEM((MOE_TILE * RT, LANES), F32), pltpu.SemaphoreType.DMA(()),
                        pltpu.SemaphoreType.DMA(())],
    )
    return pl.pallas_call(
        _dispatch_body,
        grid_spec=grid_spec,
        out_shape=jax.ShapeDtypeStruct((n_slots * RT, LANES), F32),
        compiler_params=_cparams(("arbitrary",)),
        name="dispatch",
    )(tail_blocks, n_used, dest, h2)


def _experts_body(be_ref, nb_ref, x_ref, wg_ref, wu_ref, wd_ref, y_ref, wg_sc, wu_sc, wd_sc):
    i = pl.program_id(0)
    prev = be_ref[jnp.maximum(i - 1, 0)]
    fresh = (i == 0) | (be_ref[i] != prev)

    @pl.when(fresh)
    def _():
        wg_sc[...] = wg_ref[...].astype(BF16)
        wu_sc[...] = wu_ref[...].astype(BF16)
        wd_sc[...] = wd_ref[...].astype(BF16)

    @pl.when(i < nb_ref[0])
    def _():
        x = _load_row_tiles(x_ref, MOE_TILE).astype(BF16)
        y = jnp.zeros((MOE_TILE, D_MODEL), F32)
        for j in range(D_EXPERT // EXPERT_CHUNK):
            cs = slice(j * EXPERT_CHUNK, (j + 1) * EXPERT_CHUNK)
            gp = jnp.dot(x, wg_sc[:, cs], preferred_element_type=F32)
            up = jnp.dot(x, wu_sc[:, cs], preferred_element_type=F32)
            hid = (gp * _sigmoid(gp) * up).astype(BF16)
            y = y + jnp.dot(hid, wd_sc[cs, :], preferred_element_type=F32)
        _store_row_tiles(y_ref, y)

    @pl.when(i >= nb_ref[0])
    def _():
        y_ref[...] = jnp.zeros(y_ref.shape, F32)


def _experts(block_e, n_used, xs, wg, wu, wd):
    n_slots = xs.shape[0] // RT
    n_blocks = n_slots // MOE_TILE
    xmap = lambda i, be, nb: (jnp.minimum(i, nb[0] - 1), 0)
    wmap = lambda i, be, nb: (be[i], 0, 0)
    grid_spec = pltpu.PrefetchScalarGridSpec(
        num_scalar_prefetch=2,
        grid=(n_blocks,),
        in_specs=[_row_tile_spec(MOE_TILE, xmap),
                  pl.BlockSpec((None, D_MODEL, D_EXPERT), wmap),
                  pl.BlockSpec((None, D_MODEL, D_EXPERT), wmap),
                  pl.BlockSpec((None, D_EXPERT, D_MODEL), wmap)],
        out_specs=_row_tile_spec(MOE_TILE, lambda i, be, nb: (i, 0)),
        scratch_shapes=[pltpu.VMEM((D_MODEL, D_EXPERT), BF16), pltpu.VMEM((D_MODEL, D_EXPERT), BF16),
                        pltpu.VMEM((D_EXPERT, D_MODEL), BF16)],
    )
    return pl.pallas_call(
        _experts_body,
        grid_spec=grid_spec,
        out_shape=jax.ShapeDtypeStruct((n_slots * RT, LANES), F32),
        compiler_params=_cparams(("arbitrary",)),
        name="experts",
    )(block_e, n_used, xs, wg, wu, wd)


def _combine_body(dest_ref, dest_next_ref, w_ref, u1_ref, ys_hbm, o_ref, ybuf, sems):
    i = pl.program_id(0)
    n_steps = pl.num_programs(0)
    t = DMA_TILE
    group = 32
    slot = i % 2

    def start(d_ref, s, r):
        buf = ybuf.at[s]
        pltpu.make_async_copy(_token_rows(ys_hbm, d_ref[0, 0, r]), _token_rows(buf, r), sems.at[s]).start()
        pltpu.make_async_copy(_token_rows(ys_hbm, d_ref[0, 1, r]), _token_rows(buf, t + r), sems.at[s]).start()

    @pl.when(i == 0)
    def _():
        def first(r, carry):
            start(dest_ref, 0, r)
            return carry

        lax.fori_loop(0, t, first, 0)

    buf = ybuf.at[slot]
    pltpu.make_async_copy(ys_hbm.at[pl.ds(0, buf.shape[0])], buf, sems.at[slot]).wait()

    def combine(j, prefetch):
        base = pl.multiple_of(j * group, group)
        if prefetch:
            for r in range(group):
                start(dest_next_ref, 1 - slot, base + r)
        rows = pl.ds(base, group)
        w = w_ref[rows, :]
        o_ref[rows, :] = (u1_ref[rows, :] + w[:, 0:1] * _load_row_tiles(buf, group, base)
                          + w[:, 1:2] * _load_row_tiles(buf, group, t + base))

    def loop(prefetch):
        def body(j, carry):
            combine(j, prefetch)
            return carry

        lax.fori_loop(0, t // group, body, 0)

    @pl.when(i + 1 < n_steps)
    def _():
        loop(True)

    @pl.when(i + 1 >= n_steps)
    def _():
        loop(False)


def _combine(dest, w, u1, ys):
    n = u1.shape[0]
    t = DMA_TILE
    n_steps = n // t
    return pl.pallas_call(
        _combine_body,
        grid=(n_steps,),
        in_specs=[pl.BlockSpec((1, 2, t), lambda i: (i, 0, 0), memory_space=pltpu.SMEM),
                  pl.BlockSpec((1, 2, t), lambda i: (jnp.minimum(i + 1, n_steps - 1), 0, 0),
                               memory_space=pltpu.SMEM),
                  pl.BlockSpec((t, 2), lambda i: (i, 0)),
                  pl.BlockSpec((t, D_MODEL), lambda i: (i, 0)),
                  pl.BlockSpec(memory_space=pl.ANY)],
        out_specs=pl.BlockSpec((t, D_MODEL), lambda i: (i, 0)),
        out_shape=jax.ShapeDtypeStruct((n, D_MODEL), F32),
        scratch_shapes=[pltpu.VMEM((2, 2 * t * RT, LANES), F32), pltpu.SemaphoreType.DMA((2,))],
        compiler_params=_cparams(("arbitrary",)),
        name="combine",
    )(dest, dest, w, u1, ys)


def _rope_tables(pos):
    half = DA_HEAD_DIM // 2
    inv_freq = jnp.power(ROPE_THETA, -jnp.arange(half, dtype=F32) * 2.0 / DA_HEAD_DIM)
    ang = pos[:, None] * inv_freq[None, :]
    cos, sin = jnp.cos(ang), jnp.sin(ang)
    cos_t = jnp.tile(cos, (1, LANES // half))
    sin_t = jnp.tile(jnp.concatenate([-sin, sin], axis=1), (1, LANES // DA_HEAD_DIM))
    return cos_t, sin_t


def _layer(x, meta_tokens, l, g_mix_norm, w_in, g_q_norm, g_k_norm, lambda_q1, lambda_k1, lambda_q2, lambda_k2,
           g_diff_subln, w_gla_gate_up, b_gla_gate, g_gla_norm, w_branch, b_merge_gate, w_out, g_ffn_norm,
           w_router_group, b_router_group, w_router_expert, b_router_expert, w_exp_gate, w_exp_up, w_exp_down):
    bsz, seq, _ = x.shape
    n = bsz * seq

    w_in_bf = w_in[l].astype(BF16)
    p = {
        'gmix': g_mix_norm[l][None, :],
        'gqn': jnp.tile(g_q_norm[l], LANES // DA_HEAD_DIM)[None, :],
        'gkn': jnp.tile(g_k_norm[l], LANES // DA_HEAD_DIM)[None, :],
        'w_in': w_in_bf,
        'wup': jnp.pad(w_gla_gate_up[l].astype(BF16), ((0, LANES - GLA_RANK), (0, 0))),
        'bup': b_gla_gate[l][None, :],
        'wgate': w_in_bf[:, GATE_OFFSET:GATE_OFFSET + 2 * D_MODEL],
        'bgate': b_merge_gate[l].reshape(1, 2 * D_MODEL),
        'wb0': w_branch[l, 0].astype(BF16), 'wb1': w_branch[l, 1].astype(BF16),
        'wout': w_out[l].astype(BF16),
        'gffn': g_ffn_norm[l][None, :],
        'wr': jnp.pad(jnp.concatenate([w_router_group[l], w_router_expert[l].reshape(D_MODEL, N_EXPERTS)],
                                      axis=1).T.astype(BF16), ((0, ROUTER_ROWS - N_GROUPS - N_EXPERTS), (0, 0))),
        'br': jnp.broadcast_to(
            jnp.pad(jnp.concatenate([b_router_group[l], b_router_expert[l].reshape(N_EXPERTS)]),
                    (0, ROUTER_ROWS - N_GROUPS - N_EXPERTS))[:, None], (ROUTER_ROWS, ROW_TILE)),
    }

    u_front = jnp.concatenate([jnp.zeros((FRONT - N_META, D_MODEL), F32), meta_tokens.astype(F32)], axis=0)
    cos_f, sin_f = _rope_tables(jnp.arange(FRONT, dtype=F32) - (FRONT - N_META))
    cos_x, sin_x = _rope_tables(jnp.arange(seq, dtype=F32) + N_META)
    front = _inproj(u_front, FRONT, cos_f, sin_f, p)
    xin = _inproj(x.reshape(n, D_MODEL), ROW_TILE, cos_x, sin_x, p)
    q_t = xin[0]
    k, v, gq, gk, gv, sr, la = [a.reshape(bsz, seq, a.shape[-1]) for a in xin[1:]]
    _, kf, vf, _, gkf, gvf, _, laf = front

    lam_init = 0.8 - 0.6 * math.exp(-0.3 * l)
    lam_vecs = [a[l][None, :] for a in (lambda_q1, lambda_k1, lambda_q2, lambda_k2)]
    score_bound = (ATT_BOUND_MARGIN * DA_HEAD_DIM * Q_SCALE
                   * jnp.max(jnp.abs(g_q_norm[l])) * jnp.max(jnp.abs(g_k_norm[l]))).reshape(1).astype(F32)
    gsub_t = jnp.broadcast_to(g_diff_subln[l][:, None], (DA_V_DIM, ATT_TILE))
    o_a_t = _diff_attn(score_bound, q_t, kf, vf, k, v, lam_vecs, gsub_t, lam_init)
    o_b = _gla(gkf, gvf, laf, gq, gk, gv, la, sr, g_gla_norm[l][None, :])

    u1, h2, info, cnt = _merge(o_a_t, o_b.reshape(n, -1), x.reshape(n, D_MODEL), p)

    ids = info[0:2].astype(jnp.int32)
    wts = info[2:4]
    rank = info[4:6].astype(jnp.int32)
    counts = cnt[N_GROUPS:N_GROUPS + N_EXPERTS, 0].astype(jnp.int32)
    padded = (counts + MOE_TILE - 1) // MOE_TILE * MOE_TILE
    pends = jnp.cumsum(padded)
    pstarts = pends - padded
    expert = jnp.arange(N_EXPERTS, dtype=jnp.int32)
    dest = jnp.sum(jnp.where(ids[..., None] == expert, pstarts, 0), axis=-1) + rank
    n_slots = (2 * n // MOE_TILE + N_EXPERTS) * MOE_TILE
    n_blocks = n_slots // MOE_TILE
    n_used = (pends[-1] // MOE_TILE).astype(jnp.int32)
    blk = jnp.minimum(jnp.arange(n_blocks, dtype=jnp.int32), n_used - 1) * MOE_TILE
    block_e = jnp.minimum(jnp.sum(pends[None, :] <= blk[:, None], axis=1), N_EXPERTS - 1).astype(jnp.int32)
    tail_blocks = jnp.where(counts > 0, pends // MOE_TILE - 1, -1).astype(jnp.int32)
    dest_t = dest.reshape(2, n // DMA_TILE, DMA_TILE).transpose(1, 0, 2)

    xs = _dispatch(tail_blocks, n_used[None], dest_t, h2, n_slots)
    ys = _experts(block_e, n_used[None], xs, w_exp_gate[l], w_exp_up[l], w_exp_down[l])
    out = _combine(dest_t, wts.T, u1, ys)
    return out.reshape(bsz, seq, D_MODEL)


def kernel(x, meta_tokens, g_mix_norm, w_in, g_q_norm, g_k_norm, lambda_q1, lambda_k1, lambda_q2, lambda_k2,
           g_diff_subln, w_gla_gate_up, b_gla_gate, g_gla_norm, w_branch, b_merge_gate, w_out, g_ffn_norm,
           w_router_group, b_router_group, w_router_expert, b_router_expert, w_exp_gate, w_exp_up, w_exp_down):
    depth = w_in.shape[0]
    assert depth == 1, "meta tokens are only carried through a single layer in this implementation"
    assert x.shape[1] % ROW_TILE == 0 and x.shape[2] == D_MODEL
    return _layer(x, meta_tokens, 0, g_mix_norm, w_in, g_q_norm, g_k_norm, lambda_q1, lambda_k1, lambda_q2,
                  lambda_k2, g_diff_subln, w_gla_gate_up, b_gla_gate, g_gla_norm, w_branch, b_merge_gate, w_out,
                  g_ffn_norm, w_router_group, b_router_group, w_router_expert, b_router_expert,
                  w_exp_gate, w_exp_up, w_exp_down)
```

```python
import functools
import math

import jax
import jax.numpy as jnp
from jax import lax
from jax.experimental import pallas as pl
from jax.experimental.pallas import tpu as pltpu

F32 = jnp.float32
BF16 = jnp.bfloat16

D_MODEL = 1024
N_META = 16
EPS = 1e-6
ROPE_THETA = 10000.0

DA_HEADS = 4
DA_HEAD_DIM = 64
DA_V_DIM = 128
Q_SCALE = DA_HEAD_DIM ** -0.5 * math.log2(math.e)
ATT_BOUND_MARGIN = 1.02
ATT_SAFE_BOUND = 60.0
GLA_HEADS = 4
GLA_DK = 64
GLA_DV = 128
GLA_RANK = 16
GLA_TAU = 16.0
GLA_CHUNK = 64
GLA_SAFE_DECAY = 60.0
N_GROUPS = 4
EXPERTS_PER_GROUP = 8
N_EXPERTS = 32
D_EXPERT = 512
ROUTER_ROWS = 48
MIXER_SECTIONS = ((0, 512), (512, 512), (1024, 512), (1536, 256), (1792, 256), (2048, 512), (2560, 512),
                  (3072, 128))
GATE_OFFSET = 3072 + GLA_RANK

LANES = 128
FRONT = 256
ATT_TILE = 256
ROW_TILE = 512
GLA_GROUP = 512
GLA_BATCH = 4
MOE_TILE = 512
EXPERT_CHUNK = 256
X_SLOTS = 3
DMA_TILE = 512
NEG_BIG = -1e30
VMEM_LIMIT = 56 * 1024 * 1024


def _cparams(sem):
    return pltpu.CompilerParams(dimension_semantics=sem, vmem_limit_bytes=VMEM_LIMIT)


def _const_spec(shape):
    nd = len(shape)
    return pl.BlockSpec(shape, lambda *_: (0,) * nd)


RT = D_MODEL // LANES


def _row_tile_spec(rows, index_map):
    return pl.BlockSpec((rows * RT, LANES), index_map)


def _token_rows(ref, tok):
    return ref.at[pl.ds(pl.multiple_of(tok * RT, RT), RT)]


def _load_row_tiles(ref, rows, first=0):
    return jnp.concatenate([ref[pl.ds(first * RT + c, rows, stride=RT), :] for c in range(RT)], axis=1)


def _store_row_tiles(ref, val):
    for c in range(RT):
        ref[pl.ds(c, val.shape[0], stride=RT), :] = val[:, c * LANES:(c + 1) * LANES]


def _sigmoid(x):
    return 0.5 * jnp.tanh(0.5 * x) + 0.5


def _log_sigmoid(x):
    return jnp.minimum(x, 0.0) - jnp.log1p(jnp.exp(-jnp.abs(x)))


def _inproj_body(u_ref, gmix_ref, cos_ref, sin_ref, gqn_ref, gkn_ref,
                 wq_ref, wk_ref, wv_ref, wgq_ref, wgk_ref, wgv_ref, wgr_ref, wgg_ref,
                 wup_ref, bup_ref,
                 q_out, k_out, v_out, gq_out, gk_out, gv_out, sr_out, la_out):
    x = u_ref[...]
    ms = jnp.mean(x * x, axis=-1, keepdims=True)
    h = (x * lax.rsqrt(ms + EPS) * gmix_ref[...]).astype(BF16)

    cos = cos_ref[...]
    sin = sin_ref[...]
    lane = lax.broadcasted_iota(jnp.int32, (1, LANES), 1)
    first_half = (lane % DA_HEAD_DIM) < (DA_HEAD_DIM // 2)
    gi = lax.broadcasted_iota(jnp.int32, (LANES, LANES), 0) // DA_HEAD_DIM
    gj = lax.broadcasted_iota(jnp.int32, (LANES, LANES), 1) // DA_HEAD_DIM
    group_sum = (gi == gj).astype(BF16)

    def norm_rope(w_ref, gain_ref, out_ref, scale, transposed):
        z = jnp.dot(h, w_ref[...], preferred_element_type=F32)
        for hh in range(DA_HEADS):
            hs = slice(hh * LANES, (hh + 1) * LANES)
            zh = z[:, hs]
            ssq = jnp.dot((zh * zh).astype(BF16), group_sum, preferred_element_type=F32)
            zn = zh * lax.rsqrt(ssq * (1.0 / DA_HEAD_DIM) + EPS) * gain_ref[...]
            rot = jnp.where(first_half,
                            pltpu.roll(zn, LANES - DA_HEAD_DIM // 2, 1),
                            pltpu.roll(zn, DA_HEAD_DIM // 2, 1))
            zr = (zn * cos + rot * sin) * scale
            if transposed:
                out_ref[hs, :] = jnp.transpose(zr).astype(out_ref.dtype)
            else:
                out_ref[:, hs] = zr.astype(out_ref.dtype)

    norm_rope(wq_ref, gqn_ref, q_out, Q_SCALE, transposed=True)
    norm_rope(wk_ref, gkn_ref, k_out, 1.0, transposed=False)
    v_out[...] = jnp.dot(h, wv_ref[...], preferred_element_type=F32).astype(v_out.dtype)

    gq_out[...] = (jnp.dot(h, wgq_ref[...], preferred_element_type=F32) * (GLA_DK ** -0.5)).astype(gq_out.dtype)
    gk_out[...] = jnp.dot(h, wgk_ref[...], preferred_element_type=F32).astype(gk_out.dtype)
    gv_out[...] = jnp.dot(h, wgv_ref[...], preferred_element_type=F32).astype(gv_out.dtype)
    r = jnp.dot(h, wgr_ref[...], preferred_element_type=F32)
    sr_out[...] = (r * _sigmoid(r)).astype(sr_out.dtype)

    g_lr = jnp.dot(h, wgg_ref[...], preferred_element_type=F32)
    pre = jnp.dot(g_lr.astype(BF16), wup_ref[...], preferred_element_type=F32) + bup_ref[...]
    la_out[...] = _log_sigmoid(pre) * (1.0 / GLA_TAU)


def _inproj(u, tm, cos, sin, p):
    rows = u.shape[0]
    n_tab = cos.shape[0] // tm
    row = lambda w: pl.BlockSpec((tm, w), lambda i: (i, 0))
    tab = pl.BlockSpec((tm, LANES), lambda i: (i % n_tab, 0))
    sections = [pl.BlockSpec((D_MODEL, w), lambda i, j=off // w: (0, j)) for off, w in MIXER_SECTIONS]
    assert all(off % w == 0 for off, w in MIXER_SECTIONS)
    out_widths = [(512, BF16), (512, BF16), (512, BF16), (256, BF16), (256, BF16), (512, BF16),
                  (512, BF16), (256, F32)]
    return pl.pallas_call(
        _inproj_body,
        grid=(rows // tm,),
        in_specs=[row(D_MODEL), _const_spec((1, D_MODEL)), tab, tab,
                  _const_spec((1, LANES)), _const_spec((1, LANES))]
                 + sections + [_const_spec(p['wup'].shape), _const_spec(p['bup'].shape)],
        out_specs=[pl.BlockSpec((out_widths[0][0], tm), lambda i: (0, i))] + [row(w) for w, _ in out_widths[1:]],
        out_shape=[jax.ShapeDtypeStruct((out_widths[0][0], rows), BF16)]
                  + [jax.ShapeDtypeStruct((rows, w), dt) for w, dt in out_widths[1:]],
        compiler_params=_cparams(("parallel",)),
        name="inproj",
    )(u, p['gmix'], cos, sin, p['gqn'], p['gkn'], *([p['w_in']] * len(MIXER_SECTIONS)), p['wup'], p['bup'])


def _diff_attn_body(bound_ref, lq1_ref, lk1_ref, lq2_ref, lk2_ref, gsub_ref,
                    q_ref, kf_ref, vf_ref, kx_ref, vx_ref, o_ref, vt_sc, qs_sc, s_sc, cmax_sc, m_sc, l_sc, acc_sc,
                    *, lam_init):
    qi = pl.program_id(1)
    tq = ATT_TILE
    n_kv = kx_ref.shape[0] // tq
    heads = range(DA_HEADS)
    hs = lambda h: slice(h * LANES, (h + 1) * LANES)

    @pl.when(qi == 0)
    def _():
        def tr(j, carry):
            off = pl.multiple_of(j * tq, tq)
            for h in heads:
                vt_sc[h, j] = jnp.transpose(vx_ref[pl.ds(off, tq), hs(h)].astype(F32)).astype(BF16)
            return carry

        lax.fori_loop(0, n_kv, tr, 0)

    d = lax.broadcasted_iota(jnp.int32, (LANES, tq), 0)
    for h in heads:
        qt = q_ref[hs(h), :]
        zero = jnp.zeros_like(qt)
        qs_sc[h] = jnp.concatenate([jnp.where(d < DA_HEAD_DIM, qt, zero),
                                    jnp.where(d >= DA_HEAD_DIM, qt, zero)], axis=1)

    meta = slice(FRONT - N_META, FRONT)
    key = lax.broadcasted_iota(jnp.int32, (tq, 2 * tq), 0)
    qry = lax.broadcasted_iota(jnp.int32, (tq, 2 * tq), 1) % tq
    causal = key <= qry
    bound = bound_ref[0]

    def meta_scores(h):
        s = jnp.dot(kf_ref[meta, hs(h)], qs_sc[h], preferred_element_type=F32)
        vt = jnp.transpose(vf_ref[meta, hs(h)].astype(F32)).astype(BF16)
        return s, vt

    def block_scores(j, h, diagonal):
        off = pl.multiple_of(j * tq, tq)
        s = jnp.dot(kx_ref[pl.ds(off, tq), hs(h)], qs_sc[h], preferred_element_type=F32)
        return jnp.where(causal, s, NEG_BIG) if diagonal else s

    @pl.when(bound <= ATT_SAFE_BOUND)
    def _():
        l_sc[...] = jnp.zeros(l_sc.shape, F32)
        acc_sc[...] = jnp.zeros(acc_sc.shape, F32)

        def blocks(*js, last=False):
            work = [(h, block_scores(j, h, last and j is js[-1]), (h, j)) for j in js for h in heads]
            if last:
                work += [(h,) + meta_scores(h) for h in heads]
            ps = [(h, jnp.exp2(s - bound), vt) for h, s, vt in work]
            for h, pr, _ in ps:
                l_sc[h] += jnp.sum(pr, axis=0, keepdims=True)
            for h, pr, vt in ps:
                vt = vt_sc[vt] if isinstance(vt, tuple) else vt
                acc_sc[h] += jnp.dot(vt, pr.astype(BF16), preferred_element_type=F32)

        def triple(u, carry):
            blocks(3 * u, 3 * u + 1, 3 * u + 2)
            return carry

        lax.fori_loop(0, qi // 3, triple, 0)
        for rem in range(3):
            pl.when(qi % 3 == rem)(functools.partial(blocks, *[qi - d for d in range(rem, -1, -1)], last=True))

    @pl.when(bound > ATT_SAFE_BOUND)
    def _():
        for h in heads:
            s, vt = meta_scores(h)
            m0 = jnp.max(s, axis=0, keepdims=True)
            pr = jnp.exp2(s - m0)
            m_sc[h] = m0
            l_sc[h] = jnp.sum(pr, axis=0, keepdims=True)
            acc_sc[h] = jnp.dot(vt, pr.astype(BF16), preferred_element_type=F32)

        def scores(j, slot, diagonal=False):
            for h in heads:
                s = block_scores(j, h, diagonal)
                s_sc[h, slot] = s
                cmax_sc[h, slot] = jnp.max(s, axis=0, keepdims=True)

        def accumulate(j, slot):
            for h in heads:
                m_old = m_sc[h]
                m_new = jnp.maximum(m_old, cmax_sc[h, slot])
                alpha = jnp.exp2(m_old - m_new)
                pr = jnp.exp2(s_sc[h, slot] - m_new)
                l_sc[h] = alpha * l_sc[h] + jnp.sum(pr, axis=0, keepdims=True)
                acc_sc[h] = alpha * acc_sc[h] + jnp.dot(vt_sc[h, j], pr.astype(BF16),
                                                        preferred_element_type=F32)
                m_sc[h] = m_new

        @pl.when(qi == 0)
        def _():
            scores(0, 0, diagonal=True)
            accumulate(0, 0)

        @pl.when(qi > 0)
        def _():
            scores(0, 0)

            def pair(u, carry):
                j = 2 * u
                scores(j + 1, 1)
                accumulate(j, 0)
                scores(j + 2, 0)
                accumulate(j + 1, 1)
                return carry

            lax.fori_loop(0, (qi - 1) // 2, pair, 0)

            @pl.when(qi % 2 == 1)
            def _():
                scores(qi, 1, diagonal=True)
                accumulate(qi - 1, 0)
                accumulate(qi, 1)

            @pl.when(qi % 2 == 0)
            def _():
                scores(qi - 1, 1)
                accumulate(qi - 2, 0)
                scores(qi, 0, diagonal=True)
                accumulate(qi - 1, 1)
                accumulate(qi, 0)

    lam = (jnp.exp(jnp.sum(lq1_ref[...] * lk1_ref[...], axis=-1, keepdims=True))
           - jnp.exp(jnp.sum(lq2_ref[...] * lk2_ref[...], axis=-1, keepdims=True)) + lam_init)
    for h in heads:
        acc = acc_sc[h]
        inv_l = 1.0 / l_sc[h]
        ot = acc[:, :tq] * inv_l[:, :tq] - lam * (acc[:, tq:] * inv_l[:, tq:])
        ms = jnp.mean(ot * ot, axis=0, keepdims=True)
        o_ref[hs(h), :] = (ot * lax.rsqrt(ms + EPS) * gsub_ref[...] * (1.0 - lam_init)).astype(o_ref.dtype)


def _diff_attn(bound, q_t, kf, vf, kx, vx, lam_vecs, gsub_t, lam_init):
    bsz, seq, _ = kx.shape
    tq = ATT_TILE
    vec = _const_spec((1, DA_HEAD_DIM))
    width = DA_HEADS * LANES
    tok_tile = pl.BlockSpec((width, tq), lambda b, i: (0, b * (seq // tq) + i))
    return pl.pallas_call(
        functools.partial(_diff_attn_body, lam_init=lam_init),
        grid=(bsz, seq // tq),
        in_specs=[pl.BlockSpec(memory_space=pltpu.SMEM), vec, vec, vec, vec, _const_spec((DA_V_DIM, tq)),
                  tok_tile,
                  _const_spec((FRONT, width)), _const_spec((FRONT, width)),
                  pl.BlockSpec((None, seq, width), lambda b, i: (b, 0, 0)),
                  pl.BlockSpec((None, seq, width), lambda b, i: (b, 0, 0))],
        out_specs=tok_tile,
        out_shape=jax.ShapeDtypeStruct((width, bsz * seq), BF16),
        scratch_shapes=[pltpu.VMEM((DA_HEADS, seq // tq, DA_V_DIM, tq), BF16),
                        pltpu.VMEM((DA_HEADS, LANES, 2 * tq), BF16),
                        pltpu.VMEM((DA_HEADS, 2, tq, 2 * tq), F32), pltpu.VMEM((DA_HEADS, 2, 1, 2 * tq), F32),
                        pltpu.VMEM((DA_HEADS, 1, 2 * tq), F32), pltpu.VMEM((DA_HEADS, 1, 2 * tq), F32),
                        pltpu.VMEM((DA_HEADS, DA_V_DIM, 2 * tq), F32)],
        compiler_params=_cparams(("parallel", "arbitrary")),
        name="diff_attn",
    )(bound, *lam_vecs, gsub_t, q_t, kf, vf, kx, vx)


def _split3(a):
    a1 = a.astype(BF16)
    r1 = a - a1.astype(F32)
    a2 = r1.astype(BF16)
    a3 = (r1 - a2.astype(F32)).astype(BF16)
    return a1, a2, a3


def _gla_body(gn_ref, kf_ref, vf_ref, laf_ref, q_ref, k_ref, v_ref, la_ref, sr_ref, o_ref, st_sc, b_sc):
    g = pl.program_id(1)
    c = GLA_CHUNK
    kw = GLA_HEADS * GLA_DK
    vw = GLA_HEADS * GLA_DV

    ti = lax.broadcasted_iota(jnp.int32, (c, c), 0)
    si = lax.broadcasted_iota(jnp.int32, (c, c), 1)
    tri = (si <= ti).astype(BF16)
    hv = lax.broadcasted_iota(jnp.int32, (vw, kw), 0) // GLA_DV
    hk = lax.broadcasted_iota(jnp.int32, (vw, kw), 1) // GLA_DK
    head_mask = hv == hk
    lane_head = lax.broadcasted_iota(jnp.int32, (1, kw), 1) // GLA_DK
    causal = lax.broadcasted_iota(jnp.int32, (GLA_HEADS * c, c), 0) % c >= \
        lax.broadcasted_iota(jnp.int32, (GLA_HEADS * c, c), 1)

    def cumsum(a):
        a1, a2, a3 = _split3(a)
        return (jnp.dot(tri, a1, preferred_element_type=F32)
                + jnp.dot(tri, a2, preferred_element_type=F32)
                + jnp.dot(tri, a3, preferred_element_type=F32))

    batch = range(GLA_BATCH)

    def next_state(st, k, v, b):
        b_last = b[c - 1:c, :]
        kd = (k * jnp.exp(b_last - b)).astype(BF16)
        upd = lax.dot_general(v, kd, (((0,), (0,)), ((), ())), preferred_element_type=F32)
        return jnp.where(head_mask, jnp.exp(b_last) * st + upd, 0.0)

    @pl.when(g == 0)
    def _():
        st0 = next_state(jnp.zeros((vw, kw), F32), kf_ref[...].astype(F32), vf_ref[...], cumsum(laf_ref[...]))
        for bb in batch:
            st_sc[bb] = st0

    gn = gn_ref[...]

    def finish(o, sr):
        outs = []
        for hh in range(GLA_HEADS):
            cs = slice(hh * GLA_DV, (hh + 1) * GLA_DV)
            oh = o[:, cs]
            ms = jnp.mean(oh * oh, axis=-1, keepdims=True)
            outs.append(oh * lax.rsqrt(ms + EPS) * gn * sr[:, cs])
        return jnp.concatenate(outs, axis=1)

    n_chunks = GLA_GROUP // c
    for ci in range(n_chunks):
        rs = slice(ci * c, (ci + 1) * c)
        for bb in batch:
            b_sc[bb, rs, :] = cumsum(la_ref[bb, rs, :])
    steep = jnp.min(b_sc[...]) < -GLA_SAFE_DECAY

    @pl.when(jnp.logical_not(steep))
    def _():
        nt = (((1,), (1,)), ((), ()))
        sts = [st_sc[bb] for bb in batch]
        for ci in range(n_chunks):
            rs = slice(ci * c, (ci + 1) * c)
            bs = [b_sc[bb, rs, :] for bb in batch]
            ks = [k_ref[bb, rs, :].astype(F32) for bb in batch]
            vs = [v_ref[bb, rs, :] for bb in batch]
            qes = [q_ref[bb, rs, :].astype(F32) * jnp.exp(bs[bb]) for bb in batch]
            nxt = [next_state(sts[bb], ks[bb], vs[bb], bs[bb]) for bb in batch]
            inters = [lax.dot_general(qes[bb].astype(BF16), sts[bb].astype(BF16), nt,
                                      preferred_element_type=F32) for bb in batch]
            scs = []
            for bb in batch:
                ke = (ks[bb] * jnp.exp(-bs[bb])).astype(BF16)
                qstack = jnp.concatenate([jnp.where(lane_head == hh, qes[bb], 0.0) for hh in range(GLA_HEADS)],
                                         axis=0).astype(BF16)
                sc = lax.dot_general(qstack, ke, nt, preferred_element_type=F32)
                scs.append(jnp.where(causal, sc, 0.0).astype(BF16))
            pvs = [jnp.dot(scs[bb], vs[bb], preferred_element_type=F32) for bb in batch]
            for bb in batch:
                o = jnp.concatenate([inters[bb][:, hh * GLA_DV:(hh + 1) * GLA_DV]
                                     + pvs[bb][hh * c:(hh + 1) * c, hh * GLA_DV:(hh + 1) * GLA_DV]
                                     for hh in range(GLA_HEADS)], axis=1)
                o_ref[bb, rs, :] = finish(o, sr_ref[bb, rs, :].astype(F32)).astype(o_ref.dtype)
            sts = nxt
        for bb in batch:
            st_sc[bb] = sts[bb]

    @pl.when(steep)
    def _():
        rows = 16

        def tile(i, carry):
            off = pl.multiple_of(i * rows, rows)
            for bb in batch:
                a = la_ref[bb, pl.ds(off, rows), :]
                q = q_ref[bb, pl.ds(off, rows), :].astype(F32)
                k = k_ref[bb, pl.ds(off, rows), :].astype(F32)
                v = v_ref[bb, pl.ds(off, rows), :].astype(F32)
                outs = []
                for r in range(rows):
                    row = lambda x: x[r:r + 1].astype(BF16)
                    upd = lax.dot_general(row(v), row(k), (((0,), (0,)), ((), ())),
                                          preferred_element_type=F32)
                    st = jnp.where(head_mask, jnp.exp(a[r:r + 1]) * st_sc[bb] + upd, 0.0)
                    st_sc[bb] = st
                    outs.append(lax.dot_general(row(q), st.astype(BF16), (((1,), (1,)), ((), ())),
                                                preferred_element_type=F32))
                o = jnp.concatenate(outs, axis=0)
                o_ref[bb, pl.ds(off, rows), :] = finish(
                    o, sr_ref[bb, pl.ds(off, rows), :].astype(F32)).astype(o_ref.dtype)
            return carry

        lax.fori_loop(0, GLA_GROUP // rows, tile, 0)


def _gla(kf, vf, laf, q, k, v, la, sr, gn):
    bsz, seq, _ = q.shape
    t = GLA_GROUP
    kw = GLA_HEADS * GLA_DK
    vw = GLA_HEADS * GLA_DV
    last = FRONT // GLA_CHUNK - 1
    nb = GLA_BATCH
    assert bsz % nb == 0
    fr = lambda w: pl.BlockSpec((GLA_CHUNK, w), lambda b, g: (last, 0))
    xs = lambda w: pl.BlockSpec((nb, t, w), lambda b, g: (b, g, 0))
    return pl.pallas_call(
        _gla_body,
        grid=(bsz // nb, seq // t),
        in_specs=[_const_spec((1, GLA_DV)), fr(kw), fr(vw), fr(kw), xs(kw), xs(kw), xs(vw), xs(kw), xs(vw)],
        out_specs=xs(vw),
        out_shape=jax.ShapeDtypeStruct((bsz, seq, vw), BF16),
        scratch_shapes=[pltpu.VMEM((nb, vw, kw), F32), pltpu.VMEM((nb, t, kw), F32)],
        compiler_params=_cparams(("parallel", "arbitrary")),
        name="gla",
    )(gn, kf, vf, laf, q, k, v, la, sr)


def _merge_body(oa_ref, ob_ref, u_ref, gmix_ref, wgate_ref, bgate_ref, wb0_ref, wb1_ref, wout_ref, gffn_ref,
                wr_ref, br_ref,
                u1_out, h2_out, info_out, cnt_out, cnt_sc):
    i = pl.program_id(0)
    tm = ROW_TILE

    @pl.when(i == 0)
    def _():
        cnt_sc[...] = jnp.zeros(cnt_sc.shape, F32)

    ya = lax.dot_general(oa_ref[...], wb0_ref[...], (((0,), (0,)), ((), ())), preferred_element_type=F32)
    yb = jnp.dot(ob_ref[...], wb1_ref[...], preferred_element_type=F32)
    x = u_ref[...]
    h = (x * lax.rsqrt(jnp.mean(x * x, axis=-1, keepdims=True) + EPS) * gmix_ref[...]).astype(BF16)
    gate = _sigmoid(jnp.dot(h, wgate_ref[...], preferred_element_type=F32) + bgate_ref[...])
    merged = gate[:, :D_MODEL] * ya + gate[:, D_MODEL:] * yb
    u1 = x + jnp.dot(merged.astype(BF16), wout_ref[...], preferred_element_type=F32)
    u1_out[...] = u1
    ms = jnp.mean(u1 * u1, axis=-1, keepdims=True)
    h2f = u1 * lax.rsqrt(ms + EPS) * gffn_ref[...]
    _store_row_tiles(h2_out, h2f)
    h2 = h2f.astype(BF16)

    logits = lax.dot_general(wr_ref[...], h2, (((1,), (1,)), ((), ())), preferred_element_type=F32) + br_ref[...]
    row = lax.broadcasted_iota(jnp.int32, (ROUTER_ROWS, tm), 0)
    is_group = row < N_GROUPS
    gl = jnp.where(is_group, logits, NEG_BIG)
    gmax = jnp.max(gl, axis=0, keepdims=True)
    g_idx = jnp.min(jnp.where(gl == gmax, row, ROUTER_ROWS), axis=0, keepdims=True)
    g_w = 1.0 / jnp.sum(jnp.where(is_group, jnp.exp(gl - gmax), 0.0), axis=0, keepdims=True)
    lo = N_GROUPS + EXPERTS_PER_GROUP * g_idx
    el = jnp.where((row >= lo) & (row < lo + EXPERTS_PER_GROUP), logits, NEG_BIG)
    v1 = jnp.max(el, axis=0, keepdims=True)
    i1 = jnp.min(jnp.where(el == v1, row, ROUTER_ROWS), axis=0, keepdims=True)
    el2 = jnp.where(row == i1, NEG_BIG, el)
    v2 = jnp.max(el2, axis=0, keepdims=True)
    i2 = jnp.min(jnp.where(el2 == v2, row, ROUTER_ROWS), axis=0, keepdims=True)
    e21 = jnp.exp(v2 - v1)
    w1 = g_w / (1.0 + e21)
    w2 = w1 * e21

    onehot = ((row == i1) | (row == i2)).astype(BF16)
    si = lax.broadcasted_iota(jnp.int32, (tm, tm), 0)
    ti = lax.broadcasted_iota(jnp.int32, (tm, tm), 1)
    earlier = (si < ti).astype(BF16)
    cnt = cnt_sc[...]
    before = jnp.dot(onehot, earlier, preferred_element_type=F32) + jnp.tile(cnt, (1, tm // LANES))
    r1 = jnp.sum(jnp.where(row == i1, before, 0.0), axis=0, keepdims=True)
    r2 = jnp.sum(jnp.where(row == i2, before, 0.0), axis=0, keepdims=True)
    cnt = cnt + jnp.dot(onehot, jnp.ones((tm, LANES), BF16), preferred_element_type=F32)
    cnt_sc[...] = cnt
    cnt_out[...] = cnt

    zero = jnp.zeros_like(w1)
    info_out[...] = jnp.concatenate([(i1 - N_GROUPS).astype(F32), (i2 - N_GROUPS).astype(F32),
                                     w1, w2, r1, r2, zero, zero], axis=0)


def _merge(oa, ob, u, p):
    n = u.shape[0]
    tm = ROW_TILE
    row = lambda w: pl.BlockSpec((tm, w), lambda i: (i, 0))
    return pl.pallas_call(
        _merge_body,
        grid=(n // tm,),
        in_specs=[pl.BlockSpec((512, tm), lambda i: (0, i)), row(512), row(D_MODEL),
                  _const_spec((1, D_MODEL)), _const_spec((D_MODEL, 2 * D_MODEL)), _const_spec((1, 2 * D_MODEL)),
                  _const_spec((512, D_MODEL)), _const_spec((512, D_MODEL)), _const_spec((D_MODEL, D_MODEL)),
                  _const_spec((1, D_MODEL)), _const_spec((ROUTER_ROWS, D_MODEL)), _const_spec((ROUTER_ROWS, tm))],
        out_specs=[row(D_MODEL), _row_tile_spec(tm, lambda i: (i, 0)), pl.BlockSpec((8, tm), lambda i: (0, i)),
                   _const_spec((ROUTER_ROWS, LANES))],
        out_shape=[jax.ShapeDtypeStruct((n, D_MODEL), F32), jax.ShapeDtypeStruct((n * RT, LANES), F32),
                   jax.ShapeDtypeStruct((8, n), F32), jax.ShapeDtypeStruct((ROUTER_ROWS, LANES), F32)],
        scratch_shapes=[pltpu.VMEM((ROUTER_ROWS, LANES), F32)],
        compiler_params=_cparams(("arbitrary",)),
        name="merge_router",
    )(oa, ob, u, p['gmix'], p['wgate'], p['bgate'], p['wb0'], p['wb1'], p['wout'], p['gffn'], p['wr'], p['br'])


def _dispatch_body(tail_ref, nb_ref, dest_ref, h2_ref, xs_hbm, zero_sc, sem, zsem):
    i = pl.program_id(0)
    blk_rows = MOE_TILE * RT
    n_blocks = xs_hbm.shape[0] // blk_rows

    def zero_copy(blk):
        dst = xs_hbm.at[pl.ds(pl.multiple_of(blk * blk_rows, blk_rows), blk_rows)]
        return pltpu.make_async_copy(zero_sc, dst, zsem)

    @pl.when(i == 0)
    def _():
        zero_sc[...] = jnp.zeros(zero_sc.shape, F32)

        def tails(fn):
            def body(e, carry):
                @pl.when(tail_ref[e] >= 0)
                def _():
                    fn(zero_copy(tail_ref[e]))
                return carry
            lax.fori_loop(0, N_EXPERTS, body, 0)

        def unused(fn):
            def body(b, carry):
                fn(zero_copy(b))
                return carry
            lax.fori_loop(nb_ref[0], n_blocks, body, 0)

        tails(lambda cp: cp.start())
        unused(lambda cp: cp.start())
        tails(lambda cp: cp.wait())
        unused(lambda cp: cp.wait())

    def start(r, carry):
        src = _token_rows(h2_ref, r)
        pltpu.make_async_copy(src, _token_rows(xs_hbm, dest_ref[0, 0, r]), sem).start()
        pltpu.make_async_copy(src, _token_rows(xs_hbm, dest_ref[0, 1, r]), sem).start()
        return carry

    lax.fori_loop(0, DMA_TILE, start, 0)
    for _ in range(2):
        pltpu.make_async_copy(h2_ref, xs_hbm.at[pl.ds(0, DMA_TILE * RT)], sem).wait()


def _dispatch(tail_blocks, n_used, dest, h2, n_slots):
    n = h2.shape[0] // RT
    grid_spec = pltpu.PrefetchScalarGridSpec(
        num_scalar_prefetch=2,
        grid=(n // DMA_TILE,),
        in_specs=[pl.BlockSpec((1, 2, DMA_TILE), lambda i, tb, nb: (i, 0, 0), memory_space=pltpu.SMEM),
                  _row_tile_spec(DMA_TILE, lambda i, tb, nb: (i, 0))],
        out_specs=pl.BlockSpec(memory_space=pl.ANY),
        scratch_shapes=[pltpu.VMEM((MOE_TILE * RT, LANES), F32), pltpu.SemaphoreType.DMA(()),
                        pltpu.SemaphoreType.DMA(())],
    )
    return pl.pallas_call(
        _dispatch_body,
        grid_spec=grid_spec,
        out_shape=jax.ShapeDtypeStruct((n_slots * RT, LANES), F32),
        compiler_params=_cparams(("arbitrary",)),
        name="dispatch",
    )(tail_blocks, n_used, dest, h2)


def _experts_body(be_ref, nb_ref, xs_hbm, wg_ref, wu_ref, wd_ref, y_ref, wg_sc, wu_sc, wd_sc, xbuf, xsem):
    i = pl.program_id(0)
    nb = nb_ref[0]
    blk_rows = MOE_TILE * RT
    ahead = X_SLOTS - 1

    def fetch(b):
        src = xs_hbm.at[pl.ds(pl.multiple_of(b * blk_rows, blk_rows), blk_rows)]
        return pltpu.make_async_copy(src, xbuf.at[b % X_SLOTS], xsem.at[b % X_SLOTS])

    @pl.when(i == 0)
    def _():
        for b in range(ahead):
            pl.when(b < nb)(lambda b=b: fetch(b).start())

    @pl.when(i + ahead < nb)
    def _():
        fetch(i + ahead).start()

    prev = be_ref[jnp.maximum(i - 1, 0)]
    fresh = (i == 0) | (be_ref[i] != prev)

    @pl.when(fresh)
    def _():
        wg_sc[...] = wg_ref[...].astype(BF16)
        wu_sc[...] = wu_ref[...].astype(BF16)
        wd_sc[...] = wd_ref[...].astype(BF16)

    @pl.when(i < nb)
    def _():
        fetch(i).wait()
        x = _load_row_tiles(xbuf.at[i % X_SLOTS], MOE_TILE).astype(BF16)
        y = jnp.zeros((MOE_TILE, D_MODEL), F32)
        for j in range(D_EXPERT // EXPERT_CHUNK):
            cs = slice(j * EXPERT_CHUNK, (j + 1) * EXPERT_CHUNK)
            gp = jnp.dot(x, wg_sc[:, cs], preferred_element_type=F32)
            up = jnp.dot(x, wu_sc[:, cs], preferred_element_type=F32)
            hid = (gp * _sigmoid(gp) * up).astype(BF16)
            y = y + jnp.dot(hid, wd_sc[cs, :], preferred_element_type=F32)
        _store_row_tiles(y_ref, y)

    @pl.when(i >= nb)
    def _():
        y_ref[...] = jnp.zeros(y_ref.shape, F32)


def _experts(block_e, n_used, xs, wg, wu, wd):
    n_slots = xs.shape[0] // RT
    n_blocks = n_slots // MOE_TILE
    wmap = lambda i, be, nb: (be[i], 0, 0)
    grid_spec = pltpu.PrefetchScalarGridSpec(
        num_scalar_prefetch=2,
        grid=(n_blocks,),
        in_specs=[pl.BlockSpec(memory_space=pl.ANY),
                  pl.BlockSpec((None, D_MODEL, D_EXPERT), wmap),
                  pl.BlockSpec((None, D_MODEL, D_EXPERT), wmap),
                  pl.BlockSpec((None, D_EXPERT, D_MODEL), wmap)],
        out_specs=_row_tile_spec(MOE_TILE, lambda i, be, nb: (i, 0)),
        scratch_shapes=[pltpu.VMEM((D_MODEL, D_EXPERT), BF16), pltpu.VMEM((D_MODEL, D_EXPERT), BF16),
                        pltpu.VMEM((D_EXPERT, D_MODEL), BF16),
                        pltpu.VMEM((X_SLOTS, MOE_TILE * RT, LANES), F32), pltpu.SemaphoreType.DMA((X_SLOTS,))],
    )
    return pl.pallas_call(
        _experts_body,
        grid_spec=grid_spec,
        out_shape=jax.ShapeDtypeStruct((n_slots * RT, LANES), F32),
        compiler_params=_cparams(("arbitrary",)),
        name="experts",
    )(block_e, n_used, xs, wg, wu, wd)


def _combine_body(dest_ref, dest_next_ref, w_ref, u1_ref, ys_hbm, o_ref, ybuf, sems):
    i = pl.program_id(0)
    n_steps = pl.num_programs(0)
    t = DMA_TILE
    group = 32
    slot = i % 2

    def start(d_ref, s, r):
        buf = ybuf.at[s]
        pltpu.make_async_copy(_token_rows(ys_hbm, d_ref[0, 0, r]), _token_rows(buf, r), sems.at[s]).start()
        pltpu.make_async_copy(_token_rows(ys_hbm, d_ref[0, 1, r]), _token_rows(buf, t + r), sems.at[s]).start()

    @pl.when(i == 0)
    def _():
        def first(r, carry):
            start(dest_ref, 0, r)
            return carry

        lax.fori_loop(0, t, first, 0)

    buf = ybuf.at[slot]
    pltpu.make_async_copy(ys_hbm.at[pl.ds(0, buf.shape[0])], buf, sems.at[slot]).wait()

    def combine(j, prefetch):
        base = pl.multiple_of(j * group, group)
        if prefetch:
            for r in range(group):
                start(dest_next_ref, 1 - slot, base + r)
        rows = pl.ds(base, group)
        w = w_ref[rows, :]
        o_ref[rows, :] = (u1_ref[rows, :] + w[:, 0:1] * _load_row_tiles(buf, group, base)
                          + w[:, 1:2] * _load_row_tiles(buf, group, t + base))

    def loop(prefetch):
        def body(j, carry):
            combine(j, prefetch)
            return carry

        lax.fori_loop(0, t // group, body, 0)

    @pl.when(i + 1 < n_steps)
    def _():
        loop(True)

    @pl.when(i + 1 >= n_steps)
    def _():
        loop(False)


def _combine(dest, w, u1, ys):
    n = u1.shape[0]
    t = DMA_TILE
    n_steps = n // t
    return pl.pallas_call(
        _combine_body,
        grid=(n_steps,),
        in_specs=[pl.BlockSpec((1, 2, t), lambda i: (i, 0, 0), memory_space=pltpu.SMEM),
                  pl.BlockSpec((1, 2, t), lambda i: (jnp.minimum(i + 1, n_steps - 1), 0, 0),
                               memory_space=pltpu.SMEM),
                  pl.BlockSpec((t, 2), lambda i: (i, 0)),
                  pl.BlockSpec((t, D_MODEL), lambda i: (i, 0)),
                  pl.BlockSpec(memory_space=pl.ANY)],
        out_specs=pl.BlockSpec((t, D_MODEL), lambda i: (i, 0)),
        out_shape=jax.ShapeDtypeStruct((n, D_MODEL), F32),
        scratch_shapes=[pltpu.VMEM((2, 2 * t * RT, LANES), F32), pltpu.SemaphoreType.DMA((2,))],
        compiler_params=_cparams(("arbitrary",)),
        name="combine",
    )(dest, dest, w, u1, ys)


def _rope_tables(pos):
    half = DA_HEAD_DIM // 2
    inv_freq = jnp.power(ROPE_THETA, -jnp.arange(half, dtype=F32) * 2.0 / DA_HEAD_DIM)
    ang = pos[:, None] * inv_freq[None, :]
    cos, sin = jnp.cos(ang), jnp.sin(ang)
    cos_t = jnp.tile(cos, (1, LANES // half))
    sin_t = jnp.tile(jnp.concatenate([-sin, sin], axis=1), (1, LANES // DA_HEAD_DIM))
    return cos_t, sin_t


def _layer(x, meta_tokens, l, g_mix_norm, w_in, g_q_norm, g_k_norm, lambda_q1, lambda_k1, lambda_q2, lambda_k2,
           g_diff_subln, w_gla_gate_up, b_gla_gate, g_gla_norm, w_branch, b_merge_gate, w_out, g_ffn_norm,
           w_router_group, b_router_group, w_router_expert, b_router_expert, w_exp_gate, w_exp_up, w_exp_down):
    bsz, seq, _ = x.shape
    n = bsz * seq

    w_in_bf = w_in[l].astype(BF16)
    p = {
        'gmix': g_mix_norm[l][None, :],
        'gqn': jnp.tile(g_q_norm[l], LANES // DA_HEAD_DIM)[None, :],
        'gkn': jnp.tile(g_k_norm[l], LANES // DA_HEAD_DIM)[None, :],
        'w_in': w_in_bf,
        'wup': jnp.pad(w_gla_gate_up[l].astype(BF16), ((0, LANES - GLA_RANK), (0, 0))),
        'bup': b_gla_gate[l][None, :],
        'wgate': w_in_bf[:, GATE_OFFSET:GATE_OFFSET + 2 * D_MODEL],
        'bgate': b_merge_gate[l].reshape(1, 2 * D_MODEL),
        'wb0': w_branch[l, 0].astype(BF16), 'wb1': w_branch[l, 1].astype(BF16),
        'wout': w_out[l].astype(BF16),
        'gffn': g_ffn_norm[l][None, :],
        'wr': jnp.pad(jnp.concatenate([w_router_group[l], w_router_expert[l].reshape(D_MODEL, N_EXPERTS)],
                                      axis=1).T.astype(BF16), ((0, ROUTER_ROWS - N_GROUPS - N_EXPERTS), (0, 0))),
        'br': jnp.broadcast_to(
            jnp.pad(jnp.concatenate([b_router_group[l], b_router_expert[l].reshape(N_EXPERTS)]),
                    (0, ROUTER_ROWS - N_GROUPS - N_EXPERTS))[:, None], (ROUTER_ROWS, ROW_TILE)),
    }

    u_front = jnp.concatenate([jnp.zeros((FRONT - N_META, D_MODEL), F32), meta_tokens.astype(F32)], axis=0)
    cos_f, sin_f = _rope_tables(jnp.arange(FRONT, dtype=F32) - (FRONT - N_META))
    cos_x, sin_x = _rope_tables(jnp.arange(seq, dtype=F32) + N_META)
    front = _inproj(u_front, FRONT, cos_f, sin_f, p)
    xin = _inproj(x.reshape(n, D_MODEL), ROW_TILE, cos_x, sin_x, p)
    q_t = xin[0]
    k, v, gq, gk, gv, sr, la = [a.reshape(bsz, seq, a.shape[-1]) for a in xin[1:]]
    _, kf, vf, _, gkf, gvf, _, laf = front

    lam_init = 0.8 - 0.6 * math.exp(-0.3 * l)
    lam_vecs = [a[l][None, :] for a in (lambda_q1, lambda_k1, lambda_q2, lambda_k2)]
    score_bound = (ATT_BOUND_MARGIN * DA_HEAD_DIM * Q_SCALE
                   * jnp.max(jnp.abs(g_q_norm[l])) * jnp.max(jnp.abs(g_k_norm[l]))).reshape(1).astype(F32)
    gsub_t = jnp.broadcast_to(g_diff_subln[l][:, None], (DA_V_DIM, ATT_TILE))
    o_a_t = _diff_attn(score_bound, q_t, kf, vf, k, v, lam_vecs, gsub_t, lam_init)
    o_b = _gla(gkf, gvf, laf, gq, gk, gv, la, sr, g_gla_norm[l][None, :])

    u1, h2, info, cnt = _merge(o_a_t, o_b.reshape(n, -1), x.reshape(n, D_MODEL), p)

    ids = info[0:2].astype(jnp.int32)
    wts = info[2:4]
    rank = info[4:6].astype(jnp.int32)
    counts = cnt[N_GROUPS:N_GROUPS + N_EXPERTS, 0].astype(jnp.int32)
    padded = (counts + MOE_TILE - 1) // MOE_TILE * MOE_TILE
    pends = jnp.cumsum(padded)
    pstarts = pends - padded
    expert = jnp.arange(N_EXPERTS, dtype=jnp.int32)
    dest = jnp.sum(jnp.where(ids[..., None] == expert, pstarts, 0), axis=-1) + rank
    n_slots = (2 * n // MOE_TILE + N_EXPERTS) * MOE_TILE
    n_blocks = n_slots // MOE_TILE
    n_used = (pends[-1] // MOE_TILE).astype(jnp.int32)
    blk = jnp.minimum(jnp.arange(n_blocks, dtype=jnp.int32), n_used - 1) * MOE_TILE
    block_e = jnp.minimum(jnp.sum(pends[None, :] <= blk[:, None], axis=1), N_EXPERTS - 1).astype(jnp.int32)
    tail_blocks = jnp.where(counts > 0, pends // MOE_TILE - 1, -1).astype(jnp.int32)
    dest_t = dest.reshape(2, n // DMA_TILE, DMA_TILE).transpose(1, 0, 2)

    xs = _dispatch(tail_blocks, n_used[None], dest_t, h2, n_slots)
    ys = _experts(block_e, n_used[None], xs, w_exp_gate[l], w_exp_up[l], w_exp_down[l])
    out = _combine(dest_t, wts.T, u1, ys)
    return out.reshape(bsz, seq, D_MODEL)


def kernel(x, meta_tokens, g_mix_norm, w_in, g_q_norm, g_k_norm, lambda_q1, lambda_k1, lambda_q2, lambda_k2,
           g_diff_subln, w_gla_gate_up, b_gla_gate, g_gla_norm, w_branch, b_merge_gate, w_out, g_ffn_norm,
           w_router_group, b_router_group, w_router_expert, b_router_expert, w_exp_gate, w_exp_up, w_exp_down):
    depth = w_in.shape[0]
    assert depth == 1, "meta tokens are only carried through a single layer in this implementation"
    assert x.shape[1] % ROW_TILE == 0 and x.shape[2] == D_MODEL
    return _layer(x, meta_tokens, 0, g_mix_norm, w_in, g_q_norm, g_k_norm, lambda_q1, lambda_k1, lambda_q2,
                  lambda_k2, g_diff_subln, w_gla_gate_up, b_gla_gate, g_gla_norm, w_branch, b_merge_gate, w_out,
                  g_ffn_norm, w_router_group, b_router_group, w_router_expert, b_router_expert,
                  w_exp_gate, w_exp_up, w_exp_down)
```

```python
import functools
import math

import jax
import jax.numpy as jnp
from jax import lax
from jax.experimental import pallas as pl
from jax.experimental.pallas import tpu as pltpu

F32 = jnp.float32
BF16 = jnp.bfloat16

D_MODEL = 1024
N_META = 16
EPS = 1e-6
ROPE_THETA = 10000.0

DA_HEADS = 4
DA_HEAD_DIM = 64
DA_V_DIM = 128
Q_SCALE = DA_HEAD_DIM ** -0.5 * math.log2(math.e)
ATT_BOUND_MARGIN = 1.02
ATT_SAFE_BOUND = 60.0
GLA_HEADS = 4
GLA_DK = 64
GLA_DV = 128
GLA_RANK = 16
GLA_TAU = 16.0
GLA_CHUNK = 64
GLA_SAFE_DECAY = 60.0
N_GROUPS = 4
EXPERTS_PER_GROUP = 8
N_EXPERTS = 32
D_EXPERT = 512
ROUTER_ROWS = 48
MIXER_SECTIONS = ((0, 512), (512, 512), (1024, 512), (1536, 256), (1792, 256), (2048, 512), (2560, 512),
                  (3072, 128))
GATE_OFFSET = 3072 + GLA_RANK

LANES = 128
FRONT = 256
ATT_TILE = 256
ATT_GROUP = 4
ROW_TILE = 512
GLA_GROUP = 512
GLA_BATCH = 4
MOE_TILE = 512
EXPERT_CHUNK = 256
X_SLOTS = 4
DMA_TILE = 512
NEG_BIG = -1e30
VMEM_LIMIT = 56 * 1024 * 1024


def _cparams(sem):
    return pltpu.CompilerParams(dimension_semantics=sem, vmem_limit_bytes=VMEM_LIMIT)


def _const_spec(shape):
    nd = len(shape)
    return pl.BlockSpec(shape, lambda *_: (0,) * nd)


RT = D_MODEL // LANES


def _row_tile_spec(rows, index_map):
    return pl.BlockSpec((rows * RT, LANES), index_map)


def _token_rows(ref, tok):
    return ref.at[pl.ds(pl.multiple_of(tok * RT, RT), RT)]


def _load_row_tiles(ref, rows, first=0):
    return jnp.concatenate([ref[pl.ds(first * RT + c, rows, stride=RT), :] for c in range(RT)], axis=1)


def _store_row_tiles(ref, val):
    for c in range(RT):
        ref[pl.ds(c, val.shape[0], stride=RT), :] = val[:, c * LANES:(c + 1) * LANES]


def _sigmoid(x):
    return 0.5 * jnp.tanh(0.5 * x) + 0.5


def _log_sigmoid(x):
    return jnp.minimum(x, 0.0) - jnp.log1p(jnp.exp(-jnp.abs(x)))


def _inproj_body(u_ref, gmix_ref, cos_ref, sin_ref, gqn_ref, gkn_ref,
                 wq_ref, wk_ref, wv_ref, wgq_ref, wgk_ref, wgv_ref, wgr_ref, wgg_ref,
                 wup_ref, bup_ref,
                 q_out, k_out, v_out, gq_out, gk_out, gv_out, sr_out, la_out):
    x = u_ref[...]
    ms = jnp.mean(x * x, axis=-1, keepdims=True)
    h = (x * lax.rsqrt(ms + EPS) * gmix_ref[...]).astype(BF16)

    cos = cos_ref[...]
    sin = sin_ref[...]
    lane = lax.broadcasted_iota(jnp.int32, (1, LANES), 1)
    first_half = (lane % DA_HEAD_DIM) < (DA_HEAD_DIM // 2)
    gi = lax.broadcasted_iota(jnp.int32, (LANES, LANES), 0) // DA_HEAD_DIM
    gj = lax.broadcasted_iota(jnp.int32, (LANES, LANES), 1) // DA_HEAD_DIM
    group_sum = (gi == gj).astype(BF16)

    def norm_rope(w_ref, gain_ref, out_ref, scale, transposed):
        z = jnp.dot(h, w_ref[...], preferred_element_type=F32)
        for hh in range(DA_HEADS):
            hs = slice(hh * LANES, (hh + 1) * LANES)
            zh = z[:, hs]
            ssq = jnp.dot((zh * zh).astype(BF16), group_sum, preferred_element_type=F32)
            zn = zh * lax.rsqrt(ssq * (1.0 / DA_HEAD_DIM) + EPS) * gain_ref[...]
            rot = jnp.where(first_half,
                            pltpu.roll(zn, LANES - DA_HEAD_DIM // 2, 1),
                            pltpu.roll(zn, DA_HEAD_DIM // 2, 1))
            zr = (zn * cos + rot * sin) * scale
            if transposed:
                out_ref[hs, :] = jnp.transpose(zr).astype(out_ref.dtype)
            else:
                out_ref[:, hs] = zr.astype(out_ref.dtype)

    norm_rope(wq_ref, gqn_ref, q_out, Q_SCALE, transposed=True)
    norm_rope(wk_ref, gkn_ref, k_out, 1.0, transposed=False)
    v_out[...] = jnp.dot(h, wv_ref[...], preferred_element_type=F32).astype(v_out.dtype)

    gq_out[...] = (jnp.dot(h, wgq_ref[...], preferred_element_type=F32) * (GLA_DK ** -0.5)).astype(gq_out.dtype)
    gk_out[...] = jnp.dot(h, wgk_ref[...], preferred_element_type=F32).astype(gk_out.dtype)
    gv_out[...] = jnp.dot(h, wgv_ref[...], preferred_element_type=F32).astype(gv_out.dtype)
    r = jnp.dot(h, wgr_ref[...], preferred_element_type=F32)
    sr_out[...] = (r * _sigmoid(r)).astype(sr_out.dtype)

    g_lr = jnp.dot(h, wgg_ref[...], preferred_element_type=F32)
    pre = jnp.dot(g_lr.astype(BF16), wup_ref[...], preferred_element_type=F32) + bup_ref[...]
    la_out[...] = _log_sigmoid(pre) * (1.0 / GLA_TAU)


def _inproj(u, tm, cos, sin, p):
    rows = u.shape[0]
    n_tab = cos.shape[0] // tm
    row = lambda w: pl.BlockSpec((tm, w), lambda i: (i, 0))
    tab = pl.BlockSpec((tm, LANES), lambda i: (i % n_tab, 0))
    sections = [pl.BlockSpec((D_MODEL, w), lambda i, j=off // w: (0, j)) for off, w in MIXER_SECTIONS]
    assert all(off % w == 0 for off, w in MIXER_SECTIONS)
    out_widths = [(512, BF16), (512, BF16), (512, BF16), (256, BF16), (256, BF16), (512, BF16),
                  (512, BF16), (256, F32)]
    return pl.pallas_call(
        _inproj_body,
        grid=(rows // tm,),
        in_specs=[row(D_MODEL), _const_spec((1, D_MODEL)), tab, tab,
                  _const_spec((1, LANES)), _const_spec((1, LANES))]
                 + sections + [_const_spec(p['wup'].shape), _const_spec(p['bup'].shape)],
        out_specs=[pl.BlockSpec((out_widths[0][0], tm), lambda i: (0, i))] + [row(w) for w, _ in out_widths[1:]],
        out_shape=[jax.ShapeDtypeStruct((out_widths[0][0], rows), BF16)]
                  + [jax.ShapeDtypeStruct((rows, w), dt) for w, dt in out_widths[1:]],
        compiler_params=_cparams(("parallel",)),
        name="inproj",
    )(u, p['gmix'], cos, sin, p['gqn'], p['gkn'], *([p['w_in']] * len(MIXER_SECTIONS)), p['wup'], p['bup'])


def _diff_attn_body(bound_ref, lq1_ref, lk1_ref, lq2_ref, lk2_ref, gsub_ref,
                    q_ref, kf_ref, vf_ref, kx_ref, vx_ref, o_ref, vt_sc, qs_sc, s_sc, cmax_sc, m_sc, l_sc, acc_sc,
                    *, lam_init):
    qi = pl.program_id(1)
    tq = ATT_TILE
    n_kv = kx_ref.shape[0] // tq
    heads = range(DA_HEADS)
    hs = lambda h: slice(h * LANES, (h + 1) * LANES)

    @pl.when(qi == 0)
    def _():
        def tr(j, carry):
            off = pl.multiple_of(j * tq, tq)
            for h in heads:
                vt_sc[h, j] = jnp.transpose(vx_ref[pl.ds(off, tq), hs(h)].astype(F32)).astype(BF16)
            return carry

        lax.fori_loop(0, n_kv, tr, 0)

    d = lax.broadcasted_iota(jnp.int32, (LANES, tq), 0)
    for h in heads:
        qt = q_ref[hs(h), :]
        zero = jnp.zeros_like(qt)
        qs_sc[h] = jnp.concatenate([jnp.where(d < DA_HEAD_DIM, qt, zero),
                                    jnp.where(d >= DA_HEAD_DIM, qt, zero)], axis=1)

    meta = slice(FRONT - N_META, FRONT)
    key = lax.broadcasted_iota(jnp.int32, (tq, 2 * tq), 0)
    qry = lax.broadcasted_iota(jnp.int32, (tq, 2 * tq), 1) % tq
    causal = key <= qry
    bound = bound_ref[0]

    def meta_scores(h):
        s = jnp.dot(kf_ref[meta, hs(h)], qs_sc[h], preferred_element_type=F32)
        vt = jnp.transpose(vf_ref[meta, hs(h)].astype(F32)).astype(BF16)
        return s, vt

    def block_scores(j, h, diagonal):
        off = pl.multiple_of(j * tq, tq)
        s = jnp.dot(kx_ref[pl.ds(off, tq), hs(h)], qs_sc[h], preferred_element_type=F32)
        return jnp.where(causal, s, NEG_BIG) if diagonal else s

    @pl.when(bound <= ATT_SAFE_BOUND)
    def _():
        l_sc[...] = jnp.zeros(l_sc.shape, F32)
        acc_sc[...] = jnp.zeros(acc_sc.shape, F32)

        def blocks(*js, last=False):
            work = [(h, block_scores(j, h, last and j is js[-1]), (h, j)) for j in js for h in heads]
            if last:
                work += [(h,) + meta_scores(h) for h in heads]
            ps = [(h, jnp.exp2(s - bound), vt) for h, s, vt in work]
            for h, pr, _ in ps:
                l_sc[h] += jnp.sum(pr, axis=0, keepdims=True)
            for h, pr, vt in ps:
                vt = vt_sc[vt] if isinstance(vt, tuple) else vt
                acc_sc[h] += jnp.dot(vt, pr.astype(BF16), preferred_element_type=F32)

        def full_blocks(u, carry):
            blocks(*[ATT_GROUP * u + d for d in range(ATT_GROUP)])
            return carry

        lax.fori_loop(0, qi // ATT_GROUP, full_blocks, 0)
        for rem in range(ATT_GROUP):
            pl.when(qi % ATT_GROUP == rem)(
                functools.partial(blocks, *[qi - d for d in range(rem, -1, -1)], last=True))

    @pl.when(bound > ATT_SAFE_BOUND)
    def _():
        for h in heads:
            s, vt = meta_scores(h)
            m0 = jnp.max(s, axis=0, keepdims=True)
            pr = jnp.exp2(s - m0)
            m_sc[h] = m0
            l_sc[h] = jnp.sum(pr, axis=0, keepdims=True)
            acc_sc[h] = jnp.dot(vt, pr.astype(BF16), preferred_element_type=F32)

        def scores(j, slot, diagonal=False):
            for h in heads:
                s = block_scores(j, h, diagonal)
                s_sc[h, slot] = s
                cmax_sc[h, slot] = jnp.max(s, axis=0, keepdims=True)

        def accumulate(j, slot):
            for h in heads:
                m_old = m_sc[h]
                m_new = jnp.maximum(m_old, cmax_sc[h, slot])
                alpha = jnp.exp2(m_old - m_new)
                pr = jnp.exp2(s_sc[h, slot] - m_new)
                l_sc[h] = alpha * l_sc[h] + jnp.sum(pr, axis=0, keepdims=True)
                acc_sc[h] = alpha * acc_sc[h] + jnp.dot(vt_sc[h, j], pr.astype(BF16),
                                                        preferred_element_type=F32)
                m_sc[h] = m_new

        @pl.when(qi == 0)
        def _():
            scores(0, 0, diagonal=True)
            accumulate(0, 0)

        @pl.when(qi > 0)
        def _():
            scores(0, 0)

            def pair(u, carry):
                j = 2 * u
                scores(j + 1, 1)
                accumulate(j, 0)
                scores(j + 2, 0)
                accumulate(j + 1, 1)
                return carry

            lax.fori_loop(0, (qi - 1) // 2, pair, 0)

            @pl.when(qi % 2 == 1)
            def _():
                scores(qi, 1, diagonal=True)
                accumulate(qi - 1, 0)
                accumulate(qi, 1)

            @pl.when(qi % 2 == 0)
            def _():
                scores(qi - 1, 1)
                accumulate(qi - 2, 0)
                scores(qi, 0, diagonal=True)
                accumulate(qi - 1, 1)
                accumulate(qi, 0)

    lam = (jnp.exp(jnp.sum(lq1_ref[...] * lk1_ref[...], axis=-1, keepdims=True))
           - jnp.exp(jnp.sum(lq2_ref[...] * lk2_ref[...], axis=-1, keepdims=True)) + lam_init)
    for h in heads:
        acc = acc_sc[h]
        inv_l = 1.0 / l_sc[h]
        ot = acc[:, :tq] * inv_l[:, :tq] - lam * (acc[:, tq:] * inv_l[:, tq:])
        ms = jnp.mean(ot * ot, axis=0, keepdims=True)
        o_ref[hs(h), :] = (ot * lax.rsqrt(ms + EPS) * gsub_ref[...] * (1.0 - lam_init)).astype(o_ref.dtype)


def _diff_attn(bound, q_t, kf, vf, kx, vx, lam_vecs, gsub_t, lam_init):
    bsz, seq, _ = kx.shape
    tq = ATT_TILE
    vec = _const_spec((1, DA_HEAD_DIM))
    width = DA_HEADS * LANES
    tok_tile = pl.BlockSpec((width, tq), lambda b, i: (0, b * (seq // tq) + i))
    return pl.pallas_call(
        functools.partial(_diff_attn_body, lam_init=lam_init),
        grid=(bsz, seq // tq),
        in_specs=[pl.BlockSpec(memory_space=pltpu.SMEM), vec, vec, vec, vec, _const_spec((DA_V_DIM, tq)),
                  tok_tile,
                  _const_spec((FRONT, width)), _const_spec((FRONT, width)),
                  pl.BlockSpec((None, seq, width), lambda b, i: (b, 0, 0)),
                  pl.BlockSpec((None, seq, width), lambda b, i: (b, 0, 0))],
        out_specs=tok_tile,
        out_shape=jax.ShapeDtypeStruct((width, bsz * seq), BF16),
        scratch_shapes=[pltpu.VMEM((DA_HEADS, seq // tq, DA_V_DIM, tq), BF16),
                        pltpu.VMEM((DA_HEADS, LANES, 2 * tq), BF16),
                        pltpu.VMEM((DA_HEADS, 2, tq, 2 * tq), F32), pltpu.VMEM((DA_HEADS, 2, 1, 2 * tq), F32),
                        pltpu.VMEM((DA_HEADS, 1, 2 * tq), F32), pltpu.VMEM((DA_HEADS, 1, 2 * tq), F32),
                        pltpu.VMEM((DA_HEADS, DA_V_DIM, 2 * tq), F32)],
        compiler_params=_cparams(("parallel", "arbitrary")),
        name="diff_attn",
    )(bound, *lam_vecs, gsub_t, q_t, kf, vf, kx, vx)


def _split3(a):
    a1 = a.astype(BF16)
    r1 = a - a1.astype(F32)
    a2 = r1.astype(BF16)
    a3 = (r1 - a2.astype(F32)).astype(BF16)
    return a1, a2, a3


def _gla_body(gn_ref, kf_ref, vf_ref, laf_ref, q_ref, k_ref, v_ref, la_ref, sr_ref, o_ref, st_sc, b_sc):
    g = pl.program_id(1)
    c = GLA_CHUNK
    kw = GLA_HEADS * GLA_DK
    vw = GLA_HEADS * GLA_DV

    ti = lax.broadcasted_iota(jnp.int32, (c, c), 0)
    si = lax.broadcasted_iota(jnp.int32, (c, c), 1)
    tri = (si <= ti).astype(BF16)
    hv = lax.broadcasted_iota(jnp.int32, (vw, kw), 0) // GLA_DV
    hk = lax.broadcasted_iota(jnp.int32, (vw, kw), 1) // GLA_DK
    head_mask = hv == hk
    lane_head = lax.broadcasted_iota(jnp.int32, (1, kw), 1) // GLA_DK
    causal = lax.broadcasted_iota(jnp.int32, (GLA_HEADS * c, c), 0) % c >= \
        lax.broadcasted_iota(jnp.int32, (GLA_HEADS * c, c), 1)

    def cumsum(a):
        a1, a2, a3 = _split3(a)
        return (jnp.dot(tri, a1, preferred_element_type=F32)
                + jnp.dot(tri, a2, preferred_element_type=F32)
                + jnp.dot(tri, a3, preferred_element_type=F32))

    batch = range(GLA_BATCH)

    def next_state(st, k, v, b):
        b_last = b[c - 1:c, :]
        kd = (k * jnp.exp(b_last - b)).astype(BF16)
        upd = lax.dot_general(v, kd, (((0,), (0,)), ((), ())), preferred_element_type=F32)
        return jnp.where(head_mask, jnp.exp(b_last) * st + upd, 0.0)

    @pl.when(g == 0)
    def _():
        st0 = next_state(jnp.zeros((vw, kw), F32), kf_ref[...].astype(F32), vf_ref[...], cumsum(laf_ref[...]))
        for bb in batch:
            st_sc[bb] = st0

    gn = gn_ref[...]

    def finish(o, sr):
        outs = []
        for hh in range(GLA_HEADS):
            cs = slice(hh * GLA_DV, (hh + 1) * GLA_DV)
            oh = o[:, cs]
            ms = jnp.mean(oh * oh, axis=-1, keepdims=True)
            outs.append(oh * lax.rsqrt(ms + EPS) * gn * sr[:, cs])
        return jnp.concatenate(outs, axis=1)

    n_chunks = GLA_GROUP // c
    for ci in range(n_chunks):
        rs = slice(ci * c, (ci + 1) * c)
        for bb in batch:
            b_sc[bb, rs, :] = cumsum(la_ref[bb, rs, :])
    steep = jnp.min(b_sc[...]) < -GLA_SAFE_DECAY

    @pl.when(jnp.logical_not(steep))
    def _():
        nt = (((1,), (1,)), ((), ()))
        sts = [st_sc[bb] for bb in batch]
        for ci in range(n_chunks):
            rs = slice(ci * c, (ci + 1) * c)
            bs = [b_sc[bb, rs, :] for bb in batch]
            ks = [k_ref[bb, rs, :].astype(F32) for bb in batch]
            vs = [v_ref[bb, rs, :] for bb in batch]
            qes = [q_ref[bb, rs, :].astype(F32) * jnp.exp(bs[bb]) for bb in batch]
            nxt = [next_state(sts[bb], ks[bb], vs[bb], bs[bb]) for bb in batch]
            inters = [lax.dot_general(qes[bb].astype(BF16), sts[bb].astype(BF16), nt,
                                      preferred_element_type=F32) for bb in batch]
            scs = []
            for bb in batch:
                ke = (ks[bb] * jnp.exp(-bs[bb])).astype(BF16)
                qstack = jnp.concatenate([jnp.where(lane_head == hh, qes[bb], 0.0) for hh in range(GLA_HEADS)],
                                         axis=0).astype(BF16)
                sc = lax.dot_general(qstack, ke, nt, preferred_element_type=F32)
                scs.append(jnp.where(causal, sc, 0.0).astype(BF16))
            pvs = [jnp.dot(scs[bb], vs[bb], preferred_element_type=F32) for bb in batch]
            for bb in batch:
                o = jnp.concatenate([inters[bb][:, hh * GLA_DV:(hh + 1) * GLA_DV]
                                     + pvs[bb][hh * c:(hh + 1) * c, hh * GLA_DV:(hh + 1) * GLA_DV]
                                     for hh in range(GLA_HEADS)], axis=1)
                o_ref[bb, rs, :] = finish(o, sr_ref[bb, rs, :].astype(F32)).astype(o_ref.dtype)
            sts = nxt
        for bb in batch:
            st_sc[bb] = sts[bb]

    @pl.when(steep)
    def _():
        rows = 16

        def tile(i, carry):
            off = pl.multiple_of(i * rows, rows)
            for bb in batch:
                a = la_ref[bb, pl.ds(off, rows), :]
                q = q_ref[bb, pl.ds(off, rows), :].astype(F32)
                k = k_ref[bb, pl.ds(off, rows), :].astype(F32)
                v = v_ref[bb, pl.ds(off, rows), :].astype(F32)
                outs = []
                for r in range(rows):
                    row = lambda x: x[r:r + 1].astype(BF16)
                    upd = lax.dot_general(row(v), row(k), (((0,), (0,)), ((), ())),
                                          preferred_element_type=F32)
                    st = jnp.where(head_mask, jnp.exp(a[r:r + 1]) * st_sc[bb] + upd, 0.0)
                    st_sc[bb] = st
                    outs.append(lax.dot_general(row(q), st.astype(BF16), (((1,), (1,)), ((), ())),
                                                preferred_element_type=F32))
                o = jnp.concatenate(outs, axis=0)
                o_ref[bb, pl.ds(off, rows), :] = finish(
                    o, sr_ref[bb, pl.ds(off, rows), :].astype(F32)).astype(o_ref.dtype)
            return carry

        lax.fori_loop(0, GLA_GROUP // rows, tile, 0)


def _gla(kf, vf, laf, q, k, v, la, sr, gn):
    bsz, seq, _ = q.shape
    t = GLA_GROUP
    kw = GLA_HEADS * GLA_DK
    vw = GLA_HEADS * GLA_DV
    last = FRONT // GLA_CHUNK - 1
    nb = GLA_BATCH
    assert bsz % nb == 0
    fr = lambda w: pl.BlockSpec((GLA_CHUNK, w), lambda b, g: (last, 0))
    xs = lambda w: pl.BlockSpec((nb, t, w), lambda b, g: (b, g, 0))
    return pl.pallas_call(
        _gla_body,
        grid=(bsz // nb, seq // t),
        in_specs=[_const_spec((1, GLA_DV)), fr(kw), fr(vw), fr(kw), xs(kw), xs(kw), xs(vw), xs(kw), xs(vw)],
        out_specs=xs(vw),
        out_shape=jax.ShapeDtypeStruct((bsz, seq, vw), BF16),
        scratch_shapes=[pltpu.VMEM((nb, vw, kw), F32), pltpu.VMEM((nb, t, kw), F32)],
        compiler_params=_cparams(("parallel", "arbitrary")),
        name="gla",
    )(gn, kf, vf, laf, q, k, v, la, sr)


def _merge_body(oa_ref, ob_ref, u_ref, gmix_ref, wgate_ref, bgate_ref, wb0_ref, wb1_ref, wout_ref, gffn_ref,
                wr_ref, br_ref,
                u1_out, h2_out, info_out, cnt_out, cnt_sc):
    i = pl.program_id(0)
    tm = ROW_TILE

    @pl.when(i == 0)
    def _():
        cnt_sc[...] = jnp.zeros(cnt_sc.shape, F32)

    ya = lax.dot_general(oa_ref[...], wb0_ref[...], (((0,), (0,)), ((), ())), preferred_element_type=F32)
    yb = jnp.dot(ob_ref[...], wb1_ref[...], preferred_element_type=F32)
    x = u_ref[...]
    h = (x * lax.rsqrt(jnp.mean(x * x, axis=-1, keepdims=True) + EPS) * gmix_ref[...]).astype(BF16)
    gate = _sigmoid(jnp.dot(h, wgate_ref[...], preferred_element_type=F32) + bgate_ref[...])
    merged = gate[:, :D_MODEL] * ya + gate[:, D_MODEL:] * yb
    u1 = x + jnp.dot(merged.astype(BF16), wout_ref[...], preferred_element_type=F32)
    u1_out[...] = u1
    ms = jnp.mean(u1 * u1, axis=-1, keepdims=True)
    h2f = u1 * lax.rsqrt(ms + EPS) * gffn_ref[...]
    _store_row_tiles(h2_out, h2f)
    h2 = h2f.astype(BF16)

    logits = lax.dot_general(wr_ref[...], h2, (((1,), (1,)), ((), ())), preferred_element_type=F32) + br_ref[...]
    row = lax.broadcasted_iota(jnp.int32, (ROUTER_ROWS, tm), 0)
    is_group = row < N_GROUPS
    gl = jnp.where(is_group, logits, NEG_BIG)
    gmax = jnp.max(gl, axis=0, keepdims=True)
    g_idx = jnp.min(jnp.where(gl == gmax, row, ROUTER_ROWS), axis=0, keepdims=True)
    g_w = 1.0 / jnp.sum(jnp.where(is_group, jnp.exp(gl - gmax), 0.0), axis=0, keepdims=True)
    lo = N_GROUPS + EXPERTS_PER_GROUP * g_idx
    el = jnp.where((row >= lo) & (row < lo + EXPERTS_PER_GROUP), logits, NEG_BIG)
    v1 = jnp.max(el, axis=0, keepdims=True)
    i1 = jnp.min(jnp.where(el == v1, row, ROUTER_ROWS), axis=0, keepdims=True)
    el2 = jnp.where(row == i1, NEG_BIG, el)
    v2 = jnp.max(el2, axis=0, keepdims=True)
    i2 = jnp.min(jnp.where(el2 == v2, row, ROUTER_ROWS), axis=0, keepdims=True)
    e21 = jnp.exp(v2 - v1)
    w1 = g_w / (1.0 + e21)
    w2 = w1 * e21

    onehot = ((row == i1) | (row == i2)).astype(BF16)
    si = lax.broadcasted_iota(jnp.int32, (tm, tm), 0)
    ti = lax.broadcasted_iota(jnp.int32, (tm, tm), 1)
    earlier = (si < ti).astype(BF16)
    cnt = cnt_sc[...]
    before = jnp.dot(onehot, earlier, preferred_element_type=F32) + jnp.tile(cnt, (1, tm // LANES))
    r1 = jnp.sum(jnp.where(row == i1, before, 0.0), axis=0, keepdims=True)
    r2 = jnp.sum(jnp.where(row == i2, before, 0.0), axis=0, keepdims=True)
    cnt = cnt + jnp.dot(onehot, jnp.ones((tm, LANES), BF16), preferred_element_type=F32)
    cnt_sc[...] = cnt
    cnt_out[...] = cnt

    zero = jnp.zeros_like(w1)
    info_out[...] = jnp.concatenate([(i1 - N_GROUPS).astype(F32), (i2 - N_GROUPS).astype(F32),
                                     w1, w2, r1, r2, zero, zero], axis=0)


def _merge(oa, ob, u, p):
    n = u.shape[0]
    tm = ROW_TILE
    row = lambda w: pl.BlockSpec((tm, w), lambda i: (i, 0))
    return pl.pallas_call(
        _merge_body,
        grid=(n // tm,),
        in_specs=[pl.BlockSpec((512, tm), lambda i: (0, i)), row(512), row(D_MODEL),
                  _const_spec((1, D_MODEL)), _const_spec((D_MODEL, 2 * D_MODEL)), _const_spec((1, 2 * D_MODEL)),
                  _const_spec((512, D_MODEL)), _const_spec((512, D_MODEL)), _const_spec((D_MODEL, D_MODEL)),
                  _const_spec((1, D_MODEL)), _const_spec((ROUTER_ROWS, D_MODEL)), _const_spec((ROUTER_ROWS, tm))],
        out_specs=[row(D_MODEL), _row_tile_spec(tm, lambda i: (i, 0)), pl.BlockSpec((8, tm), lambda i: (0, i)),
                   _const_spec((ROUTER_ROWS, LANES))],
        out_shape=[jax.ShapeDtypeStruct((n, D_MODEL), F32), jax.ShapeDtypeStruct((n * RT, LANES), F32),
                   jax.ShapeDtypeStruct((8, n), F32), jax.ShapeDtypeStruct((ROUTER_ROWS, LANES), F32)],
        scratch_shapes=[pltpu.VMEM((ROUTER_ROWS, LANES), F32)],
        compiler_params=_cparams(("arbitrary",)),
        name="merge_router",
    )(oa, ob, u, p['gmix'], p['wgate'], p['bgate'], p['wb0'], p['wb1'], p['wout'], p['gffn'], p['wr'], p['br'])


def _dispatch_body(tail_ref, nb_ref, dest_ref, h2_ref, xs_hbm, zero_sc, sem, zsem):
    i = pl.program_id(0)
    blk_rows = MOE_TILE * RT
    n_blocks = xs_hbm.shape[0] // blk_rows

    def zero_copy(blk):
        dst = xs_hbm.at[pl.ds(pl.multiple_of(blk * blk_rows, blk_rows), blk_rows)]
        return pltpu.make_async_copy(zero_sc, dst, zsem)

    @pl.when(i == 0)
    def _():
        zero_sc[...] = jnp.zeros(zero_sc.shape, F32)

        def tails(fn):
            def body(e, carry):
                @pl.when(tail_ref[e] >= 0)
                def _():
                    fn(zero_copy(tail_ref[e]))
                return carry
            lax.fori_loop(0, N_EXPERTS, body, 0)

        def unused(fn):
            def body(b, carry):
                fn(zero_copy(b))
                return carry
            lax.fori_loop(nb_ref[0], n_blocks, body, 0)

        tails(lambda cp: cp.start())
        unused(lambda cp: cp.start())
        tails(lambda cp: cp.wait())
        unused(lambda cp: cp.wait())

    def start(r, carry):
        src = _token_rows(h2_ref, r)
        pltpu.make_async_copy(src, _token_rows(xs_hbm, dest_ref[0, 0, r]), sem).start()
        pltpu.make_async_copy(src, _token_rows(xs_hbm, dest_ref[0, 1, r]), sem).start()
        return carry

    lax.fori_loop(0, DMA_TILE, start, 0)
    for _ in range(2):
        pltpu.make_async_copy(h2_ref, xs_hbm.at[pl.ds(0, DMA_TILE * RT)], sem).wait()


def _dispatch(tail_blocks, n_used, dest, h2, n_slots):
    n = h2.shape[0] // RT
    grid_spec = pltpu.PrefetchScalarGridSpec(
        num_scalar_prefetch=2,
        grid=(n // DMA_TILE,),
        in_specs=[pl.BlockSpec((1, 2, DMA_TILE), lambda i, tb, nb: (i, 0, 0), memory_space=pltpu.SMEM),
                  _row_tile_spec(DMA_TILE, lambda i, tb, nb: (i, 0))],
        out_specs=pl.BlockSpec(memory_space=pl.ANY),
        scratch_shapes=[pltpu.VMEM((MOE_TILE * RT, LANES), F32), pltpu.SemaphoreType.DMA(()),
                        pltpu.SemaphoreType.DMA(())],
    )
    return pl.pallas_call(
        _dispatch_body,
        grid_spec=grid_spec,
        out_shape=jax.ShapeDtypeStruct((n_slots * RT, LANES), F32),
        compiler_params=_cparams(("arbitrary",)),
        name="dispatch",
    )(tail_blocks, n_used, dest, h2)


def _experts_body(be_ref, nb_ref, xs_hbm, wg_ref, wu_ref, wd_ref, y_ref, wg_sc, wu_sc, wd_sc, xbuf, xsem):
    i = pl.program_id(0)
    nb = nb_ref[0]
    blk_rows = MOE_TILE * RT
    ahead = X_SLOTS - 1

    def fetch(b):
        src = xs_hbm.at[pl.ds(pl.multiple_of(b * blk_rows, blk_rows), blk_rows)]
        return pltpu.make_async_copy(src, xbuf.at[b % X_SLOTS], xsem.at[b % X_SLOTS])

    @pl.when(i == 0)
    def _():
        for b in range(ahead):
            pl.when(b < nb)(lambda b=b: fetch(b).start())

    @pl.when(i + ahead < nb)
    def _():
        fetch(i + ahead).start()

    prev = be_ref[jnp.maximum(i - 1, 0)]
    fresh = (i == 0) | (be_ref[i] != prev)

    @pl.when(fresh)
    def _():
        wg_sc[...] = wg_ref[...].astype(BF16)
        wu_sc[...] = wu_ref[...].astype(BF16)
        wd_sc[...] = wd_ref[...].astype(BF16)

    @pl.when(i < nb)
    def _():
        fetch(i).wait()
        x = _load_row_tiles(xbuf.at[i % X_SLOTS], MOE_TILE).astype(BF16)
        y = jnp.zeros((MOE_TILE, D_MODEL), F32)
        for j in range(D_EXPERT // EXPERT_CHUNK):
            cs = slice(j * EXPERT_CHUNK, (j + 1) * EXPERT_CHUNK)
            gp = jnp.dot(x, wg_sc[:, cs], preferred_element_type=F32)
            up = jnp.dot(x, wu_sc[:, cs], preferred_element_type=F32)
            hid = (gp * _sigmoid(gp) * up).astype(BF16)
            y = y + jnp.dot(hid, wd_sc[cs, :], preferred_element_type=F32)
        _store_row_tiles(y_ref, y)

    @pl.when(i >= nb)
    def _():
        y_ref[...] = jnp.zeros(y_ref.shape, F32)


def _experts(block_e, n_used, xs, wg, wu, wd):
    n_slots = xs.shape[0] // RT
    n_blocks = n_slots // MOE_TILE
    wmap = lambda i, be, nb: (be[i], 0, 0)
    grid_spec = pltpu.PrefetchScalarGridSpec(
        num_scalar_prefetch=2,
        grid=(n_blocks,),
        in_specs=[pl.BlockSpec(memory_space=pl.ANY),
                  pl.BlockSpec((None, D_MODEL, D_EXPERT), wmap),
                  pl.BlockSpec((None, D_MODEL, D_EXPERT), wmap),
                  pl.BlockSpec((None, D_EXPERT, D_MODEL), wmap)],
        out_specs=_row_tile_spec(MOE_TILE, lambda i, be, nb: (i, 0)),
        scratch_shapes=[pltpu.VMEM((D_MODEL, D_EXPERT), BF16), pltpu.VMEM((D_MODEL, D_EXPERT), BF16),
                        pltpu.VMEM((D_EXPERT, D_MODEL), BF16),
                        pltpu.VMEM((X_SLOTS, MOE_TILE * RT, LANES), F32), pltpu.SemaphoreType.DMA((X_SLOTS,))],
    )
    return pl.pallas_call(
        _experts_body,
        grid_spec=grid_spec,
        out_shape=jax.ShapeDtypeStruct((n_slots * RT, LANES), F32),
        compiler_params=_cparams(("arbitrary",)),
        name="experts",
    )(block_e, n_used, xs, wg, wu, wd)


def _combine_body(dest_ref, dest_next_ref, w_ref, u1_ref, ys_hbm, o_ref, ybuf, sems):
    i = pl.program_id(0)
    n_steps = pl.num_programs(0)
    t = DMA_TILE
    group = 32
    slot = i % 2

    def start(d_ref, s, r):
        buf = ybuf.at[s]
        pltpu.make_async_copy(_token_rows(ys_hbm, d_ref[0, 0, r]), _token_rows(buf, r), sems.at[s]).start()
        pltpu.make_async_copy(_token_rows(ys_hbm, d_ref[0, 1, r]), _token_rows(buf, t + r), sems.at[s]).start()

    @pl.when(i == 0)
    def _():
        def first(r, carry):
            start(dest_ref, 0, r)
            return carry

        lax.fori_loop(0, t, first, 0)

    buf = ybuf.at[slot]
    pltpu.make_async_copy(ys_hbm.at[pl.ds(0, buf.shape[0])], buf, sems.at[slot]).wait()

    def combine(j, prefetch):
        base = pl.multiple_of(j * group, group)
        if prefetch:
            for r in range(group):
                start(dest_next_ref, 1 - slot, base + r)
        rows = pl.ds(base, group)
        w = w_ref[rows, :]
        o_ref[rows, :] = (u1_ref[rows, :] + w[:, 0:1] * _load_row_tiles(buf, group, base)
                          + w[:, 1:2] * _load_row_tiles(buf, group, t + base))

    def loop(prefetch):
        def body(j, carry):
            combine(j, prefetch)
            return carry

        lax.fori_loop(0, t // group, body, 0)

    @pl.when(i + 1 < n_steps)
    def _():
        loop(True)

    @pl.when(i + 1 >= n_steps)
    def _():
        loop(False)


def _combine(dest, w, u1, ys):
    n = u1.shape[0]
    t = DMA_TILE
    n_steps = n // t
    return pl.pallas_call(
        _combine_body,
        grid=(n_steps,),
        in_specs=[pl.BlockSpec((1, 2, t), lambda i: (i, 0, 0), memory_space=pltpu.SMEM),
                  pl.BlockSpec((1, 2, t), lambda i: (jnp.minimum(i + 1, n_steps - 1), 0, 0),
                               memory_space=pltpu.SMEM),
                  pl.BlockSpec((t, 2), lambda i: (i, 0)),
                  pl.BlockSpec((t, D_MODEL), lambda i: (i, 0)),
                  pl.BlockSpec(memory_space=pl.ANY)],
        out_specs=pl.BlockSpec((t, D_MODEL), lambda i: (i, 0)),
        out_shape=jax.ShapeDtypeStruct((n, D_MODEL), F32),
        scratch_shapes=[pltpu.VMEM((2, 2 * t * RT, LANES), F32), pltpu.SemaphoreType.DMA((2,))],
        compiler_params=_cparams(("arbitrary",)),
        name="combine",
    )(dest, dest, w, u1, ys)


def _rope_tables(pos):
    half = DA_HEAD_DIM // 2
    inv_freq = jnp.power(ROPE_THETA, -jnp.arange(half, dtype=F32) * 2.0 / DA_HEAD_DIM)
    ang = pos[:, None] * inv_freq[None, :]
    cos, sin = jnp.cos(ang), jnp.sin(ang)
    cos_t = jnp.tile(cos, (1, LANES // half))
    sin_t = jnp.tile(jnp.concatenate([-sin, sin], axis=1), (1, LANES // DA_HEAD_DIM))
    return cos_t, sin_t


def _layer(x, meta_tokens, l, g_mix_norm, w_in, g_q_norm, g_k_norm, lambda_q1, lambda_k1, lambda_q2, lambda_k2,
           g_diff_subln, w_gla_gate_up, b_gla_gate, g_gla_norm, w_branch, b_merge_gate, w_out, g_ffn_norm,
           w_router_group, b_router_group, w_router_expert, b_router_expert, w_exp_gate, w_exp_up, w_exp_down):
    bsz, seq, _ = x.shape
    n = bsz * seq

    w_in_bf = w_in[l].astype(BF16)
    p = {
        'gmix': g_mix_norm[l][None, :],
        'gqn': jnp.tile(g_q_norm[l], LANES // DA_HEAD_DIM)[None, :],
        'gkn': jnp.tile(g_k_norm[l], LANES // DA_HEAD_DIM)[None, :],
        'w_in': w_in_bf,
        'wup': jnp.pad(w_gla_gate_up[l].astype(BF16), ((0, LANES - GLA_RANK), (0, 0))),
        'bup': b_gla_gate[l][None, :],
        'wgate': w_in_bf[:, GATE_OFFSET:GATE_OFFSET + 2 * D_MODEL],
        'bgate': b_merge_gate[l].reshape(1, 2 * D_MODEL),
        'wb0': w_branch[l, 0].astype(BF16), 'wb1': w_branch[l, 1].astype(BF16),
        'wout': w_out[l].astype(BF16),
        'gffn': g_ffn_norm[l][None, :],
        'wr': jnp.pad(jnp.concatenate([w_router_group[l], w_router_expert[l].reshape(D_MODEL, N_EXPERTS)],
                                      axis=1).T.astype(BF16), ((0, ROUTER_ROWS - N_GROUPS - N_EXPERTS), (0, 0))),
        'br': jnp.broadcast_to(
            jnp.pad(jnp.concatenate([b_router_group[l], b_router_expert[l].reshape(N_EXPERTS)]),
                    (0, ROUTER_ROWS - N_GROUPS - N_EXPERTS))[:, None], (ROUTER_ROWS, ROW_TILE)),
    }

    u_front = jnp.concatenate([jnp.zeros((FRONT - N_META, D_MODEL), F32), meta_tokens.astype(F32)], axis=0)
    cos_f, sin_f = _rope_tables(jnp.arange(FRONT, dtype=F32) - (FRONT - N_META))
    cos_x, sin_x = _rope_tables(jnp.arange(seq, dtype=F32) + N_META)
    front = _inproj(u_front, FRONT, cos_f, sin_f, p)
    xin = _inproj(x.reshape(n, D_MODEL), ROW_TILE, cos_x, sin_x, p)
    q_t = xin[0]
    k, v, gq, gk, gv, sr, la = [a.reshape(bsz, seq, a.shape[-1]) for a in xin[1:]]
    _, kf, vf, _, gkf, gvf, _, laf = front

    lam_init = 0.8 - 0.6 * math.exp(-0.3 * l)
    lam_vecs = [a[l][None, :] for a in (lambda_q1, lambda_k1, lambda_q2, lambda_k2)]
    score_bound = (ATT_BOUND_MARGIN * DA_HEAD_DIM * Q_SCALE
                   * jnp.max(jnp.abs(g_q_norm[l])) * jnp.max(jnp.abs(g_k_norm[l]))).reshape(1).astype(F32)
    gsub_t = jnp.broadcast_to(g_diff_subln[l][:, None], (DA_V_DIM, ATT_TILE))
    o_a_t = _diff_attn(score_bound, q_t, kf, vf, k, v, lam_vecs, gsub_t, lam_init)
    o_b = _gla(gkf, gvf, laf, gq, gk, gv, la, sr, g_gla_norm[l][None, :])

    u1, h2, info, cnt = _merge(o_a_t, o_b.reshape(n, -1), x.reshape(n, D_MODEL), p)

    ids = info[0:2].astype(jnp.int32)
    wts = info[2:4]
    rank = info[4:6].astype(jnp.int32)
    counts = cnt[N_GROUPS:N_GROUPS + N_EXPERTS, 0].astype(jnp.int32)
    padded = (counts + MOE_TILE - 1) // MOE_TILE * MOE_TILE
    pends = jnp.cumsum(padded)
    pstarts = pends - padded
    expert = jnp.arange(N_EXPERTS, dtype=jnp.int32)
    dest = jnp.sum(jnp.where(ids[..., None] == expert, pstarts, 0), axis=-1) + rank
    n_slots = (2 * n // MOE_TILE + N_EXPERTS) * MOE_TILE
    n_blocks = n_slots // MOE_TILE
    n_used = (pends[-1] // MOE_TILE).astype(jnp.int32)
    blk = jnp.minimum(jnp.arange(n_blocks, dtype=jnp.int32), n_used - 1) * MOE_TILE
    block_e = jnp.minimum(jnp.sum(pends[None, :] <= blk[:, None], axis=1), N_EXPERTS - 1).astype(jnp.int32)
    tail_blocks = jnp.where(counts > 0, pends // MOE_TILE - 1, -1).astype(jnp.int32)
    dest_t = dest.reshape(2, n // DMA_TILE, DMA_TILE).transpose(1, 0, 2)

    xs = _dispatch(tail_blocks, n_used[None], dest_t, h2, n_slots)
    ys = _experts(block_e, n_used[None], xs, w_exp_gate[l], w_exp_up[l], w_exp_down[l])
    out = _combine(dest_t, wts.T, u1, ys)
    return out.reshape(bsz, seq, D_MODEL)


def kernel(x, meta_tokens, g_mix_norm, w_in, g_q_norm, g_k_norm, lambda_q1, lambda_k1, lambda_q2, lambda_k2,
           g_diff_subln, w_gla_gate_up, b_gla_gate, g_gla_norm, w_branch, b_merge_gate, w_out, g_ffn_norm,
           w_router_group, b_router_group, w_router_expert, b_router_expert, w_exp_gate, w_exp_up, w_exp_down):
    depth = w_in.shape[0]
    assert depth == 1, "meta tokens are only carried through a single layer in this implementation"
    assert x.shape[1] % ROW_TILE == 0 and x.shape[2] == D_MODEL
    return _layer(x, meta_tokens, 0, g_mix_norm, w_in, g_q_norm, g_k_norm, lambda_q1, lambda_k1, lambda_q2,
                  lambda_k2, g_diff_subln, w_gla_gate_up, b_gla_gate, g_gla_norm, w_branch, b_merge_gate, w_out,
                  g_ffn_norm, w_router_group, b_router_group, w_router_expert, b_router_expert,
                  w_exp_gate, w_exp_up, w_exp_down)
```

```python
import functools
import math

import jax
import jax.numpy as jnp
from jax import lax
from jax.experimental import pallas as pl
from jax.experimental.pallas import tpu as pltpu

F32 = jnp.float32
BF16 = jnp.bfloat16

D_MODEL = 1024
N_META = 16
EPS = 1e-6
ROPE_THETA = 10000.0

DA_HEADS = 4
DA_HEAD_DIM = 64
DA_V_DIM = 128
Q_SCALE = DA_HEAD_DIM ** -0.5 * math.log2(math.e)
ATT_BOUND_MARGIN = 1.02
ATT_SAFE_BOUND = 60.0
GLA_HEADS = 4
GLA_DK = 64
GLA_DV = 128
GLA_RANK = 16
GLA_TAU = 16.0
GLA_CHUNK = 64
GLA_SAFE_DECAY = 60.0
N_GROUPS = 4
EXPERTS_PER_GROUP = 8
N_EXPERTS = 32
D_EXPERT = 512
ROUTER_ROWS = 48
MIXER_SECTIONS = ((0, 512), (512, 512), (1024, 512), (1536, 256), (1792, 256), (2048, 512), (2560, 512),
                  (3072, 128))
GATE_OFFSET = 3072 + GLA_RANK

LANES = 128
FRONT = 256
ATT_TILE = 256
ATT_GROUP = 6
ROW_TILE = 512
GLA_GROUP = 512
GLA_BATCH = 4
MOE_TILE = 512
EXPERT_CHUNK = 256
X_SLOTS = 3
DMA_TILE = 512
NEG_BIG = -1e30
VMEM_LIMIT = 56 * 1024 * 1024


def _cparams(sem):
    return pltpu.CompilerParams(dimension_semantics=sem, vmem_limit_bytes=VMEM_LIMIT)


def _const_spec(shape):
    nd = len(shape)
    return pl.BlockSpec(shape, lambda *_: (0,) * nd)


RT = D_MODEL // LANES


def _row_tile_spec(rows, index_map):
    return pl.BlockSpec((rows * RT, LANES), index_map)


def _token_rows(ref, tok):
    return ref.at[pl.ds(pl.multiple_of(tok * RT, RT), RT)]


def _load_row_tiles(ref, rows, first=0):
    return jnp.concatenate([ref[pl.ds(first * RT + c, rows, stride=RT), :] for c in range(RT)], axis=1)


def _store_row_tiles(ref, val):
    for c in range(RT):
        ref[pl.ds(c, val.shape[0], stride=RT), :] = val[:, c * LANES:(c + 1) * LANES]


def _sigmoid(x):
    return 0.5 * jnp.tanh(0.5 * x) + 0.5


def _log_sigmoid(x):
    return jnp.minimum(x, 0.0) - jnp.log1p(jnp.exp(-jnp.abs(x)))


def _inproj_body(u_ref, gmix_ref, cos_ref, sin_ref, gqn_ref, gkn_ref,
                 wq_ref, wk_ref, wv_ref, wgq_ref, wgk_ref, wgv_ref, wgr_ref, wgg_ref,
                 wup_ref, bup_ref,
                 q_out, k_out, v_out, gq_out, gk_out, gv_out, sr_out, la_out):
    x = u_ref[...]
    ms = jnp.mean(x * x, axis=-1, keepdims=True)
    h = (x * lax.rsqrt(ms + EPS) * gmix_ref[...]).astype(BF16)

    cos = cos_ref[...]
    sin = sin_ref[...]
    lane = lax.broadcasted_iota(jnp.int32, (1, LANES), 1)
    first_half = (lane % DA_HEAD_DIM) < (DA_HEAD_DIM // 2)
    gi = lax.broadcasted_iota(jnp.int32, (LANES, LANES), 0) // DA_HEAD_DIM
    gj = lax.broadcasted_iota(jnp.int32, (LANES, LANES), 1) // DA_HEAD_DIM
    group_sum = (gi == gj).astype(BF16)

    def norm_rope(w_ref, gain_ref, out_ref, scale, transposed):
        z = jnp.dot(h, w_ref[...], preferred_element_type=F32)
        for hh in range(DA_HEADS):
            hs = slice(hh * LANES, (hh + 1) * LANES)
            zh = z[:, hs]
            ssq = jnp.dot((zh * zh).astype(BF16), group_sum, preferred_element_type=F32)
            zn = zh * lax.rsqrt(ssq * (1.0 / DA_HEAD_DIM) + EPS) * gain_ref[...]
            rot = jnp.where(first_half,
                            pltpu.roll(zn, LANES - DA_HEAD_DIM // 2, 1),
                            pltpu.roll(zn, DA_HEAD_DIM // 2, 1))
            zr = (zn * cos + rot * sin) * scale
            if transposed:
                out_ref[hs, :] = jnp.transpose(zr).astype(out_ref.dtype)
            else:
                out_ref[:, hs] = zr.astype(out_ref.dtype)

    norm_rope(wq_ref, gqn_ref, q_out, Q_SCALE, transposed=True)
    norm_rope(wk_ref, gkn_ref, k_out, 1.0, transposed=False)
    v_out[...] = jnp.dot(h, wv_ref[...], preferred_element_type=F32).astype(v_out.dtype)

    gq_out[...] = (jnp.dot(h, wgq_ref[...], preferred_element_type=F32) * (GLA_DK ** -0.5)).astype(gq_out.dtype)
    gk_out[...] = jnp.dot(h, wgk_ref[...], preferred_element_type=F32).astype(gk_out.dtype)
    gv_out[...] = jnp.dot(h, wgv_ref[...], preferred_element_type=F32).astype(gv_out.dtype)
    r = jnp.dot(h, wgr_ref[...], preferred_element_type=F32)
    sr_out[...] = (r * _sigmoid(r)).astype(sr_out.dtype)

    g_lr = jnp.dot(h, wgg_ref[...], preferred_element_type=F32)
    pre = jnp.dot(g_lr.astype(BF16), wup_ref[...], preferred_element_type=F32) + bup_ref[...]
    la_out[...] = _log_sigmoid(pre) * (1.0 / GLA_TAU)


def _inproj(u, tm, cos, sin, p):
    rows = u.shape[0]
    n_tab = cos.shape[0] // tm
    row = lambda w: pl.BlockSpec((tm, w), lambda i: (i, 0))
    tab = pl.BlockSpec((tm, LANES), lambda i: (i % n_tab, 0))
    sections = [pl.BlockSpec((D_MODEL, w), lambda i, j=off // w: (0, j)) for off, w in MIXER_SECTIONS]
    assert all(off % w == 0 for off, w in MIXER_SECTIONS)
    out_widths = [(512, BF16), (512, BF16), (512, BF16), (256, BF16), (256, BF16), (512, BF16),
                  (512, BF16), (256, F32)]
    return pl.pallas_call(
        _inproj_body,
        grid=(rows // tm,),
        in_specs=[row(D_MODEL), _const_spec((1, D_MODEL)), tab, tab,
                  _const_spec((1, LANES)), _const_spec((1, LANES))]
                 + sections + [_const_spec(p['wup'].shape), _const_spec(p['bup'].shape)],
        out_specs=[pl.BlockSpec((out_widths[0][0], tm), lambda i: (0, i))] + [row(w) for w, _ in out_widths[1:]],
        out_shape=[jax.ShapeDtypeStruct((out_widths[0][0], rows), BF16)]
                  + [jax.ShapeDtypeStruct((rows, w), dt) for w, dt in out_widths[1:]],
        compiler_params=_cparams(("parallel",)),
        name="inproj",
    )(u, p['gmix'], cos, sin, p['gqn'], p['gkn'], *([p['w_in']] * len(MIXER_SECTIONS)), p['wup'], p['bup'])


def _diff_attn_body(bound_ref, lq1_ref, lk1_ref, lq2_ref, lk2_ref, gsub_ref,
                    q_ref, kf_ref, vf_ref, kx_ref, vx_ref, o_ref, vt_sc, qs_sc, s_sc, cmax_sc, m_sc, l_sc, acc_sc,
                    *, lam_init):
    qi = pl.program_id(1)
    tq = ATT_TILE
    n_kv = kx_ref.shape[0] // tq
    heads = range(DA_HEADS)
    hs = lambda h: slice(h * LANES, (h + 1) * LANES)

    @pl.when(qi == 0)
    def _():
        def tr(j, carry):
            off = pl.multiple_of(j * tq, tq)
            for h in heads:
                vt_sc[h, j] = jnp.transpose(vx_ref[pl.ds(off, tq), hs(h)].astype(F32)).astype(BF16)
            return carry

        lax.fori_loop(0, n_kv, tr, 0)

    d = lax.broadcasted_iota(jnp.int32, (LANES, tq), 0)
    for h in heads:
        qt = q_ref[hs(h), :]
        zero = jnp.zeros_like(qt)
        qs_sc[h] = jnp.concatenate([jnp.where(d < DA_HEAD_DIM, qt, zero),
                                    jnp.where(d >= DA_HEAD_DIM, qt, zero)], axis=1)

    meta = slice(FRONT - N_META, FRONT)
    key = lax.broadcasted_iota(jnp.int32, (tq, 2 * tq), 0)
    qry = lax.broadcasted_iota(jnp.int32, (tq, 2 * tq), 1) % tq
    causal = key <= qry
    bound = bound_ref[0]

    def meta_scores(h):
        s = jnp.dot(kf_ref[meta, hs(h)], qs_sc[h], preferred_element_type=F32)
        vt = jnp.transpose(vf_ref[meta, hs(h)].astype(F32)).astype(BF16)
        return s, vt

    def block_scores(j, h, diagonal):
        off = pl.multiple_of(j * tq, tq)
        s = jnp.dot(kx_ref[pl.ds(off, tq), hs(h)], qs_sc[h], preferred_element_type=F32)
        return jnp.where(causal, s, NEG_BIG) if diagonal else s

    @pl.when(bound <= ATT_SAFE_BOUND)
    def _():
        l_sc[...] = jnp.zeros(l_sc.shape, F32)
        acc_sc[...] = jnp.zeros(acc_sc.shape, F32)

        def blocks(*js, last=False):
            work = [(h, block_scores(j, h, last and j is js[-1]), (h, j)) for j in js for h in heads]
            if last:
                work += [(h,) + meta_scores(h) for h in heads]
            ps = [(h, jnp.exp2(s - bound), vt) for h, s, vt in work]
            for h, pr, _ in ps:
                l_sc[h] += jnp.sum(pr, axis=0, keepdims=True)
            for h, pr, vt in ps:
                vt = vt_sc[vt] if isinstance(vt, tuple) else vt
                acc_sc[h] += jnp.dot(vt, pr.astype(BF16), preferred_element_type=F32)

        def full_blocks(u, carry):
            blocks(*[ATT_GROUP * u + d for d in range(ATT_GROUP)])
            return carry

        lax.fori_loop(0, qi // ATT_GROUP, full_blocks, 0)
        for rem in range(ATT_GROUP):
            pl.when(qi % ATT_GROUP == rem)(
                functools.partial(blocks, *[qi - d for d in range(rem, -1, -1)], last=True))

    @pl.when(bound > ATT_SAFE_BOUND)
    def _():
        for h in heads:
            s, vt = meta_scores(h)
            m0 = jnp.max(s, axis=0, keepdims=True)
            pr = jnp.exp2(s - m0)
            m_sc[h] = m0
            l_sc[h] = jnp.sum(pr, axis=0, keepdims=True)
            acc_sc[h] = jnp.dot(vt, pr.astype(BF16), preferred_element_type=F32)

        def scores(j, slot, diagonal=False):
            for h in heads:
                s = block_scores(j, h, diagonal)
                s_sc[h, slot] = s
                cmax_sc[h, slot] = jnp.max(s, axis=0, keepdims=True)

        def accumulate(j, slot):
            for h in heads:
                m_old = m_sc[h]
                m_new = jnp.maximum(m_old, cmax_sc[h, slot])
                alpha = jnp.exp2(m_old - m_new)
                pr = jnp.exp2(s_sc[h, slot] - m_new)
                l_sc[h] = alpha * l_sc[h] + jnp.sum(pr, axis=0, keepdims=True)
                acc_sc[h] = alpha * acc_sc[h] + jnp.dot(vt_sc[h, j], pr.astype(BF16),
                                                        preferred_element_type=F32)
                m_sc[h] = m_new

        @pl.when(qi == 0)
        def _():
            scores(0, 0, diagonal=True)
            accumulate(0, 0)

        @pl.when(qi > 0)
        def _():
            scores(0, 0)

            def pair(u, carry):
                j = 2 * u
                scores(j + 1, 1)
                accumulate(j, 0)
                scores(j + 2, 0)
                accumulate(j + 1, 1)
                return carry

            lax.fori_loop(0, (qi - 1) // 2, pair, 0)

            @pl.when(qi % 2 == 1)
            def _():
                scores(qi, 1, diagonal=True)
                accumulate(qi - 1, 0)
                accumulate(qi, 1)

            @pl.when(qi % 2 == 0)
            def _():
                scores(qi - 1, 1)
                accumulate(qi - 2, 0)
                scores(qi, 0, diagonal=True)
                accumulate(qi - 1, 1)
                accumulate(qi, 0)

    lam = (jnp.exp(jnp.sum(lq1_ref[...] * lk1_ref[...], axis=-1, keepdims=True))
           - jnp.exp(jnp.sum(lq2_ref[...] * lk2_ref[...], axis=-1, keepdims=True)) + lam_init)
    for h in heads:
        acc = acc_sc[h]
        inv_l = 1.0 / l_sc[h]
        ot = acc[:, :tq] * inv_l[:, :tq] - lam * (acc[:, tq:] * inv_l[:, tq:])
        ms = jnp.mean(ot * ot, axis=0, keepdims=True)
        o_ref[hs(h), :] = (ot * lax.rsqrt(ms + EPS) * gsub_ref[...] * (1.0 - lam_init)).astype(o_ref.dtype)


def _diff_attn(bound, q_t, kf, vf, kx, vx, lam_vecs, gsub_t, lam_init):
    bsz, seq, _ = kx.shape
    tq = ATT_TILE
    vec = _const_spec((1, DA_HEAD_DIM))
    width = DA_HEADS * LANES
    tok_tile = pl.BlockSpec((width, tq), lambda b, i: (0, b * (seq // tq) + i))
    return pl.pallas_call(
        functools.partial(_diff_attn_body, lam_init=lam_init),
        grid=(bsz, seq // tq),
        in_specs=[pl.BlockSpec(memory_space=pltpu.SMEM), vec, vec, vec, vec, _const_spec((DA_V_DIM, tq)),
                  tok_tile,
                  _const_spec((FRONT, width)), _const_spec((FRONT, width)),
                  pl.BlockSpec((None, seq, width), lambda b, i: (b, 0, 0)),
                  pl.BlockSpec((None, seq, width), lambda b, i: (b, 0, 0))],
        out_specs=tok_tile,
        out_shape=jax.ShapeDtypeStruct((width, bsz * seq), BF16),
        scratch_shapes=[pltpu.VMEM((DA_HEADS, seq // tq, DA_V_DIM, tq), BF16),
                        pltpu.VMEM((DA_HEADS, LANES, 2 * tq), BF16),
                        pltpu.VMEM((DA_HEADS, 2, tq, 2 * tq), F32), pltpu.VMEM((DA_HEADS, 2, 1, 2 * tq), F32),
                        pltpu.VMEM((DA_HEADS, 1, 2 * tq), F32), pltpu.VMEM((DA_HEADS, 1, 2 * tq), F32),
                        pltpu.VMEM((DA_HEADS, DA_V_DIM, 2 * tq), F32)],
        compiler_params=_cparams(("parallel", "arbitrary")),
        name="diff_attn",
    )(bound, *lam_vecs, gsub_t, q_t, kf, vf, kx, vx)


def _split3(a):
    a1 = a.astype(BF16)
    r1 = a - a1.astype(F32)
    a2 = r1.astype(BF16)
    a3 = (r1 - a2.astype(F32)).astype(BF16)
    return a1, a2, a3


def _gla_body(gn_ref, kf_ref, vf_ref, laf_ref, q_ref, k_ref, v_ref, la_ref, sr_ref, o_ref, st_sc, b_sc):
    g = pl.program_id(1)
    c = GLA_CHUNK
    kw = GLA_HEADS * GLA_DK
    vw = GLA_HEADS * GLA_DV

    ti = lax.broadcasted_iota(jnp.int32, (c, c), 0)
    si = lax.broadcasted_iota(jnp.int32, (c, c), 1)
    tri = (si <= ti).astype(BF16)
    hv = lax.broadcasted_iota(jnp.int32, (vw, kw), 0) // GLA_DV
    hk = lax.broadcasted_iota(jnp.int32, (vw, kw), 1) // GLA_DK
    head_mask = hv == hk
    lane_head = lax.broadcasted_iota(jnp.int32, (1, kw), 1) // GLA_DK
    causal = lax.broadcasted_iota(jnp.int32, (GLA_HEADS * c, c), 0) % c >= \
        lax.broadcasted_iota(jnp.int32, (GLA_HEADS * c, c), 1)

    def cumsum(a):
        a1, a2, a3 = _split3(a)
        return (jnp.dot(tri, a1, preferred_element_type=F32)
                + jnp.dot(tri, a2, preferred_element_type=F32)
                + jnp.dot(tri, a3, preferred_element_type=F32))

    batch = range(GLA_BATCH)

    def next_state(st, k, v, b):
        b_last = b[c - 1:c, :]
        kd = (k * jnp.exp(b_last - b)).astype(BF16)
        upd = lax.dot_general(v, kd, (((0,), (0,)), ((), ())), preferred_element_type=F32)
        return jnp.where(head_mask, jnp.exp(b_last) * st + upd, 0.0)

    @pl.when(g == 0)
    def _():
        st0 = next_state(jnp.zeros((vw, kw), F32), kf_ref[...].astype(F32), vf_ref[...], cumsum(laf_ref[...]))
        for bb in batch:
            st_sc[bb] = st0

    gn = gn_ref[...]

    def finish(o, sr):
        outs = []
        for hh in range(GLA_HEADS):
            cs = slice(hh * GLA_DV, (hh + 1) * GLA_DV)
            oh = o[:, cs]
            ms = jnp.mean(oh * oh, axis=-1, keepdims=True)
            outs.append(oh * lax.rsqrt(ms + EPS) * gn * sr[:, cs])
        return jnp.concatenate(outs, axis=1)

    n_chunks = GLA_GROUP // c
    for ci in range(n_chunks):
        rs = slice(ci * c, (ci + 1) * c)
        for bb in batch:
            b_sc[bb, rs, :] = cumsum(la_ref[bb, rs, :])
    steep = jnp.min(b_sc[...]) < -GLA_SAFE_DECAY

    @pl.when(jnp.logical_not(steep))
    def _():
        nt = (((1,), (1,)), ((), ()))
        sts = [st_sc[bb] for bb in batch]
        for ci in range(n_chunks):
            rs = slice(ci * c, (ci + 1) * c)
            bs = [b_sc[bb, rs, :] for bb in batch]
            ks = [k_ref[bb, rs, :].astype(F32) for bb in batch]
            vs = [v_ref[bb, rs, :] for bb in batch]
            qes = [q_ref[bb, rs, :].astype(F32) * jnp.exp(bs[bb]) for bb in batch]
            nxt = [next_state(sts[bb], ks[bb], vs[bb], bs[bb]) for bb in batch]
            inters = [lax.dot_general(qes[bb].astype(BF16), sts[bb].astype(BF16), nt,
                                      preferred_element_type=F32) for bb in batch]
            scs = []
            for bb in batch:
                ke = (ks[bb] * jnp.exp(-bs[bb])).astype(BF16)
                qstack = jnp.concatenate([jnp.where(lane_head == hh, qes[bb], 0.0) for hh in range(GLA_HEADS)],
                                         axis=0).astype(BF16)
                sc = lax.dot_general(qstack, ke, nt, preferred_element_type=F32)
                scs.append(jnp.where(causal, sc, 0.0).astype(BF16))
            pvs = [jnp.dot(scs[bb], vs[bb], preferred_element_type=F32) for bb in batch]
            for bb in batch:
                o = jnp.concatenate([inters[bb][:, hh * GLA_DV:(hh + 1) * GLA_DV]
                                     + pvs[bb][hh * c:(hh + 1) * c, hh * GLA_DV:(hh + 1) * GLA_DV]
                                     for hh in range(GLA_HEADS)], axis=1)
                o_ref[bb, rs, :] = finish(o, sr_ref[bb, rs, :].astype(F32)).astype(o_ref.dtype)
            sts = nxt
        for bb in batch:
            st_sc[bb] = sts[bb]

    @pl.when(steep)
    def _():
        rows = 16

        def tile(i, carry):
            off = pl.multiple_of(i * rows, rows)
            for bb in batch:
                a = la_ref[bb, pl.ds(off, rows), :]
                q = q_ref[bb, pl.ds(off, rows), :].astype(F32)
                k = k_ref[bb, pl.ds(off, rows), :].astype(F32)
                v = v_ref[bb, pl.ds(off, rows), :].astype(F32)
                outs = []
                for r in range(rows):
                    row = lambda x: x[r:r + 1].astype(BF16)
                    upd = lax.dot_general(row(v), row(k), (((0,), (0,)), ((), ())),
                                          preferred_element_type=F32)
                    st = jnp.where(head_mask, jnp.exp(a[r:r + 1]) * st_sc[bb] + upd, 0.0)
                    st_sc[bb] = st
                    outs.append(lax.dot_general(row(q), st.astype(BF16), (((1,), (1,)), ((), ())),
                                                preferred_element_type=F32))
                o = jnp.concatenate(outs, axis=0)
                o_ref[bb, pl.ds(off, rows), :] = finish(
                    o, sr_ref[bb, pl.ds(off, rows), :].astype(F32)).astype(o_ref.dtype)
            return carry

        lax.fori_loop(0, GLA_GROUP // rows, tile, 0)


def _gla(kf, vf, laf, q, k, v, la, sr, gn):
    bsz, seq, _ = q.shape
    t = GLA_GROUP
    kw = GLA_HEADS * GLA_DK
    vw = GLA_HEADS * GLA_DV
    last = FRONT // GLA_CHUNK - 1
    nb = GLA_BATCH
    assert bsz % nb == 0
    fr = lambda w: pl.BlockSpec((GLA_CHUNK, w), lambda b, g: (last, 0))
    xs = lambda w: pl.BlockSpec((nb, t, w), lambda b, g: (b, g, 0))
    return pl.pallas_call(
        _gla_body,
        grid=(bsz // nb, seq // t),
        in_specs=[_const_spec((1, GLA_DV)), fr(kw), fr(vw), fr(kw), xs(kw), xs(kw), xs(vw), xs(kw), xs(vw)],
        out_specs=xs(vw),
        out_shape=jax.ShapeDtypeStruct((bsz, seq, vw), BF16),
        scratch_shapes=[pltpu.VMEM((nb, vw, kw), F32), pltpu.VMEM((nb, t, kw), F32)],
        compiler_params=_cparams(("parallel", "arbitrary")),
        name="gla",
    )(gn, kf, vf, laf, q, k, v, la, sr)


def _merge_body(oa_ref, ob_ref, u_ref, gmix_ref, wgate_ref, bgate_ref, wb0_ref, wb1_ref, wout_ref, gffn_ref,
                wr_ref, br_ref,
                u1_out, h2_out, info_out, cnt_out, cnt_sc):
    i = pl.program_id(0)
    tm = ROW_TILE

    @pl.when(i == 0)
    def _():
        cnt_sc[...] = jnp.zeros(cnt_sc.shape, F32)

    ya = lax.dot_general(oa_ref[...], wb0_ref[...], (((0,), (0,)), ((), ())), preferred_element_type=F32)
    yb = jnp.dot(ob_ref[...], wb1_ref[...], preferred_element_type=F32)
    x = u_ref[...]
    h = (x * lax.rsqrt(jnp.mean(x * x, axis=-1, keepdims=True) + EPS) * gmix_ref[...]).astype(BF16)
    gate = _sigmoid(jnp.dot(h, wgate_ref[...], preferred_element_type=F32) + bgate_ref[...])
    merged = gate[:, :D_MODEL] * ya + gate[:, D_MODEL:] * yb
    u1 = x + jnp.dot(merged.astype(BF16), wout_ref[...], preferred_element_type=F32)
    u1_out[...] = u1
    ms = jnp.mean(u1 * u1, axis=-1, keepdims=True)
    h2f = u1 * lax.rsqrt(ms + EPS) * gffn_ref[...]
    _store_row_tiles(h2_out, h2f)
    h2 = h2f.astype(BF16)

    logits = lax.dot_general(wr_ref[...], h2, (((1,), (1,)), ((), ())), preferred_element_type=F32) + br_ref[...]
    row = lax.broadcasted_iota(jnp.int32, (ROUTER_ROWS, tm), 0)
    is_group = row < N_GROUPS
    gl = jnp.where(is_group, logits, NEG_BIG)
    gmax = jnp.max(gl, axis=0, keepdims=True)
    g_idx = jnp.min(jnp.where(gl == gmax, row, ROUTER_ROWS), axis=0, keepdims=True)
    g_w = 1.0 / jnp.sum(jnp.where(is_group, jnp.exp(gl - gmax), 0.0), axis=0, keepdims=True)
    lo = N_GROUPS + EXPERTS_PER_GROUP * g_idx
    el = jnp.where((row >= lo) & (row < lo + EXPERTS_PER_GROUP), logits, NEG_BIG)
    v1 = jnp.max(el, axis=0, keepdims=True)
    i1 = jnp.min(jnp.where(el == v1, row, ROUTER_ROWS), axis=0, keepdims=True)
    el2 = jnp.where(row == i1, NEG_BIG, el)
    v2 = jnp.max(el2, axis=0, keepdims=True)
    i2 = jnp.min(jnp.where(el2 == v2, row, ROUTER_ROWS), axis=0, keepdims=True)
    e21 = jnp.exp(v2 - v1)
    w1 = g_w / (1.0 + e21)
    w2 = w1 * e21

    onehot = ((row == i1) | (row == i2)).astype(BF16)
    si = lax.broadcasted_iota(jnp.int32, (tm, tm), 0)
    ti = lax.broadcasted_iota(jnp.int32, (tm, tm), 1)
    earlier = (si < ti).astype(BF16)
    cnt = cnt_sc[...]
    before = jnp.dot(onehot, earlier, preferred_element_type=F32) + jnp.tile(cnt, (1, tm // LANES))
    r1 = jnp.sum(jnp.where(row == i1, before, 0.0), axis=0, keepdims=True)
    r2 = jnp.sum(jnp.where(row == i2, before, 0.0), axis=0, keepdims=True)
    cnt = cnt + jnp.dot(onehot, jnp.ones((tm, LANES), BF16), preferred_element_type=F32)
    cnt_sc[...] = cnt
    cnt_out[...] = cnt

    zero = jnp.zeros_like(w1)
    info_out[...] = jnp.concatenate([(i1 - N_GROUPS).astype(F32), (i2 - N_GROUPS).astype(F32),
                                     w1, w2, r1, r2, zero, zero], axis=0)


def _merge(oa, ob, u, p):
    n = u.shape[0]
    tm = ROW_TILE
    row = lambda w: pl.BlockSpec((tm, w), lambda i: (i, 0))
    return pl.pallas_call(
        _merge_body,
        grid=(n // tm,),
        in_specs=[pl.BlockSpec((512, tm), lambda i: (0, i)), row(512), row(D_MODEL),
                  _const_spec((1, D_MODEL)), _const_spec((D_MODEL, 2 * D_MODEL)), _const_spec((1, 2 * D_MODEL)),
                  _const_spec((512, D_MODEL)), _const_spec((512, D_MODEL)), _const_spec((D_MODEL, D_MODEL)),
                  _const_spec((1, D_MODEL)), _const_spec((ROUTER_ROWS, D_MODEL)), _const_spec((ROUTER_ROWS, tm))],
        out_specs=[row(D_MODEL), _row_tile_spec(tm, lambda i: (i, 0)), pl.BlockSpec((8, tm), lambda i: (0, i)),
                   _const_spec((ROUTER_ROWS, LANES))],
        out_shape=[jax.ShapeDtypeStruct((n, D_MODEL), F32), jax.ShapeDtypeStruct((n * RT, LANES), F32),
                   jax.ShapeDtypeStruct((8, n), F32), jax.ShapeDtypeStruct((ROUTER_ROWS, LANES), F32)],
        scratch_shapes=[pltpu.VMEM((ROUTER_ROWS, LANES), F32)],
        compiler_params=_cparams(("arbitrary",)),
        name="merge_router",
    )(oa, ob, u, p['gmix'], p['wgate'], p['bgate'], p['wb0'], p['wb1'], p['wout'], p['gffn'], p['wr'], p['br'])


def _dispatch_body(tail_ref, nb_ref, dest_ref, h2_ref, xs_hbm, zero_sc, sem, zsem):
    i = pl.program_id(0)
    blk_rows = MOE_TILE * RT
    n_blocks = xs_hbm.shape[0] // blk_rows

    def zero_copy(blk):
        dst = xs_hbm.at[pl.ds(pl.multiple_of(blk * blk_rows, blk_rows), blk_rows)]
        return pltpu.make_async_copy(zero_sc, dst, zsem)

    @pl.when(i == 0)
    def _():
        zero_sc[...] = jnp.zeros(zero_sc.shape, F32)

        def tails(fn):
            def body(e, carry):
                @pl.when(tail_ref[e] >= 0)
                def _():
                    fn(zero_copy(tail_ref[e]))
                return carry
            lax.fori_loop(0, N_EXPERTS, body, 0)

        def unused(fn):
            def body(b, carry):
                fn(zero_copy(b))
                return carry
            lax.fori_loop(nb_ref[0], n_blocks, body, 0)

        tails(lambda cp: cp.start())
        unused(lambda cp: cp.start())
        tails(lambda cp: cp.wait())
        unused(lambda cp: cp.wait())

    def start(r, carry):
        src = _token_rows(h2_ref, r)
        pltpu.make_async_copy(src, _token_rows(xs_hbm, dest_ref[0, 0, r]), sem).start()
        pltpu.make_async_copy(src, _token_rows(xs_hbm, dest_ref[0, 1, r]), sem).start()
        return carry

    lax.fori_loop(0, DMA_TILE, start, 0)
    for _ in range(2):
        pltpu.make_async_copy(h2_ref, xs_hbm.at[pl.ds(0, DMA_TILE * RT)], sem).wait()


def _dispatch(tail_blocks, n_used, dest, h2, n_slots):
    n = h2.shape[0] // RT
    grid_spec = pltpu.PrefetchScalarGridSpec(
        num_scalar_prefetch=2,
        grid=(n // DMA_TILE,),
        in_specs=[pl.BlockSpec((1, 2, DMA_TILE), lambda i, tb, nb: (i, 0, 0), memory_space=pltpu.SMEM),
                  _row_tile_spec(DMA_TILE, lambda i, tb, nb: (i, 0))],
        out_specs=pl.BlockSpec(memory_space=pl.ANY),
        scratch_shapes=[pltpu.VMEM((MOE_TILE * RT, LANES), F32), pltpu.SemaphoreType.DMA(()),
                        pltpu.SemaphoreType.DMA(())],
    )
    return pl.pallas_call(
        _dispatch_body,
        grid_spec=grid_spec,
        out_shape=jax.ShapeDtypeStruct((n_slots * RT, LANES), F32),
        compiler_params=_cparams(("arbitrary",)),
        name="dispatch",
    )(tail_blocks, n_used, dest, h2)


def _experts_body(be_ref, nb_ref, xs_hbm, wg_ref, wu_ref, wd_ref, y_ref, wg_sc, wu_sc, wd_sc, xbuf, xsem):
    i = pl.program_id(0)
    nb = nb_ref[0]
    blk_rows = MOE_TILE * RT
    ahead = X_SLOTS - 1

    def fetch(b):
        src = xs_hbm.at[pl.ds(pl.multiple_of(b * blk_rows, blk_rows), blk_rows)]
        return pltpu.make_async_copy(src, xbuf.at[b % X_SLOTS], xsem.at[b % X_SLOTS])

    @pl.when(i == 0)
    def _():
        for b in range(ahead):
            pl.when(b < nb)(lambda b=b: fetch(b).start())

    @pl.when(i + ahead < nb)
    def _():
        fetch(i + ahead).start()

    prev = be_ref[jnp.maximum(i - 1, 0)]
    fresh = (i == 0) | (be_ref[i] != prev)

    @pl.when(fresh)
    def _():
        wg_sc[...] = wg_ref[...].astype(BF16)
        wu_sc[...] = wu_ref[...].astype(BF16)
        wd_sc[...] = wd_ref[...].astype(BF16)

    @pl.when(i < nb)
    def _():
        fetch(i).wait()
        x = _load_row_tiles(xbuf.at[i % X_SLOTS], MOE_TILE).astype(BF16)
        y = jnp.zeros((MOE_TILE, D_MODEL), F32)
        for j in range(D_EXPERT // EXPERT_CHUNK):
            cs = slice(j * EXPERT_CHUNK, (j + 1) * EXPERT_CHUNK)
            gp = jnp.dot(x, wg_sc[:, cs], preferred_element_type=F32)
            up = jnp.dot(x, wu_sc[:, cs], preferred_element_type=F32)
            hid = (gp * _sigmoid(gp) * up).astype(BF16)
            y = y + jnp.dot(hid, wd_sc[cs, :], preferred_element_type=F32)
        _store_row_tiles(y_ref, y)

    @pl.when(i >= nb)
    def _():
        y_ref[...] = jnp.zeros(y_ref.shape, F32)


def _experts(block_e, n_used, xs, wg, wu, wd):
    n_slots = xs.shape[0] // RT
    n_blocks = n_slots // MOE_TILE
    wmap = lambda i, be, nb: (be[i], 0, 0)
    grid_spec = pltpu.PrefetchScalarGridSpec(
        num_scalar_prefetch=2,
        grid=(n_blocks,),
        in_specs=[pl.BlockSpec(memory_space=pl.ANY),
                  pl.BlockSpec((None, D_MODEL, D_EXPERT), wmap),
                  pl.BlockSpec((None, D_MODEL, D_EXPERT), wmap),
                  pl.BlockSpec((None, D_EXPERT, D_MODEL), wmap)],
        out_specs=_row_tile_spec(MOE_TILE, lambda i, be, nb: (i, 0)),
        scratch_shapes=[pltpu.VMEM((D_MODEL, D_EXPERT), BF16), pltpu.VMEM((D_MODEL, D_EXPERT), BF16),
                        pltpu.VMEM((D_EXPERT, D_MODEL), BF16),
                        pltpu.VMEM((X_SLOTS, MOE_TILE * RT, LANES), F32), pltpu.SemaphoreType.DMA((X_SLOTS,))],
    )
    return pl.pallas_call(
        _experts_body,
        grid_spec=grid_spec,
        out_shape=jax.ShapeDtypeStruct((n_slots * RT, LANES), F32),
        compiler_params=_cparams(("arbitrary",)),
        name="experts",
    )(block_e, n_used, xs, wg, wu, wd)


def _combine_body(dest_ref, dest_next_ref, w_ref, u1_ref, ys_hbm, o_ref, ybuf, sems):
    i = pl.program_id(0)
    n_steps = pl.num_programs(0)
    t = DMA_TILE
    group = 32
    slot = i % 2

    def start(d_ref, s, r):
        buf = ybuf.at[s]
        pltpu.make_async_copy(_token_rows(ys_hbm, d_ref[0, 0, r]), _token_rows(buf, r), sems.at[s]).start()
        pltpu.make_async_copy(_token_rows(ys_hbm, d_ref[0, 1, r]), _token_rows(buf, t + r), sems.at[s]).start()

    @pl.when(i == 0)
    def _():
        def first(r, carry):
            start(dest_ref, 0, r)
            return carry

        lax.fori_loop(0, t, first, 0)

    buf = ybuf.at[slot]
    pltpu.make_async_copy(ys_hbm.at[pl.ds(0, buf.shape[0])], buf, sems.at[slot]).wait()

    def combine(j, prefetch):
        base = pl.multiple_of(j * group, group)
        if prefetch:
            for r in range(group):
                start(dest_next_ref, 1 - slot, base + r)
        rows = pl.ds(base, group)
        w = w_ref[rows, :]
        o_ref[rows, :] = (u1_ref[rows, :] + w[:, 0:1] * _load_row_tiles(buf, group, base)
                          + w[:, 1:2] * _load_row_tiles(buf, group, t + base))

    def loop(prefetch):
        def body(j, carry):
            combine(j, prefetch)
            return carry

        lax.fori_loop(0, t // group, body, 0)

    @pl.when(i + 1 < n_steps)
    def _():
        loop(True)

    @pl.when(i + 1 >= n_steps)
    def _():
        loop(False)


def _combine(dest, w, u1, ys):
    n = u1.shape[0]
    t = DMA_TILE
    n_steps = n // t
    return pl.pallas_call(
        _combine_body,
        grid=(n_steps,),
        in_specs=[pl.BlockSpec((1, 2, t), lambda i: (i, 0, 0), memory_space=pltpu.SMEM),
                  pl.BlockSpec((1, 2, t), lambda i: (jnp.minimum(i + 1, n_steps - 1), 0, 0),
                               memory_space=pltpu.SMEM),
                  pl.BlockSpec((t, 2), lambda i: (i, 0)),
                  pl.BlockSpec((t, D_MODEL), lambda i: (i, 0)),
                  pl.BlockSpec(memory_space=pl.ANY)],
        out_specs=pl.BlockSpec((t, D_MODEL), lambda i: (i, 0)),
        out_shape=jax.ShapeDtypeStruct((n, D_MODEL), F32),
        scratch_shapes=[pltpu.VMEM((2, 2 * t * RT, LANES), F32), pltpu.SemaphoreType.DMA((2,))],
        compiler_params=_cparams(("arbitrary",)),
        name="combine",
    )(dest, dest, w, u1, ys)


def _rope_tables(pos):
    half = DA_HEAD_DIM // 2
    inv_freq = jnp.power(ROPE_THETA, -jnp.arange(half, dtype=F32) * 2.0 / DA_HEAD_DIM)
    ang = pos[:, None] * inv_freq[None, :]
    cos, sin = jnp.cos(ang), jnp.sin(ang)
    cos_t = jnp.tile(cos, (1, LANES // half))
    sin_t = jnp.tile(jnp.concatenate([-sin, sin], axis=1), (1, LANES // DA_HEAD_DIM))
    return cos_t, sin_t


def _layer(x, meta_tokens, l, g_mix_norm, w_in, g_q_norm, g_k_norm, lambda_q1, lambda_k1, lambda_q2, lambda_k2,
           g_diff_subln, w_gla_gate_up, b_gla_gate, g_gla_norm, w_branch, b_merge_gate, w_out, g_ffn_norm,
           w_router_group, b_router_group, w_router_expert, b_router_expert, w_exp_gate, w_exp_up, w_exp_down):
    bsz, seq, _ = x.shape
    n = bsz * seq

    w_in_bf = w_in[l].astype(BF16)
    p = {
        'gmix': g_mix_norm[l][None, :],
        'gqn': jnp.tile(g_q_norm[l], LANES // DA_HEAD_DIM)[None, :],
        'gkn': jnp.tile(g_k_norm[l], LANES // DA_HEAD_DIM)[None, :],
        'w_in': w_in_bf,
        'wup': jnp.pad(w_gla_gate_up[l].astype(BF16), ((0, LANES - GLA_RANK), (0, 0))),
        'bup': b_gla_gate[l][None, :],
        'wgate': w_in_bf[:, GATE_OFFSET:GATE_OFFSET + 2 * D_MODEL],
        'bgate': b_merge_gate[l].reshape(1, 2 * D_MODEL),
        'wb0': w_branch[l, 0].astype(BF16), 'wb1': w_branch[l, 1].astype(BF16),
        'wout': w_out[l].astype(BF16),
        'gffn': g_ffn_norm[l][None, :],
        'wr': jnp.pad(jnp.concatenate([w_router_group[l], w_router_expert[l].reshape(D_MODEL, N_EXPERTS)],
                                      axis=1).T.astype(BF16), ((0, ROUTER_ROWS - N_GROUPS - N_EXPERTS), (0, 0))),
        'br': jnp.broadcast_to(
            jnp.pad(jnp.concatenate([b_router_group[l], b_router_expert[l].reshape(N_EXPERTS)]),
                    (0, ROUTER_ROWS - N_GROUPS - N_EXPERTS))[:, None], (ROUTER_ROWS, ROW_TILE)),
    }

    u_front = jnp.concatenate([jnp.zeros((FRONT - N_META, D_MODEL), F32), meta_tokens.astype(F32)], axis=0)
    cos_f, sin_f = _rope_tables(jnp.arange(FRONT, dtype=F32) - (FRONT - N_META))
    cos_x, sin_x = _rope_tables(jnp.arange(seq, dtype=F32) + N_META)
    front = _inproj(u_front, FRONT, cos_f, sin_f, p)
    xin = _inproj(x.reshape(n, D_MODEL), ROW_TILE, cos_x, sin_x, p)
    q_t = xin[0]
    k, v, gq, gk, gv, sr, la = [a.reshape(bsz, seq, a.shape[-1]) for a in xin[1:]]
    _, kf, vf, _, gkf, gvf, _, laf = front

    lam_init = 0.8 - 0.6 * math.exp(-0.3 * l)
    lam_vecs = [a[l][None, :] for a in (lambda_q1, lambda_k1, lambda_q2, lambda_k2)]
    score_bound = (ATT_BOUND_MARGIN * DA_HEAD_DIM * Q_SCALE
                   * jnp.max(jnp.abs(g_q_norm[l])) * jnp.max(jnp.abs(g_k_norm[l]))).reshape(1).astype(F32)
    gsub_t = jnp.broadcast_to(g_diff_subln[l][:, None], (DA_V_DIM, ATT_TILE))
    o_a_t = _diff_attn(score_bound, q_t, kf, vf, k, v, lam_vecs, gsub_t, lam_init)
    o_b = _gla(gkf, gvf, laf, gq, gk, gv, la, sr, g_gla_norm[l][None, :])

    u1, h2, info, cnt = _merge(o_a_t, o_b.reshape(n, -1), x.reshape(n, D_MODEL), p)

    ids = info[0:2].astype(jnp.int32)
    wts = info[2:4]
    rank = info[4:6].astype(jnp.int32)
    counts = cnt[N_GROUPS:N_GROUPS + N_EXPERTS, 0].astype(jnp.int32)
    padded = (counts + MOE_TILE - 1) // MOE_TILE * MOE_TILE
    pends = jnp.cumsum(padded)
    pstarts = pends - padded
    expert = jnp.arange(N_EXPERTS, dtype=jnp.int32)
    dest = jnp.sum(jnp.where(ids[..., None] == expert, pstarts, 0), axis=-1) + rank
    n_slots = (2 * n // MOE_TILE + N_EXPERTS) * MOE_TILE
    n_blocks = n_slots // MOE_TILE
    n_used = (pends[-1] // MOE_TILE).astype(jnp.int32)
    blk = jnp.minimum(jnp.arange(n_blocks, dtype=jnp.int32), n_used - 1) * MOE_TILE
    block_e = jnp.minimum(jnp.sum(pends[None, :] <= blk[:, None], axis=1), N_EXPERTS - 1).astype(jnp.int32)
    tail_blocks = jnp.where(counts > 0, pends // MOE_TILE - 1, -1).astype(jnp.int32)
    dest_t = dest.reshape(2, n // DMA_TILE, DMA_TILE).transpose(1, 0, 2)

    xs = _dispatch(tail_blocks, n_used[None], dest_t, h2, n_slots)
    ys = _experts(block_e, n_used[None], xs, w_exp_gate[l], w_exp_up[l], w_exp_down[l])
    out = _combine(dest_t, wts.T, u1, ys)
    return out.reshape(bsz, seq, D_MODEL)


def kernel(x, meta_tokens, g_mix_norm, w_in, g_q_norm, g_k_norm, lambda_q1, lambda_k1, lambda_q2, lambda_k2,
           g_diff_subln, w_gla_gate_up, b_gla_gate, g_gla_norm, w_branch, b_merge_gate, w_out, g_ffn_norm,
           w_router_group, b_router_group, w_router_expert, b_router_expert, w_exp_gate, w_exp_up, w_exp_down):
    depth = w_in.shape[0]
    assert depth == 1, "meta tokens are only carried through a single layer in this implementation"
    assert x.shape[1] % ROW_TILE == 0 and x.shape[2] == D_MODEL
    return _layer(x, meta_tokens, 0, g_mix_norm, w_in, g_q_norm, g_k_norm, lambda_q1, lambda_k1, lambda_q2,
                  lambda_k2, g_diff_subln, w_gla_gate_up, b_gla_gate, g_gla_norm, w_branch, b_merge_gate, w_out,
                  g_ffn_norm, w_router_group, b_router_group, w_router_expert, b_router_expert,
                  w_exp_gate, w_exp_up, w_exp_down)
```

```python
import functools
import math

import jax
import jax.numpy as jnp
from jax import lax
from jax.experimental import pallas as pl
from jax.experimental.pallas import tpu as pltpu

F32 = jnp.float32
BF16 = jnp.bfloat16

D_MODEL = 1024
N_META = 16
EPS = 1e-6
ROPE_THETA = 10000.0

DA_HEADS = 4
DA_HEAD_DIM = 64
DA_V_DIM = 128
Q_SCALE = DA_HEAD_DIM ** -0.5 * math.log2(math.e)
ATT_BOUND_MARGIN = 1.02
ATT_SAFE_BOUND = 60.0
GLA_HEADS = 4
GLA_DK = 64
GLA_DV = 128
GLA_RANK = 16
GLA_TAU = 16.0
GLA_CHUNK = 64
GLA_SAFE_DECAY = 60.0
N_GROUPS = 4
EXPERTS_PER_GROUP = 8
N_EXPERTS = 32
D_EXPERT = 512
ROUTER_ROWS = 48
MIXER_SECTIONS = ((0, 512), (512, 512), (1024, 512), (1536, 256), (1792, 256), (2048, 512), (2560, 512),
                  (3072, 128))
GATE_OFFSET = 3072 + GLA_RANK

LANES = 128
FRONT = 256
ATT_TILE = 256
ATT_GROUP = 6
ROW_TILE = 512
GLA_GROUP = 512
GLA_BATCH = 4
MOE_TILE = 512
EXPERT_CHUNK = 256
X_SLOTS = 3
DMA_TILE = 1024
NEG_BIG = -1e30
VMEM_LIMIT = 56 * 1024 * 1024


def _cparams(sem):
    return pltpu.CompilerParams(dimension_semantics=sem, vmem_limit_bytes=VMEM_LIMIT)


def _const_spec(shape):
    nd = len(shape)
    return pl.BlockSpec(shape, lambda *_: (0,) * nd)


RT = D_MODEL // LANES


def _row_tile_spec(rows, index_map):
    return pl.BlockSpec((rows * RT, LANES), index_map)


def _token_rows(ref, tok):
    return ref.at[pl.ds(pl.multiple_of(tok * RT, RT), RT)]


def _load_row_tiles(ref, rows, first=0):
    return jnp.concatenate([ref[pl.ds(first * RT + c, rows, stride=RT), :] for c in range(RT)], axis=1)


def _store_row_tiles(ref, val):
    for c in range(RT):
        ref[pl.ds(c, val.shape[0], stride=RT), :] = val[:, c * LANES:(c + 1) * LANES]


def _sigmoid(x):
    return 0.5 * jnp.tanh(0.5 * x) + 0.5


def _log_sigmoid(x):
    return jnp.minimum(x, 0.0) - jnp.log1p(jnp.exp(-jnp.abs(x)))


def _inproj_body(u_ref, gmix_ref, cos_ref, sin_ref, gqn_ref, gkn_ref,
                 wq_ref, wk_ref, wv_ref, wgq_ref, wgk_ref, wgv_ref, wgr_ref, wgg_ref,
                 wup_ref, bup_ref,
                 q_out, k_out, v_out, gq_out, gk_out, gv_out, sr_out, la_out):
    x = u_ref[...]
    ms = jnp.mean(x * x, axis=-1, keepdims=True)
    h = (x * lax.rsqrt(ms + EPS) * gmix_ref[...]).astype(BF16)

    cos = cos_ref[...]
    sin = sin_ref[...]
    lane = lax.broadcasted_iota(jnp.int32, (1, LANES), 1)
    first_half = (lane % DA_HEAD_DIM) < (DA_HEAD_DIM // 2)
    gi = lax.broadcasted_iota(jnp.int32, (LANES, LANES), 0) // DA_HEAD_DIM
    gj = lax.broadcasted_iota(jnp.int32, (LANES, LANES), 1) // DA_HEAD_DIM
    group_sum = (gi == gj).astype(BF16)

    def norm_rope(w_ref, gain_ref, out_ref, scale, transposed):
        z = jnp.dot(h, w_ref[...], preferred_element_type=F32)
        for hh in range(DA_HEADS):
            hs = slice(hh * LANES, (hh + 1) * LANES)
            zh = z[:, hs]
            ssq = jnp.dot((zh * zh).astype(BF16), group_sum, preferred_element_type=F32)
            zn = zh * lax.rsqrt(ssq * (1.0 / DA_HEAD_DIM) + EPS) * gain_ref[...]
            rot = jnp.where(first_half,
                            pltpu.roll(zn, LANES - DA_HEAD_DIM // 2, 1),
                            pltpu.roll(zn, DA_HEAD_DIM // 2, 1))
            zr = (zn * cos + rot * sin) * scale
            if transposed:
                out_ref[hs, :] = jnp.transpose(zr).astype(out_ref.dtype)
            else:
                out_ref[:, hs] = zr.astype(out_ref.dtype)

    norm_rope(wq_ref, gqn_ref, q_out, Q_SCALE, transposed=True)
    norm_rope(wk_ref, gkn_ref, k_out, 1.0, transposed=False)
    v_out[...] = jnp.dot(h, wv_ref[...], preferred_element_type=F32).astype(v_out.dtype)

    gq_out[...] = (jnp.dot(h, wgq_ref[...], preferred_element_type=F32) * (GLA_DK ** -0.5)).astype(gq_out.dtype)
    gk_out[...] = jnp.dot(h, wgk_ref[...], preferred_element_type=F32).astype(gk_out.dtype)
    gv_out[...] = jnp.dot(h, wgv_ref[...], preferred_element_type=F32).astype(gv_out.dtype)
    r = jnp.dot(h, wgr_ref[...], preferred_element_type=F32)
    sr_out[...] = (r * _sigmoid(r)).astype(sr_out.dtype)

    g_lr = jnp.dot(h, wgg_ref[...], preferred_element_type=F32)
    pre = jnp.dot(g_lr.astype(BF16), wup_ref[...], preferred_element_type=F32) + bup_ref[...]
    la_out[...] = _log_sigmoid(pre) * (1.0 / GLA_TAU)


def _inproj(u, tm, cos, sin, p):
    rows = u.shape[0]
    n_tab = cos.shape[0] // tm
    row = lambda w: pl.BlockSpec((tm, w), lambda i: (i, 0))
    tab = pl.BlockSpec((tm, LANES), lambda i: (i % n_tab, 0))
    sections = [pl.BlockSpec((D_MODEL, w), lambda i, j=off // w: (0, j)) for off, w in MIXER_SECTIONS]
    assert all(off % w == 0 for off, w in MIXER_SECTIONS)
    out_widths = [(512, BF16), (512, BF16), (512, BF16), (256, BF16), (256, BF16), (512, BF16),
                  (512, BF16), (256, F32)]
    return pl.pallas_call(
        _inproj_body,
        grid=(rows // tm,),
        in_specs=[row(D_MODEL), _const_spec((1, D_MODEL)), tab, tab,
                  _const_spec((1, LANES)), _const_spec((1, LANES))]
                 + sections + [_const_spec(p['wup'].shape), _const_spec(p['bup'].shape)],
        out_specs=[pl.BlockSpec((out_widths[0][0], tm), lambda i: (0, i))] + [row(w) for w, _ in out_widths[1:]],
        out_shape=[jax.ShapeDtypeStruct((out_widths[0][0], rows), BF16)]
                  + [jax.ShapeDtypeStruct((rows, w), dt) for w, dt in out_widths[1:]],
        compiler_params=_cparams(("parallel",)),
        name="inproj",
    )(u, p['gmix'], cos, sin, p['gqn'], p['gkn'], *([p['w_in']] * len(MIXER_SECTIONS)), p['wup'], p['bup'])


def _diff_attn_body(bound_ref, lq1_ref, lk1_ref, lq2_ref, lk2_ref, gsub_ref,
                    q_ref, kf_ref, vf_ref, kx_ref, vx_ref, o_ref, vt_sc, qs_sc, s_sc, cmax_sc, m_sc, l_sc, acc_sc,
                    *, lam_init):
    qi = pl.program_id(1)
    tq = ATT_TILE
    n_kv = kx_ref.shape[0] // tq
    heads = range(DA_HEADS)
    hs = lambda h: slice(h * LANES, (h + 1) * LANES)

    @pl.when(qi == 0)
    def _():
        def tr(j, carry):
            off = pl.multiple_of(j * tq, tq)
            for h in heads:
                vt_sc[h, j] = jnp.transpose(vx_ref[pl.ds(off, tq), hs(h)].astype(F32)).astype(BF16)
            return carry

        lax.fori_loop(0, n_kv, tr, 0)

    d = lax.broadcasted_iota(jnp.int32, (LANES, tq), 0)
    for h in heads:
        qt = q_ref[hs(h), :]
        zero = jnp.zeros_like(qt)
        qs_sc[h] = jnp.concatenate([jnp.where(d < DA_HEAD_DIM, qt, zero),
                                    jnp.where(d >= DA_HEAD_DIM, qt, zero)], axis=1)

    meta = slice(FRONT - N_META, FRONT)
    key = lax.broadcasted_iota(jnp.int32, (tq, 2 * tq), 0)
    qry = lax.broadcasted_iota(jnp.int32, (tq, 2 * tq), 1) % tq
    causal = key <= qry
    bound = bound_ref[0]

    def meta_scores(h):
        s = jnp.dot(kf_ref[meta, hs(h)], qs_sc[h], preferred_element_type=F32)
        vt = jnp.transpose(vf_ref[meta, hs(h)].astype(F32)).astype(BF16)
        return s, vt

    def block_scores(j, h, diagonal):
        off = pl.multiple_of(j * tq, tq)
        s = jnp.dot(kx_ref[pl.ds(off, tq), hs(h)], qs_sc[h], preferred_element_type=F32)
        return jnp.where(causal, s, NEG_BIG) if diagonal else s

    @pl.when(bound <= ATT_SAFE_BOUND)
    def _():
        l_sc[...] = jnp.zeros(l_sc.shape, F32)
        acc_sc[...] = jnp.zeros(acc_sc.shape, F32)

        def blocks(*js, last=False):
            work = [(h, block_scores(j, h, last and j is js[-1]), (h, j)) for j in js for h in heads]
            if last:
                work += [(h,) + meta_scores(h) for h in heads]
            ps = [(h, jnp.exp2(s - bound), vt) for h, s, vt in work]
            for h, pr, _ in ps:
                l_sc[h] += jnp.sum(pr, axis=0, keepdims=True)
            for h, pr, vt in ps:
                vt = vt_sc[vt] if isinstance(vt, tuple) else vt
                acc_sc[h] += jnp.dot(vt, pr.astype(BF16), preferred_element_type=F32)

        def full_blocks(u, carry):
            blocks(*[ATT_GROUP * u + d for d in range(ATT_GROUP)])
            return carry

        lax.fori_loop(0, qi // ATT_GROUP, full_blocks, 0)
        for rem in range(ATT_GROUP):
            pl.when(qi % ATT_GROUP == rem)(
                functools.partial(blocks, *[qi - d for d in range(rem, -1, -1)], last=True))

    @pl.when(bound > ATT_SAFE_BOUND)
    def _():
        for h in heads:
            s, vt = meta_scores(h)
            m0 = jnp.max(s, axis=0, keepdims=True)
            pr = jnp.exp2(s - m0)
            m_sc[h] = m0
            l_sc[h] = jnp.sum(pr, axis=0, keepdims=True)
            acc_sc[h] = jnp.dot(vt, pr.astype(BF16), preferred_element_type=F32)

        def scores(j, slot, diagonal=False):
            for h in heads:
                s = block_scores(j, h, diagonal)
                s_sc[h, slot] = s
                cmax_sc[h, slot] = jnp.max(s, axis=0, keepdims=True)

        def accumulate(j, slot):
            for h in heads:
                m_old = m_sc[h]
                m_new = jnp.maximum(m_old, cmax_sc[h, slot])
                alpha = jnp.exp2(m_old - m_new)
                pr = jnp.exp2(s_sc[h, slot] - m_new)
                l_sc[h] = alpha * l_sc[h] + jnp.sum(pr, axis=0, keepdims=True)
                acc_sc[h] = alpha * acc_sc[h] + jnp.dot(vt_sc[h, j], pr.astype(BF16),
                                                        preferred_element_type=F32)
                m_sc[h] = m_new

        @pl.when(qi == 0)
        def _():
            scores(0, 0, diagonal=True)
            accumulate(0, 0)

        @pl.when(qi > 0)
        def _():
            scores(0, 0)

            def pair(u, carry):
                j = 2 * u
                scores(j + 1, 1)
                accumulate(j, 0)
                scores(j + 2, 0)
                accumulate(j + 1, 1)
                return carry

            lax.fori_loop(0, (qi - 1) // 2, pair, 0)

            @pl.when(qi % 2 == 1)
            def _():
                scores(qi, 1, diagonal=True)
                accumulate(qi - 1, 0)
                accumulate(qi, 1)

            @pl.when(qi % 2 == 0)
            def _():
                scores(qi - 1, 1)
                accumulate(qi - 2, 0)
                scores(qi, 0, diagonal=True)
                accumulate(qi - 1, 1)
                accumulate(qi, 0)

    lam = (jnp.exp(jnp.sum(lq1_ref[...] * lk1_ref[...], axis=-1, keepdims=True))
           - jnp.exp(jnp.sum(lq2_ref[...] * lk2_ref[...], axis=-1, keepdims=True)) + lam_init)
    for h in heads:
        acc = acc_sc[h]
        inv_l = 1.0 / l_sc[h]
        ot = acc[:, :tq] * inv_l[:, :tq] - lam * (acc[:, tq:] * inv_l[:, tq:])
        ms = jnp.mean(ot * ot, axis=0, keepdims=True)
        o_ref[hs(h), :] = (ot * lax.rsqrt(ms + EPS) * gsub_ref[...] * (1.0 - lam_init)).astype(o_ref.dtype)


def _diff_attn(bound, q_t, kf, vf, kx, vx, lam_vecs, gsub_t, lam_init):
    bsz, seq, _ = kx.shape
    tq = ATT_TILE
    vec = _const_spec((1, DA_HEAD_DIM))
    width = DA_HEADS * LANES
    tok_tile = pl.BlockSpec((width, tq), lambda b, i: (0, b * (seq // tq) + i))
    return pl.pallas_call(
        functools.partial(_diff_attn_body, lam_init=lam_init),
        grid=(bsz, seq // tq),
        in_specs=[pl.BlockSpec(memory_space=pltpu.SMEM), vec, vec, vec, vec, _const_spec((DA_V_DIM, tq)),
                  tok_tile,
                  _const_spec((FRONT, width)), _const_spec((FRONT, width)),
                  pl.BlockSpec((None, seq, width), lambda b, i: (b, 0, 0)),
                  pl.BlockSpec((None, seq, width), lambda b, i: (b, 0, 0))],
        out_specs=tok_tile,
        out_shape=jax.ShapeDtypeStruct((width, bsz * seq), BF16),
        scratch_shapes=[pltpu.VMEM((DA_HEADS, seq // tq, DA_V_DIM, tq), BF16),
                        pltpu.VMEM((DA_HEADS, LANES, 2 * tq), BF16),
                        pltpu.VMEM((DA_HEADS, 2, tq, 2 * tq), F32), pltpu.VMEM((DA_HEADS, 2, 1, 2 * tq), F32),
                        pltpu.VMEM((DA_HEADS, 1, 2 * tq), F32), pltpu.VMEM((DA_HEADS, 1, 2 * tq), F32),
                        pltpu.VMEM((DA_HEADS, DA_V_DIM, 2 * tq), F32)],
        compiler_params=_cparams(("parallel", "arbitrary")),
        name="diff_attn",
    )(bound, *lam_vecs, gsub_t, q_t, kf, vf, kx, vx)


def _split3(a):
    a1 = a.astype(BF16)
    r1 = a - a1.astype(F32)
    a2 = r1.astype(BF16)
    a3 = (r1 - a2.astype(F32)).astype(BF16)
    return a1, a2, a3


def _gla_body(gn_ref, kf_ref, vf_ref, laf_ref, q_ref, k_ref, v_ref, la_ref, sr_ref, o_ref, st_sc, b_sc):
    g = pl.program_id(1)
    c = GLA_CHUNK
    kw = GLA_HEADS * GLA_DK
    vw = GLA_HEADS * GLA_DV

    ti = lax.broadcasted_iota(jnp.int32, (c, c), 0)
    si = lax.broadcasted_iota(jnp.int32, (c, c), 1)
    tri = (si <= ti).astype(BF16)
    hv = lax.broadcasted_iota(jnp.int32, (vw, kw), 0) // GLA_DV
    hk = lax.broadcasted_iota(jnp.int32, (vw, kw), 1) // GLA_DK
    head_mask = hv == hk
    lane_head = lax.broadcasted_iota(jnp.int32, (1, kw), 1) // GLA_DK
    causal = lax.broadcasted_iota(jnp.int32, (GLA_HEADS * c, c), 0) % c >= \
        lax.broadcasted_iota(jnp.int32, (GLA_HEADS * c, c), 1)

    def cumsum(a):
        a1, a2, a3 = _split3(a)
        return (jnp.dot(tri, a1, preferred_element_type=F32)
                + jnp.dot(tri, a2, preferred_element_type=F32)
                + jnp.dot(tri, a3, preferred_element_type=F32))

    batch = range(GLA_BATCH)

    def next_state(st, k, v, b):
        b_last = b[c - 1:c, :]
        kd = (k * jnp.exp(b_last - b)).astype(BF16)
        upd = lax.dot_general(v, kd, (((0,), (0,)), ((), ())), preferred_element_type=F32)
        return jnp.where(head_mask, jnp.exp(b_last) * st + upd, 0.0)

    @pl.when(g == 0)
    def _():
        st0 = next_state(jnp.zeros((vw, kw), F32), kf_ref[...].astype(F32), vf_ref[...], cumsum(laf_ref[...]))
        for bb in batch:
            st_sc[bb] = st0

    gn = gn_ref[...]

    def finish(o, sr):
        outs = []
        for hh in range(GLA_HEADS):
            cs = slice(hh * GLA_DV, (hh + 1) * GLA_DV)
            oh = o[:, cs]
            ms = jnp.mean(oh * oh, axis=-1, keepdims=True)
            outs.append(oh * lax.rsqrt(ms + EPS) * gn * sr[:, cs])
        return jnp.concatenate(outs, axis=1)

    n_chunks = GLA_GROUP // c
    for ci in range(n_chunks):
        rs = slice(ci * c, (ci + 1) * c)
        for bb in batch:
            b_sc[bb, rs, :] = cumsum(la_ref[bb, rs, :])
    steep = jnp.min(b_sc[...]) < -GLA_SAFE_DECAY

    @pl.when(jnp.logical_not(steep))
    def _():
        nt = (((1,), (1,)), ((), ()))
        sts = [st_sc[bb] for bb in batch]
        for ci in range(n_chunks):
            rs = slice(ci * c, (ci + 1) * c)
            bs = [b_sc[bb, rs, :] for bb in batch]
            ks = [k_ref[bb, rs, :].astype(F32) for bb in batch]
            vs = [v_ref[bb, rs, :] for bb in batch]
            qes = [q_ref[bb, rs, :].astype(F32) * jnp.exp(bs[bb]) for bb in batch]
            nxt = [next_state(sts[bb], ks[bb], vs[bb], bs[bb]) for bb in batch]
            inters = [lax.dot_general(qes[bb].astype(BF16), sts[bb].astype(BF16), nt,
                                      preferred_element_type=F32) for bb in batch]
            scs = []
            for bb in batch:
                ke = (ks[bb] * jnp.exp(-bs[bb])).astype(BF16)
                qstack = jnp.concatenate([jnp.where(lane_head == hh, qes[bb], 0.0) for hh in range(GLA_HEADS)],
                                         axis=0).astype(BF16)
                sc = lax.dot_general(qstack, ke, nt, preferred_element_type=F32)
                scs.append(jnp.where(causal, sc, 0.0).astype(BF16))
            pvs = [jnp.dot(scs[bb], vs[bb], preferred_element_type=F32) for bb in batch]
            for bb in batch:
                o = jnp.concatenate([inters[bb][:, hh * GLA_DV:(hh + 1) * GLA_DV]
                                     + pvs[bb][hh * c:(hh + 1) * c, hh * GLA_DV:(hh + 1) * GLA_DV]
                                     for hh in range(GLA_HEADS)], axis=1)
                o_ref[bb, rs, :] = finish(o, sr_ref[bb, rs, :].astype(F32)).astype(o_ref.dtype)
            sts = nxt
        for bb in batch:
            st_sc[bb] = sts[bb]

    @pl.when(steep)
    def _():
        rows = 16

        def tile(i, carry):
            off = pl.multiple_of(i * rows, rows)
            for bb in batch:
                a = la_ref[bb, pl.ds(off, rows), :]
                q = q_ref[bb, pl.ds(off, rows), :].astype(F32)
                k = k_ref[bb, pl.ds(off, rows), :].astype(F32)
                v = v_ref[bb, pl.ds(off, rows), :].astype(F32)
                outs = []
                for r in range(rows):
                    row = lambda x: x[r:r + 1].astype(BF16)
                    upd = lax.dot_general(row(v), row(k), (((0,), (0,)), ((), ())),
                                          preferred_element_type=F32)
                    st = jnp.where(head_mask, jnp.exp(a[r:r + 1]) * st_sc[bb] + upd, 0.0)
                    st_sc[bb] = st
                    outs.append(lax.dot_general(row(q), st.astype(BF16), (((1,), (1,)), ((), ())),
                                                preferred_element_type=F32))
                o = jnp.concatenate(outs, axis=0)
                o_ref[bb, pl.ds(off, rows), :] = finish(
                    o, sr_ref[bb, pl.ds(off, rows), :].astype(F32)).astype(o_ref.dtype)
            return carry

        lax.fori_loop(0, GLA_GROUP // rows, tile, 0)


def _gla(kf, vf, laf, q, k, v, la, sr, gn):
    bsz, seq, _ = q.shape
    t = GLA_GROUP
    kw = GLA_HEADS * GLA_DK
    vw = GLA_HEADS * GLA_DV
    last = FRONT // GLA_CHUNK - 1
    nb = GLA_BATCH
    assert bsz % nb == 0
    fr = lambda w: pl.BlockSpec((GLA_CHUNK, w), lambda b, g: (last, 0))
    xs = lambda w: pl.BlockSpec((nb, t, w), lambda b, g: (b, g, 0))
    return pl.pallas_call(
        _gla_body,
        grid=(bsz // nb, seq // t),
        in_specs=[_const_spec((1, GLA_DV)), fr(kw), fr(vw), fr(kw), xs(kw), xs(kw), xs(vw), xs(kw), xs(vw)],
        out_specs=xs(vw),
        out_shape=jax.ShapeDtypeStruct((bsz, seq, vw), BF16),
        scratch_shapes=[pltpu.VMEM((nb, vw, kw), F32), pltpu.VMEM((nb, t, kw), F32)],
        compiler_params=_cparams(("parallel", "arbitrary")),
        name="gla",
    )(gn, kf, vf, laf, q, k, v, la, sr)


def _merge_body(oa_ref, ob_ref, u_ref, gmix_ref, wgate_ref, bgate_ref, wb0_ref, wb1_ref, wout_ref, gffn_ref,
                wr_ref, br_ref,
                u1_out, h2_out, info_out, cnt_out, cnt_sc):
    i = pl.program_id(0)
    tm = ROW_TILE

    @pl.when(i == 0)
    def _():
        cnt_sc[...] = jnp.zeros(cnt_sc.shape, F32)

    ya = lax.dot_general(oa_ref[...], wb0_ref[...], (((0,), (0,)), ((), ())), preferred_element_type=F32)
    yb = jnp.dot(ob_ref[...], wb1_ref[...], preferred_element_type=F32)
    x = u_ref[...]
    h = (x * lax.rsqrt(jnp.mean(x * x, axis=-1, keepdims=True) + EPS) * gmix_ref[...]).astype(BF16)
    gate = _sigmoid(jnp.dot(h, wgate_ref[...], preferred_element_type=F32) + bgate_ref[...])
    merged = gate[:, :D_MODEL] * ya + gate[:, D_MODEL:] * yb
    u1 = x + jnp.dot(merged.astype(BF16), wout_ref[...], preferred_element_type=F32)
    u1_out[...] = u1
    ms = jnp.mean(u1 * u1, axis=-1, keepdims=True)
    h2f = u1 * lax.rsqrt(ms + EPS) * gffn_ref[...]
    _store_row_tiles(h2_out, h2f)
    h2 = h2f.astype(BF16)

    logits = lax.dot_general(wr_ref[...], h2, (((1,), (1,)), ((), ())), preferred_element_type=F32) + br_ref[...]
    row = lax.broadcasted_iota(jnp.int32, (ROUTER_ROWS, tm), 0)
    is_group = row < N_GROUPS
    gl = jnp.where(is_group, logits, NEG_BIG)
    gmax = jnp.max(gl, axis=0, keepdims=True)
    g_idx = jnp.min(jnp.where(gl == gmax, row, ROUTER_ROWS), axis=0, keepdims=True)
    g_w = 1.0 / jnp.sum(jnp.where(is_group, jnp.exp(gl - gmax), 0.0), axis=0, keepdims=True)
    lo = N_GROUPS + EXPERTS_PER_GROUP * g_idx
    el = jnp.where((row >= lo) & (row < lo + EXPERTS_PER_GROUP), logits, NEG_BIG)
    v1 = jnp.max(el, axis=0, keepdims=True)
    i1 = jnp.min(jnp.where(el == v1, row, ROUTER_ROWS), axis=0, keepdims=True)
    el2 = jnp.where(row == i1, NEG_BIG, el)
    v2 = jnp.max(el2, axis=0, keepdims=True)
    i2 = jnp.min(jnp.where(el2 == v2, row, ROUTER_ROWS), axis=0, keepdims=True)
    e21 = jnp.exp(v2 - v1)
    w1 = g_w / (1.0 + e21)
    w2 = w1 * e21

    onehot = ((row == i1) | (row == i2)).astype(BF16)
    si = lax.broadcasted_iota(jnp.int32, (tm, tm), 0)
    ti = lax.broadcasted_iota(jnp.int32, (tm, tm), 1)
    earlier = (si < ti).astype(BF16)
    cnt = cnt_sc[...]
    before = jnp.dot(onehot, earlier, preferred_element_type=F32) + jnp.tile(cnt, (1, tm // LANES))
    r1 = jnp.sum(jnp.where(row == i1, before, 0.0), axis=0, keepdims=True)
    r2 = jnp.sum(jnp.where(row == i2, before, 0.0), axis=0, keepdims=True)
    cnt = cnt + jnp.dot(onehot, jnp.ones((tm, LANES), BF16), preferred_element_type=F32)
    cnt_sc[...] = cnt
    cnt_out[...] = cnt

    zero = jnp.zeros_like(w1)
    info_out[...] = jnp.concatenate([(i1 - N_GROUPS).astype(F32), (i2 - N_GROUPS).astype(F32),
                                     w1, w2, r1, r2, zero, zero], axis=0)


def _merge(oa, ob, u, p):
    n = u.shape[0]
    tm = ROW_TILE
    row = lambda w: pl.BlockSpec((tm, w), lambda i: (i, 0))
    return pl.pallas_call(
        _merge_body,
        grid=(n // tm,),
        in_specs=[pl.BlockSpec((512, tm), lambda i: (0, i)), row(512), row(D_MODEL),
                  _const_spec((1, D_MODEL)), _const_spec((D_MODEL, 2 * D_MODEL)), _const_spec((1, 2 * D_MODEL)),
                  _const_spec((512, D_MODEL)), _const_spec((512, D_MODEL)), _const_spec((D_MODEL, D_MODEL)),
                  _const_spec((1, D_MODEL)), _const_spec((ROUTER_ROWS, D_MODEL)), _const_spec((ROUTER_ROWS, tm))],
        out_specs=[row(D_MODEL), _row_tile_spec(tm, lambda i: (i, 0)), pl.BlockSpec((8, tm), lambda i: (0, i)),
                   _const_spec((ROUTER_ROWS, LANES))],
        out_shape=[jax.ShapeDtypeStruct((n, D_MODEL), F32), jax.ShapeDtypeStruct((n * RT, LANES), F32),
                   jax.ShapeDtypeStruct((8, n), F32), jax.ShapeDtypeStruct((ROUTER_ROWS, LANES), F32)],
        scratch_shapes=[pltpu.VMEM((ROUTER_ROWS, LANES), F32)],
        compiler_params=_cparams(("arbitrary",)),
        name="merge_router",
    )(oa, ob, u, p['gmix'], p['wgate'], p['bgate'], p['wb0'], p['wb1'], p['wout'], p['gffn'], p['wr'], p['br'])


def _dispatch_body(tail_ref, nb_ref, dest_ref, h2_ref, xs_hbm, zero_sc, sem, zsem):
    i = pl.program_id(0)
    blk_rows = MOE_TILE * RT
    n_blocks = xs_hbm.shape[0] // blk_rows

    def zero_copy(blk):
        dst = xs_hbm.at[pl.ds(pl.multiple_of(blk * blk_rows, blk_rows), blk_rows)]
        return pltpu.make_async_copy(zero_sc, dst, zsem)

    @pl.when(i == 0)
    def _():
        zero_sc[...] = jnp.zeros(zero_sc.shape, F32)

        def tails(fn):
            def body(e, carry):
                @pl.when(tail_ref[e] >= 0)
                def _():
                    fn(zero_copy(tail_ref[e]))
                return carry
            lax.fori_loop(0, N_EXPERTS, body, 0)

        def unused(fn):
            def body(b, carry):
                fn(zero_copy(b))
                return carry
            lax.fori_loop(nb_ref[0], n_blocks, body, 0)

        tails(lambda cp: cp.start())
        unused(lambda cp: cp.start())
        tails(lambda cp: cp.wait())
        unused(lambda cp: cp.wait())

    def start(r, carry):
        src = _token_rows(h2_ref, r)
        pltpu.make_async_copy(src, _token_rows(xs_hbm, dest_ref[0, 0, r]), sem).start()
        pltpu.make_async_copy(src, _token_rows(xs_hbm, dest_ref[0, 1, r]), sem).start()
        return carry

    lax.fori_loop(0, DMA_TILE, start, 0)
    for _ in range(2):
        pltpu.make_async_copy(h2_ref, xs_hbm.at[pl.ds(0, DMA_TILE * RT)], sem).wait()


def _dispatch(tail_blocks, n_used, dest, h2, n_slots):
    n = h2.shape[0] // RT
    grid_spec = pltpu.PrefetchScalarGridSpec(
        num_scalar_prefetch=2,
        grid=(n // DMA_TILE,),
        in_specs=[pl.BlockSpec((1, 2, DMA_TILE), lambda i, tb, nb: (i, 0, 0), memory_space=pltpu.SMEM),
                  _row_tile_spec(DMA_TILE, lambda i, tb, nb: (i, 0))],
        out_specs=pl.BlockSpec(memory_space=pl.ANY),
        scratch_shapes=[pltpu.VMEM((MOE_TILE * RT, LANES), F32), pltpu.SemaphoreType.DMA(()),
                        pltpu.SemaphoreType.DMA(())],
    )
    return pl.pallas_call(
        _dispatch_body,
        grid_spec=grid_spec,
        out_shape=jax.ShapeDtypeStruct((n_slots * RT, LANES), F32),
        compiler_params=_cparams(("arbitrary",)),
        name="dispatch",
    )(tail_blocks, n_used, dest, h2)


def _experts_body(be_ref, nb_ref, xs_hbm, wg_ref, wu_ref, wd_ref, y_ref, wg_sc, wu_sc, wd_sc, xbuf, xsem):
    i = pl.program_id(0)
    nb = nb_ref[0]
    blk_rows = MOE_TILE * RT
    ahead = X_SLOTS - 1

    def fetch(b):
        src = xs_hbm.at[pl.ds(pl.multiple_of(b * blk_rows, blk_rows), blk_rows)]
        return pltpu.make_async_copy(src, xbuf.at[b % X_SLOTS], xsem.at[b % X_SLOTS])

    @pl.when(i == 0)
    def _():
        for b in range(ahead):
            pl.when(b < nb)(lambda b=b: fetch(b).start())

    @pl.when(i + ahead < nb)
    def _():
        fetch(i + ahead).start()

    prev = be_ref[jnp.maximum(i - 1, 0)]
    fresh = (i == 0) | (be_ref[i] != prev)

    @pl.when(fresh)
    def _():
        wg_sc[...] = wg_ref[...].astype(BF16)
        wu_sc[...] = wu_ref[...].astype(BF16)
        wd_sc[...] = wd_ref[...].astype(BF16)

    @pl.when(i < nb)
    def _():
        fetch(i).wait()
        x = _load_row_tiles(xbuf.at[i % X_SLOTS], MOE_TILE).astype(BF16)
        y = jnp.zeros((MOE_TILE, D_MODEL), F32)
        for j in range(D_EXPERT // EXPERT_CHUNK):
            cs = slice(j * EXPERT_CHUNK, (j + 1) * EXPERT_CHUNK)
            gp = jnp.dot(x, wg_sc[:, cs], preferred_element_type=F32)
            up = jnp.dot(x, wu_sc[:, cs], preferred_element_type=F32)
            hid = (gp * _sigmoid(gp) * up).astype(BF16)
            y = y + jnp.dot(hid, wd_sc[cs, :], preferred_element_type=F32)
        _store_row_tiles(y_ref, y)

    @pl.when(i >= nb)
    def _():
        y_ref[...] = jnp.zeros(y_ref.shape, F32)


def _experts(block_e, n_used, xs, wg, wu, wd):
    n_slots = xs.shape[0] // RT
    n_blocks = n_slots // MOE_TILE
    wmap = lambda i, be, nb: (be[i], 0, 0)
    grid_spec = pltpu.PrefetchScalarGridSpec(
        num_scalar_prefetch=2,
        grid=(n_blocks,),
        in_specs=[pl.BlockSpec(memory_space=pl.ANY),
                  pl.BlockSpec((None, D_MODEL, D_EXPERT), wmap),
                  pl.BlockSpec((None, D_MODEL, D_EXPERT), wmap),
                  pl.BlockSpec((None, D_EXPERT, D_MODEL), wmap)],
        out_specs=_row_tile_spec(MOE_TILE, lambda i, be, nb: (i, 0)),
        scratch_shapes=[pltpu.VMEM((D_MODEL, D_EXPERT), BF16), pltpu.VMEM((D_MODEL, D_EXPERT), BF16),
                        pltpu.VMEM((D_EXPERT, D_MODEL), BF16),
                        pltpu.VMEM((X_SLOTS, MOE_TILE * RT, LANES), F32), pltpu.SemaphoreType.DMA((X_SLOTS,))],
    )
    return pl.pallas_call(
        _experts_body,
        grid_spec=grid_spec,
        out_shape=jax.ShapeDtypeStruct((n_slots * RT, LANES), F32),
        compiler_params=_cparams(("arbitrary",)),
        name="experts",
    )(block_e, n_used, xs, wg, wu, wd)


def _combine_body(dest_ref, dest_next_ref, w_ref, u1_ref, ys_hbm, o_ref, ybuf, sems):
    i = pl.program_id(0)
    n_steps = pl.num_programs(0)
    t = DMA_TILE
    group = 32
    slot = i % 2

    def start(d_ref, s, r):
        buf = ybuf.at[s]
        pltpu.make_async_copy(_token_rows(ys_hbm, d_ref[0, 0, r]), _token_rows(buf, r), sems.at[s]).start()
        pltpu.make_async_copy(_token_rows(ys_hbm, d_ref[0, 1, r]), _token_rows(buf, t + r), sems.at[s]).start()

    @pl.when(i == 0)
    def _():
        def first(r, carry):
            start(dest_ref, 0, r)
            return carry

        lax.fori_loop(0, t, first, 0)

    buf = ybuf.at[slot]
    pltpu.make_async_copy(ys_hbm.at[pl.ds(0, buf.shape[0])], buf, sems.at[slot]).wait()

    def combine(j, prefetch):
        base = pl.multiple_of(j * group, group)
        if prefetch:
            for r in range(group):
                start(dest_next_ref, 1 - slot, base + r)
        rows = pl.ds(base, group)
        w = w_ref[rows, :]
        o_ref[rows, :] = (u1_ref[rows, :] + w[:, 0:1] * _load_row_tiles(buf, group, base)
                          + w[:, 1:2] * _load_row_tiles(buf, group, t + base))

    def loop(prefetch):
        def body(j, carry):
            combine(j, prefetch)
            return carry

        lax.fori_loop(0, t // group, body, 0)

    @pl.when(i + 1 < n_steps)
    def _():
        loop(True)

    @pl.when(i + 1 >= n_steps)
    def _():
        loop(False)


def _combine(dest, w, u1, ys):
    n = u1.shape[0]
    t = DMA_TILE
    n_steps = n // t
    return pl.pallas_call(
        _combine_body,
        grid=(n_steps,),
        in_specs=[pl.BlockSpec((1, 2, t), lambda i: (i, 0, 0), memory_space=pltpu.SMEM),
                  pl.BlockSpec((1, 2, t), lambda i: (jnp.minimum(i + 1, n_steps - 1), 0, 0),
                               memory_space=pltpu.SMEM),
                  pl.BlockSpec((t, 2), lambda i: (i, 0)),
                  pl.BlockSpec((t, D_MODEL), lambda i: (i, 0)),
                  pl.BlockSpec(memory_space=pl.ANY)],
        out_specs=pl.BlockSpec((t, D_MODEL), lambda i: (i, 0)),
        out_shape=jax.ShapeDtypeStruct((n, D_MODEL), F32),
        scratch_shapes=[pltpu.VMEM((2, 2 * t * RT, LANES), F32), pltpu.SemaphoreType.DMA((2,))],
        compiler_params=_cparams(("arbitrary",)),
        name="combine",
    )(dest, dest, w, u1, ys)


def _rope_tables(pos):
    half = DA_HEAD_DIM // 2
    inv_freq = jnp.power(ROPE_THETA, -jnp.arange(half, dtype=F32) * 2.0 / DA_HEAD_DIM)
    ang = pos[:, None] * inv_freq[None, :]
    cos, sin = jnp.cos(ang), jnp.sin(ang)
    cos_t = jnp.tile(cos, (1, LANES // half))
    sin_t = jnp.tile(jnp.concatenate([-sin, sin], axis=1), (1, LANES // DA_HEAD_DIM))
    return cos_t, sin_t


def _layer(x, meta_tokens, l, g_mix_norm, w_in, g_q_norm, g_k_norm, lambda_q1, lambda_k1, lambda_q2, lambda_k2,
           g_diff_subln, w_gla_gate_up, b_gla_gate, g_gla_norm, w_branch, b_merge_gate, w_out, g_ffn_norm,
           w_router_group, b_router_group, w_router_expert, b_router_expert, w_exp_gate, w_exp_up, w_exp_down):
    bsz, seq, _ = x.shape
    n = bsz * seq

    w_in_bf = w_in[l].astype(BF16)
    p = {
        'gmix': g_mix_norm[l][None, :],
        'gqn': jnp.tile(g_q_norm[l], LANES // DA_HEAD_DIM)[None, :],
        'gkn': jnp.tile(g_k_norm[l], LANES // DA_HEAD_DIM)[None, :],
        'w_in': w_in_bf,
        'wup': jnp.pad(w_gla_gate_up[l].astype(BF16), ((0, LANES - GLA_RANK), (0, 0))),
        'bup': b_gla_gate[l][None, :],
        'wgate': w_in_bf[:, GATE_OFFSET:GATE_OFFSET + 2 * D_MODEL],
        'bgate': b_merge_gate[l].reshape(1, 2 * D_MODEL),
        'wb0': w_branch[l, 0].astype(BF16), 'wb1': w_branch[l, 1].astype(BF16),
        'wout': w_out[l].astype(BF16),
        'gffn': g_ffn_norm[l][None, :],
        'wr': jnp.pad(jnp.concatenate([w_router_group[l], w_router_expert[l].reshape(D_MODEL, N_EXPERTS)],
                                      axis=1).T.astype(BF16), ((0, ROUTER_ROWS - N_GROUPS - N_EXPERTS), (0, 0))),
        'br': jnp.broadcast_to(
            jnp.pad(jnp.concatenate([b_router_group[l], b_router_expert[l].reshape(N_EXPERTS)]),
                    (0, ROUTER_ROWS - N_GROUPS - N_EXPERTS))[:, None], (ROUTER_ROWS, ROW_TILE)),
    }

    u_front = jnp.concatenate([jnp.zeros((FRONT - N_META, D_MODEL), F32), meta_tokens.astype(F32)], axis=0)
    cos_f, sin_f = _rope_tables(jnp.arange(FRONT, dtype=F32) - (FRONT - N_META))
    cos_x, sin_x = _rope_tables(jnp.arange(seq, dtype=F32) + N_META)
    front = _inproj(u_front, FRONT, cos_f, sin_f, p)
    xin = _inproj(x.reshape(n, D_MODEL), ROW_TILE, cos_x, sin_x, p)
    q_t = xin[0]
    k, v, gq, gk, gv, sr, la = [a.reshape(bsz, seq, a.shape[-1]) for a in xin[1:]]
    _, kf, vf, _, gkf, gvf, _, laf = front

    lam_init = 0.8 - 0.6 * math.exp(-0.3 * l)
    lam_vecs = [a[l][None, :] for a in (lambda_q1, lambda_k1, lambda_q2, lambda_k2)]
    score_bound = (ATT_BOUND_MARGIN * DA_HEAD_DIM * Q_SCALE
                   * jnp.max(jnp.abs(g_q_norm[l])) * jnp.max(jnp.abs(g_k_norm[l]))).reshape(1).astype(F32)
    gsub_t = jnp.broadcast_to(g_diff_subln[l][:, None], (DA_V_DIM, ATT_TILE))
    o_a_t = _diff_attn(score_bound, q_t, kf, vf, k, v, lam_vecs, gsub_t, lam_init)
    o_b = _gla(gkf, gvf, laf, gq, gk, gv, la, sr, g_gla_norm[l][None, :])

    u1, h2, info, cnt = _merge(o_a_t, o_b.reshape(n, -1), x.reshape(n, D_MODEL), p)

    ids = info[0:2].astype(jnp.int32)
    wts = info[2:4]
    rank = info[4:6].astype(jnp.int32)
    counts = cnt[N_GROUPS:N_GROUPS + N_EXPERTS, 0].astype(jnp.int32)
    padded = (counts + MOE_TILE - 1) // MOE_TILE * MOE_TILE
    pends = jnp.cumsum(padded)
    pstarts = pends - padded
    expert = jnp.arange(N_EXPERTS, dtype=jnp.int32)
    dest = jnp.sum(jnp.where(ids[..., None] == expert, pstarts, 0), axis=-1) + rank
    n_slots = (2 * n // MOE_TILE + N_EXPERTS) * MOE_TILE
    n_blocks = n_slots // MOE_TILE
    n_used = (pends[-1] // MOE_TILE).astype(jnp.int32)
    blk = jnp.minimum(jnp.arange(n_blocks, dtype=jnp.int32), n_used - 1) * MOE_TILE
    block_e = jnp.minimum(jnp.sum(pends[None, :] <= blk[:, None], axis=1), N_EXPERTS - 1).astype(jnp.int32)
    tail_blocks = jnp.where(counts > 0, pends // MOE_TILE - 1, -1).astype(jnp.int32)
    dest_t = dest.reshape(2, n // DMA_TILE, DMA_TILE).transpose(1, 0, 2)

    xs = _dispatch(tail_blocks, n_used[None], dest_t, h2, n_slots)
    ys = _experts(block_e, n_used[None], xs, w_exp_gate[l], w_exp_up[l], w_exp_down[l])
    out = _combine(dest_t, wts.T, u1, ys)
    return out.reshape(bsz, seq, D_MODEL)


def kernel(x, meta_tokens, g_mix_norm, w_in, g_q_norm, g_k_norm, lambda_q1, lambda_k1, lambda_q2, lambda_k2,
           g_diff_subln, w_gla_gate_up, b_gla_gate, g_gla_norm, w_branch, b_merge_gate, w_out, g_ffn_norm,
           w_router_group, b_router_group, w_router_expert, b_router_expert, w_exp_gate, w_exp_up, w_exp_down):
    depth = w_in.shape[0]
    assert depth == 1, "meta tokens are only carried through a single layer in this implementation"
    assert x.shape[1] % ROW_TILE == 0 and x.shape[2] == D_MODEL
    return _layer(x, meta_tokens, 0, g_mix_norm, w_in, g_q_norm, g_k_norm, lambda_q1, lambda_k1, lambda_q2,
                  lambda_k2, g_diff_subln, w_gla_gate_up, b_gla_gate, g_gla_norm, w_branch, b_merge_gate, w_out,
                  g_ffn_norm, w_router_group, b_router_group, w_router_expert, b_router_expert,
                  w_exp_gate, w_exp_up, w_exp_down)
```

```python
import functools
import math

import jax
import jax.numpy as jnp
from jax import lax
from jax.experimental import pallas as pl
from jax.experimental.pallas import tpu as pltpu

F32 = jnp.float32
BF16 = jnp.bfloat16

D_MODEL = 1024
N_META = 16
EPS = 1e-6
ROPE_THETA = 10000.0

DA_HEADS = 4
DA_HEAD_DIM = 64
DA_V_DIM = 128
Q_SCALE = DA_HEAD_DIM ** -0.5 * math.log2(math.e)
ATT_BOUND_MARGIN = 1.02
ATT_SAFE_BOUND = 60.0
GLA_HEADS = 4
GLA_DK = 64
GLA_DV = 128
GLA_RANK = 16
GLA_TAU = 16.0
GLA_CHUNK = 64
GLA_SAFE_DECAY = 60.0
N_GROUPS = 4
EXPERTS_PER_GROUP = 8
N_EXPERTS = 32
D_EXPERT = 512
ROUTER_ROWS = 48
MIXER_SECTIONS = ((0, 512), (512, 512), (1024, 512), (1536, 256), (1792, 256), (2048, 512), (2560, 512),
                  (3072, 128))
GATE_OFFSET = 3072 + GLA_RANK

LANES = 128
FRONT = 256
ATT_TILE = 256
ATT_GROUP = 6
ROW_TILE = 512
GLA_GROUP = 512
GLA_BATCH = 4
MOE_TILE = 512
EXPERT_CHUNK = 256
X_SLOTS = 3
DMA_TILE = 1024
DISPATCH_TILE = 2048
NEG_BIG = -1e30
VMEM_LIMIT = 56 * 1024 * 1024


def _cparams(sem):
    return pltpu.CompilerParams(dimension_semantics=sem, vmem_limit_bytes=VMEM_LIMIT)


def _const_spec(shape):
    nd = len(shape)
    return pl.BlockSpec(shape, lambda *_: (0,) * nd)


RT = D_MODEL // LANES


def _row_tile_spec(rows, index_map):
    return pl.BlockSpec((rows * RT, LANES), index_map)


def _token_rows(ref, tok):
    return ref.at[pl.ds(pl.multiple_of(tok * RT, RT), RT)]


def _load_row_tiles(ref, rows, first=0):
    return jnp.concatenate([ref[pl.ds(first * RT + c, rows, stride=RT), :] for c in range(RT)], axis=1)


def _store_row_tiles(ref, val):
    for c in range(RT):
        ref[pl.ds(c, val.shape[0], stride=RT), :] = val[:, c * LANES:(c + 1) * LANES]


def _sigmoid(x):
    return 0.5 * jnp.tanh(0.5 * x) + 0.5


def _log_sigmoid(x):
    return jnp.minimum(x, 0.0) - jnp.log1p(jnp.exp(-jnp.abs(x)))


def _inproj_body(u_ref, gmix_ref, cos_ref, sin_ref, gqn_ref, gkn_ref,
                 wq_ref, wk_ref, wv_ref, wgq_ref, wgk_ref, wgv_ref, wgr_ref, wgg_ref,
                 wup_ref, bup_ref,
                 q_out, k_out, v_out, gq_out, gk_out, gv_out, sr_out, la_out):
    x = u_ref[...]
    ms = jnp.mean(x * x, axis=-1, keepdims=True)
    h = (x * lax.rsqrt(ms + EPS) * gmix_ref[...]).astype(BF16)

    cos = cos_ref[...]
    sin = sin_ref[...]
    lane = lax.broadcasted_iota(jnp.int32, (1, LANES), 1)
    first_half = (lane % DA_HEAD_DIM) < (DA_HEAD_DIM // 2)
    gi = lax.broadcasted_iota(jnp.int32, (LANES, LANES), 0) // DA_HEAD_DIM
    gj = lax.broadcasted_iota(jnp.int32, (LANES, LANES), 1) // DA_HEAD_DIM
    group_sum = (gi == gj).astype(BF16)

    def norm_rope(w_ref, gain_ref, out_ref, scale, transposed):
        z = jnp.dot(h, w_ref[...], preferred_element_type=F32)
        for hh in range(DA_HEADS):
            hs = slice(hh * LANES, (hh + 1) * LANES)
            zh = z[:, hs]
            ssq = jnp.dot((zh * zh).astype(BF16), group_sum, preferred_element_type=F32)
            zn = zh * lax.rsqrt(ssq * (1.0 / DA_HEAD_DIM) + EPS) * gain_ref[...]
            rot = jnp.where(first_half,
                            pltpu.roll(zn, LANES - DA_HEAD_DIM // 2, 1),
                            pltpu.roll(zn, DA_HEAD_DIM // 2, 1))
            zr = (zn * cos + rot * sin) * scale
            if transposed:
                out_ref[hs, :] = jnp.transpose(zr).astype(out_ref.dtype)
            else:
                out_ref[:, hs] = zr.astype(out_ref.dtype)

    norm_rope(wq_ref, gqn_ref, q_out, Q_SCALE, transposed=True)
    norm_rope(wk_ref, gkn_ref, k_out, 1.0, transposed=False)
    v_out[...] = jnp.dot(h, wv_ref[...], preferred_element_type=F32).astype(v_out.dtype)

    gq_out[...] = (jnp.dot(h, wgq_ref[...], preferred_element_type=F32) * (GLA_DK ** -0.5)).astype(gq_out.dtype)
    gk_out[...] = jnp.dot(h, wgk_ref[...], preferred_element_type=F32).astype(gk_out.dtype)
    gv_out[...] = jnp.dot(h, wgv_ref[...], preferred_element_type=F32).astype(gv_out.dtype)
    r = jnp.dot(h, wgr_ref[...], preferred_element_type=F32)
    sr_out[...] = (r * _sigmoid(r)).astype(sr_out.dtype)

    g_lr = jnp.dot(h, wgg_ref[...], preferred_element_type=F32)
    pre = jnp.dot(g_lr.astype(BF16), wup_ref[...], preferred_element_type=F32) + bup_ref[...]
    la_out[...] = _log_sigmoid(pre) * (1.0 / GLA_TAU)


def _inproj(u, tm, cos, sin, p):
    rows = u.shape[0]
    n_tab = cos.shape[0] // tm
    row = lambda w: pl.BlockSpec((tm, w), lambda i: (i, 0))
    tab = pl.BlockSpec((tm, LANES), lambda i: (i % n_tab, 0))
    sections = [pl.BlockSpec((D_MODEL, w), lambda i, j=off // w: (0, j)) for off, w in MIXER_SECTIONS]
    assert all(off % w == 0 for off, w in MIXER_SECTIONS)
    out_widths = [(512, BF16), (512, BF16), (512, BF16), (256, BF16), (256, BF16), (512, BF16),
                  (512, BF16), (256, F32)]
    return pl.pallas_call(
        _inproj_body,
        grid=(rows // tm,),
        in_specs=[row(D_MODEL), _const_spec((1, D_MODEL)), tab, tab,
                  _const_spec((1, LANES)), _const_spec((1, LANES))]
                 + sections + [_const_spec(p['wup'].shape), _const_spec(p['bup'].shape)],
        out_specs=[pl.BlockSpec((out_widths[0][0], tm), lambda i: (0, i))] + [row(w) for w, _ in out_widths[1:]],
        out_shape=[jax.ShapeDtypeStruct((out_widths[0][0], rows), BF16)]
                  + [jax.ShapeDtypeStruct((rows, w), dt) for w, dt in out_widths[1:]],
        compiler_params=_cparams(("parallel",)),
        name="inproj",
    )(u, p['gmix'], cos, sin, p['gqn'], p['gkn'], *([p['w_in']] * len(MIXER_SECTIONS)), p['wup'], p['bup'])


def _diff_attn_body(bound_ref, lq1_ref, lk1_ref, lq2_ref, lk2_ref, gsub_ref,
                    q_ref, kf_ref, vf_ref, kx_ref, vx_ref, o_ref, vt_sc, qs_sc, s_sc, cmax_sc, m_sc, l_sc, acc_sc,
                    *, lam_init):
    qi = pl.program_id(1)
    tq = ATT_TILE
    n_kv = kx_ref.shape[0] // tq
    heads = range(DA_HEADS)
    hs = lambda h: slice(h * LANES, (h + 1) * LANES)

    @pl.when(qi == 0)
    def _():
        def tr(j, carry):
            off = pl.multiple_of(j * tq, tq)
            for h in heads:
                vt_sc[h, j] = jnp.transpose(vx_ref[pl.ds(off, tq), hs(h)].astype(F32)).astype(BF16)
            return carry

        lax.fori_loop(0, n_kv, tr, 0)

    d = lax.broadcasted_iota(jnp.int32, (LANES, tq), 0)
    for h in heads:
        qt = q_ref[hs(h), :]
        zero = jnp.zeros_like(qt)
        qs_sc[h] = jnp.concatenate([jnp.where(d < DA_HEAD_DIM, qt, zero),
                                    jnp.where(d >= DA_HEAD_DIM, qt, zero)], axis=1)

    meta = slice(FRONT - N_META, FRONT)
    key = lax.broadcasted_iota(jnp.int32, (tq, 2 * tq), 0)
    qry = lax.broadcasted_iota(jnp.int32, (tq, 2 * tq), 1) % tq
    causal = key <= qry
    bound = bound_ref[0]

    def meta_scores(h):
        s = jnp.dot(kf_ref[meta, hs(h)], qs_sc[h], preferred_element_type=F32)
        vt = jnp.transpose(vf_ref[meta, hs(h)].astype(F32)).astype(BF16)
        return s, vt

    def block_scores(j, h, diagonal):
        off = pl.multiple_of(j * tq, tq)
        s = jnp.dot(kx_ref[pl.ds(off, tq), hs(h)], qs_sc[h], preferred_element_type=F32)
        return jnp.where(causal, s, NEG_BIG) if diagonal else s

    @pl.when(bound <= ATT_SAFE_BOUND)
    def _():
        l_sc[...] = jnp.zeros(l_sc.shape, F32)
        acc_sc[...] = jnp.zeros(acc_sc.shape, F32)

        def blocks(*js, last=False):
            work = [(h, block_scores(j, h, last and j is js[-1]), (h, j)) for j in js for h in heads]
            if last:
                work += [(h,) + meta_scores(h) for h in heads]
            ps = [(h, jnp.exp2(s - bound), vt) for h, s, vt in work]
            for h, pr, _ in ps:
                l_sc[h] += jnp.sum(pr, axis=0, keepdims=True)
            for h, pr, vt in ps:
                vt = vt_sc[vt] if isinstance(vt, tuple) else vt
                acc_sc[h] += jnp.dot(vt, pr.astype(BF16), preferred_element_type=F32)

        def full_blocks(u, carry):
            blocks(*[ATT_GROUP * u + d for d in range(ATT_GROUP)])
            return carry

        lax.fori_loop(0, qi // ATT_GROUP, full_blocks, 0)
        for rem in range(ATT_GROUP):
            pl.when(qi % ATT_GROUP == rem)(
                functools.partial(blocks, *[qi - d for d in range(rem, -1, -1)], last=True))

    @pl.when(bound > ATT_SAFE_BOUND)
    def _():
        for h in heads:
            s, vt = meta_scores(h)
            m0 = jnp.max(s, axis=0, keepdims=True)
            pr = jnp.exp2(s - m0)
            m_sc[h] = m0
            l_sc[h] = jnp.sum(pr, axis=0, keepdims=True)
            acc_sc[h] = jnp.dot(vt, pr.astype(BF16), preferred_element_type=F32)

        def scores(j, slot, diagonal=False):
            for h in heads:
                s = block_scores(j, h, diagonal)
                s_sc[h, slot] = s
                cmax_sc[h, slot] = jnp.max(s, axis=0, keepdims=True)

        def accumulate(j, slot):
            for h in heads:
                m_old = m_sc[h]
                m_new = jnp.maximum(m_old, cmax_sc[h, slot])
                alpha = jnp.exp2(m_old - m_new)
                pr = jnp.exp2(s_sc[h, slot] - m_new)
                l_sc[h] = alpha * l_sc[h] + jnp.sum(pr, axis=0, keepdims=True)
                acc_sc[h] = alpha * acc_sc[h] + jnp.dot(vt_sc[h, j], pr.astype(BF16),
                                                        preferred_element_type=F32)
                m_sc[h] = m_new

        @pl.when(qi == 0)
        def _():
            scores(0, 0, diagonal=True)
            accumulate(0, 0)

        @pl.when(qi > 0)
        def _():
            scores(0, 0)

            def pair(u, carry):
                j = 2 * u
                scores(j + 1, 1)
                accumulate(j, 0)
                scores(j + 2, 0)
                accumulate(j + 1, 1)
                return carry

            lax.fori_loop(0, (qi - 1) // 2, pair, 0)

            @pl.when(qi % 2 == 1)
            def _():
                scores(qi, 1, diagonal=True)
                accumulate(qi - 1, 0)
                accumulate(qi, 1)

            @pl.when(qi % 2 == 0)
            def _():
                scores(qi - 1, 1)
                accumulate(qi - 2, 0)
                scores(qi, 0, diagonal=True)
                accumulate(qi - 1, 1)
                accumulate(qi, 0)

    lam = (jnp.exp(jnp.sum(lq1_ref[...] * lk1_ref[...], axis=-1, keepdims=True))
           - jnp.exp(jnp.sum(lq2_ref[...] * lk2_ref[...], axis=-1, keepdims=True)) + lam_init)
    for h in heads:
        acc = acc_sc[h]
        inv_l = 1.0 / l_sc[h]
        ot = acc[:, :tq] * inv_l[:, :tq] - lam * (acc[:, tq:] * inv_l[:, tq:])
        ms = jnp.mean(ot * ot, axis=0, keepdims=True)
        o_ref[hs(h), :] = (ot * lax.rsqrt(ms + EPS) * gsub_ref[...] * (1.0 - lam_init)).astype(o_ref.dtype)


def _diff_attn(bound, q_t, kf, vf, kx, vx, lam_vecs, gsub_t, lam_init):
    bsz, seq, _ = kx.shape
    tq = ATT_TILE
    vec = _const_spec((1, DA_HEAD_DIM))
    width = DA_HEADS * LANES
    tok_tile = pl.BlockSpec((width, tq), lambda b, i: (0, b * (seq // tq) + i))
    return pl.pallas_call(
        functools.partial(_diff_attn_body, lam_init=lam_init),
        grid=(bsz, seq // tq),
        in_specs=[pl.BlockSpec(memory_space=pltpu.SMEM), vec, vec, vec, vec, _const_spec((DA_V_DIM, tq)),
                  tok_tile,
                  _const_spec((FRONT, width)), _const_spec((FRONT, width)),
                  pl.BlockSpec((None, seq, width), lambda b, i: (b, 0, 0)),
                  pl.BlockSpec((None, seq, width), lambda b, i: (b, 0, 0))],
        out_specs=tok_tile,
        out_shape=jax.ShapeDtypeStruct((width, bsz * seq), BF16),
        scratch_shapes=[pltpu.VMEM((DA_HEADS, seq // tq, DA_V_DIM, tq), BF16),
                        pltpu.VMEM((DA_HEADS, LANES, 2 * tq), BF16),
                        pltpu.VMEM((DA_HEADS, 2, tq, 2 * tq), F32), pltpu.VMEM((DA_HEADS, 2, 1, 2 * tq), F32),
                        pltpu.VMEM((DA_HEADS, 1, 2 * tq), F32), pltpu.VMEM((DA_HEADS, 1, 2 * tq), F32),
                        pltpu.VMEM((DA_HEADS, DA_V_DIM, 2 * tq), F32)],
        compiler_params=_cparams(("parallel", "arbitrary")),
        name="diff_attn",
    )(bound, *lam_vecs, gsub_t, q_t, kf, vf, kx, vx)


def _split3(a):
    a1 = a.astype(BF16)
    r1 = a - a1.astype(F32)
    a2 = r1.astype(BF16)
    a3 = (r1 - a2.astype(F32)).astype(BF16)
    return a1, a2, a3


def _gla_body(gn_ref, kf_ref, vf_ref, laf_ref, q_ref, k_ref, v_ref, la_ref, sr_ref, o_ref, st_sc, b_sc):
    g = pl.program_id(1)
    c = GLA_CHUNK
    kw = GLA_HEADS * GLA_DK
    vw = GLA_HEADS * GLA_DV

    ti = lax.broadcasted_iota(jnp.int32, (c, c), 0)
    si = lax.broadcasted_iota(jnp.int32, (c, c), 1)
    tri = (si <= ti).astype(BF16)
    hv = lax.broadcasted_iota(jnp.int32, (vw, kw), 0) // GLA_DV
    hk = lax.broadcasted_iota(jnp.int32, (vw, kw), 1) // GLA_DK
    head_mask = hv == hk
    lane_head = lax.broadcasted_iota(jnp.int32, (1, kw), 1) // GLA_DK
    causal = lax.broadcasted_iota(jnp.int32, (GLA_HEADS * c, c), 0) % c >= \
        lax.broadcasted_iota(jnp.int32, (GLA_HEADS * c, c), 1)

    def cumsum(a):
        a1, a2, a3 = _split3(a)
        return (jnp.dot(tri, a1, preferred_element_type=F32)
                + jnp.dot(tri, a2, preferred_element_type=F32)
                + jnp.dot(tri, a3, preferred_element_type=F32))

    batch = range(GLA_BATCH)

    def next_state(st, k, v, b):
        b_last = b[c - 1:c, :]
        kd = (k * jnp.exp(b_last - b)).astype(BF16)
        upd = lax.dot_general(v, kd, (((0,), (0,)), ((), ())), preferred_element_type=F32)
        return jnp.where(head_mask, jnp.exp(b_last) * st + upd, 0.0)

    @pl.when(g == 0)
    def _():
        st0 = next_state(jnp.zeros((vw, kw), F32), kf_ref[...].astype(F32), vf_ref[...], cumsum(laf_ref[...]))
        for bb in batch:
            st_sc[bb] = st0

    gn = gn_ref[...]

    def finish(o, sr):
        outs = []
        for hh in range(GLA_HEADS):
            cs = slice(hh * GLA_DV, (hh + 1) * GLA_DV)
            oh = o[:, cs]
            ms = jnp.mean(oh * oh, axis=-1, keepdims=True)
            outs.append(oh * lax.rsqrt(ms + EPS) * gn * sr[:, cs])
        return jnp.concatenate(outs, axis=1)

    n_chunks = GLA_GROUP // c
    for ci in range(n_chunks):
        rs = slice(ci * c, (ci + 1) * c)
        for bb in batch:
            b_sc[bb, rs, :] = cumsum(la_ref[bb, rs, :])
    steep = jnp.min(b_sc[...]) < -GLA_SAFE_DECAY

    @pl.when(jnp.logical_not(steep))
    def _():
        nt = (((1,), (1,)), ((), ()))
        sts = [st_sc[bb] for bb in batch]
        for ci in range(n_chunks):
            rs = slice(ci * c, (ci + 1) * c)
            bs = [b_sc[bb, rs, :] for bb in batch]
            ks = [k_ref[bb, rs, :].astype(F32) for bb in batch]
            vs = [v_ref[bb, rs, :] for bb in batch]
            qes = [q_ref[bb, rs, :].astype(F32) * jnp.exp(bs[bb]) for bb in batch]
            nxt = [next_state(sts[bb], ks[bb], vs[bb], bs[bb]) for bb in batch]
            inters = [lax.dot_general(qes[bb].astype(BF16), sts[bb].astype(BF16), nt,
                                      preferred_element_type=F32) for bb in batch]
            scs = []
            for bb in batch:
                ke = (ks[bb] * jnp.exp(-bs[bb])).astype(BF16)
                qstack = jnp.concatenate([jnp.where(lane_head == hh, qes[bb], 0.0) for hh in range(GLA_HEADS)],
                                         axis=0).astype(BF16)
                sc = lax.dot_general(qstack, ke, nt, preferred_element_type=F32)
                scs.append(jnp.where(causal, sc, 0.0).astype(BF16))
            pvs = [jnp.dot(scs[bb], vs[bb], preferred_element_type=F32) for bb in batch]
            for bb in batch:
                o = jnp.concatenate([inters[bb][:, hh * GLA_DV:(hh + 1) * GLA_DV]
                                     + pvs[bb][hh * c:(hh + 1) * c, hh * GLA_DV:(hh + 1) * GLA_DV]
                                     for hh in range(GLA_HEADS)], axis=1)
                o_ref[bb, rs, :] = finish(o, sr_ref[bb, rs, :].astype(F32)).astype(o_ref.dtype)
            sts = nxt
        for bb in batch:
            st_sc[bb] = sts[bb]

    @pl.when(steep)
    def _():
        rows = 16

        def tile(i, carry):
            off = pl.multiple_of(i * rows, rows)
            for bb in batch:
                a = la_ref[bb, pl.ds(off, rows), :]
                q = q_ref[bb, pl.ds(off, rows), :].astype(F32)
                k = k_ref[bb, pl.ds(off, rows), :].astype(F32)
                v = v_ref[bb, pl.ds(off, rows), :].astype(F32)
                outs = []
                for r in range(rows):
                    row = lambda x: x[r:r + 1].astype(BF16)
                    upd = lax.dot_general(row(v), row(k), (((0,), (0,)), ((), ())),
                                          preferred_element_type=F32)
                    st = jnp.where(head_mask, jnp.exp(a[r:r + 1]) * st_sc[bb] + upd, 0.0)
                    st_sc[bb] = st
                    outs.append(lax.dot_general(row(q), st.astype(BF16), (((1,), (1,)), ((), ())),
                                                preferred_element_type=F32))
                o = jnp.concatenate(outs, axis=0)
                o_ref[bb, pl.ds(off, rows), :] = finish(
                    o, sr_ref[bb, pl.ds(off, rows), :].astype(F32)).astype(o_ref.dtype)
            return carry

        lax.fori_loop(0, GLA_GROUP // rows, tile, 0)


def _gla(kf, vf, laf, q, k, v, la, sr, gn):
    bsz, seq, _ = q.shape
    t = GLA_GROUP
    kw = GLA_HEADS * GLA_DK
    vw = GLA_HEADS * GLA_DV
    last = FRONT // GLA_CHUNK - 1
    nb = GLA_BATCH
    assert bsz % nb == 0
    fr = lambda w: pl.BlockSpec((GLA_CHUNK, w), lambda b, g: (last, 0))
    xs = lambda w: pl.BlockSpec((nb, t, w), lambda b, g: (b, g, 0))
    return pl.pallas_call(
        _gla_body,
        grid=(bsz // nb, seq // t),
        in_specs=[_const_spec((1, GLA_DV)), fr(kw), fr(vw), fr(kw), xs(kw), xs(kw), xs(vw), xs(kw), xs(vw)],
        out_specs=xs(vw),
        out_shape=jax.ShapeDtypeStruct((bsz, seq, vw), BF16),
        scratch_shapes=[pltpu.VMEM((nb, vw, kw), F32), pltpu.VMEM((nb, t, kw), F32)],
        compiler_params=_cparams(("parallel", "arbitrary")),
        name="gla",
    )(gn, kf, vf, laf, q, k, v, la, sr)


def _merge_body(oa_ref, ob_ref, u_ref, gmix_ref, wgate_ref, bgate_ref, wb0_ref, wb1_ref, wout_ref, gffn_ref,
                wr_ref, br_ref,
                u1_out, h2_out, info_out, cnt_out, cnt_sc):
    i = pl.program_id(0)
    tm = ROW_TILE

    @pl.when(i == 0)
    def _():
        cnt_sc[...] = jnp.zeros(cnt_sc.shape, F32)

    ya = lax.dot_general(oa_ref[...], wb0_ref[...], (((0,), (0,)), ((), ())), preferred_element_type=F32)
    yb = jnp.dot(ob_ref[...], wb1_ref[...], preferred_element_type=F32)
    x = u_ref[...]
    h = (x * lax.rsqrt(jnp.mean(x * x, axis=-1, keepdims=True) + EPS) * gmix_ref[...]).astype(BF16)
    gate = _sigmoid(jnp.dot(h, wgate_ref[...], preferred_element_type=F32) + bgate_ref[...])
    merged = gate[:, :D_MODEL] * ya + gate[:, D_MODEL:] * yb
    u1 = x + jnp.dot(merged.astype(BF16), wout_ref[...], preferred_element_type=F32)
    u1_out[...] = u1
    ms = jnp.mean(u1 * u1, axis=-1, keepdims=True)
    h2f = u1 * lax.rsqrt(ms + EPS) * gffn_ref[...]
    _store_row_tiles(h2_out, h2f)
    h2 = h2f.astype(BF16)

    logits = lax.dot_general(wr_ref[...], h2, (((1,), (1,)), ((), ())), preferred_element_type=F32) + br_ref[...]
    row = lax.broadcasted_iota(jnp.int32, (ROUTER_ROWS, tm), 0)
    is_group = row < N_GROUPS
    gl = jnp.where(is_group, logits, NEG_BIG)
    gmax = jnp.max(gl, axis=0, keepdims=True)
    g_idx = jnp.min(jnp.where(gl == gmax, row, ROUTER_ROWS), axis=0, keepdims=True)
    g_w = 1.0 / jnp.sum(jnp.where(is_group, jnp.exp(gl - gmax), 0.0), axis=0, keepdims=True)
    lo = N_GROUPS + EXPERTS_PER_GROUP * g_idx
    el = jnp.where((row >= lo) & (row < lo + EXPERTS_PER_GROUP), logits, NEG_BIG)
    v1 = jnp.max(el, axis=0, keepdims=True)
    i1 = jnp.min(jnp.where(el == v1, row, ROUTER_ROWS), axis=0, keepdims=True)
    el2 = jnp.where(row == i1, NEG_BIG, el)
    v2 = jnp.max(el2, axis=0, keepdims=True)
    i2 = jnp.min(jnp.where(el2 == v2, row, ROUTER_ROWS), axis=0, keepdims=True)
    e21 = jnp.exp(v2 - v1)
    w1 = g_w / (1.0 + e21)
    w2 = w1 * e21

    onehot = ((row == i1) | (row == i2)).astype(BF16)
    si = lax.broadcasted_iota(jnp.int32, (tm, tm), 0)
    ti = lax.broadcasted_iota(jnp.int32, (tm, tm), 1)
    earlier = (si < ti).astype(BF16)
    cnt = cnt_sc[...]
    before = jnp.dot(onehot, earlier, preferred_element_type=F32) + jnp.tile(cnt, (1, tm // LANES))
    r1 = jnp.sum(jnp.where(row == i1, before, 0.0), axis=0, keepdims=True)
    r2 = jnp.sum(jnp.where(row == i2, before, 0.0), axis=0, keepdims=True)
    cnt = cnt + jnp.dot(onehot, jnp.ones((tm, LANES), BF16), preferred_element_type=F32)
    cnt_sc[...] = cnt
    cnt_out[...] = cnt

    zero = jnp.zeros_like(w1)
    info_out[...] = jnp.concatenate([(i1 - N_GROUPS).astype(F32), (i2 - N_GROUPS).astype(F32),
                                     w1, w2, r1, r2, zero, zero], axis=0)


def _merge(oa, ob, u, p):
    n = u.shape[0]
    tm = ROW_TILE
    row = lambda w: pl.BlockSpec((tm, w), lambda i: (i, 0))
    return pl.pallas_call(
        _merge_body,
        grid=(n // tm,),
        in_specs=[pl.BlockSpec((512, tm), lambda i: (0, i)), row(512), row(D_MODEL),
                  _const_spec((1, D_MODEL)), _const_spec((D_MODEL, 2 * D_MODEL)), _const_spec((1, 2 * D_MODEL)),
                  _const_spec((512, D_MODEL)), _const_spec((512, D_MODEL)), _const_spec((D_MODEL, D_MODEL)),
                  _const_spec((1, D_MODEL)), _const_spec((ROUTER_ROWS, D_MODEL)), _const_spec((ROUTER_ROWS, tm))],
        out_specs=[row(D_MODEL), _row_tile_spec(tm, lambda i: (i, 0)), pl.BlockSpec((8, tm), lambda i: (0, i)),
                   _const_spec((ROUTER_ROWS, LANES))],
        out_shape=[jax.ShapeDtypeStruct((n, D_MODEL), F32), jax.ShapeDtypeStruct((n * RT, LANES), F32),
                   jax.ShapeDtypeStruct((8, n), F32), jax.ShapeDtypeStruct((ROUTER_ROWS, LANES), F32)],
        scratch_shapes=[pltpu.VMEM((ROUTER_ROWS, LANES), F32)],
        compiler_params=_cparams(("arbitrary",)),
        name="merge_router",
    )(oa, ob, u, p['gmix'], p['wgate'], p['bgate'], p['wb0'], p['wb1'], p['wout'], p['gffn'], p['wr'], p['br'])


def _dispatch_body(tail_ref, nb_ref, dest_ref, h2_ref, xs_hbm, zero_sc, sem, zsem):
    i = pl.program_id(0)
    blk_rows = MOE_TILE * RT
    n_blocks = xs_hbm.shape[0] // blk_rows

    def zero_copy(blk):
        dst = xs_hbm.at[pl.ds(pl.multiple_of(blk * blk_rows, blk_rows), blk_rows)]
        return pltpu.make_async_copy(zero_sc, dst, zsem)

    @pl.when(i == 0)
    def _():
        zero_sc[...] = jnp.zeros(zero_sc.shape, F32)

        def tails(fn):
            def body(e, carry):
                @pl.when(tail_ref[e] >= 0)
                def _():
                    fn(zero_copy(tail_ref[e]))
                return carry
            lax.fori_loop(0, N_EXPERTS, body, 0)

        def unused(fn):
            def body(b, carry):
                fn(zero_copy(b))
                return carry
            lax.fori_loop(nb_ref[0], n_blocks, body, 0)

        tails(lambda cp: cp.start())
        unused(lambda cp: cp.start())
        tails(lambda cp: cp.wait())
        unused(lambda cp: cp.wait())

    def start(r, carry):
        src = _token_rows(h2_ref, r)
        pltpu.make_async_copy(src, _token_rows(xs_hbm, dest_ref[0, 0, r]), sem).start()
        pltpu.make_async_copy(src, _token_rows(xs_hbm, dest_ref[0, 1, r]), sem).start()
        return carry

    lax.fori_loop(0, DISPATCH_TILE, start, 0)
    for _ in range(2):
        pltpu.make_async_copy(h2_ref, xs_hbm.at[pl.ds(0, DISPATCH_TILE * RT)], sem).wait()


def _dispatch(tail_blocks, n_used, dest, h2, n_slots):
    n = h2.shape[0] // RT
    grid_spec = pltpu.PrefetchScalarGridSpec(
        num_scalar_prefetch=2,
        grid=(n // DISPATCH_TILE,),
        in_specs=[pl.BlockSpec((1, 2, DISPATCH_TILE), lambda i, tb, nb: (i, 0, 0), memory_space=pltpu.SMEM),
                  _row_tile_spec(DISPATCH_TILE, lambda i, tb, nb: (i, 0))],
        out_specs=pl.BlockSpec(memory_space=pl.ANY),
        scratch_shapes=[pltpu.VMEM((MOE_TILE * RT, LANES), F32), pltpu.SemaphoreType.DMA(()),
                        pltpu.SemaphoreType.DMA(())],
    )
    return pl.pallas_call(
        _dispatch_body,
        grid_spec=grid_spec,
        out_shape=jax.ShapeDtypeStruct((n_slots * RT, LANES), F32),
        compiler_params=_cparams(("arbitrary",)),
        name="dispatch",
    )(tail_blocks, n_used, dest, h2)


def _experts_body(be_ref, nb_ref, xs_hbm, wg_ref, wu_ref, wd_ref, y_ref, wg_sc, wu_sc, wd_sc, xbuf, xsem):
    i = pl.program_id(0)
    nb = nb_ref[0]
    blk_rows = MOE_TILE * RT
    ahead = X_SLOTS - 1

    def fetch(b):
        src = xs_hbm.at[pl.ds(pl.multiple_of(b * blk_rows, blk_rows), blk_rows)]
        return pltpu.make_async_copy(src, xbuf.at[b % X_SLOTS], xsem.at[b % X_SLOTS])

    @pl.when(i == 0)
    def _():
        for b in range(ahead):
            pl.when(b < nb)(lambda b=b: fetch(b).start())

    @pl.when(i + ahead < nb)
    def _():
        fetch(i + ahead).start()

    prev = be_ref[jnp.maximum(i - 1, 0)]
    fresh = (i == 0) | (be_ref[i] != prev)

    @pl.when(fresh)
    def _():
        wg_sc[...] = wg_ref[...].astype(BF16)
        wu_sc[...] = wu_ref[...].astype(BF16)
        wd_sc[...] = wd_ref[...].astype(BF16)

    @pl.when(i < nb)
    def _():
        fetch(i).wait()
        x = _load_row_tiles(xbuf.at[i % X_SLOTS], MOE_TILE).astype(BF16)
        y = jnp.zeros((MOE_TILE, D_MODEL), F32)
        for j in range(D_EXPERT // EXPERT_CHUNK):
            cs = slice(j * EXPERT_CHUNK, (j + 1) * EXPERT_CHUNK)
            gp = jnp.dot(x, wg_sc[:, cs], preferred_element_type=F32)
            up = jnp.dot(x, wu_sc[:, cs], preferred_element_type=F32)
            hid = (gp * _sigmoid(gp) * up).astype(BF16)
            y = y + jnp.dot(hid, wd_sc[cs, :], preferred_element_type=F32)
        _store_row_tiles(y_ref, y)

    @pl.when(i >= nb)
    def _():
        y_ref[...] = jnp.zeros(y_ref.shape, F32)


def _experts(block_e, n_used, xs, wg, wu, wd):
    n_slots = xs.shape[0] // RT
    n_blocks = n_slots // MOE_TILE
    wmap = lambda i, be, nb: (be[i], 0, 0)
    grid_spec = pltpu.PrefetchScalarGridSpec(
        num_scalar_prefetch=2,
        grid=(n_blocks,),
        in_specs=[pl.BlockSpec(memory_space=pl.ANY),
                  pl.BlockSpec((None, D_MODEL, D_EXPERT), wmap),
                  pl.BlockSpec((None, D_MODEL, D_EXPERT), wmap),
                  pl.BlockSpec((None, D_EXPERT, D_MODEL), wmap)],
        out_specs=_row_tile_spec(MOE_TILE, lambda i, be, nb: (i, 0)),
        scratch_shapes=[pltpu.VMEM((D_MODEL, D_EXPERT), BF16), pltpu.VMEM((D_MODEL, D_EXPERT), BF16),
                        pltpu.VMEM((D_EXPERT, D_MODEL), BF16),
                        pltpu.VMEM((X_SLOTS, MOE_TILE * RT, LANES), F32), pltpu.SemaphoreType.DMA((X_SLOTS,))],
    )
    return pl.pallas_call(
        _experts_body,
        grid_spec=grid_spec,
        out_shape=jax.ShapeDtypeStruct((n_slots * RT, LANES), F32),
        compiler_params=_cparams(("arbitrary",)),
        name="experts",
    )(block_e, n_used, xs, wg, wu, wd)


def _combine_body(dest_ref, dest_next_ref, w_ref, u1_ref, ys_hbm, o_ref, ybuf, sems):
    i = pl.program_id(0)
    n_steps = pl.num_programs(0)
    t = DMA_TILE
    group = 32
    slot = i % 2

    def start(d_ref, s, r):
        buf = ybuf.at[s]
        pltpu.make_async_copy(_token_rows(ys_hbm, d_ref[0, 0, r]), _token_rows(buf, r), sems.at[s]).start()
        pltpu.make_async_copy(_token_rows(ys_hbm, d_ref[0, 1, r]), _token_rows(buf, t + r), sems.at[s]).start()

    @pl.when(i == 0)
    def _():
        def first(r, carry):
            start(dest_ref, 0, r)
            return carry

        lax.fori_loop(0, t, first, 0)

    buf = ybuf.at[slot]
    pltpu.make_async_copy(ys_hbm.at[pl.ds(0, buf.shape[0])], buf, sems.at[slot]).wait()

    def combine(j, prefetch):
        base = pl.multiple_of(j * group, group)
        if prefetch:
            for r in range(group):
                start(dest_next_ref, 1 - slot, base + r)
        rows = pl.ds(base, group)
        w = w_ref[rows, :]
        o_ref[rows, :] = (u1_ref[rows, :] + w[:, 0:1] * _load_row_tiles(buf, group, base)
                          + w[:, 1:2] * _load_row_tiles(buf, group, t + base))

    def loop(prefetch):
        def body(j, carry):
            combine(j, prefetch)
            return carry

        lax.fori_loop(0, t // group, body, 0)

    @pl.when(i + 1 < n_steps)
    def _():
        loop(True)

    @pl.when(i + 1 >= n_steps)
    def _():
        loop(False)


def _combine(dest, w, u1, ys):
    n = u1.shape[0]
    t = DMA_TILE
    n_steps = n // t
    return pl.pallas_call(
        _combine_body,
        grid=(n_steps,),
        in_specs=[pl.BlockSpec((1, 2, t), lambda i: (i, 0, 0), memory_space=pltpu.SMEM),
                  pl.BlockSpec((1, 2, t), lambda i: (jnp.minimum(i + 1, n_steps - 1), 0, 0),
                               memory_space=pltpu.SMEM),
                  pl.BlockSpec((t, 2), lambda i: (i, 0)),
                  pl.BlockSpec((t, D_MODEL), lambda i: (i, 0)),
                  pl.BlockSpec(memory_space=pl.ANY)],
        out_specs=pl.BlockSpec((t, D_MODEL), lambda i: (i, 0)),
        out_shape=jax.ShapeDtypeStruct((n, D_MODEL), F32),
        scratch_shapes=[pltpu.VMEM((2, 2 * t * RT, LANES), F32), pltpu.SemaphoreType.DMA((2,))],
        compiler_params=_cparams(("arbitrary",)),
        name="combine",
    )(dest, dest, w, u1, ys)


def _rope_tables(pos):
    half = DA_HEAD_DIM // 2
    inv_freq = jnp.power(ROPE_THETA, -jnp.arange(half, dtype=F32) * 2.0 / DA_HEAD_DIM)
    ang = pos[:, None] * inv_freq[None, :]
    cos, sin = jnp.cos(ang), jnp.sin(ang)
    cos_t = jnp.tile(cos, (1, LANES // half))
    sin_t = jnp.tile(jnp.concatenate([-sin, sin], axis=1), (1, LANES // DA_HEAD_DIM))
    return cos_t, sin_t


def _layer(x, meta_tokens, l, g_mix_norm, w_in, g_q_norm, g_k_norm, lambda_q1, lambda_k1, lambda_q2, lambda_k2,
           g_diff_subln, w_gla_gate_up, b_gla_gate, g_gla_norm, w_branch, b_merge_gate, w_out, g_ffn_norm,
           w_router_group, b_router_group, w_router_expert, b_router_expert, w_exp_gate, w_exp_up, w_exp_down):
    bsz, seq, _ = x.shape
    n = bsz * seq

    w_in_bf = w_in[l].astype(BF16)
    p = {
        'gmix': g_mix_norm[l][None, :],
        'gqn': jnp.tile(g_q_norm[l], LANES // DA_HEAD_DIM)[None, :],
        'gkn': jnp.tile(g_k_norm[l], LANES // DA_HEAD_DIM)[None, :],
        'w_in': w_in_bf,
        'wup': jnp.pad(w_gla_gate_up[l].astype(BF16), ((0, LANES - GLA_RANK), (0, 0))),
        'bup': b_gla_gate[l][None, :],
        'wgate': w_in_bf[:, GATE_OFFSET:GATE_OFFSET + 2 * D_MODEL],
        'bgate': b_merge_gate[l].reshape(1, 2 * D_MODEL),
        'wb0': w_branch[l, 0].astype(BF16), 'wb1': w_branch[l, 1].astype(BF16),
        'wout': w_out[l].astype(BF16),
        'gffn': g_ffn_norm[l][None, :],
        'wr': jnp.pad(jnp.concatenate([w_router_group[l], w_router_expert[l].reshape(D_MODEL, N_EXPERTS)],
                                      axis=1).T.astype(BF16), ((0, ROUTER_ROWS - N_GROUPS - N_EXPERTS), (0, 0))),
        'br': jnp.broadcast_to(
            jnp.pad(jnp.concatenate([b_router_group[l], b_router_expert[l].reshape(N_EXPERTS)]),
                    (0, ROUTER_ROWS - N_GROUPS - N_EXPERTS))[:, None], (ROUTER_ROWS, ROW_TILE)),
    }

    u_front = jnp.concatenate([jnp.zeros((FRONT - N_META, D_MODEL), F32), meta_tokens.astype(F32)], axis=0)
    cos_f, sin_f = _rope_tables(jnp.arange(FRONT, dtype=F32) - (FRONT - N_META))
    cos_x, sin_x = _rope_tables(jnp.arange(seq, dtype=F32) + N_META)
    front = _inproj(u_front, FRONT, cos_f, sin_f, p)
    xin = _inproj(x.reshape(n, D_MODEL), ROW_TILE, cos_x, sin_x, p)
    q_t = xin[0]
    k, v, gq, gk, gv, sr, la = [a.reshape(bsz, seq, a.shape[-1]) for a in xin[1:]]
    _, kf, vf, _, gkf, gvf, _, laf = front

    lam_init = 0.8 - 0.6 * math.exp(-0.3 * l)
    lam_vecs = [a[l][None, :] for a in (lambda_q1, lambda_k1, lambda_q2, lambda_k2)]
    score_bound = (ATT_BOUND_MARGIN * DA_HEAD_DIM * Q_SCALE
                   * jnp.max(jnp.abs(g_q_norm[l])) * jnp.max(jnp.abs(g_k_norm[l]))).reshape(1).astype(F32)
    gsub_t = jnp.broadcast_to(g_diff_subln[l][:, None], (DA_V_DIM, ATT_TILE))
    o_a_t = _diff_attn(score_bound, q_t, kf, vf, k, v, lam_vecs, gsub_t, lam_init)
    o_b = _gla(gkf, gvf, laf, gq, gk, gv, la, sr, g_gla_norm[l][None, :])

    u1, h2, info, cnt = _merge(o_a_t, o_b.reshape(n, -1), x.reshape(n, D_MODEL), p)

    ids = info[0:2].astype(jnp.int32)
    wts = info[2:4]
    rank = info[4:6].astype(jnp.int32)
    counts = cnt[N_GROUPS:N_GROUPS + N_EXPERTS, 0].astype(jnp.int32)
    padded = (counts + MOE_TILE - 1) // MOE_TILE * MOE_TILE
    pends = jnp.cumsum(padded)
    pstarts = pends - padded
    expert = jnp.arange(N_EXPERTS, dtype=jnp.int32)
    dest = jnp.sum(jnp.where(ids[..., None] == expert, pstarts, 0), axis=-1) + rank
    n_slots = (2 * n // MOE_TILE + N_EXPERTS) * MOE_TILE
    n_blocks = n_slots // MOE_TILE
    n_used = (pends[-1] // MOE_TILE).astype(jnp.int32)
    blk = jnp.minimum(jnp.arange(n_blocks, dtype=jnp.int32), n_used - 1) * MOE_TILE
    block_e = jnp.minimum(jnp.sum(pends[None, :] <= blk[:, None], axis=1), N_EXPERTS - 1).astype(jnp.int32)
    tail_blocks = jnp.where(counts > 0, pends // MOE_TILE - 1, -1).astype(jnp.int32)
    dest_t = dest.reshape(2, n // DMA_TILE, DMA_TILE).transpose(1, 0, 2)
    dest_d = dest.reshape(2, n // DISPATCH_TILE, DISPATCH_TILE).transpose(1, 0, 2)

    xs = _dispatch(tail_blocks, n_used[None], dest_d, h2, n_slots)
    ys = _experts(block_e, n_used[None], xs, w_exp_gate[l], w_exp_up[l], w_exp_down[l])
    out = _combine(dest_t, wts.T, u1, ys)
    return out.reshape(bsz, seq, D_MODEL)


def kernel(x, meta_tokens, g_mix_norm, w_in, g_q_norm, g_k_norm, lambda_q1, lambda_k1, lambda_q2, lambda_k2,
           g_diff_subln, w_gla_gate_up, b_gla_gate, g_gla_norm, w_branch, b_merge_gate, w_out, g_ffn_norm,
           w_router_group, b_router_group, w_router_expert, b_router_expert, w_exp_gate, w_exp_up, w_exp_down):
    depth = w_in.shape[0]
    assert depth == 1, "meta tokens are only carried through a single layer in this implementation"
    assert x.shape[1] % ROW_TILE == 0 and x.shape[2] == D_MODEL
    return _layer(x, meta_tokens, 0, g_mix_norm, w_in, g_q_norm, g_k_norm, lambda_q1, lambda_k1, lambda_q2,
                  lambda_k2, g_diff_subln, w_gla_gate_up, b_gla_gate, g_gla_norm, w_branch, b_merge_gate, w_out,
                  g_ffn_norm, w_router_group, b_router_group, w_router_expert, b_router_expert,
                  w_exp_gate, w_exp_up, w_exp_down)
```
